```python
import jax
import jax.numpy as jnp
from jax import lax
import numpy as np

D_MODEL = 4096
BATCH = 1
SEQ = 8192
DEPTH = 2
DEC_BATCH = 32
DEC_SEQ = 64
PAST_LEN = 1024

CHUNK = 64
HEAD_DIM = 128
A_HEADS = 16
A_WIDTH = A_HEADS * HEAD_DIM
A_PAST_CHUNKS = 8
A_BAND_PAST = A_PAST_CHUNKS * CHUNK
A_BAND_LEN = A_BAND_PAST + CHUNK
REL_CLIP = 128
B_GROUPS = 4
B_GROUP_WIDTH = 512
B_WIDTH = B_GROUPS * B_GROUP_WIDTH
POOL_WINDOWS = (2, 4, 8, 16)
POOL_HIST = max(POOL_WINDOWS) - 1
AB_IN_WIDTH = 3 * A_WIDTH + B_WIDTH
C_HEADS = 32
C_WIDTH = C_HEADS * HEAD_DIM
SB_BLOCK = 128
PEER_HEADS = 8
N_KEYS = 128
N_EXPERTS = N_KEYS * N_KEYS
PEER_TOPK = 16
PEER_QDIM = 256
PEER_HALF = PEER_QDIM // 2
PEER_TOKEN_BLOCK = 64
N_AB_LAYERS = (DEPTH + 1) // 2
N_C_LAYERS = DEPTH // 2
EPS = 1e-6
NEG_INF = -1e30

kernel_name = 'streaming_band_pool_stickbreak_peer'


def rmsnorm(x, g):
    xf = x.astype(jnp.float32)
    y = xf * lax.rsqrt(jnp.mean(xf * xf, axis=-1, keepdims=True) + EPS)
    return (y * g.astype(jnp.float32)).astype(x.dtype)


def ab_project(x, w_in, q_gain, k_gain):
    b, l, _ = x.shape
    up = x @ w_in
    q = up[..., :A_WIDTH].reshape(b, l, A_HEADS, HEAD_DIM)
    k = up[..., A_WIDTH:2 * A_WIDTH].reshape(b, l, A_HEADS, HEAD_DIM)
    v = up[..., 2 * A_WIDTH:3 * A_WIDTH].reshape(b, l, A_HEADS, HEAD_DIM)
    p = up[..., 3 * A_WIDTH:]
    return rmsnorm(q, q_gain), rmsnorm(k, k_gain), v, p


def band_attention(q, k, v, q_pos, k_pos, rel_bias):
    s = jnp.einsum('bqhd,bkhd->bhqk', q, k).astype(jnp.float32) * (HEAD_DIM ** -0.5)
    rel = jnp.clip(q_pos[:, None] - k_pos[None, :], -REL_CLIP, REL_CLIP) + REL_CLIP
    s = s + rel_bias.astype(jnp.float32)[:, rel][None]
    qc = q_pos[:, None] // CHUNK
    kc = k_pos[None, :] // CHUNK
    mask = (k_pos[None, :] >= 0) & (kc <= qc) & (kc >= qc - A_PAST_CHUNKS)
    p = jax.nn.softmax(jnp.where(mask, s, NEG_INF), axis=-1)
    return jnp.einsum('bhqk,bkhd->bqhd', p.astype(v.dtype), v)


def band_attention_prompt(q, k, v, rel_bias):
    b, s, h, d = q.shape
    n_chunks = s // CHUNK
    pad = ((0, 0), (A_BAND_PAST, 0), (0, 0), (0, 0))
    kp = jnp.pad(k, pad)
    vp = jnp.pad(v, pad)

    def one_chunk(c):
        start = c * CHUNK
        qc = lax.dynamic_slice_in_dim(q, start, CHUNK, 1)
        kc = lax.dynamic_slice_in_dim(kp, start, A_BAND_LEN, 1)
        vc = lax.dynamic_slice_in_dim(vp, start, A_BAND_LEN, 1)
        q_pos = start + jnp.arange(CHUNK, dtype=jnp.int32)
        k_pos = start - A_BAND_PAST + jnp.arange(A_BAND_LEN, dtype=jnp.int32)
        return band_attention(qc, kc, vc, q_pos, k_pos, rel_bias)

    o = lax.map(one_chunk, jnp.arange(n_chunks, dtype=jnp.int32))
    return o.transpose(1, 0, 2, 3, 4).reshape(b, s, h, d)


def pool_mix(buf, pos, pool_w, pool_scale):
    b, n, c = buf.shape
    l = n - POOL_HIST
    cur = buf[:, POOL_HIST:].astype(jnp.float32)
    cs = jnp.cumsum(buf.astype(jnp.float32), axis=1)
    cs = jnp.concatenate([jnp.zeros((b, 1, c), jnp.float32), cs], axis=1)
    end = cs[:, POOL_HIST + 1:]
    means = []
    for g, w in enumerate(POOL_WINDOWS):
        sl = slice(g * B_GROUP_WIDTH, (g + 1) * B_GROUP_WIDTH)
        start = cs[:, POOL_HIST + 1 - w:POOL_HIST + 1 - w + l, sl]
        cnt = jnp.minimum(pos + 1, w).astype(jnp.float32)[None, :, None]
        means.append((end[..., sl] - start) / cnt)
    d = (jnp.concatenate(means, axis=-1) - cur).reshape(b, l, B_GROUPS, B_GROUP_WIDTH)
    y = jnp.einsum('blgc,gce->blge', d.astype(pool_w.dtype), pool_w).reshape(b, l, B_WIDTH)
    return y * pool_scale


def c_project(x, w_in):
    b, l, _ = x.shape
    up = x @ w_in
    q = up[..., :C_WIDTH].reshape(b, l, C_HEADS, HEAD_DIM)
    k = up[..., C_WIDTH:2 * C_WIDTH].reshape(b, l, C_HEADS, HEAD_DIM)
    v = up[..., 2 * C_WIDTH:].reshape(b, l, C_HEADS, HEAD_DIM)
    return q, k, v


def stick_breaking(q, k, v, q_pos, k_pos):
    z = jnp.einsum('bqhd,bkhd->bhqk', q, k).astype(jnp.float32) * (HEAD_DIM ** -0.5)
    causal = k_pos[None, :] < q_pos[:, None]
    log_keep = jnp.where(causal, jax.nn.log_sigmoid(-z), 0.0)
    later = lax.cumsum(log_keep, axis=3, reverse=True) - log_keep
    a = jnp.where(causal, jnp.exp(jax.nn.log_sigmoid(z) + later), 0.0)
    return jnp.einsum('bhqk,bkhd->bqhd', a.astype(v.dtype), v)


def stick_breaking_prompt(q, k, v):
    b, s, h, d = q.shape
    n_blocks = s // SB_BLOCK
    k_pos = jnp.arange(s, dtype=jnp.int32)

    def one_block(i):
        start = i * SB_BLOCK
        qb = lax.dynamic_slice_in_dim(q, start, SB_BLOCK, 1)
        return stick_breaking(qb, k, v, start + jnp.arange(SB_BLOCK, dtype=jnp.int32), k_pos)

    o = lax.map(one_block, jnp.arange(n_blocks, dtype=jnp.int32))
    return o.transpose(1, 0, 2, 3, 4).reshape(b, s, h, d)


def peer_ffn(x, w_q, sub_keys, u, v):
    b, l, dm = x.shape
    xt = x.reshape(b * l, dm)
    t = xt.shape[0]
    q = (xt @ w_q).reshape(t, PEER_HEADS, 2, PEER_HALF)
    s = jnp.einsum('thcd,hcnd->thcn', q, sub_keys).astype(jnp.float32)
    s1, i1 = lax.top_k(s[:, :, 0], PEER_TOPK)
    s2, i2 = lax.top_k(s[:, :, 1], PEER_TOPK)
    cand = (s1[..., :, None] + s2[..., None, :]).reshape(t, PEER_HEADS, PEER_TOPK * PEER_TOPK)
    g, ci = lax.top_k(cand, PEER_TOPK)
    e1 = jnp.take_along_axis(i1, ci // PEER_TOPK, axis=-1)
    e2 = jnp.take_along_axis(i2, ci % PEER_TOPK, axis=-1)
    idx = e1 * N_KEYS + e2
    gate = jax.nn.softmax(g, axis=-1)
    n_blk = -(-t // PEER_TOKEN_BLOCK)
    pad = n_blk * PEER_TOKEN_BLOCK - t
    xb = jnp.pad(xt, ((0, pad), (0, 0))).reshape(n_blk, PEER_TOKEN_BLOCK, dm)
    ib = jnp.pad(idx, ((0, pad), (0, 0), (0, 0))).reshape(n_blk, PEER_TOKEN_BLOCK, PEER_HEADS, PEER_TOPK)
    gb = jnp.pad(gate, ((0, pad), (0, 0), (0, 0))).reshape(n_blk, PEER_TOKEN_BLOCK, PEER_HEADS, PEER_TOPK)

    def one_block(args):
        xk, ik, gk = args
        h = jnp.einsum('td,thkd->thk', xk, u[ik]).astype(jnp.float32)
        act = gk * jax.nn.gelu(h, approximate=False)
        return jnp.einsum('thk,thkd->td', act.astype(v.dtype), v[ik])

    out = lax.map(one_block, (xb, ib, gb)).reshape(n_blk * PEER_TOKEN_BLOCK, dm)[:t]
    return out.reshape(b, l, dm)


def setup_inputs(seed: int = 0) -> dict:
    key = jax.random.key(seed)
    ks = jax.random.split(key, 24)

    def nrm(k, shape, scale):
        return jax.random.normal(k, shape, jnp.float32) * scale

    win_a = min(A_BAND_PAST, PAST_LEN)
    return {
        'x_prompt': nrm(ks[0], (BATCH, SEQ, D_MODEL), 1.0),
        'x_sample': nrm(ks[1], (DEC_BATCH, DEC_SEQ, D_MODEL), 1.0),
        'cache_a_k': nrm(ks[2], (N_AB_LAYERS, DEC_BATCH, win_a, A_HEADS, HEAD_DIM), 1.0),
        'cache_a_v': nrm(ks[3], (N_AB_LAYERS, DEC_BATCH, win_a, A_HEADS, HEAD_DIM), 1.0),
        'state_b_pool': nrm(ks[4], (N_AB_LAYERS, DEC_BATCH, POOL_HIST, B_WIDTH), 1.0),
        'cache_c_k': nrm(ks[5], (N_C_LAYERS, DEC_BATCH, PAST_LEN, C_HEADS, HEAD_DIM), 1.0),
        'cache_c_v': nrm(ks[6], (N_C_LAYERS, DEC_BATCH, PAST_LEN, C_HEADS, HEAD_DIM), 1.0),
        'norm_mix': 1.0 + nrm(ks[7], (DEPTH, D_MODEL), 0.02),
        'norm_ffn': 1.0 + nrm(ks[8], (DEPTH, D_MODEL), 0.02),
        'ab_w_in': nrm(ks[9], (N_AB_LAYERS, D_MODEL, AB_IN_WIDTH), D_MODEL ** -0.5),
        'ab_q_gain': 1.0 + nrm(ks[10], (N_AB_LAYERS, HEAD_DIM), 0.02),
        'ab_k_gain': 1.0 + nrm(ks[11], (N_AB_LAYERS, HEAD_DIM), 0.02),
        'ab_rel_bias': nrm(ks[12], (N_AB_LAYERS, A_HEADS, 2 * REL_CLIP + 1), 0.2),
        'ab_pool_w': nrm(ks[13], (N_AB_LAYERS, B_GROUPS, B_GROUP_WIDTH, B_GROUP_WIDTH), B_GROUP_WIDTH ** -0.5),
        'ab_pool_scale': 1.0 + nrm(ks[14], (N_AB_LAYERS, B_WIDTH), 0.02),
        'ab_w_out': nrm(ks[15], (N_AB_LAYERS, A_WIDTH + B_WIDTH, D_MODEL), (A_WIDTH + B_WIDTH) ** -0.5),
        'c_w_in': nrm(ks[16], (N_C_LAYERS, D_MODEL, 3 * C_WIDTH), D_MODEL ** -0.5),
        'c_w_out': nrm(ks[17], (N_C_LAYERS, C_WIDTH, D_MODEL), C_WIDTH ** -0.5),
        'peer_w_q': nrm(ks[18], (DEPTH, D_MODEL, PEER_HEADS * PEER_QDIM), D_MODEL ** -0.5),
        'peer_sub_keys': nrm(ks[19], (DEPTH, PEER_HEADS, 2, N_KEYS, PEER_HALF), PEER_HALF ** -0.5),
        'peer_u': nrm(ks[20], (DEPTH, N_EXPERTS, D_MODEL), D_MODEL ** -0.5),
        'peer_v': nrm(ks[21], (DEPTH, N_EXPERTS, D_MODEL), 0.25),
    }


def reference(x_prompt, x_sample, cache_a_k, cache_a_v, state_b_pool, cache_c_k, cache_c_v,
              norm_mix, norm_ffn, ab_w_in, ab_q_gain, ab_k_gain, ab_rel_bias, ab_pool_w,
              ab_pool_scale, ab_w_out, c_w_in, c_w_out, peer_w_q, peer_sub_keys, peer_u, peer_v):
    bp, sp, _ = x_prompt.shape
    bs, ls, _ = x_sample.shape
    pos_p = jnp.arange(sp, dtype=jnp.int32)
    pos_s = PAST_LEN + jnp.arange(ls, dtype=jnp.int32)
    win_p = min(A_BAND_PAST, sp)
    win_s = cache_a_k.shape[2]
    k_pos_a_s = jnp.concatenate([PAST_LEN - win_s + jnp.arange(win_s, dtype=jnp.int32), pos_s])
    k_pos_c_s = jnp.arange(PAST_LEN + ls, dtype=jnp.int32)

    hp, hs = x_prompt, x_sample
    a_k_p, a_v_p, b_pool_p, c_k_p, c_v_p = [], [], [], [], []
    a_k_s, a_v_s, b_pool_s, c_k_s, c_v_s = [], [], [], [], []
    for layer in range(DEPTH):
        xp = rmsnorm(hp, norm_mix[layer])
        xs = rmsnorm(hs, norm_mix[layer])
        if layer % 2 == 0:
            i = layer // 2
            qp, kp, vp, pp = ab_project(xp, ab_w_in[i], ab_q_gain[i], ab_k_gain[i])
            qs, ks, vs, ps = ab_project(xs, ab_w_in[i], ab_q_gain[i], ab_k_gain[i])
            att_p = band_attention_prompt(qp, kp, vp, ab_rel_bias[i])
            att_s = band_attention(qs, jnp.concatenate([cache_a_k[i], ks], axis=1),
                                   jnp.concatenate([cache_a_v[i], vs], axis=1),
                                   pos_s, k_pos_a_s, ab_rel_bias[i])
            pool_buf_p = jnp.concatenate([jnp.zeros((bp, POOL_HIST, B_WIDTH), pp.dtype), pp], axis=1)
            pool_buf_s = jnp.concatenate([state_b_pool[i], ps], axis=1)
            pool_p = pool_mix(pool_buf_p, pos_p, ab_pool_w[i], ab_pool_scale[i])
            pool_s = pool_mix(pool_buf_s, pos_s, ab_pool_w[i], ab_pool_scale[i])
            mix_p = jnp.concatenate([att_p.reshape(bp, sp, A_WIDTH), pool_p], axis=-1) @ ab_w_out[i]
            mix_s = jnp.concatenate([att_s.reshape(bs, ls, A_WIDTH), pool_s], axis=-1) @ ab_w_out[i]
            a_k_p.append(kp[:, sp - win_p:])
            a_v_p.append(vp[:, sp - win_p:])
            b_pool_p.append(pp[:, sp - POOL_HIST:])
            a_k_s.append(ks)
            a_v_s.append(vs)
            b_pool_s.append(pool_buf_s[:, ls:])
        else:
            j = layer // 2
            qp, kp, vp = c_project(xp, c_w_in[j])
            qs, ks, vs = c_project(xs, c_w_in[j])
            att_p = stick_breaking_prompt(qp, kp, vp)
            att_s = stick_breaking(qs, jnp.concatenate([cache_c_k[j], ks], axis=1),
                                   jnp.concatenate([cache_c_v[j], vs], axis=1), pos_s, k_pos_c_s)
            mix_p = att_p.reshape(bp, sp, C_WIDTH) @ c_w_out[j]
            mix_s = att_s.reshape(bs, ls, C_WIDTH) @ c_w_out[j]
            c_k_p.append(kp)
            c_v_p.append(vp)
            c_k_s.append(ks)
            c_v_s.append(vs)
        hp = hp + mix_p
        hs = hs + mix_s
        hp = hp + peer_ffn(rmsnorm(hp, norm_ffn[layer]), peer_w_q[layer], peer_sub_keys[layer],
                           peer_u[layer], peer_v[layer])
        hs = hs + peer_ffn(rmsnorm(hs, norm_ffn[layer]), peer_w_q[layer], peer_sub_keys[layer],
                           peer_u[layer], peer_v[layer])

    return (hp, hs,
            jnp.stack(a_k_p), jnp.stack(a_v_p), jnp.stack(b_pool_p), jnp.stack(c_k_p), jnp.stack(c_v_p),
            jnp.stack(a_k_s), jnp.stack(a_v_s), jnp.stack(b_pool_s), jnp.stack(c_k_s), jnp.stack(c_v_s))
```

```python
import functools

import jax
import jax.numpy as jnp
from jax import lax
from jax.experimental import pallas as pl
from jax.experimental.pallas import tpu as pltpu

F32 = jnp.float32
BF16 = jnp.bfloat16

HEAD_DIM = 128
CHUNK = 64
A_PAST_CHUNKS = 8
REL_CLIP = 128
POOL_WINDOWS = (2, 4, 8, 16)
POOL_HIST = max(POOL_WINDOWS) - 1
POOL_HIST_PAD = 16
B_GROUP_WIDTH = 512
PEER_HEADS = 8
N_KEYS = 128
PEER_TOPK = 16
EPS = 1e-6
NEG_INF = -1e30
ATTN_SCALE = HEAD_DIM ** -0.5
SB_EXIT = -104.0
SB_KEYS = 128
VMEM_LIMIT_BYTES = 56 * 1024 * 1024


def _params(*sem):
    return pltpu.CompilerParams(dimension_semantics=sem, vmem_limit_bytes=VMEM_LIMIT_BYTES)


def _tile(dim, pref):
    return pref if dim % pref == 0 else dim


def _dot(a, b):
    return jnp.dot(a, b, preferred_element_type=F32)


def _dot_nt(a, b):
    return lax.dot_general(a, b, (((1,), (1,)), ((), ())), preferred_element_type=F32)


def _rmsnorm_kernel(x_ref, g_ref, o_ref):
    x = x_ref[...]
    ms = jnp.mean(x * x, axis=-1, keepdims=True)
    o_ref[...] = (x * lax.rsqrt(ms + EPS) * g_ref[...]).astype(o_ref.dtype)


def _rmsnorm(x, g):
    t, d = x.shape
    tr = _tile(t, 256)
    return pl.pallas_call(
        _rmsnorm_kernel,
        grid=(t // tr,),
        in_specs=[pl.BlockSpec((tr, d), lambda i: (i, 0)),
                  pl.BlockSpec((1, d), lambda i: (0, 0))],
        out_specs=pl.BlockSpec((tr, d), lambda i: (i, 0)),
        out_shape=jax.ShapeDtypeStruct((t, d), BF16),
        compiler_params=_params("parallel"),
        name="rmsnorm",
    )(x, g.reshape(1, d))


def _mm_kernel(*refs, nk, mode):
    x_ref, w_ref = refs[0], refs[1]
    if mode in ("headnorm", "residual"):
        aux_ref, o_ref, scratch = refs[2], refs[3], refs[4:]
    else:
        aux_ref, o_ref, scratch = None, refs[2], refs[3:]

    def finish(acc):
        if mode == "headnorm":
            g = aux_ref[...]
            for c in range(acc.shape[1] // HEAD_DIM):
                sl = slice(c * HEAD_DIM, (c + 1) * HEAD_DIM)
                y = acc[:, sl]
                ms = jnp.mean(y * y, axis=-1, keepdims=True)
                o_ref[:, sl] = (y * lax.rsqrt(ms + EPS) * g).astype(o_ref.dtype)
        elif mode == "residual":
            o_ref[...] = aux_ref[...] + acc
        else:
            o_ref[...] = acc.astype(o_ref.dtype)

    if nk == 1:
        finish(_dot(x_ref[...], w_ref[...]))
    else:
        acc_ref = scratch[0]
        k = pl.program_id(2)

        @pl.when(k == 0)
        def _():
            acc_ref[...] = jnp.zeros_like(acc_ref)

        acc_ref[...] += _dot(x_ref[...], w_ref[...])

        @pl.when(k == nk - 1)
        def _():
            finish(acc_ref[...])


def _matmul(x, w, *, col0=0, n=None, out_dtype=F32, mode="plain", aux=None,
            tm=1024, tn=512, tk=None, name="matmul"):
    m, kdim = x.shape
    n = w.shape[1] if n is None else n
    tm, tn = _tile(m, tm), _tile(n, tn)
    tk = kdim if tk is None else _tile(kdim, tk)
    nk = kdim // tk
    cb = col0 // tn
    assert col0 % tn == 0
    in_specs = [pl.BlockSpec((tm, tk), lambda i, j, k: (i, k)),
                pl.BlockSpec((tk, tn), lambda i, j, k: (k, j + cb))]
    args = [x, w]
    if mode == "headnorm":
        in_specs.append(pl.BlockSpec((1, HEAD_DIM), lambda i, j, k: (0, 0)))
        args.append(aux.reshape(1, HEAD_DIM))
    elif mode == "residual":
        in_specs.append(pl.BlockSpec((tm, tn), lambda i, j, k: (i, j)))
        args.append(aux)
    scratch = [pltpu.VMEM((tm, tn), F32)] if nk > 1 else []
    return pl.pallas_call(
        functools.partial(_mm_kernel, nk=nk, mode=mode),
        grid=(m // tm, n // tn, nk),
        in_specs=in_specs,
        out_specs=pl.BlockSpec((tm, tn), lambda i, j, k: (i, j)),
        out_shape=jax.ShapeDtypeStruct((m, n), out_dtype),
        scratch_shapes=scratch,
        compiler_params=_params("parallel", "parallel", "arbitrary"),
        name=name,
    )(*args)


def _band_kernel(q_ref, ka_ref, kb_ref, va_ref, vb_ref, bias_ref, o_ref, *, na, first_has_no_past):
    q = q_ref[...]
    sa = _dot_nt(q, ka_ref[...].astype(BF16)) * ATTN_SCALE + bias_ref[:, :na]
    sb = _dot_nt(q, kb_ref[...].astype(BF16)) * ATTN_SCALE + bias_ref[:, na:]
    if first_has_no_past:
        sa = jnp.where(pl.program_id(1) > 0, sa, NEG_INF)
    m = jnp.maximum(jnp.max(sa, axis=-1, keepdims=True), jnp.max(sb, axis=-1, keepdims=True))
    pa = jnp.exp(sa - m)
    pb = jnp.exp(sb - m)
    l = jnp.sum(pa, axis=-1, keepdims=True) + jnp.sum(pb, axis=-1, keepdims=True)
    o = _dot(pa.astype(BF16), va_ref[...].astype(BF16)) + _dot(pb.astype(BF16), vb_ref[...].astype(BF16))
    o_ref[...] = (o / l).astype(o_ref.dtype)


def _band_bias(rel_bias, q_pos, k_pos):
    rel = jnp.clip(q_pos[:, None] - k_pos[None, :], -REL_CLIP, REL_CLIP) + REL_CLIP
    qc = q_pos[:, None] // CHUNK
    kc = k_pos[None, :] // CHUNK
    mask = (kc <= qc) & (kc >= qc - A_PAST_CHUNKS)
    return jnp.where(mask[None], rel_bias.astype(F32)[:, rel], NEG_INF)


def _band_prompt(q, k, v, rel_bias, sp):
    heads = rel_bias.shape[0]
    tq = A_PAST_CHUNKS * CHUNK
    assert sp % tq == 0
    pos = jnp.arange(tq, dtype=jnp.int32)
    bias = _band_bias(rel_bias, tq + pos, jnp.arange(2 * tq, dtype=jnp.int32))
    prev = lambda h, i: (jnp.maximum(i - 1, 0), h)
    cur = lambda h, i: (i, h)
    blk = lambda im: pl.BlockSpec((tq, HEAD_DIM), im)
    return pl.pallas_call(
        functools.partial(_band_kernel, na=tq, first_has_no_past=True),
        grid=(heads, sp // tq),
        in_specs=[blk(cur), blk(prev), blk(cur), blk(prev), blk(cur),
                  pl.BlockSpec((None, tq, 2 * tq), lambda h, i: (h, 0, 0))],
        out_specs=blk(cur),
        out_shape=jax.ShapeDtypeStruct((sp, heads * HEAD_DIM), BF16),
        compiler_params=_params("parallel", "arbitrary"),
        name="band_prompt",
    )(q, k, k, v, v, bias)


def _band_sample(q, k, v, cache_k, cache_v, rel_bias, sp, bs, ls, past_len):
    heads = rel_bias.shape[0]
    win = cache_k.shape[0] // bs
    assert sp % ls == 0
    row0 = sp // ls
    q_pos = past_len + jnp.arange(ls, dtype=jnp.int32)
    k_pos = jnp.concatenate([past_len - win + jnp.arange(win, dtype=jnp.int32), q_pos])
    bias = _band_bias(rel_bias, q_pos, k_pos)
    new = lambda h, b: (row0 + b, h)
    old = lambda h, b: (b, h)
    blk_new = pl.BlockSpec((ls, HEAD_DIM), new)
    blk_old = pl.BlockSpec((win, HEAD_DIM), old)
    return pl.pallas_call(
        functools.partial(_band_kernel, na=win, first_has_no_past=False),
        grid=(heads, bs),
        in_specs=[blk_new, blk_old, blk_new, blk_old, blk_new,
                  pl.BlockSpec((None, ls, win + ls), lambda h, b: (h, 0, 0))],
        out_specs=pl.BlockSpec((ls, HEAD_DIM), old),
        out_shape=jax.ShapeDtypeStruct((bs * ls, heads * HEAD_DIM), BF16),
        compiler_params=_params("parallel", "arbitrary"),
        name="band_sample",
    )(q, cache_k, k, cache_v, v, bias)


def _pool_kernel(cur_ref, hist_ref, w_ref, sc_ref, o_ref, ext_ref, *, pos_base, pos_step):
    tr = cur_ref.shape[0]
    ext_ref[0:POOL_HIST_PAD, :] = hist_ref[...]
    ext_ref[POOL_HIST_PAD:POOL_HIST_PAD + tr, :] = cur_ref[...]
    pos = pos_base + pl.program_id(0) * pos_step + lax.broadcasted_iota(jnp.int32, (tr, 1), 0)
    for g, w in enumerate(POOL_WINDOWS):
        sl = slice(g * B_GROUP_WIDTH, (g + 1) * B_GROUP_WIDTH)
        cur = cur_ref[:, sl]
        tot = cur
        for j in range(1, w):
            tot = tot + ext_ref[POOL_HIST_PAD - j:POOL_HIST_PAD - j + tr, sl]
        cnt = jnp.minimum(pos + 1, w).astype(F32)
        d = tot / cnt - cur
        y = _dot(d.astype(BF16), w_ref[g]) * sc_ref[:, sl]
        o_ref[:, sl] = y.astype(o_ref.dtype)


def _pool(p, hist, pool_w, pool_scale, *, row_block0, nblk, tr, pos_base, pos_step, name):
    width = p.shape[1]
    return pl.pallas_call(
        functools.partial(_pool_kernel, pos_base=pos_base, pos_step=pos_step),
        grid=(nblk,),
        in_specs=[pl.BlockSpec((tr, width), lambda i: (row_block0 + i, 0)),
                  pl.BlockSpec((None, POOL_HIST_PAD, width), lambda i: (i, 0, 0)),
                  pl.BlockSpec(pool_w.shape, lambda i: (0, 0, 0)),
                  pl.BlockSpec((1, width), lambda i: (0, 0))],
        out_specs=pl.BlockSpec((tr, width), lambda i: (i, 0)),
        out_shape=jax.ShapeDtypeStruct((nblk * tr, width), BF16),
        scratch_shapes=[pltpu.VMEM((POOL_HIST_PAD + tr, width), F32)],
        compiler_params=_params("parallel"),
        name=name,
    )(p, hist, pool_w, pool_scale.reshape(1, width))


def _sb_step(q, k, v, tri, carry, acc, causal):
    nk = k.shape[0]
    z = _dot_nt(q, k) * ATTN_SCALE
    log_keep = -(jnp.maximum(z, 0.0) + jnp.log1p(jnp.exp(-jnp.abs(z))))
    if causal is not None:
        log_keep = jnp.where(causal, log_keep, 0.0)
    hi = log_keep.astype(BF16)
    lo = (log_keep - hi.astype(F32)).astype(BF16)
    sums = _dot(hi, tri) + _dot(lo, tri)
    later = sums[:, :nk] + carry[:, :nk]
    a = jnp.exp(z + log_keep + later)
    if causal is not None:
        a = jnp.where(causal, a, 0.0)
    return carry + sums[:, nk:], acc + _dot(a.astype(BF16), v)


def _sb_sweep(q, k_ref, v_ref, tri, kb0, carry, acc, qpos):
    lane = lax.broadcasted_iota(jnp.int32, (q.shape[0], SB_KEYS), 1)

    def cond(st):
        return (st[0] >= 0) & (st[1] == 0)

    def body(st):
        kb, _, carry, acc = st
        start = pl.multiple_of(kb * SB_KEYS, SB_KEYS)
        k = k_ref[pl.ds(start, SB_KEYS), :].astype(BF16)
        v = v_ref[pl.ds(start, SB_KEYS), :].astype(BF16)
        causal = None if qpos is None else (start + lane) < qpos
        carry, acc = _sb_step(q, k, v, tri, carry, acc, causal)
        done = (jnp.max(carry) < SB_EXIT).astype(jnp.int32)
        return kb - 1, done, carry, acc

    return lax.while_loop(cond, body, (jnp.asarray(kb0, jnp.int32), jnp.int32(0), carry, acc))[3]


def _sb_prompt_kernel(q_ref, k_ref, v_ref, tri_ref, o_ref):
    tq = q_ref.shape[0]
    row0 = pl.program_id(1) * tq
    qpos = row0 + lax.broadcasted_iota(jnp.int32, (tq, SB_KEYS), 0)
    zeros = jnp.zeros((tq, HEAD_DIM), F32)
    kb0 = (row0 + tq) // SB_KEYS - 1
    o_ref[...] = _sb_sweep(q_ref[...], k_ref, v_ref, tri_ref[...], kb0, zeros, zeros, qpos).astype(o_ref.dtype)


def _sb_sample_kernel(q_ref, kc_ref, vc_ref, kn_ref, vn_ref, trin_ref, tri_ref, o_ref):
    ls = q_ref.shape[0]
    q = q_ref[...]
    causal = lax.broadcasted_iota(jnp.int32, (ls, ls), 1) < lax.broadcasted_iota(jnp.int32, (ls, ls), 0)
    zeros = jnp.zeros((ls, HEAD_DIM), F32)
    carry, acc = _sb_step(q, kn_ref[...].astype(BF16), vn_ref[...].astype(BF16), trin_ref[...],
                          zeros, zeros, causal)
    kb0 = kc_ref.shape[0] // SB_KEYS - 1
    o_ref[...] = _sb_sweep(q, kc_ref, vc_ref, tri_ref[...], kb0, carry, acc, None).astype(o_ref.dtype)


def _sb_tri(nk):
    j = jnp.arange(nk)[:, None]
    s = jnp.arange(nk + HEAD_DIM)[None, :]
    return ((s >= nk) | (j > s)).astype(BF16)


def _sb_prompt(q, k, v, sp):
    heads = q.shape[1] // HEAD_DIM
    tq = _tile(sp, 256)
    assert sp % SB_KEYS == 0 and tq % SB_KEYS == 0
    return pl.pallas_call(
        _sb_prompt_kernel,
        grid=(heads, sp // tq),
        in_specs=[pl.BlockSpec((tq, HEAD_DIM), lambda h, i: (i, h)),
                  pl.BlockSpec((sp, HEAD_DIM), lambda h, i: (0, h)),
                  pl.BlockSpec((sp, HEAD_DIM), lambda h, i: (0, h)),
                  pl.BlockSpec((SB_KEYS, SB_KEYS + HEAD_DIM), lambda h, i: (0, 0))],
        out_specs=pl.BlockSpec((tq, HEAD_DIM), lambda h, i: (i, h)),
        out_shape=jax.ShapeDtypeStruct((sp, heads * HEAD_DIM), BF16),
        compiler_params=_params("parallel", "arbitrary"),
        name="stickbreak_prompt",
    )(q, k, v, _sb_tri(SB_KEYS))


def _sb_sample(q, k, v, cache_k, cache_v, sp, bs, ls):
    heads = q.shape[1] // HEAD_DIM
    past = cache_k.shape[0] // bs
    assert past % SB_KEYS == 0 and sp % ls == 0
    row0 = sp // ls
    new = lambda h, b: (row0 + b, h)
    old = lambda h, b: (b, h)
    blk_new = pl.BlockSpec((ls, HEAD_DIM), new)
    blk_old = pl.BlockSpec((past, HEAD_DIM), old)
    return pl.pallas_call(
        _sb_sample_kernel,
        grid=(heads, bs),
        in_specs=[blk_new, blk_old, blk_old, blk_new, blk_new,
                  pl.BlockSpec((ls, ls + HEAD_DIM), lambda h, b: (0, 0)),
                  pl.BlockSpec((SB_KEYS, SB_KEYS + HEAD_DIM), lambda h, b: (0, 0))],
        out_specs=pl.BlockSpec((ls, HEAD_DIM), old),
        out_shape=jax.ShapeDtypeStruct((bs * ls, heads * HEAD_DIM), BF16),
        compiler_params=_params("parallel", "arbitrary"),
        name="stickbreak_sample",
    )(q, cache_k, cache_v, k, v, _sb_tri(ls), _sb_tri(SB_KEYS))


def _peer_scores_kernel(key_ref, q_ref, o_ref):
    o_ref[...] = _dot_nt(key_ref[...], q_ref[...])


def _peer_scores(pq, sub_keys):
    t = pq.shape[0]
    nhc, nkeys, half = sub_keys.shape
    tt = _tile(t, 1024)
    return pl.pallas_call(
        _peer_scores_kernel,
        grid=(nhc, t // tt),
        in_specs=[pl.BlockSpec((None, nkeys, half), lambda c, i: (c, 0, 0)),
                  pl.BlockSpec((tt, half), lambda c, i: (i, c))],
        out_specs=pl.BlockSpec((None, nkeys, tt), lambda c, i: (c, 0, i)),
        out_shape=jax.ShapeDtypeStruct((nhc, nkeys, t), F32),
        compiler_params=_params("parallel", "parallel"),
        name="peer_scores",
    )(sub_keys, pq)


def _top_values(s, k):
    rows = lax.broadcasted_iota(jnp.int32, s.shape, 0)
    vals = []
    for _ in range(k):
        m = jnp.max(s, axis=0, keepdims=True)
        vals.append(m)
        first = jnp.min(jnp.where(s == m, rows, s.shape[0]), axis=0, keepdims=True)
        s = jnp.where(rows == first, -jnp.inf, s)
    return vals


def _peer_select_kernel(s_ref, s1x_ref, e1x_ref, e2_ref, thr_ref, *, ne):
    def one_head(h, _):
        s1 = s_ref[2 * h]
        s2 = s_ref[2 * h + 1]
        a = _top_values(s1, PEER_TOPK)
        b = jnp.concatenate(_top_values(s2, PEER_TOPK), axis=0)
        cand = jnp.concatenate([ai + b for ai in a], axis=0)
        g = _top_values(cand, PEER_TOPK)
        z = jnp.ones_like(g[0])
        for gk in g[1:]:
            z = z + jnp.exp(gk - g[0])
        e1 = jnp.exp(s1 - a[0]) / z
        e2_ref[h] = jnp.exp(s2 - b[0:1])
        thr_ref[pl.ds(h, 1), :] = g[PEER_TOPK - 1]
        row = pl.multiple_of(h * ne, ne)
        for j in range(N_KEYS // ne):
            s1x_ref[j, pl.ds(row, ne), :] = s1[j * ne:(j + 1) * ne]
            e1x_ref[j, pl.ds(row, ne), :] = e1[j * ne:(j + 1) * ne]
        return 0

    lax.fori_loop(0, PEER_HEADS, one_head, 0)


def _peer_select(scores, ne):
    nhc, nkeys, t = scores.shape
    tt = _tile(t, 256)
    x_shape = jax.ShapeDtypeStruct((nkeys // ne, PEER_HEADS * ne, t), F32)
    x_spec = pl.BlockSpec((nkeys // ne, PEER_HEADS * ne, tt), lambda i: (0, 0, i))
    return pl.pallas_call(
        functools.partial(_peer_select_kernel, ne=ne),
        grid=(t // tt,),
        in_specs=[pl.BlockSpec((nhc, nkeys, tt), lambda i: (0, 0, i))],
        out_specs=[x_spec, x_spec,
                   pl.BlockSpec((PEER_HEADS, nkeys, tt), lambda i: (0, 0, i)),
                   pl.BlockSpec((PEER_HEADS, tt), lambda i: (0, i))],
        out_shape=[x_shape, x_shape,
                   jax.ShapeDtypeStruct((PEER_HEADS, nkeys, t), F32),
                   jax.ShapeDtypeStruct((PEER_HEADS, t), F32)],
        compiler_params=_params("parallel"),
        name="peer_select",
    )(scores)


def _peer_act_kernel(u_ref, x_ref, s1x_ref, e1x_ref, s2_ref, e2_ref, thr_ref, o_ref, *, ne):
    ht = _dot_nt(u_ref[...], x_ref[...])
    for e in range(ne):
        gate = jnp.zeros((N_KEYS, ht.shape[1]), F32)
        for h in range(PEER_HEADS):
            r = h * ne + e
            pair = s1x_ref[r:r + 1, :] + s2_ref[h]
            w = e1x_ref[r:r + 1, :] * e2_ref[h]
            gate = gate + jnp.where(pair >= thr_ref[h:h + 1, :], w, 0.0)
        hh = ht[e * N_KEYS:(e + 1) * N_KEYS, :]
        gelu = 0.5 * hh * (1.0 + lax.erf(hh * (0.5 ** 0.5)))
        o_ref[e * N_KEYS:(e + 1) * N_KEYS, :] = (gate * gelu).astype(o_ref.dtype)


def _peer_act(xn, u, scores, s1x, e1x, e2, thr, ne):
    t, d = xn.shape
    n_exp = u.shape[0]
    tt = _tile(t, 512)
    te = ne * N_KEYS
    s4 = scores.reshape(PEER_HEADS, 2, N_KEYS, t)
    return pl.pallas_call(
        functools.partial(_peer_act_kernel, ne=ne),
        grid=(t // tt, n_exp // te),
        in_specs=[pl.BlockSpec((te, d), lambda i, j: (j, 0)),
                  pl.BlockSpec((tt, d), lambda i, j: (i, 0)),
                  pl.BlockSpec((None, PEER_HEADS * ne, tt), lambda i, j: (j, 0, i)),
                  pl.BlockSpec((None, PEER_HEADS * ne, tt), lambda i, j: (j, 0, i)),
                  pl.BlockSpec((PEER_HEADS, None, N_KEYS, tt), lambda i, j: (0, 1, 0, i)),
                  pl.BlockSpec((PEER_HEADS, N_KEYS, tt), lambda i, j: (0, 0, i)),
                  pl.BlockSpec((PEER_HEADS, tt), lambda i, j: (0, i))],
        out_specs=pl.BlockSpec((te, tt), lambda i, j: (j, i)),
        out_shape=jax.ShapeDtypeStruct((n_exp, t), BF16),
        compiler_params=_params("parallel", "arbitrary"),
        name="peer_act",
    )(u, xn, s1x, e1x, s4, e2, thr)


def _peer_ffn(h, norm_g, w_q, sub_keys, u, v_t):
    ne = 8
    xn = _rmsnorm(h, norm_g)
    pq = _matmul(xn, w_q, out_dtype=BF16, name="peer_query")
    scores = _peer_scores(pq, sub_keys)
    s1x, e1x, e2, thr = _peer_select(scores, ne)
    act_t = _peer_act(xn, u, scores, s1x, e1x, e2, thr, ne)
    out_t = _matmul(v_t, act_t, tm=1024, tn=1024, tk=2048, name="peer_down")
    return h + out_t.T


def kernel(x_prompt, x_sample, cache_a_k, cache_a_v, state_b_pool, cache_c_k, cache_c_v, norm_mix, norm_ffn, ab_w_in, ab_q_gain, ab_k_gain, ab_rel_bias, ab_pool_w, ab_pool_scale, ab_w_out, c_w_in, c_w_out, peer_w_q, peer_sub_keys, peer_u, peer_v):
    bp, sp, d = x_prompt.shape
    bs, ls, _ = x_sample.shape
    assert bp == 1
    depth = norm_mix.shape[0]
    past_len = cache_c_k.shape[2]
    a_heads = ab_rel_bias.shape[1]
    a_width = a_heads * HEAD_DIM
    b_width = ab_pool_scale.shape[1]
    c_width = c_w_out.shape[1]
    win_p = min(A_PAST_CHUNKS * CHUNK, sp)
    n_s = bs * ls

    h = jnp.concatenate([x_prompt.reshape(sp, d), x_sample.reshape(n_s, d)], axis=0)

    outs = {name: [] for name in ("a_k_p", "a_v_p", "b_p", "c_k_p", "c_v_p",
                                  "a_k_s", "a_v_s", "b_s", "c_k_s", "c_v_s")}
    for layer in range(depth):
        xn = _rmsnorm(h, norm_mix[layer])
        if layer % 2 == 0:
            i = layer // 2
            w_in = ab_w_in[i].astype(BF16)
            q = _matmul(xn, w_in, col0=0, n=a_width, out_dtype=BF16, mode="headnorm",
                        aux=ab_q_gain[i], name="a_q")
            k = _matmul(xn, w_in, col0=a_width, n=a_width, mode="headnorm", aux=ab_k_gain[i], name="a_k")
            v = _matmul(xn, w_in, col0=2 * a_width, n=a_width, name="a_v")
            p = _matmul(xn, w_in, col0=3 * a_width, n=b_width, name="b_in")

            att_p = _band_prompt(q, k, v, ab_rel_bias[i], sp)
            att_s = _band_sample(q, k, v, cache_a_k[i].reshape(-1, a_width), cache_a_v[i].reshape(-1, a_width),
                                 ab_rel_bias[i], sp, bs, ls, past_len)

            pool_w = ab_pool_w[i].astype(BF16)
            tr = _tile(sp, 256)
            pp = p[:sp].reshape(sp // tr, tr, b_width)
            hist_p = jnp.concatenate([jnp.zeros((1, POOL_HIST_PAD, b_width), F32),
                                      pp[:-1, tr - POOL_HIST_PAD:]], axis=0)
            pool_p = _pool(p, hist_p, pool_w, ab_pool_scale[i], row_block0=0, nblk=sp // tr, tr=tr,
                           pos_base=0, pos_step=tr, name="pool_prompt")
            hist_s = jnp.pad(state_b_pool[i], ((0, 0), (POOL_HIST_PAD - POOL_HIST, 0), (0, 0)))
            pool_s = _pool(p, hist_s, pool_w, ab_pool_scale[i], row_block0=sp // ls, nblk=bs, tr=ls,
                           pos_base=past_len, pos_step=0, name="pool_sample")

            mix_in = jnp.concatenate([jnp.concatenate([att_p, pool_p], axis=1),
                                      jnp.concatenate([att_s, pool_s], axis=1)], axis=0)
            h = _matmul(mix_in, ab_w_out[i].astype(BF16), mode="residual", aux=h, name="a_out")

            ps = p[sp:].reshape(bs, ls, b_width)
            outs["a_k_p"].append(k[sp - win_p:sp].reshape(bp, win_p, a_heads, HEAD_DIM))
            outs["a_v_p"].append(v[sp - win_p:sp].reshape(bp, win_p, a_heads, HEAD_DIM))
            outs["b_p"].append(p[sp - POOL_HIST:sp].reshape(bp, POOL_HIST, b_width))
            outs["a_k_s"].append(k[sp:].reshape(bs, ls, a_heads, HEAD_DIM))
            outs["a_v_s"].append(v[sp:].reshape(bs, ls, a_heads, HEAD_DIM))
            outs["b_s"].append(jnp.concatenate([state_b_pool[i], ps], axis=1)[:, ls:])
        else:
            j = layer // 2
            c_heads = c_width // HEAD_DIM
            w_in = c_w_in[j].astype(BF16)
            q = _matmul(xn, w_in, col0=0, n=c_width, out_dtype=BF16, name="c_q")
            k = _matmul(xn, w_in, col0=c_width, n=c_width, name="c_k")
            v = _matmul(xn, w_in, col0=2 * c_width, n=c_width, name="c_v")
            att_p = _sb_prompt(q, k, v, sp)
            att_s = _sb_sample(q, k, v, cache_c_k[j].reshape(-1, c_width), cache_c_v[j].reshape(-1, c_width),
                               sp, bs, ls)
            mix_in = jnp.concatenate([att_p, att_s], axis=0)
            h = _matmul(mix_in, c_w_out[j].astype(BF16), mode="residual", aux=h, name="c_out")
            outs["c_k_p"].append(k[:sp].reshape(bp, sp, c_heads, HEAD_DIM))
            outs["c_v_p"].append(v[:sp].reshape(bp, sp, c_heads, HEAD_DIM))
            outs["c_k_s"].append(k[sp:].reshape(bs, ls, c_heads, HEAD_DIM))
            outs["c_v_s"].append(v[sp:].reshape(bs, ls, c_heads, HEAD_DIM))

        sub_keys = peer_sub_keys[layer].astype(BF16).reshape(PEER_HEADS * 2, N_KEYS, -1)
        h = _peer_ffn(h, norm_ffn[layer], peer_w_q[layer].astype(BF16), sub_keys,
                      peer_u[layer].astype(BF16), peer_v[layer].astype(BF16).T)

    st = {name: jnp.stack(vals) for name, vals in outs.items()}
    return (h[:sp].reshape(bp, sp, d), h[sp:].reshape(bs, ls, d),
            st["a_k_p"], st["a_v_p"], st["b_p"], st["c_k_p"], st["c_v_p"],
            st["a_k_s"], st["a_v_s"], st["b_s"], st["c_k_s"], st["c_v_s"])
```

```python
import functools

import jax
import jax.numpy as jnp
from jax import lax
from jax.experimental import pallas as pl
from jax.experimental.pallas import tpu as pltpu

F32 = jnp.float32
BF16 = jnp.bfloat16

HEAD_DIM = 128
CHUNK = 64
A_PAST_CHUNKS = 8
REL_CLIP = 128
POOL_WINDOWS = (2, 4, 8, 16)
POOL_HIST = max(POOL_WINDOWS) - 1
POOL_HIST_PAD = 16
B_GROUP_WIDTH = 512
PEER_HEADS = 8
N_KEYS = 128
PEER_TOPK = 16
EPS = 1e-6
NEG_INF = -1e30
ATTN_SCALE = HEAD_DIM ** -0.5
SB_EXIT = -104.0
SB_KEYS = 128
VMEM_LIMIT_BYTES = 56 * 1024 * 1024


def _params(*sem):
    return pltpu.CompilerParams(dimension_semantics=sem, vmem_limit_bytes=VMEM_LIMIT_BYTES)


def _tile(dim, pref):
    return pref if dim % pref == 0 else dim


def _dot(a, b):
    return jnp.dot(a, b, preferred_element_type=F32)


def _dot_nt(a, b):
    return lax.dot_general(a, b, (((1,), (1,)), ((), ())), preferred_element_type=F32)


def _rmsnorm_kernel(x_ref, g_ref, o_ref):
    x = x_ref[...]
    ms = jnp.mean(x * x, axis=-1, keepdims=True)
    o_ref[...] = (x * lax.rsqrt(ms + EPS) * g_ref[...]).astype(o_ref.dtype)


def _rmsnorm(x, g):
    t, d = x.shape
    tr = _tile(t, 256)
    return pl.pallas_call(
        _rmsnorm_kernel,
        grid=(t // tr,),
        in_specs=[pl.BlockSpec((tr, d), lambda i: (i, 0)),
                  pl.BlockSpec((1, d), lambda i: (0, 0))],
        out_specs=pl.BlockSpec((tr, d), lambda i: (i, 0)),
        out_shape=jax.ShapeDtypeStruct((t, d), BF16),
        compiler_params=_params("parallel"),
        name="rmsnorm",
    )(x, g.reshape(1, d))


def _mm_kernel(*refs, nk, mode):
    x_ref, w_ref = refs[0], refs[1]
    if mode in ("headnorm", "residual"):
        aux_ref, o_ref, scratch = refs[2], refs[3], refs[4:]
    else:
        aux_ref, o_ref, scratch = None, refs[2], refs[3:]

    def finish(acc):
        if mode == "headnorm":
            g = aux_ref[...]
            for c in range(acc.shape[1] // HEAD_DIM):
                sl = slice(c * HEAD_DIM, (c + 1) * HEAD_DIM)
                y = acc[:, sl]
                ms = jnp.mean(y * y, axis=-1, keepdims=True)
                o_ref[:, sl] = (y * lax.rsqrt(ms + EPS) * g).astype(o_ref.dtype)
        elif mode == "residual":
            o_ref[...] = aux_ref[...] + acc
        else:
            o_ref[...] = acc.astype(o_ref.dtype)

    if nk == 1:
        finish(_dot(x_ref[...], w_ref[...]))
    else:
        acc_ref = scratch[0]
        k = pl.program_id(2)

        @pl.when(k == 0)
        def _():
            acc_ref[...] = jnp.zeros_like(acc_ref)

        acc_ref[...] += _dot(x_ref[...], w_ref[...])

        @pl.when(k == nk - 1)
        def _():
            finish(acc_ref[...])


def _matmul(x, w, layer, *, col0=0, n=None, out_dtype=F32, mode="plain", aux=None,
            tm=1024, tn=512, tk=None, name="matmul"):
    m, kdim = x.shape
    n = w.shape[2] if n is None else n
    tm, tn = _tile(m, tm), _tile(n, tn)
    tk = kdim if tk is None else _tile(kdim, tk)
    nk = kdim // tk
    cb = col0 // tn
    assert col0 % tn == 0
    in_specs = [pl.BlockSpec((tm, tk), lambda i, j, k: (i, k)),
                pl.BlockSpec((None, tk, tn), lambda i, j, k: (layer, k, j + cb))]
    args = [x, w]
    if mode == "headnorm":
        in_specs.append(pl.BlockSpec((1, HEAD_DIM), lambda i, j, k: (0, 0)))
        args.append(aux.reshape(1, HEAD_DIM))
    elif mode == "residual":
        in_specs.append(pl.BlockSpec((tm, tn), lambda i, j, k: (i, j)))
        args.append(aux)
    scratch = [pltpu.VMEM((tm, tn), F32)] if nk > 1 else []
    return pl.pallas_call(
        functools.partial(_mm_kernel, nk=nk, mode=mode),
        grid=(m // tm, n // tn, nk),
        in_specs=in_specs,
        out_specs=pl.BlockSpec((tm, tn), lambda i, j, k: (i, j)),
        out_shape=jax.ShapeDtypeStruct((m, n), out_dtype),
        scratch_shapes=scratch,
        compiler_params=_params("parallel", "parallel", "arbitrary"),
        name=name,
    )(*args)


def _band_head(q, ka, kb, va, vb, bias_a, bias_b, past_visible):
    sa = _dot_nt(q, ka.astype(BF16)) * ATTN_SCALE + bias_a
    sb = _dot_nt(q, kb.astype(BF16)) * ATTN_SCALE + bias_b
    if past_visible is not None:
        sa = jnp.where(past_visible, sa, NEG_INF)
    m = jnp.maximum(jnp.max(sa, axis=-1, keepdims=True), jnp.max(sb, axis=-1, keepdims=True))
    pa = jnp.exp(sa - m)
    pb = jnp.exp(sb - m)
    l = jnp.sum(pa, axis=-1, keepdims=True) + jnp.sum(pb, axis=-1, keepdims=True)
    o = _dot(pa.astype(BF16), va.astype(BF16)) + _dot(pb.astype(BF16), vb.astype(BF16))
    return o / l


def _band_prompt_kernel(q_ref, ka_ref, kb_ref, va_ref, vb_ref, bias_ref, o_ref):
    na = ka_ref.shape[0]
    o = _band_head(q_ref[...], ka_ref[...], kb_ref[...], va_ref[...], vb_ref[...],
                   bias_ref[:, :na], bias_ref[:, na:], pl.program_id(1) > 0)
    o_ref[...] = o.astype(o_ref.dtype)


def _band_sample_kernel(q_ref, kn_ref, vn_ref, kc_ref, vc_ref, bias_ref, o_ref):
    win, heads = kc_ref.shape[0], kc_ref.shape[1]
    kc = pltpu.einshape("phd->hpd", kc_ref[...])
    vc = pltpu.einshape("phd->hpd", vc_ref[...])
    for hh in range(heads):
        sl = slice(hh * HEAD_DIM, (hh + 1) * HEAD_DIM)
        o = _band_head(q_ref[:, sl], kc[hh], kn_ref[:, sl], vc[hh], vn_ref[:, sl],
                       bias_ref[hh, :, :win], bias_ref[hh, :, win:], None)
        o_ref[:, sl] = o.astype(o_ref.dtype)


def _band_bias(rel_bias, q_pos, k_pos):
    lq, lk = q_pos.shape[0], k_pos.shape[0]
    d_min = q_pos[0] - k_pos[lk - 1]
    diag = jnp.clip(d_min + jnp.arange(lq + lk - 1, dtype=jnp.int32), -REL_CLIP, REL_CLIP) + REL_CLIP
    g = rel_bias.astype(F32)[:, diag]
    period = lq + lk
    u = jnp.pad(g[:, ::-1], ((0, 0), (0, 1)))
    skew = jnp.tile(u, (1, lq))[:, :lq * (period - 1)].reshape(-1, lq, period - 1)
    bias = skew[:, :, lq - 1:lq - 1 + lk]
    qc = q_pos[:, None] // CHUNK
    kc = k_pos[None, :] // CHUNK
    mask = (kc <= qc) & (kc >= qc - A_PAST_CHUNKS)
    return jnp.where(mask[None], bias, NEG_INF)


def _band_prompt(q, k, v, rel_bias, sp):
    heads = rel_bias.shape[0]
    tq = A_PAST_CHUNKS * CHUNK
    assert sp % tq == 0
    pos = jnp.arange(tq, dtype=jnp.int32)
    bias = _band_bias(rel_bias, tq + pos, jnp.arange(2 * tq, dtype=jnp.int32))
    prev = lambda h, i: (jnp.maximum(i - 1, 0), h)
    cur = lambda h, i: (i, h)
    blk = lambda im: pl.BlockSpec((tq, HEAD_DIM), im)
    return pl.pallas_call(
        _band_prompt_kernel,
        grid=(heads, sp // tq),
        in_specs=[blk(cur), blk(prev), blk(cur), blk(prev), blk(cur),
                  pl.BlockSpec((None, tq, 2 * tq), lambda h, i: (h, 0, 0))],
        out_specs=blk(cur),
        out_shape=jax.ShapeDtypeStruct((sp, heads * HEAD_DIM), BF16),
        compiler_params=_params("parallel", "arbitrary"),
        name="band_prompt",
    )(q, k, k, v, v, bias)


def _band_sample(q, k, v, cache_k, cache_v, layer, rel_bias, sp, ls, past_len):
    _, bs, win, heads, _ = cache_k.shape
    width = heads * HEAD_DIM
    assert sp % ls == 0
    row0 = sp // ls
    q_pos = past_len + jnp.arange(ls, dtype=jnp.int32)
    k_pos = past_len - win + jnp.arange(win + ls, dtype=jnp.int32)
    bias = _band_bias(rel_bias, q_pos, k_pos)
    blk_new = pl.BlockSpec((ls, width), lambda b: (row0 + b, 0))
    blk_old = pl.BlockSpec((None, None, win, heads, HEAD_DIM), lambda b: (layer, b, 0, 0, 0))
    return pl.pallas_call(
        _band_sample_kernel,
        grid=(bs,),
        in_specs=[blk_new, blk_new, blk_new, blk_old, blk_old,
                  pl.BlockSpec((heads, ls, win + ls), lambda b: (0, 0, 0))],
        out_specs=pl.BlockSpec((ls, width), lambda b: (b, 0)),
        out_shape=jax.ShapeDtypeStruct((bs * ls, width), BF16),
        compiler_params=_params("parallel"),
        name="band_sample",
    )(q, k, v, cache_k, cache_v, bias)


def _pool_kernel(cur_ref, hist_ref, w_ref, sc_ref, o_ref, ext_ref, *, pos_base, pos_step):
    tr = cur_ref.shape[0]
    ext_ref[0:POOL_HIST_PAD, :] = hist_ref[...]
    ext_ref[POOL_HIST_PAD:POOL_HIST_PAD + tr, :] = cur_ref[...]
    pos = pos_base + pl.program_id(0) * pos_step + lax.broadcasted_iota(jnp.int32, (tr, 1), 0)
    for g, w in enumerate(POOL_WINDOWS):
        sl = slice(g * B_GROUP_WIDTH, (g + 1) * B_GROUP_WIDTH)
        cur = cur_ref[:, sl]
        tot = cur
        for j in range(1, w):
            tot = tot + ext_ref[POOL_HIST_PAD - j:POOL_HIST_PAD - j + tr, sl]
        cnt = jnp.minimum(pos + 1, w).astype(F32)
        d = tot / cnt - cur
        y = _dot(d.astype(BF16), w_ref[g]) * sc_ref[:, sl]
        o_ref[:, sl] = y.astype(o_ref.dtype)


def _pool(p, hist, pool_w, pool_scale, *, row_block0, nblk, tr, pos_base, pos_step, name):
    width = p.shape[1]
    return pl.pallas_call(
        functools.partial(_pool_kernel, pos_base=pos_base, pos_step=pos_step),
        grid=(nblk,),
        in_specs=[pl.BlockSpec((tr, width), lambda i: (row_block0 + i, 0)),
                  pl.BlockSpec((None, POOL_HIST_PAD, width), lambda i: (i, 0, 0)),
                  pl.BlockSpec(pool_w.shape, lambda i: (0, 0, 0)),
                  pl.BlockSpec((1, width), lambda i: (0, 0))],
        out_specs=pl.BlockSpec((tr, width), lambda i: (i, 0)),
        out_shape=jax.ShapeDtypeStruct((nblk * tr, width), BF16),
        scratch_shapes=[pltpu.VMEM((POOL_HIST_PAD + tr, width), F32)],
        compiler_params=_params("parallel"),
        name=name,
    )(p, hist, pool_w, pool_scale.reshape(1, width))


def _sb_steps(qs, kvs, tri, carries, accs, visibles):
    nk = kvs[0][0].shape[0]
    zs = [_dot_nt(q, k) * ATTN_SCALE for q, (k, _) in zip(qs, kvs)]
    log_keeps = [jnp.where(vis, -(jnp.maximum(z, 0.0) + jnp.log1p(jnp.exp(-jnp.abs(z)))), 0.0)
                 for z, vis in zip(zs, visibles)]
    his = [lk.astype(BF16) for lk in log_keeps]
    los = [(lk - hi.astype(F32)).astype(BF16) for lk, hi in zip(log_keeps, his)]
    sums = [_dot(hi, tri) + _dot(lo, tri) for hi, lo in zip(his, los)]
    weights = [jnp.where(vis, jnp.exp(z + lk + s[:, :nk] + carry[:, :nk]), 0.0).astype(BF16)
               for z, lk, s, carry, vis in zip(zs, log_keeps, sums, carries, visibles)]
    new_accs = [acc + _dot(a, v) for acc, a, (_, v) in zip(accs, weights, kvs)]
    new_carries = [carry + s[:, nk:] for carry, s in zip(carries, sums)]
    return new_carries, new_accs


def _sb_sweep(chains, load_kv, tri, carries, accs):
    rows = chains[0][0].shape[0]
    lane = lax.broadcasted_iota(jnp.int32, (rows, SB_KEYS), 1)
    beyond = jnp.int32(2 ** 30)

    def cond(st):
        return st[1] == 0

    def body(st):
        n, _, carries, accs = st
        kbs = [kb0 - n for _, kb0, _ in chains]
        starts = [pl.multiple_of(jnp.maximum(kb, 0) * SB_KEYS, SB_KEYS) for kb in kbs]
        visibles = [(jnp.where(kb >= 0, start, beyond) + lane) < qpos
                    for (_, _, qpos), kb, start in zip(chains, kbs, starts)]
        carries, accs = _sb_steps([q for q, _, _ in chains], load_kv(starts), tri, carries, accs, visibles)
        pending = jnp.full((rows, HEAD_DIM), -jnp.inf, F32)
        for kb, carry in zip(kbs, carries):
            pending = jnp.maximum(pending, jnp.where(kb > 0, carry, -jnp.inf))
        done = (jnp.max(pending) < SB_EXIT).astype(jnp.int32)
        return n + 1, done, carries, accs

    return lax.while_loop(cond, body, (jnp.int32(0), jnp.int32(0), list(carries), list(accs)))[3]


def _sb_prompt_kernel(q_ref, k_ref, v_ref, tri_ref, o_ref, *, tq):
    n_chains = q_ref.shape[0] // tq
    row0 = pl.program_id(1) * q_ref.shape[0]
    row = lax.broadcasted_iota(jnp.int32, (tq, SB_KEYS), 0)

    def load_kv(starts):
        return [(k_ref[pl.ds(s, SB_KEYS), :].astype(BF16), v_ref[pl.ds(s, SB_KEYS), :].astype(BF16))
                for s in starts]

    chains = [(q_ref[c * tq:(c + 1) * tq, :], (row0 + (c + 1) * tq) // SB_KEYS - 1, row0 + c * tq + row)
              for c in range(n_chains)]
    zeros = [jnp.zeros((tq, HEAD_DIM), F32)] * n_chains
    accs = _sb_sweep(chains, load_kv, tri_ref[...], zeros, zeros)
    for c in range(n_chains):
        o_ref[c * tq:(c + 1) * tq, :] = accs[c].astype(o_ref.dtype)


def _sb_sample_kernel(q_ref, kn_ref, vn_ref, kc_ref, vc_ref, trin_ref, tri_ref, o_ref):
    ls = q_ref.shape[0]
    past, heads = kc_ref.shape[0], kc_ref.shape[1]
    causal = lax.broadcasted_iota(jnp.int32, (ls, ls), 1) < lax.broadcasted_iota(jnp.int32, (ls, ls), 0)
    all_rows = jnp.full((ls, SB_KEYS), 2 ** 30 - SB_KEYS, jnp.int32)
    zeros = jnp.zeros((ls, HEAD_DIM), F32)
    head_cols = [slice(hh * HEAD_DIM, (hh + 1) * HEAD_DIM) for hh in range(heads)]
    chains = [(q_ref[:, sl], past // SB_KEYS - 1, all_rows) for sl in head_cols]
    carries, accs = _sb_steps([q for q, _, _ in chains],
                              [(kn_ref[:, sl].astype(BF16), vn_ref[:, sl].astype(BF16)) for sl in head_cols],
                              trin_ref[...], [zeros] * heads, [zeros] * heads, [causal] * heads)

    def load_kv(starts):
        k = pltpu.einshape("phd->hpd", kc_ref[pl.ds(starts[0], SB_KEYS), :, :])
        v = pltpu.einshape("phd->hpd", vc_ref[pl.ds(starts[0], SB_KEYS), :, :])
        return [(k[hh].astype(BF16), v[hh].astype(BF16)) for hh in range(heads)]

    accs = _sb_sweep(chains, load_kv, tri_ref[...], carries, accs)
    for hh in range(heads):
        o_ref[:, hh * HEAD_DIM:(hh + 1) * HEAD_DIM] = accs[hh].astype(o_ref.dtype)


def _sb_tri(nk):
    j = jnp.arange(nk)[:, None]
    s = jnp.arange(nk + HEAD_DIM)[None, :]
    return ((s >= nk) | (j > s)).astype(BF16)


def _sb_prompt(q, k, v, sp):
    heads = q.shape[1] // HEAD_DIM
    tq = 2 * SB_KEYS
    tb = _tile(sp, 4 * tq)
    assert sp % SB_KEYS == 0 and tb % tq == 0
    return pl.pallas_call(
        functools.partial(_sb_prompt_kernel, tq=tq),
        grid=(heads, sp // tb),
        in_specs=[pl.BlockSpec((tb, HEAD_DIM), lambda h, i: (i, h)),
                  pl.BlockSpec((sp, HEAD_DIM), lambda h, i: (0, h)),
                  pl.BlockSpec((sp, HEAD_DIM), lambda h, i: (0, h)),
                  pl.BlockSpec((SB_KEYS, SB_KEYS + HEAD_DIM), lambda h, i: (0, 0))],
        out_specs=pl.BlockSpec((tb, HEAD_DIM), lambda h, i: (i, h)),
        out_shape=jax.ShapeDtypeStruct((sp, heads * HEAD_DIM), BF16),
        compiler_params=_params("parallel", "arbitrary"),
        name="stickbreak_prompt",
    )(q, k, v, _sb_tri(SB_KEYS))


def _sb_sample(q, k, v, cache_k, cache_v, layer, sp, ls):
    n_layers, bs, past, heads, _ = cache_k.shape
    hg = 8
    assert past % SB_KEYS == 0 and past >= SB_KEYS and sp % ls == 0 and heads % hg == 0
    row0 = sp // ls
    gw = hg * HEAD_DIM
    blk_new = pl.BlockSpec((ls, gw), lambda b, g: (row0 + b, g))
    blk_old = pl.BlockSpec((None, None, past, None, hg, HEAD_DIM), lambda b, g: (layer, b, 0, g, 0, 0))
    grouped = (n_layers, bs, past, heads // hg, hg, HEAD_DIM)
    return pl.pallas_call(
        _sb_sample_kernel,
        grid=(bs, heads // hg),
        in_specs=[blk_new, blk_new, blk_new, blk_old, blk_old,
                  pl.BlockSpec((ls, ls + HEAD_DIM), lambda b, g: (0, 0)),
                  pl.BlockSpec((SB_KEYS, SB_KEYS + HEAD_DIM), lambda b, g: (0, 0))],
        out_specs=pl.BlockSpec((ls, gw), lambda b, g: (b, g)),
        out_shape=jax.ShapeDtypeStruct((bs * ls, heads * HEAD_DIM), BF16),
        compiler_params=_params("parallel", "arbitrary"),
        name="stickbreak_sample",
    )(q, k, v, cache_k.reshape(grouped), cache_v.reshape(grouped), _sb_tri(ls), _sb_tri(SB_KEYS))


def _peer_scores_kernel(key_ref, q_ref, o_ref):
    o_ref[...] = _dot_nt(key_ref[...], q_ref[...])


def _peer_scores(pq, sub_keys, layer):
    t = pq.shape[0]
    _, nhc, nkeys, half = sub_keys.shape
    tt = _tile(t, 1024)
    return pl.pallas_call(
        _peer_scores_kernel,
        grid=(nhc, t // tt),
        in_specs=[pl.BlockSpec((None, None, nkeys, half), lambda c, i: (layer, c, 0, 0)),
                  pl.BlockSpec((tt, half), lambda c, i: (i, c))],
        out_specs=pl.BlockSpec((None, nkeys, tt), lambda c, i: (c, 0, i)),
        out_shape=jax.ShapeDtypeStruct((nhc, nkeys, t), F32),
        compiler_params=_params("parallel", "parallel"),
        name="peer_scores",
    )(sub_keys, pq)


def _top_values(s, k):
    rows = lax.broadcasted_iota(jnp.int32, s.shape, 0)
    vals = []
    for _ in range(k):
        m = jnp.max(s, axis=0, keepdims=True)
        vals.append(m)
        first = jnp.min(jnp.where(s == m, rows, s.shape[0]), axis=0, keepdims=True)
        s = jnp.where(rows == first, -jnp.inf, s)
    return vals


def _peer_select_kernel(s_ref, s1x_ref, e1x_ref, e2_ref, thr_ref, *, ne):
    def one_head(h, _):
        s1 = s_ref[2 * h]
        s2 = s_ref[2 * h + 1]
        a = _top_values(s1, PEER_TOPK)
        b = jnp.concatenate(_top_values(s2, PEER_TOPK), axis=0)
        sub = lax.broadcasted_iota(jnp.int32, (8, b.shape[1]), 0)
        cand = [a[0] + b, a[1] + b[:8]]
        for i in range(2, 8):
            cand.append(jnp.where(sub < PEER_TOPK // (i + 1), a[i] + b[:8], -jnp.inf))
        cand.append(jnp.concatenate(a[8:], axis=0) + b[0:1])
        g = _top_values(jnp.concatenate(cand, axis=0), PEER_TOPK)
        z = jnp.ones_like(g[0])
        for gk in g[1:]:
            z = z + jnp.exp(gk - g[0])
        e1 = jnp.exp(s1 - a[0]) / z
        e2_ref[h] = jnp.exp(s2 - b[0:1])
        thr_ref[pl.ds(h, 1), :] = g[PEER_TOPK - 1]
        row = pl.multiple_of(h * ne, ne)
        for j in range(N_KEYS // ne):
            s1x_ref[j, pl.ds(row, ne), :] = s1[j * ne:(j + 1) * ne]
            e1x_ref[j, pl.ds(row, ne), :] = e1[j * ne:(j + 1) * ne]
        return 0

    lax.fori_loop(0, PEER_HEADS, one_head, 0)


def _peer_select(scores, ne):
    nhc, nkeys, t = scores.shape
    tt = _tile(t, 256)
    x_shape = jax.ShapeDtypeStruct((nkeys // ne, PEER_HEADS * ne, t), F32)
    x_spec = pl.BlockSpec((nkeys // ne, PEER_HEADS * ne, tt), lambda i: (0, 0, i))
    return pl.pallas_call(
        functools.partial(_peer_select_kernel, ne=ne),
        grid=(t // tt,),
        in_specs=[pl.BlockSpec((nhc, nkeys, tt), lambda i: (0, 0, i))],
        out_specs=[x_spec, x_spec,
                   pl.BlockSpec((PEER_HEADS, nkeys, tt), lambda i: (0, 0, i)),
                   pl.BlockSpec((PEER_HEADS, tt), lambda i: (0, i))],
        out_shape=[x_shape, x_shape,
                   jax.ShapeDtypeStruct((PEER_HEADS, nkeys, t), F32),
                   jax.ShapeDtypeStruct((PEER_HEADS, t), F32)],
        compiler_params=_params("parallel"),
        name="peer_select",
    )(scores)


def _peer_act_kernel(u_ref, x_ref, s1x_ref, e1x_ref, s2_ref, e2_ref, thr_ref, o_ref, *, ne):
    x = x_ref[...]
    for p in range(ne // 2):
        hp = _dot_nt(x, u_ref[2 * p * N_KEYS:2 * (p + 1) * N_KEYS, :])
        for half in range(2):
            e = 2 * p + half
            cols = slice(e * N_KEYS, (e + 1) * N_KEYS)
            for c in range(x.shape[0] // HEAD_DIM):
                tok = slice(c * HEAD_DIM, (c + 1) * HEAD_DIM)
                gate = jnp.zeros((N_KEYS, HEAD_DIM), F32)
                for h in range(PEER_HEADS):
                    r = h * ne + e
                    pair = s1x_ref[r:r + 1, tok] + s2_ref[h, :, tok]
                    w = e1x_ref[r:r + 1, tok] * e2_ref[h, :, tok]
                    gate = gate + jnp.where(pair >= thr_ref[h:h + 1, tok], w, 0.0)
                hh = hp[tok, half * N_KEYS:(half + 1) * N_KEYS]
                gelu = 0.5 * hh * (1.0 + lax.erf(hh * (0.5 ** 0.5)))
                o_ref[tok, cols] = (gate.T * gelu).astype(o_ref.dtype)


def _peer_act(xn, u, layer, scores, s1x, e1x, e2, thr, ne):
    t, d = xn.shape
    n_exp = u.shape[1]
    tt = _tile(t, 512)
    te = ne * N_KEYS
    s4 = scores.reshape(PEER_HEADS, 2, N_KEYS, t)
    return pl.pallas_call(
        functools.partial(_peer_act_kernel, ne=ne),
        grid=(t // tt, n_exp // te),
        in_specs=[pl.BlockSpec((None, te, d), lambda i, j: (layer, j, 0)),
                  pl.BlockSpec((tt, d), lambda i, j: (i, 0)),
                  pl.BlockSpec((None, PEER_HEADS * ne, tt), lambda i, j: (j, 0, i)),
                  pl.BlockSpec((None, PEER_HEADS * ne, tt), lambda i, j: (j, 0, i)),
                  pl.BlockSpec((PEER_HEADS, None, N_KEYS, tt), lambda i, j: (0, 1, 0, i)),
                  pl.BlockSpec((PEER_HEADS, N_KEYS, tt), lambda i, j: (0, 0, i)),
                  pl.BlockSpec((PEER_HEADS, tt), lambda i, j: (0, i))],
        out_specs=pl.BlockSpec((tt, te), lambda i, j: (i, j)),
        out_shape=jax.ShapeDtypeStruct((t, n_exp), BF16),
        compiler_params=_params("parallel", "arbitrary"),
        name="peer_act",
    )(u, xn, s1x, e1x, s4, e2, thr)


def _peer_ffn(h, layer, norm_g, w_q, sub_keys, u, v):
    ne = 8
    xn = _rmsnorm(h, norm_g)
    pq = _matmul(xn, w_q, layer, out_dtype=BF16, name="peer_query")
    scores = _peer_scores(pq, sub_keys, layer)
    s1x, e1x, e2, thr = _peer_select(scores, ne)
    act = _peer_act(xn, u, layer, scores, s1x, e1x, e2, thr, ne)
    return _matmul(act, v, layer, mode="residual", aux=h, tm=1024, tn=1024, tk=2048, name="peer_down")


def kernel(x_prompt, x_sample, cache_a_k, cache_a_v, state_b_pool, cache_c_k, cache_c_v, norm_mix, norm_ffn, ab_w_in, ab_q_gain, ab_k_gain, ab_rel_bias, ab_pool_w, ab_pool_scale, ab_w_out, c_w_in, c_w_out, peer_w_q, peer_sub_keys, peer_u, peer_v):
    bp, sp, d = x_prompt.shape
    bs, ls, _ = x_sample.shape
    assert bp == 1
    depth = norm_mix.shape[0]
    past_len = cache_c_k.shape[2]
    a_heads = ab_rel_bias.shape[1]
    a_width = a_heads * HEAD_DIM
    b_width = ab_pool_scale.shape[1]
    c_width = c_w_out.shape[1]
    win_p = min(A_PAST_CHUNKS * CHUNK, sp)
    n_s = bs * ls

    h = jnp.concatenate([x_prompt.reshape(sp, d), x_sample.reshape(n_s, d)], axis=0)

    ab_w_in_b, ab_w_out_b = ab_w_in.astype(BF16), ab_w_out.astype(BF16)
    c_w_in_b, c_w_out_b = c_w_in.astype(BF16), c_w_out.astype(BF16)
    peer_w_q_b, peer_u_b, peer_v_b = peer_w_q.astype(BF16), peer_u.astype(BF16), peer_v.astype(BF16)
    sub_keys_b = peer_sub_keys.astype(BF16).reshape(depth, PEER_HEADS * 2, N_KEYS, -1)

    outs = {name: [] for name in ("a_k_p", "a_v_p", "b_p", "c_k_p", "c_v_p",
                                  "a_k_s", "a_v_s", "b_s", "c_k_s", "c_v_s")}
    for layer in range(depth):
        xn = _rmsnorm(h, norm_mix[layer])
        if layer % 2 == 0:
            i = layer // 2
            q = _matmul(xn, ab_w_in_b, i, col0=0, n=a_width, out_dtype=BF16, mode="headnorm",
                        aux=ab_q_gain[i], name="a_q")
            k = _matmul(xn, ab_w_in_b, i, col0=a_width, n=a_width, mode="headnorm", aux=ab_k_gain[i], name="a_k")
            v = _matmul(xn, ab_w_in_b, i, col0=2 * a_width, n=a_width, name="a_v")
            p = _matmul(xn, ab_w_in_b, i, col0=3 * a_width, n=b_width, name="b_in")

            att_p = _band_prompt(q, k, v, ab_rel_bias[i], sp)
            att_s = _band_sample(q, k, v, cache_a_k, cache_a_v, i, ab_rel_bias[i], sp, ls, past_len)

            pool_w = ab_pool_w[i].astype(BF16)
            tr = _tile(sp, 256)
            pp = p[:sp].reshape(sp // tr, tr, b_width)
            hist_p = jnp.concatenate([jnp.zeros((1, POOL_HIST_PAD, b_width), F32),
                                      pp[:-1, tr - POOL_HIST_PAD:]], axis=0)
            pool_p = _pool(p, hist_p, pool_w, ab_pool_scale[i], row_block0=0, nblk=sp // tr, tr=tr,
                           pos_base=0, pos_step=tr, name="pool_prompt")
            hist_s = jnp.pad(state_b_pool[i], ((0, 0), (POOL_HIST_PAD - POOL_HIST, 0), (0, 0)))
            pool_s = _pool(p, hist_s, pool_w, ab_pool_scale[i], row_block0=sp // ls, nblk=bs, tr=ls,
                           pos_base=past_len, pos_step=0, name="pool_sample")

            mix_in = jnp.concatenate([jnp.concatenate([att_p, pool_p], axis=1),
                                      jnp.concatenate([att_s, pool_s], axis=1)], axis=0)
            h = _matmul(mix_in, ab_w_out_b, i, mode="residual", aux=h, name="a_out")

            ps = p[sp:].reshape(bs, ls, b_width)
            outs["a_k_p"].append(k[sp - win_p:sp].reshape(bp, win_p, a_heads, HEAD_DIM))
            outs["a_v_p"].append(v[sp - win_p:sp].reshape(bp, win_p, a_heads, HEAD_DIM))
            outs["b_p"].append(p[sp - POOL_HIST:sp].reshape(bp, POOL_HIST, b_width))
            outs["a_k_s"].append(k[sp:].reshape(bs, ls, a_heads, HEAD_DIM))
            outs["a_v_s"].append(v[sp:].reshape(bs, ls, a_heads, HEAD_DIM))
            outs["b_s"].append(jnp.concatenate([state_b_pool[i], ps], axis=1)[:, ls:])
        else:
            j = layer // 2
            c_heads = c_width // HEAD_DIM
            q = _matmul(xn, c_w_in_b, j, col0=0, n=c_width, out_dtype=BF16, name="c_q")
            k = _matmul(xn, c_w_in_b, j, col0=c_width, n=c_width, name="c_k")
            v = _matmul(xn, c_w_in_b, j, col0=2 * c_width, n=c_width, name="c_v")
            att_p = _sb_prompt(q, k, v, sp)
            att_s = _sb_sample(q, k, v, cache_c_k, cache_c_v, j, sp, ls)
            mix_in = jnp.concatenate([att_p, att_s], axis=0)
            h = _matmul(mix_in, c_w_out_b, j, mode="residual", aux=h, name="c_out")
            outs["c_k_p"].append(k[:sp].reshape(bp, sp, c_heads, HEAD_DIM))
            outs["c_v_p"].append(v[:sp].reshape(bp, sp, c_heads, HEAD_DIM))
            outs["c_k_s"].append(k[sp:].reshape(bs, ls, c_heads, HEAD_DIM))
            outs["c_v_s"].append(v[sp:].reshape(bs, ls, c_heads, HEAD_DIM))

        h = _peer_ffn(h, layer, norm_ffn[layer], peer_w_q_b, sub_keys_b, peer_u_b, peer_v_b)

    st = {name: jnp.stack(vals) for name, vals in outs.items()}
    return (h[:sp].reshape(bp, sp, d), h[sp:].reshape(bs, ls, d),
            st["a_k_p"], st["a_v_p"], st["b_p"], st["c_k_p"], st["c_v_p"],
            st["a_k_s"], st["a_v_s"], st["b_s"], st["c_k_s"], st["c_v_s"])
```

```python
import functools

import jax
import jax.numpy as jnp
from jax import lax
from jax.experimental import pallas as pl
from jax.experimental.pallas import tpu as pltpu

F32 = jnp.float32
BF16 = jnp.bfloat16

HEAD_DIM = 128
CHUNK = 64
A_PAST_CHUNKS = 8
REL_CLIP = 128
POOL_WINDOWS = (2, 4, 8, 16)
POOL_HIST = max(POOL_WINDOWS) - 1
POOL_HIST_PAD = 16
B_GROUP_WIDTH = 512
PEER_HEADS = 8
N_KEYS = 128
PEER_TOPK = 16
EPS = 1e-6
NEG_INF = -1e30
ATTN_SCALE = HEAD_DIM ** -0.5
SB_EXIT = -104.0
SB_KEYS = 128
VMEM_LIMIT_BYTES = 56 * 1024 * 1024


def _params(*sem):
    return pltpu.CompilerParams(dimension_semantics=sem, vmem_limit_bytes=VMEM_LIMIT_BYTES)


def _tile(dim, pref):
    return pref if dim % pref == 0 else dim


def _dot(a, b):
    return jnp.dot(a, b, preferred_element_type=F32)


def _dot_nt(a, b):
    return lax.dot_general(a, b, (((1,), (1,)), ((), ())), preferred_element_type=F32)


def _rmsnorm_kernel(x_ref, g_ref, o_ref):
    x = x_ref[...]
    ms = jnp.mean(x * x, axis=-1, keepdims=True)
    o_ref[...] = (x * lax.rsqrt(ms + EPS) * g_ref[...]).astype(o_ref.dtype)


def _rmsnorm(x, g):
    t, d = x.shape
    tr = _tile(t, 256)
    return pl.pallas_call(
        _rmsnorm_kernel,
        grid=(t // tr,),
        in_specs=[pl.BlockSpec((tr, d), lambda i: (i, 0)),
                  pl.BlockSpec((1, d), lambda i: (0, 0))],
        out_specs=pl.BlockSpec((tr, d), lambda i: (i, 0)),
        out_shape=jax.ShapeDtypeStruct((t, d), BF16),
        compiler_params=_params("parallel"),
        name="rmsnorm",
    )(x, g.reshape(1, d))


def _mm_kernel(*refs, nk, mode):
    x_ref, w_ref = refs[0], refs[1]
    if mode in ("headnorm", "residual"):
        aux_ref, o_ref, scratch = refs[2], refs[3], refs[4:]
    else:
        aux_ref, o_ref, scratch = None, refs[2], refs[3:]

    def finish(acc):
        if mode == "headnorm":
            g = aux_ref[...]
            for c in range(acc.shape[1] // HEAD_DIM):
                sl = slice(c * HEAD_DIM, (c + 1) * HEAD_DIM)
                y = acc[:, sl]
                ms = jnp.mean(y * y, axis=-1, keepdims=True)
                o_ref[:, sl] = (y * lax.rsqrt(ms + EPS) * g).astype(o_ref.dtype)
        elif mode == "residual":
            o_ref[...] = aux_ref[...] + acc
        else:
            o_ref[...] = acc.astype(o_ref.dtype)

    if nk == 1:
        finish(_dot(x_ref[...], w_ref[...]))
    else:
        acc_ref = scratch[0]
        k = pl.program_id(2)

        @pl.when(k == 0)
        def _():
            acc_ref[...] = jnp.zeros_like(acc_ref)

        acc_ref[...] += _dot(x_ref[...], w_ref[...])

        @pl.when(k == nk - 1)
        def _():
            finish(acc_ref[...])


def _matmul(x, w, layer, *, col0=0, n=None, out_dtype=F32, mode="plain", aux=None,
            tm=1024, tn=512, tk=None, name="matmul"):
    m, kdim = x.shape
    n = w.shape[2] if n is None else n
    tm, tn = _tile(m, tm), _tile(n, tn)
    tk = kdim if tk is None else _tile(kdim, tk)
    nk = kdim // tk
    cb = col0 // tn
    assert col0 % tn == 0
    in_specs = [pl.BlockSpec((tm, tk), lambda i, j, k: (i, k)),
                pl.BlockSpec((None, tk, tn), lambda i, j, k: (layer, k, j + cb))]
    args = [x, w]
    if mode == "headnorm":
        in_specs.append(pl.BlockSpec((1, HEAD_DIM), lambda i, j, k: (0, 0)))
        args.append(aux.reshape(1, HEAD_DIM))
    elif mode == "residual":
        in_specs.append(pl.BlockSpec((tm, tn), lambda i, j, k: (i, j)))
        args.append(aux)
    scratch = [pltpu.VMEM((tm, tn), F32)] if nk > 1 else []
    return pl.pallas_call(
        functools.partial(_mm_kernel, nk=nk, mode=mode),
        grid=(m // tm, n // tn, nk),
        in_specs=in_specs,
        out_specs=pl.BlockSpec((tm, tn), lambda i, j, k: (i, j)),
        out_shape=jax.ShapeDtypeStruct((m, n), out_dtype),
        scratch_shapes=scratch,
        compiler_params=_params("parallel", "parallel", "arbitrary"),
        name=name,
    )(*args)


def _band_head(q, ka, kb, va, vb, bias_a, bias_b, past_visible):
    sa = _dot_nt(q, ka.astype(BF16)) * ATTN_SCALE + bias_a
    sb = _dot_nt(q, kb.astype(BF16)) * ATTN_SCALE + bias_b
    if past_visible is not None:
        sa = jnp.where(past_visible, sa, NEG_INF)
    m = jnp.maximum(jnp.max(sa, axis=-1, keepdims=True), jnp.max(sb, axis=-1, keepdims=True))
    pa = jnp.exp(sa - m)
    pb = jnp.exp(sb - m)
    l = jnp.sum(pa, axis=-1, keepdims=True) + jnp.sum(pb, axis=-1, keepdims=True)
    o = _dot(pa.astype(BF16), va.astype(BF16)) + _dot(pb.astype(BF16), vb.astype(BF16))
    return o / l


def _band_prompt_kernel(q_ref, ka_ref, kb_ref, va_ref, vb_ref, bias_ref, o_ref):
    na = ka_ref.shape[0]
    o = _band_head(q_ref[...], ka_ref[...], kb_ref[...], va_ref[...], vb_ref[...],
                   bias_ref[:, :na], bias_ref[:, na:], pl.program_id(1) > 0)
    o_ref[...] = o.astype(o_ref.dtype)


def _band_sample_kernel(q_ref, kn_ref, vn_ref, kc_ref, vc_ref, bias_ref, o_ref):
    win, heads = kc_ref.shape[0], kc_ref.shape[1]
    kc = pltpu.einshape("phd->hpd", kc_ref[...])
    vc = pltpu.einshape("phd->hpd", vc_ref[...])
    for hh in range(heads):
        sl = slice(hh * HEAD_DIM, (hh + 1) * HEAD_DIM)
        o = _band_head(q_ref[:, sl], kc[hh], kn_ref[:, sl], vc[hh], vn_ref[:, sl],
                       bias_ref[hh, :, :win], bias_ref[hh, :, win:], None)
        o_ref[:, sl] = o.astype(o_ref.dtype)


def _band_bias(rel_bias, q_pos, k_pos):
    lq, lk = q_pos.shape[0], k_pos.shape[0]
    d_min = q_pos[0] - k_pos[lk - 1]
    diag = jnp.clip(d_min + jnp.arange(lq + lk - 1, dtype=jnp.int32), -REL_CLIP, REL_CLIP) + REL_CLIP
    g = rel_bias.astype(F32)[:, diag]
    period = lq + lk
    u = jnp.pad(g[:, ::-1], ((0, 0), (0, 1)))
    skew = jnp.tile(u, (1, lq))[:, :lq * (period - 1)].reshape(-1, lq, period - 1)
    bias = skew[:, :, lq - 1:lq - 1 + lk]
    qc = q_pos[:, None] // CHUNK
    kc = k_pos[None, :] // CHUNK
    mask = (kc <= qc) & (kc >= qc - A_PAST_CHUNKS)
    return jnp.where(mask[None], bias, NEG_INF)


def _band_prompt(q, k, v, rel_bias, sp):
    heads = rel_bias.shape[0]
    tq = A_PAST_CHUNKS * CHUNK
    assert sp % tq == 0
    pos = jnp.arange(tq, dtype=jnp.int32)
    bias = _band_bias(rel_bias, tq + pos, jnp.arange(2 * tq, dtype=jnp.int32))
    prev = lambda h, i: (jnp.maximum(i - 1, 0), h)
    cur = lambda h, i: (i, h)
    blk = lambda im: pl.BlockSpec((tq, HEAD_DIM), im)
    return pl.pallas_call(
        _band_prompt_kernel,
        grid=(heads, sp // tq),
        in_specs=[blk(cur), blk(prev), blk(cur), blk(prev), blk(cur),
                  pl.BlockSpec((None, tq, 2 * tq), lambda h, i: (h, 0, 0))],
        out_specs=blk(cur),
        out_shape=jax.ShapeDtypeStruct((sp, heads * HEAD_DIM), BF16),
        compiler_params=_params("parallel", "arbitrary"),
        name="band_prompt",
    )(q, k, k, v, v, bias)


def _band_sample(q, k, v, cache_k, cache_v, layer, rel_bias, sp, ls, past_len):
    _, bs, win, heads, _ = cache_k.shape
    width = heads * HEAD_DIM
    assert sp % ls == 0
    row0 = sp // ls
    q_pos = past_len + jnp.arange(ls, dtype=jnp.int32)
    k_pos = past_len - win + jnp.arange(win + ls, dtype=jnp.int32)
    bias = _band_bias(rel_bias, q_pos, k_pos)
    blk_new = pl.BlockSpec((ls, width), lambda b: (row0 + b, 0))
    blk_old = pl.BlockSpec((None, None, win, heads, HEAD_DIM), lambda b: (layer, b, 0, 0, 0))
    return pl.pallas_call(
        _band_sample_kernel,
        grid=(bs,),
        in_specs=[blk_new, blk_new, blk_new, blk_old, blk_old,
                  pl.BlockSpec((heads, ls, win + ls), lambda b: (0, 0, 0))],
        out_specs=pl.BlockSpec((ls, width), lambda b: (b, 0)),
        out_shape=jax.ShapeDtypeStruct((bs * ls, width), BF16),
        compiler_params=_params("parallel"),
        name="band_sample",
    )(q, k, v, cache_k, cache_v, bias)


def _pool_kernel(cur_ref, hist_ref, w_ref, sc_ref, o_ref, ext_ref, *, pos_base, pos_step):
    tr = cur_ref.shape[0]
    ext_ref[0:POOL_HIST_PAD, :] = hist_ref[...]
    ext_ref[POOL_HIST_PAD:POOL_HIST_PAD + tr, :] = cur_ref[...]
    pos = pos_base + pl.program_id(0) * pos_step + lax.broadcasted_iota(jnp.int32, (tr, 1), 0)
    for g, w in enumerate(POOL_WINDOWS):
        sl = slice(g * B_GROUP_WIDTH, (g + 1) * B_GROUP_WIDTH)
        cur = cur_ref[:, sl]
        tot = cur
        for j in range(1, w):
            tot = tot + ext_ref[POOL_HIST_PAD - j:POOL_HIST_PAD - j + tr, sl]
        cnt = jnp.minimum(pos + 1, w).astype(F32)
        d = tot / cnt - cur
        y = _dot(d.astype(BF16), w_ref[g]) * sc_ref[:, sl]
        o_ref[:, sl] = y.astype(o_ref.dtype)


def _pool(p, hist, pool_w, pool_scale, *, row_block0, nblk, tr, pos_base, pos_step, name):
    width = p.shape[1]
    return pl.pallas_call(
        functools.partial(_pool_kernel, pos_base=pos_base, pos_step=pos_step),
        grid=(nblk,),
        in_specs=[pl.BlockSpec((tr, width), lambda i: (row_block0 + i, 0)),
                  pl.BlockSpec((None, POOL_HIST_PAD, width), lambda i: (i, 0, 0)),
                  pl.BlockSpec(pool_w.shape, lambda i: (0, 0, 0)),
                  pl.BlockSpec((1, width), lambda i: (0, 0))],
        out_specs=pl.BlockSpec((tr, width), lambda i: (i, 0)),
        out_shape=jax.ShapeDtypeStruct((nblk * tr, width), BF16),
        scratch_shapes=[pltpu.VMEM((POOL_HIST_PAD + tr, width), F32)],
        compiler_params=_params("parallel"),
        name=name,
    )(p, hist, pool_w, pool_scale.reshape(1, width))


def _sb_steps(qs, kvs, tri, carries, accs, visibles):
    nk = kvs[0][0].shape[0]
    zs = [_dot_nt(q, k) * ATTN_SCALE for q, (k, _) in zip(qs, kvs)]
    log_keeps = [jnp.where(vis, -(jnp.maximum(z, 0.0) + jnp.log1p(jnp.exp(-jnp.abs(z)))), 0.0)
                 for z, vis in zip(zs, visibles)]
    his = [lk.astype(BF16) for lk in log_keeps]
    los = [(lk - hi.astype(F32)).astype(BF16) for lk, hi in zip(log_keeps, his)]
    sums = [_dot(hi, tri) + _dot(lo, tri) for hi, lo in zip(his, los)]
    weights = [jnp.where(vis, jnp.exp(z + lk + s[:, :nk] + carry[:, :nk]), 0.0).astype(BF16)
               for z, lk, s, carry, vis in zip(zs, log_keeps, sums, carries, visibles)]
    new_accs = [acc + _dot(a, v) for acc, a, (_, v) in zip(accs, weights, kvs)]
    new_carries = [carry + s[:, nk:] for carry, s in zip(carries, sums)]
    return new_carries, new_accs


def _sb_sweep(chains, load_kv, tri, carries, accs):
    rows = chains[0][0].shape[0]
    lane = lax.broadcasted_iota(jnp.int32, (rows, SB_KEYS), 1)
    beyond = jnp.int32(2 ** 30)

    def cond(st):
        return st[1] == 0

    def body(st):
        n, _, carries, accs = st
        kbs = [kb0 - n for _, kb0, _ in chains]
        starts = [pl.multiple_of(jnp.maximum(kb, 0) * SB_KEYS, SB_KEYS) for kb in kbs]
        visibles = [(jnp.where(kb >= 0, start, beyond) + lane) < qpos
                    for (_, _, qpos), kb, start in zip(chains, kbs, starts)]
        carries, accs = _sb_steps([q for q, _, _ in chains], load_kv(starts), tri, carries, accs, visibles)
        pending = jnp.full((rows, HEAD_DIM), -jnp.inf, F32)
        for kb, carry in zip(kbs, carries):
            pending = jnp.maximum(pending, jnp.where(kb > 0, carry, -jnp.inf))
        done = (jnp.max(pending) < SB_EXIT).astype(jnp.int32)
        return n + 1, done, carries, accs

    return lax.while_loop(cond, body, (jnp.int32(0), jnp.int32(0), list(carries), list(accs)))[3]


def _sb_prompt_kernel(q_ref, k_ref, v_ref, tri_ref, o_ref, *, tq):
    n_chains = q_ref.shape[0] // tq
    row0 = pl.program_id(1) * q_ref.shape[0]
    row = lax.broadcasted_iota(jnp.int32, (tq, SB_KEYS), 0)

    def load_kv(starts):
        return [(k_ref[pl.ds(s, SB_KEYS), :].astype(BF16), v_ref[pl.ds(s, SB_KEYS), :].astype(BF16))
                for s in starts]

    chains = [(q_ref[c * tq:(c + 1) * tq, :], (row0 + (c + 1) * tq) // SB_KEYS - 1, row0 + c * tq + row)
              for c in range(n_chains)]
    zeros = [jnp.zeros((tq, HEAD_DIM), F32)] * n_chains
    accs = _sb_sweep(chains, load_kv, tri_ref[...], zeros, zeros)
    for c in range(n_chains):
        o_ref[c * tq:(c + 1) * tq, :] = accs[c].astype(o_ref.dtype)


def _sb_sample_kernel(q_ref, kn_ref, vn_ref, kc_ref, vc_ref, trin_ref, tri_ref, o_ref):
    ls = q_ref.shape[0]
    past, heads = kc_ref.shape[0], kc_ref.shape[1]
    causal = lax.broadcasted_iota(jnp.int32, (ls, ls), 1) < lax.broadcasted_iota(jnp.int32, (ls, ls), 0)
    all_rows = jnp.full((ls, SB_KEYS), 2 ** 30 - SB_KEYS, jnp.int32)
    zeros = jnp.zeros((ls, HEAD_DIM), F32)
    head_cols = [slice(hh * HEAD_DIM, (hh + 1) * HEAD_DIM) for hh in range(heads)]
    chains = [(q_ref[:, sl], past // SB_KEYS - 1, all_rows) for sl in head_cols]
    carries, accs = _sb_steps([q for q, _, _ in chains],
                              [(kn_ref[:, sl].astype(BF16), vn_ref[:, sl].astype(BF16)) for sl in head_cols],
                              trin_ref[...], [zeros] * heads, [zeros] * heads, [causal] * heads)

    def load_kv(starts):
        k = pltpu.einshape("phd->hpd", kc_ref[pl.ds(starts[0], SB_KEYS), :, :])
        v = pltpu.einshape("phd->hpd", vc_ref[pl.ds(starts[0], SB_KEYS), :, :])
        return [(k[hh].astype(BF16), v[hh].astype(BF16)) for hh in range(heads)]

    accs = _sb_sweep(chains, load_kv, tri_ref[...], carries, accs)
    for hh in range(heads):
        o_ref[:, hh * HEAD_DIM:(hh + 1) * HEAD_DIM] = accs[hh].astype(o_ref.dtype)


def _sb_tri(nk):
    j = jnp.arange(nk)[:, None]
    s = jnp.arange(nk + HEAD_DIM)[None, :]
    return ((s >= nk) | (j > s)).astype(BF16)


def _sb_prompt(q, k, v, sp):
    heads = q.shape[1] // HEAD_DIM
    tq = SB_KEYS
    tb = _tile(sp, 8 * tq)
    assert sp % SB_KEYS == 0 and tb % tq == 0
    return pl.pallas_call(
        functools.partial(_sb_prompt_kernel, tq=tq),
        grid=(heads, sp // tb),
        in_specs=[pl.BlockSpec((tb, HEAD_DIM), lambda h, i: (i, h)),
                  pl.BlockSpec((sp, HEAD_DIM), lambda h, i: (0, h)),
                  pl.BlockSpec((sp, HEAD_DIM), lambda h, i: (0, h)),
                  pl.BlockSpec((SB_KEYS, SB_KEYS + HEAD_DIM), lambda h, i: (0, 0))],
        out_specs=pl.BlockSpec((tb, HEAD_DIM), lambda h, i: (i, h)),
        out_shape=jax.ShapeDtypeStruct((sp, heads * HEAD_DIM), BF16),
        compiler_params=_params("parallel", "arbitrary"),
        name="stickbreak_prompt",
    )(q, k, v, _sb_tri(SB_KEYS))


def _sb_sample(q, k, v, cache_k, cache_v, layer, sp, ls):
    n_layers, bs, past, heads, _ = cache_k.shape
    hg = 8
    assert past % SB_KEYS == 0 and past >= SB_KEYS and sp % ls == 0 and heads % hg == 0
    row0 = sp // ls
    gw = hg * HEAD_DIM
    blk_new = pl.BlockSpec((ls, gw), lambda b, g: (row0 + b, g))
    blk_old = pl.BlockSpec((None, None, past, None, hg, HEAD_DIM), lambda b, g: (layer, b, 0, g, 0, 0))
    grouped = (n_layers, bs, past, heads // hg, hg, HEAD_DIM)
    return pl.pallas_call(
        _sb_sample_kernel,
        grid=(bs, heads // hg),
        in_specs=[blk_new, blk_new, blk_new, blk_old, blk_old,
                  pl.BlockSpec((ls, ls + HEAD_DIM), lambda b, g: (0, 0)),
                  pl.BlockSpec((SB_KEYS, SB_KEYS + HEAD_DIM), lambda b, g: (0, 0))],
        out_specs=pl.BlockSpec((ls, gw), lambda b, g: (b, g)),
        out_shape=jax.ShapeDtypeStruct((bs * ls, heads * HEAD_DIM), BF16),
        compiler_params=_params("parallel", "arbitrary"),
        name="stickbreak_sample",
    )(q, k, v, cache_k.reshape(grouped), cache_v.reshape(grouped), _sb_tri(ls), _sb_tri(SB_KEYS))


def _peer_scores_kernel(key_ref, q_ref, o_ref):
    o_ref[...] = _dot_nt(key_ref[...], q_ref[...])


def _peer_scores(pq, sub_keys, layer):
    t = pq.shape[0]
    _, nhc, nkeys, half = sub_keys.shape
    tt = _tile(t, 1024)
    return pl.pallas_call(
        _peer_scores_kernel,
        grid=(nhc, t // tt),
        in_specs=[pl.BlockSpec((None, None, nkeys, half), lambda c, i: (layer, c, 0, 0)),
                  pl.BlockSpec((tt, half), lambda c, i: (i, c))],
        out_specs=pl.BlockSpec((None, nkeys, tt), lambda c, i: (c, 0, i)),
        out_shape=jax.ShapeDtypeStruct((nhc, nkeys, t), F32),
        compiler_params=_params("parallel", "parallel"),
        name="peer_scores",
    )(sub_keys, pq)


def _extract_top(s, k, tie_safe):
    rows = lax.broadcasted_iota(jnp.int32, s.shape, 0)
    rank = jnp.full(s.shape, float(k), F32)
    vals = []
    for r in range(k):
        m = jnp.max(s, axis=0, keepdims=True)
        vals.append(m)
        hit = s == m
        if tie_safe:
            hit = rows == jnp.min(jnp.where(hit, rows, s.shape[0]), axis=0, keepdims=True)
        rank = jnp.where(hit, float(r), rank)
        s = jnp.where(hit, -jnp.inf, s)
    return vals, rank, s


def _peer_select_kernel(s_ref, m1x_ref, e1x_ref, rank2_ref, e2_ref, *, ne):
    def select(h, tie_safe):
        s1 = s_ref[2 * h]
        s2 = s_ref[2 * h + 1]
        a, rank1, _ = _extract_top(s1, PEER_TOPK, tie_safe)
        b, rank2, _ = _extract_top(s2, PEER_TOPK, tie_safe)
        b = jnp.concatenate(b, axis=0)
        sub = lax.broadcasted_iota(jnp.int32, (8, b.shape[1]), 0)
        cand = [a[0] + b, a[1] + b[:8]]
        for i in range(2, 8):
            cand.append(jnp.where(sub < PEER_TOPK // (i + 1), a[i] + b[:8], -jnp.inf))
        cand.append(jnp.concatenate(a[8:], axis=0) + b[0:1])
        cand = jnp.concatenate(cand, axis=0)
        g, _, left = _extract_top(cand, PEER_TOPK, tie_safe)
        z = jnp.ones_like(g[0])
        for gk in g[1:]:
            z = z + jnp.exp(gk - g[0])
        took = (left != cand).astype(F32)
        taken = [jnp.sum(took[0:16], axis=0, keepdims=True), jnp.sum(took[16:24], axis=0, keepdims=True)]
        taken += [jnp.sum(took[8 * i + 8:8 * i + 16], axis=0, keepdims=True) for i in range(2, 8)]
        taken += [took[72 + i:73 + i] for i in range(8)]
        m1 = jnp.zeros_like(s1)
        for i in range(PEER_TOPK):
            m1 = jnp.where(rank1 == float(i), taken[i], m1)
        e1 = jnp.exp(s1 - a[0]) / z
        rank2_ref[h] = rank2.astype(rank2_ref.dtype)
        e2_ref[h] = jnp.exp(s2 - b[0:1]).astype(e2_ref.dtype)
        row = pl.multiple_of(h * ne, ne)
        for j in range(N_KEYS // ne):
            m1x_ref[j, pl.ds(row, ne), :] = m1[j * ne:(j + 1) * ne]
            e1x_ref[j, pl.ds(row, ne), :] = e1[j * ne:(j + 1) * ne]
        removed = (jnp.sum((rank1 < PEER_TOPK).astype(F32), axis=0, keepdims=True)
                   + jnp.sum((rank2 < PEER_TOPK).astype(F32), axis=0, keepdims=True)
                   + jnp.sum(took, axis=0, keepdims=True))
        return jnp.max(removed) - 3.0 * PEER_TOPK

    def one_head(h, _):
        extra = select(h, tie_safe=False)

        @pl.when(extra > 0.0)
        def _():
            select(h, tie_safe=True)

        return 0

    lax.fori_loop(0, PEER_HEADS, one_head, 0)


def _peer_select(scores, ne):
    nhc, nkeys, t = scores.shape
    tt = _tile(t, 256)
    x_shape = jax.ShapeDtypeStruct((nkeys // ne, PEER_HEADS * ne, t), F32)
    x_spec = pl.BlockSpec((nkeys // ne, PEER_HEADS * ne, tt), lambda i: (0, 0, i))
    y_shape = jax.ShapeDtypeStruct((PEER_HEADS, nkeys, t), BF16)
    y_spec = pl.BlockSpec((PEER_HEADS, nkeys, tt), lambda i: (0, 0, i))
    return pl.pallas_call(
        functools.partial(_peer_select_kernel, ne=ne),
        grid=(t // tt,),
        in_specs=[pl.BlockSpec((nhc, nkeys, tt), lambda i: (0, 0, i))],
        out_specs=[x_spec, x_spec, y_spec, y_spec],
        out_shape=[x_shape, x_shape, y_shape, y_shape],
        compiler_params=_params("parallel"),
        name="peer_select",
    )(scores)


def _peer_act_kernel(u_ref, x_ref, m1x_ref, e1x_ref, rank2_ref, e2_ref, o_ref, *, ne):
    x = x_ref[...]
    for p in range(ne // 2):
        hp = _dot_nt(x, u_ref[2 * p * N_KEYS:2 * (p + 1) * N_KEYS, :])
        for half in range(2):
            e = 2 * p + half
            gate = jnp.zeros((N_KEYS, x.shape[0]), BF16)
            for h in range(PEER_HEADS):
                r = h * ne + e
                hit = rank2_ref[h] < m1x_ref[r:r + 1, :].astype(BF16)
                w = e2_ref[h] * e1x_ref[r:r + 1, :].astype(BF16)
                gate = gate + jnp.where(hit, w, jnp.zeros_like(w))
            hh = hp[:, half * N_KEYS:(half + 1) * N_KEYS]
            gelu = 0.5 * hh * (1.0 + lax.erf(hh * (0.5 ** 0.5)))
            o_ref[:, e * N_KEYS:(e + 1) * N_KEYS] = gate.T * gelu.astype(BF16)


def _peer_act(xn, u, layer, m1x, e1x, rank2, e2, ne):
    t, d = xn.shape
    n_exp = u.shape[1]
    tt = _tile(t, 512)
    te = ne * N_KEYS
    row_spec = pl.BlockSpec((None, PEER_HEADS * ne, tt), lambda i, j: (j, 0, i))
    tile_spec = pl.BlockSpec((PEER_HEADS, N_KEYS, tt), lambda i, j: (0, 0, i))
    return pl.pallas_call(
        functools.partial(_peer_act_kernel, ne=ne),
        grid=(t // tt, n_exp // te),
        in_specs=[pl.BlockSpec((None, te, d), lambda i, j: (layer, j, 0)),
                  pl.BlockSpec((tt, d), lambda i, j: (i, 0)),
                  row_spec, row_spec, tile_spec, tile_spec],
        out_specs=pl.BlockSpec((tt, te), lambda i, j: (i, j)),
        out_shape=jax.ShapeDtypeStruct((t, n_exp), BF16),
        compiler_params=_params("parallel", "arbitrary"),
        name="peer_act",
    )(u, xn, m1x, e1x, rank2, e2)


def _peer_ffn(h, layer, norm_g, w_q, sub_keys, u, v):
    ne = 8
    xn = _rmsnorm(h, norm_g)
    pq = _matmul(xn, w_q, layer, out_dtype=BF16, name="peer_query")
    scores = _peer_scores(pq, sub_keys, layer)
    m1x, e1x, rank2, e2 = _peer_select(scores, ne)
    act = _peer_act(xn, u, layer, m1x, e1x, rank2, e2, ne)
    return _matmul(act, v, layer, mode="residual", aux=h, tm=1024, tn=1024, tk=2048, name="peer_down")


def kernel(x_prompt, x_sample, cache_a_k, cache_a_v, state_b_pool, cache_c_k, cache_c_v, norm_mix, norm_ffn, ab_w_in, ab_q_gain, ab_k_gain, ab_rel_bias, ab_pool_w, ab_pool_scale, ab_w_out, c_w_in, c_w_out, peer_w_q, peer_sub_keys, peer_u, peer_v):
    bp, sp, d = x_prompt.shape
    bs, ls, _ = x_sample.shape
    assert bp == 1
    depth = norm_mix.shape[0]
    past_len = cache_c_k.shape[2]
    a_heads = ab_rel_bias.shape[1]
    a_width = a_heads * HEAD_DIM
    b_width = ab_pool_scale.shape[1]
    c_width = c_w_out.shape[1]
    win_p = min(A_PAST_CHUNKS * CHUNK, sp)
    n_s = bs * ls

    h = jnp.concatenate([x_prompt.reshape(sp, d), x_sample.reshape(n_s, d)], axis=0)

    ab_w_in_b, ab_w_out_b = ab_w_in.astype(BF16), ab_w_out.astype(BF16)
    c_w_in_b, c_w_out_b = c_w_in.astype(BF16), c_w_out.astype(BF16)
    peer_w_q_b, peer_u_b, peer_v_b = peer_w_q.astype(BF16), peer_u.astype(BF16), peer_v.astype(BF16)
    sub_keys_b = peer_sub_keys.astype(BF16).reshape(depth, PEER_HEADS * 2, N_KEYS, -1)

    outs = {name: [] for name in ("a_k_p", "a_v_p", "b_p", "c_k_p", "c_v_p",
                                  "a_k_s", "a_v_s", "b_s", "c_k_s", "c_v_s")}
    for layer in range(depth):
        xn = _rmsnorm(h, norm_mix[layer])
        if layer % 2 == 0:
            i = layer // 2
            q = _matmul(xn, ab_w_in_b, i, col0=0, n=a_width, out_dtype=BF16, mode="headnorm",
                        aux=ab_q_gain[i], name="a_q")
            k = _matmul(xn, ab_w_in_b, i, col0=a_width, n=a_width, mode="headnorm", aux=ab_k_gain[i], name="a_k")
            v = _matmul(xn, ab_w_in_b, i, col0=2 * a_width, n=a_width, name="a_v")
            p = _matmul(xn, ab_w_in_b, i, col0=3 * a_width, n=b_width, name="b_in")

            att_p = _band_prompt(q, k, v, ab_rel_bias[i], sp)
            att_s = _band_sample(q, k, v, cache_a_k, cache_a_v, i, ab_rel_bias[i], sp, ls, past_len)

            pool_w = ab_pool_w[i].astype(BF16)
            tr = _tile(sp, 256)
            pp = p[:sp].reshape(sp // tr, tr, b_width)
            hist_p = jnp.concatenate([jnp.zeros((1, POOL_HIST_PAD, b_width), F32),
                                      pp[:-1, tr - POOL_HIST_PAD:]], axis=0)
            pool_p = _pool(p, hist_p, pool_w, ab_pool_scale[i], row_block0=0, nblk=sp // tr, tr=tr,
                           pos_base=0, pos_step=tr, name="pool_prompt")
            hist_s = jnp.pad(state_b_pool[i], ((0, 0), (POOL_HIST_PAD - POOL_HIST, 0), (0, 0)))
            pool_s = _pool(p, hist_s, pool_w, ab_pool_scale[i], row_block0=sp // ls, nblk=bs, tr=ls,
                           pos_base=past_len, pos_step=0, name="pool_sample")

            mix_in = jnp.concatenate([jnp.concatenate([att_p, pool_p], axis=1),
                                      jnp.concatenate([att_s, pool_s], axis=1)], axis=0)
            h = _matmul(mix_in, ab_w_out_b, i, mode="residual", aux=h, name="a_out")

            ps = p[sp:].reshape(bs, ls, b_width)
            outs["a_k_p"].append(k[sp - win_p:sp].reshape(bp, win_p, a_heads, HEAD_DIM))
            outs["a_v_p"].append(v[sp - win_p:sp].reshape(bp, win_p, a_heads, HEAD_DIM))
            outs["b_p"].append(p[sp - POOL_HIST:sp].reshape(bp, POOL_HIST, b_width))
            outs["a_k_s"].append(k[sp:].reshape(bs, ls, a_heads, HEAD_DIM))
            outs["a_v_s"].append(v[sp:].reshape(bs, ls, a_heads, HEAD_DIM))
            outs["b_s"].append(jnp.concatenate([state_b_pool[i], ps], axis=1)[:, ls:])
        else:
            j = layer // 2
            c_heads = c_width // HEAD_DIM
            q = _matmul(xn, c_w_in_b, j, col0=0, n=c_width, out_dtype=BF16, name="c_q")
            k = _matmul(xn, c_w_in_b, j, col0=c_width, n=c_width, name="c_k")
            v = _matmul(xn, c_w_in_b, j, col0=2 * c_width, n=c_width, name="c_v")
            att_p = _sb_prompt(q, k, v, sp)
            att_s = _sb_sample(q, k, v, cache_c_k, cache_c_v, j, sp, ls)
            mix_in = jnp.concatenate([att_p, att_s], axis=0)
            h = _matmul(mix_in, c_w_out_b, j, mode="residual", aux=h, name="c_out")
            outs["c_k_p"].append(k[:sp].reshape(bp, sp, c_heads, HEAD_DIM))
            outs["c_v_p"].append(v[:sp].reshape(bp, sp, c_heads, HEAD_DIM))
            outs["c_k_s"].append(k[sp:].reshape(bs, ls, c_heads, HEAD_DIM))
            outs["c_v_s"].append(v[sp:].reshape(bs, ls, c_heads, HEAD_DIM))

        h = _peer_ffn(h, layer, norm_ffn[layer], peer_w_q_b, sub_keys_b, peer_u_b, peer_v_b)

    st = {name: jnp.stack(vals) for name, vals in outs.items()}
    return (h[:sp].reshape(bp, sp, d), h[sp:].reshape(bs, ls, d),
            st["a_k_p"], st["a_v_p"], st["b_p"], st["c_k_p"], st["c_v_p"],
            st["a_k_s"], st["a_v_s"], st["b_s"], st["c_k_s"], st["c_v_s"])
```

```python
import functools

import jax
import jax.numpy as jnp
from jax import lax
from jax.experimental import pallas as pl
from jax.experimental.pallas import tpu as pltpu

F32 = jnp.float32
BF16 = jnp.bfloat16

HEAD_DIM = 128
CHUNK = 64
A_PAST_CHUNKS = 8
REL_CLIP = 128
POOL_WINDOWS = (2, 4, 8, 16)
POOL_HIST = max(POOL_WINDOWS) - 1
POOL_HIST_PAD = 16
B_GROUP_WIDTH = 512
PEER_HEADS = 8
N_KEYS = 128
PEER_TOPK = 16
ROW_BLOCK = 8
EPS = 1e-6
NEG_INF = -1e30
ATTN_SCALE = HEAD_DIM ** -0.5
SB_EXIT = -104.0
SB_KEYS = 128
VMEM_LIMIT_BYTES = 56 * 1024 * 1024


def _params(*sem):
    return pltpu.CompilerParams(dimension_semantics=sem, vmem_limit_bytes=VMEM_LIMIT_BYTES)


def _tile(dim, pref):
    return pref if dim % pref == 0 else dim


def _dot(a, b):
    return jnp.dot(a, b, preferred_element_type=F32)


def _dot_nt(a, b):
    return lax.dot_general(a, b, (((1,), (1,)), ((), ())), preferred_element_type=F32)


def _rmsnorm_kernel(x_ref, g_ref, o_ref):
    x = x_ref[...]
    ms = jnp.mean(x * x, axis=-1, keepdims=True)
    o_ref[...] = (x * lax.rsqrt(ms + EPS) * g_ref[...]).astype(o_ref.dtype)


def _rmsnorm(x, g):
    t, d = x.shape
    tr = _tile(t, 256)
    return pl.pallas_call(
        _rmsnorm_kernel,
        grid=(t // tr,),
        in_specs=[pl.BlockSpec((tr, d), lambda i: (i, 0)),
                  pl.BlockSpec((1, d), lambda i: (0, 0))],
        out_specs=pl.BlockSpec((tr, d), lambda i: (i, 0)),
        out_shape=jax.ShapeDtypeStruct((t, d), BF16),
        compiler_params=_params("parallel"),
        name="rmsnorm",
    )(x, g.reshape(1, d))


def _mm_kernel(*refs, nk, mode):
    x_ref, w_ref = refs[0], refs[1]
    if mode in ("headnorm", "residual"):
        aux_ref, o_ref, scratch = refs[2], refs[3], refs[4:]
    else:
        aux_ref, o_ref, scratch = None, refs[2], refs[3:]

    def finish(acc):
        if mode == "headnorm":
            g = aux_ref[...]
            for c in range(acc.shape[1] // HEAD_DIM):
                sl = slice(c * HEAD_DIM, (c + 1) * HEAD_DIM)
                y = acc[:, sl]
                ms = jnp.mean(y * y, axis=-1, keepdims=True)
                o_ref[:, sl] = (y * lax.rsqrt(ms + EPS) * g).astype(o_ref.dtype)
        elif mode == "residual":
            o_ref[...] = aux_ref[...] + acc
        else:
            o_ref[...] = acc.astype(o_ref.dtype)

    if nk == 1:
        finish(_dot(x_ref[...], w_ref[...]))
    else:
        acc_ref = scratch[0]
        k = pl.program_id(2)

        @pl.when(k == 0)
        def _():
            acc_ref[...] = jnp.zeros_like(acc_ref)

        acc_ref[...] += _dot(x_ref[...], w_ref[...])

        @pl.when(k == nk - 1)
        def _():
            finish(acc_ref[...])


def _matmul(x, w, layer, *, col0=0, n=None, out_dtype=F32, mode="plain", aux=None,
            tm=1024, tn=512, tk=None, name="matmul"):
    m, kdim = x.shape
    n = w.shape[2] if n is None else n
    tm, tn = _tile(m, tm), _tile(n, tn)
    tk = kdim if tk is None else _tile(kdim, tk)
    nk = kdim // tk
    cb = col0 // tn
    assert col0 % tn == 0
    in_specs = [pl.BlockSpec((tm, tk), lambda i, j, k: (i, k)),
                pl.BlockSpec((None, tk, tn), lambda i, j, k: (layer, k, j + cb))]
    args = [x, w]
    if mode == "headnorm":
        in_specs.append(pl.BlockSpec((1, HEAD_DIM), lambda i, j, k: (0, 0)))
        args.append(aux.reshape(1, HEAD_DIM))
    elif mode == "residual":
        in_specs.append(pl.BlockSpec((tm, tn), lambda i, j, k: (i, j)))
        args.append(aux)
    scratch = [pltpu.VMEM((tm, tn), F32)] if nk > 1 else []
    return pl.pallas_call(
        functools.partial(_mm_kernel, nk=nk, mode=mode),
        grid=(m // tm, n // tn, nk),
        in_specs=in_specs,
        out_specs=pl.BlockSpec((tm, tn), lambda i, j, k: (i, j)),
        out_shape=jax.ShapeDtypeStruct((m, n), out_dtype),
        scratch_shapes=scratch,
        compiler_params=_params("parallel", "parallel", "arbitrary"),
        name=name,
    )(*args)


def _band_head(q, ka, kb, va, vb, bias_a, bias_b, past_visible):
    sa = _dot_nt(q, ka.astype(BF16)) * ATTN_SCALE + bias_a
    sb = _dot_nt(q, kb.astype(BF16)) * ATTN_SCALE + bias_b
    if past_visible is not None:
        sa = jnp.where(past_visible, sa, NEG_INF)
    m = jnp.maximum(jnp.max(sa, axis=-1, keepdims=True), jnp.max(sb, axis=-1, keepdims=True))
    pa = jnp.exp(sa - m)
    pb = jnp.exp(sb - m)
    l = jnp.sum(pa, axis=-1, keepdims=True) + jnp.sum(pb, axis=-1, keepdims=True)
    o = _dot(pa.astype(BF16), va.astype(BF16)) + _dot(pb.astype(BF16), vb.astype(BF16))
    return o / l


def _band_prompt_kernel(q_ref, ka_ref, kb_ref, va_ref, vb_ref, bias_ref, o_ref):
    na = ka_ref.shape[0]
    o = _band_head(q_ref[...], ka_ref[...], kb_ref[...], va_ref[...], vb_ref[...],
                   bias_ref[:, :na], bias_ref[:, na:], pl.program_id(1) > 0)
    o_ref[...] = o.astype(o_ref.dtype)


def _band_sample_kernel(q_ref, kn_ref, vn_ref, kc_ref, vc_ref, bias_ref, o_ref):
    win, heads = kc_ref.shape[0], kc_ref.shape[1]
    kc = pltpu.einshape("phd->hpd", kc_ref[...])
    vc = pltpu.einshape("phd->hpd", vc_ref[...])
    for hh in range(heads):
        sl = slice(hh * HEAD_DIM, (hh + 1) * HEAD_DIM)
        o = _band_head(q_ref[:, sl], kc[hh], kn_ref[:, sl], vc[hh], vn_ref[:, sl],
                       bias_ref[hh, :, :win], bias_ref[hh, :, win:], None)
        o_ref[:, sl] = o.astype(o_ref.dtype)


def _band_bias(rel_bias, q_pos, k_pos):
    lq, lk = q_pos.shape[0], k_pos.shape[0]
    d_min = q_pos[0] - k_pos[lk - 1]
    diag = jnp.clip(d_min + jnp.arange(lq + lk - 1, dtype=jnp.int32), -REL_CLIP, REL_CLIP) + REL_CLIP
    g = rel_bias.astype(F32)[:, diag]
    period = lq + lk
    u = jnp.pad(g[:, ::-1], ((0, 0), (0, 1)))
    skew = jnp.tile(u, (1, lq))[:, :lq * (period - 1)].reshape(-1, lq, period - 1)
    bias = skew[:, :, lq - 1:lq - 1 + lk]
    qc = q_pos[:, None] // CHUNK
    kc = k_pos[None, :] // CHUNK
    mask = (kc <= qc) & (kc >= qc - A_PAST_CHUNKS)
    return jnp.where(mask[None], bias, NEG_INF)


def _band_prompt(q, k, v, rel_bias, sp):
    heads = rel_bias.shape[0]
    tq = A_PAST_CHUNKS * CHUNK
    assert sp % tq == 0
    pos = jnp.arange(tq, dtype=jnp.int32)
    bias = _band_bias(rel_bias, tq + pos, jnp.arange(2 * tq, dtype=jnp.int32))
    prev = lambda h, i: (jnp.maximum(i - 1, 0), h)
    cur = lambda h, i: (i, h)
    blk = lambda im: pl.BlockSpec((tq, HEAD_DIM), im)
    return pl.pallas_call(
        _band_prompt_kernel,
        grid=(heads, sp // tq),
        in_specs=[blk(cur), blk(prev), blk(cur), blk(prev), blk(cur),
                  pl.BlockSpec((None, tq, 2 * tq), lambda h, i: (h, 0, 0))],
        out_specs=blk(cur),
        out_shape=jax.ShapeDtypeStruct((sp, heads * HEAD_DIM), BF16),
        compiler_params=_params("parallel", "arbitrary"),
        name="band_prompt",
    )(q, k, k, v, v, bias)


def _band_sample(q, k, v, cache_k, cache_v, layer, rel_bias, sp, ls, past_len):
    _, bs, win, heads, _ = cache_k.shape
    width = heads * HEAD_DIM
    assert sp % ls == 0
    row0 = sp // ls
    q_pos = past_len + jnp.arange(ls, dtype=jnp.int32)
    k_pos = past_len - win + jnp.arange(win + ls, dtype=jnp.int32)
    bias = _band_bias(rel_bias, q_pos, k_pos)
    blk_new = pl.BlockSpec((ls, width), lambda b: (row0 + b, 0))
    blk_old = pl.BlockSpec((None, None, win, heads, HEAD_DIM), lambda b: (layer, b, 0, 0, 0))
    return pl.pallas_call(
        _band_sample_kernel,
        grid=(bs,),
        in_specs=[blk_new, blk_new, blk_new, blk_old, blk_old,
                  pl.BlockSpec((heads, ls, win + ls), lambda b: (0, 0, 0))],
        out_specs=pl.BlockSpec((ls, width), lambda b: (b, 0)),
        out_shape=jax.ShapeDtypeStruct((bs * ls, width), BF16),
        compiler_params=_params("parallel"),
        name="band_sample",
    )(q, k, v, cache_k, cache_v, bias)


def _pool_kernel(cur_ref, hist_ref, w_ref, sc_ref, o_ref, ext_ref, *, pos_base, pos_step):
    tr = cur_ref.shape[0]
    ext_ref[0:POOL_HIST_PAD, :] = hist_ref[...]
    ext_ref[POOL_HIST_PAD:POOL_HIST_PAD + tr, :] = cur_ref[...]
    pos = pos_base + pl.program_id(0) * pos_step + lax.broadcasted_iota(jnp.int32, (tr, 1), 0)
    for g, w in enumerate(POOL_WINDOWS):
        sl = slice(g * B_GROUP_WIDTH, (g + 1) * B_GROUP_WIDTH)
        cur = cur_ref[:, sl]
        tot = cur
        for j in range(1, w):
            tot = tot + ext_ref[POOL_HIST_PAD - j:POOL_HIST_PAD - j + tr, sl]
        cnt = jnp.minimum(pos + 1, w).astype(F32)
        d = tot / cnt - cur
        y = _dot(d.astype(BF16), w_ref[g]) * sc_ref[:, sl]
        o_ref[:, sl] = y.astype(o_ref.dtype)


def _pool(p, hist, pool_w, pool_scale, *, row_block0, nblk, tr, pos_base, pos_step, name):
    width = p.shape[1]
    return pl.pallas_call(
        functools.partial(_pool_kernel, pos_base=pos_base, pos_step=pos_step),
        grid=(nblk,),
        in_specs=[pl.BlockSpec((tr, width), lambda i: (row_block0 + i, 0)),
                  pl.BlockSpec((None, POOL_HIST_PAD, width), lambda i: (i, 0, 0)),
                  pl.BlockSpec(pool_w.shape, lambda i: (0, 0, 0)),
                  pl.BlockSpec((1, width), lambda i: (0, 0))],
        out_specs=pl.BlockSpec((tr, width), lambda i: (i, 0)),
        out_shape=jax.ShapeDtypeStruct((nblk * tr, width), BF16),
        scratch_shapes=[pltpu.VMEM((POOL_HIST_PAD + tr, width), F32)],
        compiler_params=_params("parallel"),
        name=name,
    )(p, hist, pool_w, pool_scale.reshape(1, width))


def _sb_steps(qs, kvs, tri, carries, accs, visibles):
    nk = kvs[0][0].shape[0]
    zs = [_dot_nt(q, k) * ATTN_SCALE for q, (k, _) in zip(qs, kvs)]
    log_keeps = [jnp.where(vis, -(jnp.maximum(z, 0.0) + jnp.log1p(jnp.exp(-jnp.abs(z)))), 0.0)
                 for z, vis in zip(zs, visibles)]
    his = [lk.astype(BF16) for lk in log_keeps]
    los = [(lk - hi.astype(F32)).astype(BF16) for lk, hi in zip(log_keeps, his)]
    sums = [_dot(hi, tri) + _dot(lo, tri) for hi, lo in zip(his, los)]
    weights = [jnp.where(vis, jnp.exp(z + lk + s[:, :nk] + carry[:, :nk]), 0.0).astype(BF16)
               for z, lk, s, carry, vis in zip(zs, log_keeps, sums, carries, visibles)]
    new_accs = [acc + _dot(a, v) for acc, a, (_, v) in zip(accs, weights, kvs)]
    new_carries = [carry + s[:, nk:] for carry, s in zip(carries, sums)]
    return new_carries, new_accs


def _sb_sweep(chains, load_kv, tri, carries, accs):
    rows = chains[0][0].shape[0]
    lane = lax.broadcasted_iota(jnp.int32, (rows, SB_KEYS), 1)
    beyond = jnp.int32(2 ** 30)

    def cond(st):
        return st[1] == 0

    def body(st):
        n, _, carries, accs = st
        kbs = [kb0 - n for _, kb0, _ in chains]
        starts = [pl.multiple_of(jnp.maximum(kb, 0) * SB_KEYS, SB_KEYS) for kb in kbs]
        visibles = [(jnp.where(kb >= 0, start, beyond) + lane) < qpos
                    for (_, _, qpos), kb, start in zip(chains, kbs, starts)]
        carries, accs = _sb_steps([q for q, _, _ in chains], load_kv(starts), tri, carries, accs, visibles)
        pending = jnp.full((rows, HEAD_DIM), -jnp.inf, F32)
        for kb, carry in zip(kbs, carries):
            pending = jnp.maximum(pending, jnp.where(kb > 0, carry, -jnp.inf))
        done = (jnp.max(pending) < SB_EXIT).astype(jnp.int32)
        return n + 1, done, carries, accs

    return lax.while_loop(cond, body, (jnp.int32(0), jnp.int32(0), list(carries), list(accs)))[3]


def _sb_prompt_kernel(q_ref, k_ref, v_ref, tri_ref, o_ref, *, tq):
    n_chains = q_ref.shape[0] // tq
    row0 = pl.program_id(1) * q_ref.shape[0]
    row = lax.broadcasted_iota(jnp.int32, (tq, SB_KEYS), 0)

    def load_kv(starts):
        return [(k_ref[pl.ds(s, SB_KEYS), :].astype(BF16), v_ref[pl.ds(s, SB_KEYS), :].astype(BF16))
                for s in starts]

    chains = [(q_ref[c * tq:(c + 1) * tq, :], (row0 + (c + 1) * tq) // SB_KEYS - 1, row0 + c * tq + row)
              for c in range(n_chains)]
    zeros = [jnp.zeros((tq, HEAD_DIM), F32)] * n_chains
    accs = _sb_sweep(chains, load_kv, tri_ref[...], zeros, zeros)
    for c in range(n_chains):
        o_ref[c * tq:(c + 1) * tq, :] = accs[c].astype(o_ref.dtype)


def _sb_sample_kernel(q_ref, kn_ref, vn_ref, kc_ref, vc_ref, trin_ref, tri_ref, o_ref):
    ls = q_ref.shape[0]
    past, heads = kc_ref.shape[0], kc_ref.shape[1]
    causal = lax.broadcasted_iota(jnp.int32, (ls, ls), 1) < lax.broadcasted_iota(jnp.int32, (ls, ls), 0)
    all_rows = jnp.full((ls, SB_KEYS), 2 ** 30 - SB_KEYS, jnp.int32)
    zeros = jnp.zeros((ls, HEAD_DIM), F32)
    head_cols = [slice(hh * HEAD_DIM, (hh + 1) * HEAD_DIM) for hh in range(heads)]
    chains = [(q_ref[:, sl], past // SB_KEYS - 1, all_rows) for sl in head_cols]
    carries, accs = _sb_steps([q for q, _, _ in chains],
                              [(kn_ref[:, sl].astype(BF16), vn_ref[:, sl].astype(BF16)) for sl in head_cols],
                              trin_ref[...], [zeros] * heads, [zeros] * heads, [causal] * heads)

    def load_kv(starts):
        k = pltpu.einshape("phd->hpd", kc_ref[pl.ds(starts[0], SB_KEYS), :, :])
        v = pltpu.einshape("phd->hpd", vc_ref[pl.ds(starts[0], SB_KEYS), :, :])
        return [(k[hh].astype(BF16), v[hh].astype(BF16)) for hh in range(heads)]

    accs = _sb_sweep(chains, load_kv, tri_ref[...], carries, accs)
    for hh in range(heads):
        o_ref[:, hh * HEAD_DIM:(hh + 1) * HEAD_DIM] = accs[hh].astype(o_ref.dtype)


def _sb_tri(nk):
    j = jnp.arange(nk)[:, None]
    s = jnp.arange(nk + HEAD_DIM)[None, :]
    return ((s >= nk) | (j > s)).astype(BF16)


def _sb_prompt(q, k, v, sp):
    heads = q.shape[1] // HEAD_DIM
    tq = SB_KEYS
    tb = _tile(sp, 8 * tq)
    assert sp % SB_KEYS == 0 and tb % tq == 0
    return pl.pallas_call(
        functools.partial(_sb_prompt_kernel, tq=tq),
        grid=(heads, sp // tb),
        in_specs=[pl.BlockSpec((tb, HEAD_DIM), lambda h, i: (i, h)),
                  pl.BlockSpec((sp, HEAD_DIM), lambda h, i: (0, h)),
                  pl.BlockSpec((sp, HEAD_DIM), lambda h, i: (0, h)),
                  pl.BlockSpec((SB_KEYS, SB_KEYS + HEAD_DIM), lambda h, i: (0, 0))],
        out_specs=pl.BlockSpec((tb, HEAD_DIM), lambda h, i: (i, h)),
        out_shape=jax.ShapeDtypeStruct((sp, heads * HEAD_DIM), BF16),
        compiler_params=_params("parallel", "arbitrary"),
        name="stickbreak_prompt",
    )(q, k, v, _sb_tri(SB_KEYS))


def _sb_sample(q, k, v, cache_k, cache_v, layer, sp, ls):
    n_layers, bs, past, heads, _ = cache_k.shape
    hg = 8
    assert past % SB_KEYS == 0 and past >= SB_KEYS and sp % ls == 0 and heads % hg == 0
    row0 = sp // ls
    gw = hg * HEAD_DIM
    blk_new = pl.BlockSpec((ls, gw), lambda b, g: (row0 + b, g))
    blk_old = pl.BlockSpec((None, None, past, None, hg, HEAD_DIM), lambda b, g: (layer, b, 0, g, 0, 0))
    grouped = (n_layers, bs, past, heads // hg, hg, HEAD_DIM)
    return pl.pallas_call(
        _sb_sample_kernel,
        grid=(bs, heads // hg),
        in_specs=[blk_new, blk_new, blk_new, blk_old, blk_old,
                  pl.BlockSpec((ls, ls + HEAD_DIM), lambda b, g: (0, 0)),
                  pl.BlockSpec((SB_KEYS, SB_KEYS + HEAD_DIM), lambda b, g: (0, 0))],
        out_specs=pl.BlockSpec((ls, gw), lambda b, g: (b, g)),
        out_shape=jax.ShapeDtypeStruct((bs * ls, heads * HEAD_DIM), BF16),
        compiler_params=_params("parallel", "arbitrary"),
        name="stickbreak_sample",
    )(q, k, v, cache_k.reshape(grouped), cache_v.reshape(grouped), _sb_tri(ls), _sb_tri(SB_KEYS))


def _peer_scores_kernel(key_ref, q_ref, o_ref):
    o_ref[...] = _dot_nt(key_ref[...], q_ref[...])


def _peer_scores(pq, sub_keys, layer):
    t = pq.shape[0]
    _, nhc, nkeys, half = sub_keys.shape
    tt = _tile(t, 1024)
    return pl.pallas_call(
        _peer_scores_kernel,
        grid=(nhc, t // tt),
        in_specs=[pl.BlockSpec((None, None, nkeys, half), lambda c, i: (layer, c, 0, 0)),
                  pl.BlockSpec((tt, half), lambda c, i: (i, c))],
        out_specs=pl.BlockSpec((None, nkeys, tt), lambda c, i: (c, 0, i)),
        out_shape=jax.ShapeDtypeStruct((nhc, nkeys, t), F32),
        compiler_params=_params("parallel", "parallel"),
        name="peer_scores",
    )(sub_keys, pq)


def _extract_top(s, k, tie_safe):
    rows = lax.broadcasted_iota(jnp.int32, s.shape, 0)
    rank = jnp.full(s.shape, float(k), F32)
    vals = []
    for r in range(k):
        m = jnp.max(s, axis=0, keepdims=True)
        vals.append(m)
        hit = s == m
        if tie_safe:
            hit = rows == jnp.min(jnp.where(hit, rows, s.shape[0]), axis=0, keepdims=True)
        rank = jnp.where(hit, float(r), rank)
        s = jnp.where(hit, -jnp.inf, s)
    return vals, rank, s


def _peer_select_kernel(s_ref, m1x_ref, e1x_ref, rank2_ref, e2_ref, *, ne):
    def select(h, tie_safe):
        s1 = s_ref[2 * h]
        s2 = s_ref[2 * h + 1]
        a, rank1, _ = _extract_top(s1, PEER_TOPK, tie_safe)
        b, rank2, _ = _extract_top(s2, PEER_TOPK, tie_safe)
        b = jnp.concatenate(b, axis=0)
        sub = lax.broadcasted_iota(jnp.int32, (8, b.shape[1]), 0)
        cand = [a[0] + b, a[1] + b[:8]]
        for i in range(2, 8):
            cand.append(jnp.where(sub < PEER_TOPK // (i + 1), a[i] + b[:8], -jnp.inf))
        cand.append(jnp.concatenate(a[8:], axis=0) + b[0:1])
        cand = jnp.concatenate(cand, axis=0)
        g, _, left = _extract_top(cand, PEER_TOPK, tie_safe)
        z = jnp.ones_like(g[0])
        for gk in g[1:]:
            z = z + jnp.exp(gk - g[0])
        took = (left != cand).astype(F32)
        taken = [jnp.sum(took[0:16], axis=0, keepdims=True), jnp.sum(took[16:24], axis=0, keepdims=True)]
        taken += [jnp.sum(took[8 * i + 8:8 * i + 16], axis=0, keepdims=True) for i in range(2, 8)]
        taken += [took[72 + i:73 + i] for i in range(8)]
        m1 = jnp.zeros_like(s1)
        for i in range(PEER_TOPK):
            m1 = jnp.where(rank1 == float(i), taken[i], m1)
        e1 = jnp.exp(s1 - a[0]) / z
        rank2_ref[h] = rank2.astype(rank2_ref.dtype)
        e2_ref[h] = jnp.exp(s2 - b[0:1]).astype(e2_ref.dtype)
        row = pl.multiple_of(h * ne, ne)
        for j in range(N_KEYS // ne):
            m1x_ref[j, pl.ds(row, ne), :] = m1[j * ne:(j + 1) * ne]
            e1x_ref[j, pl.ds(row, ne), :] = e1[j * ne:(j + 1) * ne]
        removed = (jnp.sum((rank1 < PEER_TOPK).astype(F32), axis=0, keepdims=True)
                   + jnp.sum((rank2 < PEER_TOPK).astype(F32), axis=0, keepdims=True)
                   + jnp.sum(took, axis=0, keepdims=True))
        return jnp.max(removed) - 3.0 * PEER_TOPK

    def one_head(h, _):
        extra = select(h, tie_safe=False)

        @pl.when(extra > 0.0)
        def _():
            select(h, tie_safe=True)

        return 0

    lax.fori_loop(0, PEER_HEADS, one_head, 0)


def _peer_select(scores, ne):
    nhc, nkeys, t = scores.shape
    tt = _tile(t, 256)
    x_shape = jax.ShapeDtypeStruct((nkeys // ne, PEER_HEADS * ne, t), F32)
    x_spec = pl.BlockSpec((nkeys // ne, PEER_HEADS * ne, tt), lambda i: (0, 0, i))
    y_shape = jax.ShapeDtypeStruct((PEER_HEADS, nkeys, t), BF16)
    y_spec = pl.BlockSpec((PEER_HEADS, nkeys, tt), lambda i: (0, 0, i))
    return pl.pallas_call(
        functools.partial(_peer_select_kernel, ne=ne),
        grid=(t // tt,),
        in_specs=[pl.BlockSpec((nhc, nkeys, tt), lambda i: (0, 0, i))],
        out_specs=[x_spec, x_spec, y_spec, y_spec],
        out_shape=[x_shape, x_shape, y_shape, y_shape],
        compiler_params=_params("parallel"),
        name="peer_select",
    )(scores)


def _peer_act_kernel(*refs, ne, n_parts):
    u_ref, x_ref = refs[0], refs[1]
    part_refs = refs[2:2 + 4 * n_parts]
    o_ref, gate_ref = refs[2 + 4 * n_parts], refs[3 + 4 * n_parts]
    first_row = (pl.program_id(1) % (ROW_BLOCK // ne)) * ne
    for part in range(n_parts):
        m1x_ref, e1x_ref, rank2_ref, e2_ref = part_refs[4 * part:4 * part + 4]
        tp = rank2_ref.shape[2]
        for e in range(ne):
            gate = jnp.zeros((N_KEYS, tp), BF16)
            for h in range(PEER_HEADS):
                r = pl.ds(h * ROW_BLOCK + first_row + e, 1)
                hit = rank2_ref[h] < m1x_ref[r, :].astype(BF16)
                w = e2_ref[h] * e1x_ref[r, :].astype(BF16)
                gate = gate + jnp.where(hit, w, jnp.zeros_like(w))
            gate_ref[part * tp:(part + 1) * tp, e * N_KEYS:(e + 1) * N_KEYS] = gate.T
    x = x_ref[...]
    for p in range(ne // 2):
        cols = slice(2 * p * N_KEYS, 2 * (p + 1) * N_KEYS)
        hp = _dot_nt(x, u_ref[cols, :].astype(BF16))
        gelu = 0.5 * hp * (1.0 + lax.erf(hp * (0.5 ** 0.5)))
        o_ref[:, cols] = gate_ref[:, cols] * gelu.astype(BF16)


def _peer_act(xn, u, layer, m1x, e1x, rank2, e2, ne):
    t, d = xn.shape
    n_exp = u.shape[1]
    tp = _tile(t, 512)
    n_parts = 2 if t % (2 * tp) == 0 else 1
    tt = n_parts * tp
    te = ne * N_KEYS
    part_specs, part_args = [], []
    for part in range(n_parts):
        row_spec = pl.BlockSpec((None, PEER_HEADS * ROW_BLOCK, tp),
                                lambda i, j, part=part: (j // (ROW_BLOCK // ne), 0, n_parts * i + part))
        tile_spec = pl.BlockSpec((PEER_HEADS, N_KEYS, tp), lambda i, j, part=part: (0, 0, n_parts * i + part))
        part_specs += [row_spec, row_spec, tile_spec, tile_spec]
        part_args += [m1x, e1x, rank2, e2]
    return pl.pallas_call(
        functools.partial(_peer_act_kernel, ne=ne, n_parts=n_parts),
        grid=(t // tt, n_exp // te),
        in_specs=[pl.BlockSpec((None, te, d), lambda i, j: (layer, j, 0)),
                  pl.BlockSpec((tt, d), lambda i, j: (i, 0))] + part_specs,
        out_specs=pl.BlockSpec((tt, te), lambda i, j: (i, j)),
        out_shape=jax.ShapeDtypeStruct((t, n_exp), BF16),
        scratch_shapes=[pltpu.VMEM((tt, te), BF16)],
        compiler_params=_params("parallel", "arbitrary"),
        name="peer_act",
    )(u, xn, *part_args)


def _peer_ffn(h, layer, norm_g, w_q, sub_keys, u, v):
    ne = 4
    xn = _rmsnorm(h, norm_g)
    pq = _matmul(xn, w_q, layer, out_dtype=BF16, name="peer_query")
    scores = _peer_scores(pq, sub_keys, layer)
    m1x, e1x, rank2, e2 = _peer_select(scores, ROW_BLOCK)
    act = _peer_act(xn, u, layer, m1x, e1x, rank2, e2, ne)
    return _matmul(act, v, layer, mode="residual", aux=h, tm=1024, tn=1024, tk=2048, name="peer_down")


def kernel(x_prompt, x_sample, cache_a_k, cache_a_v, state_b_pool, cache_c_k, cache_c_v, norm_mix, norm_ffn, ab_w_in, ab_q_gain, ab_k_gain, ab_rel_bias, ab_pool_w, ab_pool_scale, ab_w_out, c_w_in, c_w_out, peer_w_q, peer_sub_keys, peer_u, peer_v):
    bp, sp, d = x_prompt.shape
    bs, ls, _ = x_sample.shape
    assert bp == 1
    depth = norm_mix.shape[0]
    past_len = cache_c_k.shape[2]
    a_heads = ab_rel_bias.shape[1]
    a_width = a_heads * HEAD_DIM
    b_width = ab_pool_scale.shape[1]
    c_width = c_w_out.shape[1]
    win_p = min(A_PAST_CHUNKS * CHUNK, sp)
    n_s = bs * ls

    h = jnp.concatenate([x_prompt.reshape(sp, d), x_sample.reshape(n_s, d)], axis=0)

    ab_w_in_b, ab_w_out_b = ab_w_in.astype(BF16), ab_w_out.astype(BF16)
    c_w_in_b, c_w_out_b = c_w_in.astype(BF16), c_w_out.astype(BF16)
    peer_w_q_b, peer_v_b = peer_w_q.astype(BF16), peer_v.astype(BF16)
    sub_keys_b = peer_sub_keys.astype(BF16).reshape(depth, PEER_HEADS * 2, N_KEYS, -1)

    outs = {name: [] for name in ("a_k_p", "a_v_p", "b_p", "c_k_p", "c_v_p",
                                  "a_k_s", "a_v_s", "b_s", "c_k_s", "c_v_s")}
    for layer in range(depth):
        xn = _rmsnorm(h, norm_mix[layer])
        if layer % 2 == 0:
            i = layer // 2
            q = _matmul(xn, ab_w_in_b, i, col0=0, n=a_width, out_dtype=BF16, mode="headnorm",
                        aux=ab_q_gain[i], name="a_q")
            k = _matmul(xn, ab_w_in_b, i, col0=a_width, n=a_width, mode="headnorm", aux=ab_k_gain[i], name="a_k")
            v = _matmul(xn, ab_w_in_b, i, col0=2 * a_width, n=a_width, name="a_v")
            p = _matmul(xn, ab_w_in_b, i, col0=3 * a_width, n=b_width, name="b_in")

            att_p = _band_prompt(q, k, v, ab_rel_bias[i], sp)
            att_s = _band_sample(q, k, v, cache_a_k, cache_a_v, i, ab_rel_bias[i], sp, ls, past_len)

            pool_w = ab_pool_w[i].astype(BF16)
            tr = _tile(sp, 256)
            pp = p[:sp].reshape(sp // tr, tr, b_width)
            hist_p = jnp.concatenate([jnp.zeros((1, POOL_HIST_PAD, b_width), F32),
                                      pp[:-1, tr - POOL_HIST_PAD:]], axis=0)
            pool_p = _pool(p, hist_p, pool_w, ab_pool_scale[i], row_block0=0, nblk=sp // tr, tr=tr,
                           pos_base=0, pos_step=tr, name="pool_prompt")
            hist_s = jnp.pad(state_b_pool[i], ((0, 0), (POOL_HIST_PAD - POOL_HIST, 0), (0, 0)))
            pool_s = _pool(p, hist_s, pool_w, ab_pool_scale[i], row_block0=sp // ls, nblk=bs, tr=ls,
                           pos_base=past_len, pos_step=0, name="pool_sample")

            mix_in = jnp.concatenate([jnp.concatenate([att_p, pool_p], axis=1),
                                      jnp.concatenate([att_s, pool_s], axis=1)], axis=0)
            h = _matmul(mix_in, ab_w_out_b, i, mode="residual", aux=h, name="a_out")

            ps = p[sp:].reshape(bs, ls, b_width)
            outs["a_k_p"].append(k[sp - win_p:sp].reshape(bp, win_p, a_heads, HEAD_DIM))
            outs["a_v_p"].append(v[sp - win_p:sp].reshape(bp, win_p, a_heads, HEAD_DIM))
            outs["b_p"].append(p[sp - POOL_HIST:sp].reshape(bp, POOL_HIST, b_width))
            outs["a_k_s"].append(k[sp:].reshape(bs, ls, a_heads, HEAD_DIM))
            outs["a_v_s"].append(v[sp:].reshape(bs, ls, a_heads, HEAD_DIM))
            outs["b_s"].append(jnp.concatenate([state_b_pool[i], ps], axis=1)[:, ls:])
        else:
            j = layer // 2
            c_heads = c_width // HEAD_DIM
            q = _matmul(xn, c_w_in_b, j, col0=0, n=c_width, out_dtype=BF16, name="c_q")
            k = _matmul(xn, c_w_in_b, j, col0=c_width, n=c_width, name="c_k")
            v = _matmul(xn, c_w_in_b, j, col0=2 * c_width, n=c_width, name="c_v")
            att_p = _sb_prompt(q, k, v, sp)
            att_s = _sb_sample(q, k, v, cache_c_k, cache_c_v, j, sp, ls)
            mix_in = jnp.concatenate([att_p, att_s], axis=0)
            h = _matmul(mix_in, c_w_out_b, j, mode="residual", aux=h, name="c_out")
            outs["c_k_p"].append(k[:sp].reshape(bp, sp, c_heads, HEAD_DIM))
            outs["c_v_p"].append(v[:sp].reshape(bp, sp, c_heads, HEAD_DIM))
            outs["c_k_s"].append(k[sp:].reshape(bs, ls, c_heads, HEAD_DIM))
            outs["c_v_s"].append(v[sp:].reshape(bs, ls, c_heads, HEAD_DIM))

        h = _peer_ffn(h, layer, norm_ffn[layer], peer_w_q_b, sub_keys_b, peer_u, peer_v_b)

    st = {name: jnp.stack(vals) for name, vals in outs.items()}
    return (h[:sp].reshape(bp, sp, d), h[sp:].reshape(bs, ls, d),
            st["a_k_p"], st["a_v_p"], st["b_p"], st["c_k_p"], st["c_v_p"],
            st["a_k_s"], st["a_v_s"], st["b_s"], st["c_k_s"], st["c_v_s"])
```

```python
import functools

import jax
import jax.numpy as jnp
from jax import lax
from jax.experimental import pallas as pl
from jax.experimental.pallas import tpu as pltpu

F32 = jnp.float32
BF16 = jnp.bfloat16

HEAD_DIM = 128
CHUNK = 64
A_PAST_CHUNKS = 8
REL_CLIP = 128
POOL_WINDOWS = (2, 4, 8, 16)
POOL_HIST = max(POOL_WINDOWS) - 1
POOL_HIST_PAD = 16
B_GROUP_WIDTH = 512
PEER_HEADS = 8
N_KEYS = 128
PEER_TOPK = 16
ROW_BLOCK = 8
EPS = 1e-6
NEG_INF = -1e30
ATTN_SCALE = HEAD_DIM ** -0.5
SB_EXIT = -104.0
SB_KEYS = 128
VMEM_LIMIT_BYTES = 56 * 1024 * 1024


def _params(*sem):
    return pltpu.CompilerParams(dimension_semantics=sem, vmem_limit_bytes=VMEM_LIMIT_BYTES)


def _tile(dim, pref):
    return pref if dim % pref == 0 else dim


def _ignore_input(body, index):
    def wrapped(*refs):
        return body(*refs[:index], *refs[index + 1:])
    return wrapped


def _dot(a, b):
    return jnp.dot(a, b, preferred_element_type=F32)


def _dot_nt(a, b):
    return lax.dot_general(a, b, (((1,), (1,)), ((), ())), preferred_element_type=F32)


def _rmsnorm_kernel(x_ref, g_ref, o_ref):
    x = x_ref[...]
    ms = jnp.mean(x * x, axis=-1, keepdims=True)
    o_ref[...] = (x * lax.rsqrt(ms + EPS) * g_ref[...]).astype(o_ref.dtype)


def _rmsnorm(x, g):
    t, d = x.shape
    tr = _tile(t, 256)
    return pl.pallas_call(
        _rmsnorm_kernel,
        grid=(t // tr,),
        in_specs=[pl.BlockSpec((tr, d), lambda i: (i, 0)),
                  pl.BlockSpec((1, d), lambda i: (0, 0))],
        out_specs=pl.BlockSpec((tr, d), lambda i: (i, 0)),
        out_shape=jax.ShapeDtypeStruct((t, d), BF16),
        compiler_params=_params("parallel"),
        name="rmsnorm",
    )(x, g.reshape(1, d))


def _mm_kernel(*refs, nk, mode, two_x):
    x_ref, w_ref = refs[0], refs[1]
    refs = refs[2:]
    if two_x:
        x2_ref, refs = refs[0], refs[1:]
    if mode in ("headnorm", "residual"):
        aux_ref, o_ref, scratch = refs[0], refs[1], refs[2:]
    else:
        aux_ref, o_ref, scratch = None, refs[0], refs[1:]

    def finish(acc):
        if mode == "headnorm":
            g = aux_ref[...]
            for c in range(acc.shape[1] // HEAD_DIM):
                sl = slice(c * HEAD_DIM, (c + 1) * HEAD_DIM)
                y = acc[:, sl]
                ms = jnp.mean(y * y, axis=-1, keepdims=True)
                o_ref[:, sl] = (y * lax.rsqrt(ms + EPS) * g).astype(o_ref.dtype)
        elif mode == "residual":
            o_ref[...] = aux_ref[...] + acc
        else:
            o_ref[...] = acc.astype(o_ref.dtype)

    if two_x:
        k1 = x_ref.shape[1]
        finish(_dot(x_ref[...], w_ref[:k1, :]) + _dot(x2_ref[...], w_ref[k1:, :]))
    elif nk == 1:
        finish(_dot(x_ref[...], w_ref[...]))
    else:
        acc_ref = scratch[0]
        k = pl.program_id(2)

        @pl.when(k == 0)
        def _():
            acc_ref[...] = jnp.zeros_like(acc_ref)

        acc_ref[...] += _dot(x_ref[...], w_ref[...])

        @pl.when(k == nk - 1)
        def _():
            finish(acc_ref[...])


def _matmul(x, w, layer, *, x2=None, col0=0, n=None, out_dtype=F32, mode="plain", aux=None,
            tm=1024, tn=512, tk=None, name="matmul"):
    m, kdim = x.shape
    if x2 is not None:
        assert tk is None
        kdim += x2.shape[1]
    n = w.shape[2] if n is None else n
    tm, tn = _tile(m, tm), _tile(n, tn)
    tk = kdim if tk is None else _tile(kdim, tk)
    nk = kdim // tk
    cb = col0 // tn
    assert col0 % tn == 0
    in_specs = [pl.BlockSpec((tm, x.shape[1] if x2 is not None else tk), lambda i, j, k: (i, k)),
                pl.BlockSpec((None, tk, tn), lambda i, j, k: (layer, k, j + cb))]
    args = [x, w]
    if x2 is not None:
        in_specs.append(pl.BlockSpec((tm, x2.shape[1]), lambda i, j, k: (i, 0)))
        args.append(x2)
    if mode == "headnorm":
        in_specs.append(pl.BlockSpec((1, HEAD_DIM), lambda i, j, k: (0, 0)))
        args.append(aux.reshape(1, HEAD_DIM))
    elif mode == "residual":
        in_specs.append(pl.BlockSpec((tm, tn), lambda i, j, k: (i, j)))
        args.append(aux)
    scratch = [pltpu.VMEM((tm, tn), F32)] if nk > 1 else []
    return pl.pallas_call(
        functools.partial(_mm_kernel, nk=nk, mode=mode, two_x=x2 is not None),
        grid=(m // tm, n // tn, nk),
        in_specs=in_specs,
        out_specs=pl.BlockSpec((tm, tn), lambda i, j, k: (i, j)),
        out_shape=jax.ShapeDtypeStruct((m, n), out_dtype),
        scratch_shapes=scratch,
        compiler_params=_params("parallel", "parallel", "arbitrary"),
        name=name,
    )(*args)


def _band_head(q, ka, kb, va, vb, bias_a, bias_b, past_visible):
    sa = _dot_nt(q, ka.astype(BF16)) * ATTN_SCALE + bias_a
    sb = _dot_nt(q, kb.astype(BF16)) * ATTN_SCALE + bias_b
    if past_visible is not None:
        sa = jnp.where(past_visible, sa, NEG_INF)
    m = jnp.maximum(jnp.max(sa, axis=-1, keepdims=True), jnp.max(sb, axis=-1, keepdims=True))
    pa = jnp.exp(sa - m)
    pb = jnp.exp(sb - m)
    l = jnp.sum(pa, axis=-1, keepdims=True) + jnp.sum(pb, axis=-1, keepdims=True)
    o = _dot(pa.astype(BF16), va.astype(BF16)) + _dot(pb.astype(BF16), vb.astype(BF16))
    return o / l


def _band_prompt_kernel(q_ref, ka_ref, kb_ref, va_ref, vb_ref, bias_ref, o_ref):
    na = ka_ref.shape[0]
    o = _band_head(q_ref[...], ka_ref[...], kb_ref[...], va_ref[...], vb_ref[...],
                   bias_ref[:, :na], bias_ref[:, na:], pl.program_id(1) > 0)
    o_ref[...] = o.astype(o_ref.dtype)


def _band_sample_kernel(q_ref, kn_ref, vn_ref, kc_ref, vc_ref, bias_ref, o_ref):
    win, heads = kc_ref.shape[0], kc_ref.shape[1]
    kc = pltpu.einshape("phd->hpd", kc_ref[...])
    vc = pltpu.einshape("phd->hpd", vc_ref[...])
    for hh in range(heads):
        sl = slice(hh * HEAD_DIM, (hh + 1) * HEAD_DIM)
        o = _band_head(q_ref[:, sl], kc[hh], kn_ref[:, sl], vc[hh], vn_ref[:, sl],
                       bias_ref[hh, :, :win], bias_ref[hh, :, win:], None)
        o_ref[:, sl] = o.astype(o_ref.dtype)


def _band_bias(rel_bias, q_pos, k_pos):
    lq, lk = q_pos.shape[0], k_pos.shape[0]
    d_min = q_pos[0] - k_pos[lk - 1]
    diag = jnp.clip(d_min + jnp.arange(lq + lk - 1, dtype=jnp.int32), -REL_CLIP, REL_CLIP) + REL_CLIP
    g = rel_bias.astype(F32)[:, diag]
    period = lq + lk
    u = jnp.pad(g[:, ::-1], ((0, 0), (0, 1)))
    skew = jnp.tile(u, (1, lq))[:, :lq * (period - 1)].reshape(-1, lq, period - 1)
    bias = skew[:, :, lq - 1:lq - 1 + lk]
    qc = q_pos[:, None] // CHUNK
    kc = k_pos[None, :] // CHUNK
    mask = (kc <= qc) & (kc >= qc - A_PAST_CHUNKS)
    return jnp.where(mask[None], bias, NEG_INF)


def _band_prompt(q, k, v, rel_bias, sp):
    heads = rel_bias.shape[0]
    tq = A_PAST_CHUNKS * CHUNK
    assert sp % tq == 0
    pos = jnp.arange(tq, dtype=jnp.int32)
    bias = _band_bias(rel_bias, tq + pos, jnp.arange(2 * tq, dtype=jnp.int32))
    prev = lambda h, i: (jnp.maximum(i - 1, 0), h)
    cur = lambda h, i: (i, h)
    blk = lambda im: pl.BlockSpec((tq, HEAD_DIM), im)
    return pl.pallas_call(
        _band_prompt_kernel,
        grid=(heads, sp // tq),
        in_specs=[blk(cur), blk(prev), blk(cur), blk(prev), blk(cur),
                  pl.BlockSpec((None, tq, 2 * tq), lambda h, i: (h, 0, 0))],
        out_specs=blk(cur),
        out_shape=jax.ShapeDtypeStruct((q.shape[0], heads * HEAD_DIM), BF16),
        compiler_params=_params("parallel", "arbitrary"),
        name="band_prompt",
    )(q, k, k, v, v, bias)


def _band_sample(q, k, v, cache_k, cache_v, layer, rel_bias, sp, ls, past_len, out):
    _, bs, win, heads, _ = cache_k.shape
    width = heads * HEAD_DIM
    assert sp % ls == 0
    row0 = sp // ls
    q_pos = past_len + jnp.arange(ls, dtype=jnp.int32)
    k_pos = past_len - win + jnp.arange(win + ls, dtype=jnp.int32)
    bias = _band_bias(rel_bias, q_pos, k_pos)
    blk_new = pl.BlockSpec((ls, width), lambda b: (row0 + b, 0))
    blk_old = pl.BlockSpec((None, None, win, heads, HEAD_DIM), lambda b: (layer, b, 0, 0, 0))
    return pl.pallas_call(
        _ignore_input(_band_sample_kernel, 6),
        grid=(bs,),
        in_specs=[blk_new, blk_new, blk_new, blk_old, blk_old,
                  pl.BlockSpec((heads, ls, win + ls), lambda b: (0, 0, 0)),
                  pl.BlockSpec(memory_space=pl.ANY)],
        out_specs=blk_new,
        out_shape=jax.ShapeDtypeStruct(out.shape, out.dtype),
        input_output_aliases={6: 0},
        compiler_params=_params("parallel"),
        name="band_sample",
    )(q, k, v, cache_k, cache_v, bias, out)


def _pool_kernel(cur_ref, hist_ref, w_ref, sc_ref, o_ref, ext_ref, *, pos_base, pos_step):
    tr = cur_ref.shape[0]
    ext_ref[0:POOL_HIST_PAD, :] = hist_ref[...]
    ext_ref[POOL_HIST_PAD:POOL_HIST_PAD + tr, :] = cur_ref[...]
    pos = pos_base + pl.program_id(0) * pos_step + lax.broadcasted_iota(jnp.int32, (tr, 1), 0)
    for g, w in enumerate(POOL_WINDOWS):
        sl = slice(g * B_GROUP_WIDTH, (g + 1) * B_GROUP_WIDTH)
        cur = cur_ref[:, sl]
        tot = cur
        for j in range(1, w):
            tot = tot + ext_ref[POOL_HIST_PAD - j:POOL_HIST_PAD - j + tr, sl]
        cnt = jnp.minimum(pos + 1, w).astype(F32)
        d = tot / cnt - cur
        y = _dot(d.astype(BF16), w_ref[g]) * sc_ref[:, sl]
        o_ref[:, sl] = y.astype(o_ref.dtype)


def _pool(p, hist, pool_w, pool_scale, *, row_block0, nblk, tr, pos_base, pos_step, name, out=None):
    width = p.shape[1]
    rows = pl.BlockSpec((tr, width), lambda i: (row_block0 + i, 0))
    body = functools.partial(_pool_kernel, pos_base=pos_base, pos_step=pos_step)
    in_specs = [rows,
                pl.BlockSpec((None, POOL_HIST_PAD, width), lambda i: (i, 0, 0)),
                pl.BlockSpec(pool_w.shape, lambda i: (0, 0, 0)),
                pl.BlockSpec((1, width), lambda i: (0, 0))]
    args = [p, hist, pool_w, pool_scale.reshape(1, width)]
    aliases = {}
    if out is not None:
        body = _ignore_input(body, len(args))
        aliases = {len(args): 0}
        in_specs.append(pl.BlockSpec(memory_space=pl.ANY))
        args.append(out)
    return pl.pallas_call(
        body,
        grid=(nblk,),
        in_specs=in_specs,
        out_specs=rows,
        out_shape=jax.ShapeDtypeStruct(p.shape, BF16),
        input_output_aliases=aliases,
        scratch_shapes=[pltpu.VMEM((POOL_HIST_PAD + tr, width), F32)],
        compiler_params=_params("parallel"),
        name=name,
    )(*args)


def _sb_steps(qs, kvs, tri, carries, accs, visibles):
    nk = kvs[0][0].shape[0]
    zs = [_dot_nt(q, k) * ATTN_SCALE for q, (k, _) in zip(qs, kvs)]
    log_keeps = [jnp.where(vis, -(jnp.maximum(z, 0.0) + jnp.log1p(jnp.exp(-jnp.abs(z)))), 0.0)
                 for z, vis in zip(zs, visibles)]
    his = [lk.astype(BF16) for lk in log_keeps]
    los = [(lk - hi.astype(F32)).astype(BF16) for lk, hi in zip(log_keeps, his)]
    sums = [_dot(hi, tri) + _dot(lo, tri) for hi, lo in zip(his, los)]
    weights = [jnp.where(vis, jnp.exp(z + lk + s[:, :nk] + carry[:, :nk]), 0.0).astype(BF16)
               for z, lk, s, carry, vis in zip(zs, log_keeps, sums, carries, visibles)]
    new_accs = [acc + _dot(a, v) for acc, a, (_, v) in zip(accs, weights, kvs)]
    new_carries = [carry + s[:, nk:] for carry, s in zip(carries, sums)]
    return new_carries, new_accs


def _sb_sweep(chains, load_kv, tri, carries, accs):
    rows = chains[0][0].shape[0]
    lane = lax.broadcasted_iota(jnp.int32, (rows, SB_KEYS), 1)
    beyond = jnp.int32(2 ** 30)

    def cond(st):
        return st[1] == 0

    def body(st):
        n, _, carries, accs = st
        kbs = [kb0 - n for _, kb0, _ in chains]
        starts = [pl.multiple_of(jnp.maximum(kb, 0) * SB_KEYS, SB_KEYS) for kb in kbs]
        visibles = [(jnp.where(kb >= 0, start, beyond) + lane) < qpos
                    for (_, _, qpos), kb, start in zip(chains, kbs, starts)]
        carries, accs = _sb_steps([q for q, _, _ in chains], load_kv(starts), tri, carries, accs, visibles)
        pending = jnp.full((rows, HEAD_DIM), -jnp.inf, F32)
        for kb, carry in zip(kbs, carries):
            pending = jnp.maximum(pending, jnp.where(kb > 0, carry, -jnp.inf))
        done = (jnp.max(pending) < SB_EXIT).astype(jnp.int32)
        return n + 1, done, carries, accs

    return lax.while_loop(cond, body, (jnp.int32(0), jnp.int32(0), list(carries), list(accs)))[3]


def _sb_prompt_kernel(q_ref, k_ref, v_ref, tri_ref, o_ref, *, tq):
    n_chains = q_ref.shape[0] // tq
    row0 = pl.program_id(1) * q_ref.shape[0]
    row = lax.broadcasted_iota(jnp.int32, (tq, SB_KEYS), 0)

    def load_kv(starts):
        return [(k_ref[pl.ds(s, SB_KEYS), :].astype(BF16), v_ref[pl.ds(s, SB_KEYS), :].astype(BF16))
                for s in starts]

    chains = [(q_ref[c * tq:(c + 1) * tq, :], (row0 + (c + 1) * tq) // SB_KEYS - 1, row0 + c * tq + row)
              for c in range(n_chains)]
    zeros = [jnp.zeros((tq, HEAD_DIM), F32)] * n_chains
    accs = _sb_sweep(chains, load_kv, tri_ref[...], zeros, zeros)
    for c in range(n_chains):
        o_ref[c * tq:(c + 1) * tq, :] = accs[c].astype(o_ref.dtype)


def _sb_sample_kernel(q_ref, kn_ref, vn_ref, kc_ref, vc_ref, trin_ref, tri_ref, o_ref):
    ls = q_ref.shape[0]
    past, heads = kc_ref.shape[0], kc_ref.shape[1]
    causal = lax.broadcasted_iota(jnp.int32, (ls, ls), 1) < lax.broadcasted_iota(jnp.int32, (ls, ls), 0)
    all_rows = jnp.full((ls, SB_KEYS), 2 ** 30 - SB_KEYS, jnp.int32)
    zeros = jnp.zeros((ls, HEAD_DIM), F32)
    head_cols = [slice(hh * HEAD_DIM, (hh + 1) * HEAD_DIM) for hh in range(heads)]
    chains = [(q_ref[:, sl], past // SB_KEYS - 1, all_rows) for sl in head_cols]
    carries, accs = _sb_steps([q for q, _, _ in chains],
                              [(kn_ref[:, sl].astype(BF16), vn_ref[:, sl].astype(BF16)) for sl in head_cols],
                              trin_ref[...], [zeros] * heads, [zeros] * heads, [causal] * heads)

    def load_kv(starts):
        k = pltpu.einshape("phd->hpd", kc_ref[pl.ds(starts[0], SB_KEYS), :, :])
        v = pltpu.einshape("phd->hpd", vc_ref[pl.ds(starts[0], SB_KEYS), :, :])
        return [(k[hh].astype(BF16), v[hh].astype(BF16)) for hh in range(heads)]

    accs = _sb_sweep(chains, load_kv, tri_ref[...], carries, accs)
    for hh in range(heads):
        o_ref[:, hh * HEAD_DIM:(hh + 1) * HEAD_DIM] = accs[hh].astype(o_ref.dtype)


def _sb_tri(nk):
    j = jnp.arange(nk)[:, None]
    s = jnp.arange(nk + HEAD_DIM)[None, :]
    return ((s >= nk) | (j > s)).astype(BF16)


def _sb_prompt(q, k, v, sp):
    heads = q.shape[1] // HEAD_DIM
    tq = SB_KEYS
    tb = _tile(sp, 8 * tq)
    assert sp % SB_KEYS == 0 and tb % tq == 0
    return pl.pallas_call(
        functools.partial(_sb_prompt_kernel, tq=tq),
        grid=(heads, sp // tb),
        in_specs=[pl.BlockSpec((tb, HEAD_DIM), lambda h, i: (i, h)),
                  pl.BlockSpec((sp, HEAD_DIM), lambda h, i: (0, h)),
                  pl.BlockSpec((sp, HEAD_DIM), lambda h, i: (0, h)),
                  pl.BlockSpec((SB_KEYS, SB_KEYS + HEAD_DIM), lambda h, i: (0, 0))],
        out_specs=pl.BlockSpec((tb, HEAD_DIM), lambda h, i: (i, h)),
        out_shape=jax.ShapeDtypeStruct(q.shape, BF16),
        compiler_params=_params("parallel", "arbitrary"),
        name="stickbreak_prompt",
    )(q, k, v, _sb_tri(SB_KEYS))


def _sb_sample(q, k, v, cache_k, cache_v, layer, sp, ls, out):
    n_layers, bs, past, heads, _ = cache_k.shape
    hg = 8
    assert past % SB_KEYS == 0 and past >= SB_KEYS and sp % ls == 0 and heads % hg == 0
    row0 = sp // ls
    gw = hg * HEAD_DIM
    blk_new = pl.BlockSpec((ls, gw), lambda b, g: (row0 + b, g))
    blk_old = pl.BlockSpec((None, None, past, None, hg, HEAD_DIM), lambda b, g: (layer, b, 0, g, 0, 0))
    grouped = (n_layers, bs, past, heads // hg, hg, HEAD_DIM)
    return pl.pallas_call(
        _ignore_input(_sb_sample_kernel, 7),
        grid=(bs, heads // hg),
        in_specs=[blk_new, blk_new, blk_new, blk_old, blk_old,
                  pl.BlockSpec((ls, ls + HEAD_DIM), lambda b, g: (0, 0)),
                  pl.BlockSpec((SB_KEYS, SB_KEYS + HEAD_DIM), lambda b, g: (0, 0)),
                  pl.BlockSpec(memory_space=pl.ANY)],
        out_specs=blk_new,
        out_shape=jax.ShapeDtypeStruct(out.shape, out.dtype),
        input_output_aliases={7: 0},
        compiler_params=_params("parallel", "arbitrary"),
        name="stickbreak_sample",
    )(q, k, v, cache_k.reshape(grouped), cache_v.reshape(grouped), _sb_tri(ls), _sb_tri(SB_KEYS), out)


def _peer_scores_kernel(key_ref, q_ref, o_ref):
    o_ref[...] = _dot_nt(key_ref[...], q_ref[...])


def _peer_scores(pq, sub_keys, layer):
    t = pq.shape[0]
    _, nhc, nkeys, half = sub_keys.shape
    tt = _tile(t, 1024)
    return pl.pallas_call(
        _peer_scores_kernel,
        grid=(nhc, t // tt),
        in_specs=[pl.BlockSpec((None, None, nkeys, half), lambda c, i: (layer, c, 0, 0)),
                  pl.BlockSpec((tt, half), lambda c, i: (i, c))],
        out_specs=pl.BlockSpec((None, nkeys, tt), lambda c, i: (c, 0, i)),
        out_shape=jax.ShapeDtypeStruct((nhc, nkeys, t), F32),
        compiler_params=_params("parallel", "parallel"),
        name="peer_scores",
    )(sub_keys, pq)


def _extract_top(s, k, tie_safe):
    rows = lax.broadcasted_iota(jnp.int32, s.shape, 0)
    rank = jnp.full(s.shape, float(k), F32)
    vals = []
    for r in range(k):
        m = jnp.max(s, axis=0, keepdims=True)
        vals.append(m)
        hit = s == m
        if tie_safe:
            hit = rows == jnp.min(jnp.where(hit, rows, s.shape[0]), axis=0, keepdims=True)
        rank = jnp.where(hit, float(r), rank)
        s = jnp.where(hit, -jnp.inf, s)
    return vals, rank, s


def _peer_select_kernel(s_ref, m1x_ref, e1x_ref, rank2_ref, e2_ref, *, ne):
    def select(h, tie_safe):
        s1 = s_ref[2 * h]
        s2 = s_ref[2 * h + 1]
        a, rank1, _ = _extract_top(s1, PEER_TOPK, tie_safe)
        b, rank2, _ = _extract_top(s2, PEER_TOPK, tie_safe)
        b = jnp.concatenate(b, axis=0)
        sub = lax.broadcasted_iota(jnp.int32, (8, b.shape[1]), 0)
        cand = [a[0] + b, a[1] + b[:8]]
        for i in range(2, 8):
            cand.append(jnp.where(sub < PEER_TOPK // (i + 1), a[i] + b[:8], -jnp.inf))
        cand.append(jnp.concatenate(a[8:], axis=0) + b[0:1])
        cand = jnp.concatenate(cand, axis=0)
        g, _, left = _extract_top(cand, PEER_TOPK, tie_safe)
        z = jnp.ones_like(g[0])
        for gk in g[1:]:
            z = z + jnp.exp(gk - g[0])
        took = (left != cand).astype(F32)
        taken = [jnp.sum(took[0:16], axis=0, keepdims=True), jnp.sum(took[16:24], axis=0, keepdims=True)]
        taken += [jnp.sum(took[8 * i + 8:8 * i + 16], axis=0, keepdims=True) for i in range(2, 8)]
        taken += [took[72 + i:73 + i] for i in range(8)]
        m1 = jnp.zeros_like(s1)
        for i in range(PEER_TOPK):
            m1 = jnp.where(rank1 == float(i), taken[i], m1)
        e1 = jnp.exp(s1 - a[0]) / z
        rank2_ref[h] = rank2.astype(rank2_ref.dtype)
        e2_ref[h] = jnp.exp(s2 - b[0:1]).astype(e2_ref.dtype)
        row = pl.multiple_of(h * ne, ne)
        for j in range(N_KEYS // ne):
            m1x_ref[j, pl.ds(row, ne), :] = m1[j * ne:(j + 1) * ne]
            e1x_ref[j, pl.ds(row, ne), :] = e1[j * ne:(j + 1) * ne]
        removed = (jnp.sum((rank1 < PEER_TOPK).astype(F32), axis=0, keepdims=True)
                   + jnp.sum((rank2 < PEER_TOPK).astype(F32), axis=0, keepdims=True)
                   + jnp.sum(took, axis=0, keepdims=True))
        return jnp.max(removed) - 3.0 * PEER_TOPK

    def one_head(h, _):
        extra = select(h, tie_safe=False)

        @pl.when(extra > 0.0)
        def _():
            select(h, tie_safe=True)

        return 0

    lax.fori_loop(0, PEER_HEADS, one_head, 0)


def _peer_select(scores, ne):
    nhc, nkeys, t = scores.shape
    tt = _tile(t, 256)
    x_shape = jax.ShapeDtypeStruct((nkeys // ne, PEER_HEADS * ne, t), F32)
    x_spec = pl.BlockSpec((nkeys // ne, PEER_HEADS * ne, tt), lambda i: (0, 0, i))
    y_shape = jax.ShapeDtypeStruct((PEER_HEADS, nkeys, t), BF16)
    y_spec = pl.BlockSpec((PEER_HEADS, nkeys, tt), lambda i: (0, 0, i))
    return pl.pallas_call(
        functools.partial(_peer_select_kernel, ne=ne),
        grid=(t // tt,),
        in_specs=[pl.BlockSpec((nhc, nkeys, tt), lambda i: (0, 0, i))],
        out_specs=[x_spec, x_spec, y_spec, y_spec],
        out_shape=[x_shape, x_shape, y_shape, y_shape],
        compiler_params=_params("parallel"),
        name="peer_select",
    )(scores)


def _peer_act_kernel(*refs, ne, n_parts):
    u_ref, xt_ref = refs[0], refs[1]
    part_refs = refs[2:2 + 4 * n_parts]
    o_ref, gate_ref = refs[2 + 4 * n_parts], refs[3 + 4 * n_parts]
    first_row = (pl.program_id(1) % (ROW_BLOCK // ne)) * ne
    for part in range(n_parts):
        m1x_ref, e1x_ref, rank2_ref, e2_ref = part_refs[4 * part:4 * part + 4]
        tp = rank2_ref.shape[2]
        for e in range(ne):
            gate = jnp.zeros((N_KEYS, tp), BF16)
            for h in range(PEER_HEADS):
                r = pl.ds(h * ROW_BLOCK + first_row + e, 1)
                hit = rank2_ref[h] < m1x_ref[r, :].astype(BF16)
                w = e2_ref[h] * e1x_ref[r, :].astype(BF16)
                gate = gate + jnp.where(hit, w, jnp.zeros_like(w))
            gate_ref[e * N_KEYS:(e + 1) * N_KEYS, part * tp:(part + 1) * tp] = gate
    xt = xt_ref[...]
    for p in range(ne // 2):
        rows = slice(2 * p * N_KEYS, 2 * (p + 1) * N_KEYS)
        ht = _dot(u_ref[rows, :].astype(BF16), xt)
        gelu = 0.5 * ht * (1.0 + lax.erf(ht * (0.5 ** 0.5)))
        act_t = gate_ref[rows, :] * gelu.astype(BF16)
        o_ref[:, rows] = act_t.T


def _peer_act(xn, u, layer, m1x, e1x, rank2, e2, ne):
    t, d = xn.shape
    n_exp = u.shape[1]
    tp = _tile(t, 512)
    n_parts = 2 if t % (2 * tp) == 0 else 1
    tt = n_parts * tp
    te = ne * N_KEYS
    part_specs, part_args = [], []
    for part in range(n_parts):
        row_spec = pl.BlockSpec((None, PEER_HEADS * ROW_BLOCK, tp),
                                lambda i, j, part=part: (j // (ROW_BLOCK // ne), 0, n_parts * i + part))
        tile_spec = pl.BlockSpec((PEER_HEADS, N_KEYS, tp), lambda i, j, part=part: (0, 0, n_parts * i + part))
        part_specs += [row_spec, row_spec, tile_spec, tile_spec]
        part_args += [m1x, e1x, rank2, e2]
    return pl.pallas_call(
        functools.partial(_peer_act_kernel, ne=ne, n_parts=n_parts),
        grid=(t // tt, n_exp // te),
        in_specs=[pl.BlockSpec((None, te, d), lambda i, j: (layer, j, 0)),
                  pl.BlockSpec((d, tt), lambda i, j: (0, i))] + part_specs,
        out_specs=pl.BlockSpec((tt, te), lambda i, j: (i, j)),
        out_shape=jax.ShapeDtypeStruct((t, n_exp), BF16),
        scratch_shapes=[pltpu.VMEM((te, tt), BF16)],
        compiler_params=_params("parallel", "arbitrary"),
        name="peer_act",
    )(u, xn.T, *part_args)


def _peer_ffn(h, layer, norm_g, w_q, sub_keys, u, v):
    ne = 4
    xn = _rmsnorm(h, norm_g)
    pq = _matmul(xn, w_q, layer, out_dtype=BF16, name="peer_query")
    scores = _peer_scores(pq, sub_keys, layer)
    m1x, e1x, rank2, e2 = _peer_select(scores, ROW_BLOCK)
    act = _peer_act(xn, u, layer, m1x, e1x, rank2, e2, ne)
    return _matmul(act, v, layer, mode="residual", aux=h, tm=1024, tn=1024, tk=2048, name="peer_down")


def kernel(x_prompt, x_sample, cache_a_k, cache_a_v, state_b_pool, cache_c_k, cache_c_v, norm_mix, norm_ffn, ab_w_in, ab_q_gain, ab_k_gain, ab_rel_bias, ab_pool_w, ab_pool_scale, ab_w_out, c_w_in, c_w_out, peer_w_q, peer_sub_keys, peer_u, peer_v):
    bp, sp, d = x_prompt.shape
    bs, ls, _ = x_sample.shape
    assert bp == 1
    depth = norm_mix.shape[0]
    past_len = cache_c_k.shape[2]
    a_heads = ab_rel_bias.shape[1]
    a_width = a_heads * HEAD_DIM
    b_width = ab_pool_scale.shape[1]
    c_width = c_w_out.shape[1]
    win_p = min(A_PAST_CHUNKS * CHUNK, sp)
    n_s = bs * ls

    h = jnp.concatenate([x_prompt.reshape(sp, d), x_sample.reshape(n_s, d)], axis=0)

    ab_w_in_b, ab_w_out_b = ab_w_in.astype(BF16), ab_w_out.astype(BF16)
    c_w_in_b, c_w_out_b = c_w_in.astype(BF16), c_w_out.astype(BF16)
    peer_w_q_b, peer_v_b = peer_w_q.astype(BF16), peer_v.astype(BF16)
    sub_keys_b = peer_sub_keys.astype(BF16).reshape(depth, PEER_HEADS * 2, N_KEYS, -1)

    outs = {name: [] for name in ("a_k_p", "a_v_p", "b_p", "c_k_p", "c_v_p",
                                  "a_k_s", "a_v_s", "b_s", "c_k_s", "c_v_s")}
    for layer in range(depth):
        xn = _rmsnorm(h, norm_mix[layer])
        if layer % 2 == 0:
            i = layer // 2
            q = _matmul(xn, ab_w_in_b, i, col0=0, n=a_width, out_dtype=BF16, mode="headnorm",
                        aux=ab_q_gain[i], name="a_q")
            k = _matmul(xn, ab_w_in_b, i, col0=a_width, n=a_width, mode="headnorm", aux=ab_k_gain[i], name="a_k")
            v = _matmul(xn, ab_w_in_b, i, col0=2 * a_width, n=a_width, name="a_v")
            p = _matmul(xn, ab_w_in_b, i, col0=3 * a_width, n=b_width, name="b_in")

            att = _band_prompt(q, k, v, ab_rel_bias[i], sp)
            att = _band_sample(q, k, v, cache_a_k, cache_a_v, i, ab_rel_bias[i], sp, ls, past_len, att)

            pool_w = ab_pool_w[i].astype(BF16)
            tr = _tile(sp, 256)
            pp = p[:sp].reshape(sp // tr, tr, b_width)
            hist_p = jnp.concatenate([jnp.zeros((1, POOL_HIST_PAD, b_width), F32),
                                      pp[:-1, tr - POOL_HIST_PAD:]], axis=0)
            pool = _pool(p, hist_p, pool_w, ab_pool_scale[i], row_block0=0, nblk=sp // tr, tr=tr,
                         pos_base=0, pos_step=tr, name="pool_prompt")
            hist_s = jnp.pad(state_b_pool[i], ((0, 0), (POOL_HIST_PAD - POOL_HIST, 0), (0, 0)))
            pool = _pool(p, hist_s, pool_w, ab_pool_scale[i], row_block0=sp // ls, nblk=bs, tr=ls,
                         pos_base=past_len, pos_step=0, name="pool_sample", out=pool)

            h = _matmul(att, ab_w_out_b, i, x2=pool, mode="residual", aux=h, name="a_out")

            ps = p[sp:].reshape(bs, ls, b_width)
            outs["a_k_p"].append(k[sp - win_p:sp].reshape(bp, win_p, a_heads, HEAD_DIM))
            outs["a_v_p"].append(v[sp - win_p:sp].reshape(bp, win_p, a_heads, HEAD_DIM))
            outs["b_p"].append(p[sp - POOL_HIST:sp].reshape(bp, POOL_HIST, b_width))
            outs["a_k_s"].append(k[sp:].reshape(bs, ls, a_heads, HEAD_DIM))
            outs["a_v_s"].append(v[sp:].reshape(bs, ls, a_heads, HEAD_DIM))
            outs["b_s"].append(jnp.concatenate([state_b_pool[i], ps], axis=1)[:, ls:])
        else:
            j = layer // 2
            c_heads = c_width // HEAD_DIM
            q = _matmul(xn, c_w_in_b, j, col0=0, n=c_width, out_dtype=BF16, name="c_q")
            k = _matmul(xn, c_w_in_b, j, col0=c_width, n=c_width, name="c_k")
            v = _matmul(xn, c_w_in_b, j, col0=2 * c_width, n=c_width, name="c_v")
            att = _sb_prompt(q, k, v, sp)
            att = _sb_sample(q, k, v, cache_c_k, cache_c_v, j, sp, ls, att)
            h = _matmul(att, c_w_out_b, j, mode="residual", aux=h, name="c_out")
            outs["c_k_p"].append(k[:sp].reshape(bp, sp, c_heads, HEAD_DIM))
            outs["c_v_p"].append(v[:sp].reshape(bp, sp, c_heads, HEAD_DIM))
            outs["c_k_s"].append(k[sp:].reshape(bs, ls, c_heads, HEAD_DIM))
            outs["c_v_s"].append(v[sp:].reshape(bs, ls, c_heads, HEAD_DIM))

        h = _peer_ffn(h, layer, norm_ffn[layer], peer_w_q_b, sub_keys_b, peer_u, peer_v_b)

    st = {name: jnp.stack(vals) for name, vals in outs.items()}
    return (h[:sp].reshape(bp, sp, d), h[sp:].reshape(bs, ls, d),
            st["a_k_p"], st["a_v_p"], st["b_p"], st["c_k_p"], st["c_v_p"],
            st["a_k_s"], st["a_v_s"], st["b_s"], st["c_k_s"], st["c_v_s"])
```

```python
import functools

import jax
import jax.numpy as jnp
from jax import lax
from jax.experimental import pallas as pl
from jax.experimental.pallas import tpu as pltpu

F32 = jnp.float32
BF16 = jnp.bfloat16

HEAD_DIM = 128
CHUNK = 64
A_PAST_CHUNKS = 8
REL_CLIP = 128
POOL_WINDOWS = (2, 4, 8, 16)
POOL_HIST = max(POOL_WINDOWS) - 1
POOL_HIST_PAD = 16
B_GROUP_WIDTH = 512
PEER_HEADS = 8
N_KEYS = 128
PEER_TOPK = 16
ROW_BLOCK = 8
EPS = 1e-6
NEG_INF = -1e30
ATTN_SCALE = HEAD_DIM ** -0.5
SB_EXIT = -104.0
SB_KEYS = 128
VMEM_LIMIT_BYTES = 56 * 1024 * 1024


def _params(*sem):
    return pltpu.CompilerParams(dimension_semantics=sem, vmem_limit_bytes=VMEM_LIMIT_BYTES)


def _tile(dim, pref):
    return pref if dim % pref == 0 else dim


def _ignore_input(body, index):
    def wrapped(*refs):
        return body(*refs[:index], *refs[index + 1:])
    return wrapped


def _dot(a, b):
    return jnp.dot(a, b, preferred_element_type=F32)


def _dot_nt(a, b):
    return lax.dot_general(a, b, (((1,), (1,)), ((), ())), preferred_element_type=F32)


def _rmsnorm_kernel(x_ref, g_ref, o_ref):
    x = x_ref[...]
    ms = jnp.mean(x * x, axis=-1, keepdims=True)
    o_ref[...] = (x * lax.rsqrt(ms + EPS) * g_ref[...]).astype(o_ref.dtype)


def _rmsnorm(x, g):
    t, d = x.shape
    tr = _tile(t, 256)
    return pl.pallas_call(
        _rmsnorm_kernel,
        grid=(t // tr,),
        in_specs=[pl.BlockSpec((tr, d), lambda i: (i, 0)),
                  pl.BlockSpec((1, d), lambda i: (0, 0))],
        out_specs=pl.BlockSpec((tr, d), lambda i: (i, 0)),
        out_shape=jax.ShapeDtypeStruct((t, d), BF16),
        compiler_params=_params("parallel"),
        name="rmsnorm",
    )(x, g.reshape(1, d))


def _mm_kernel(*refs, nk, mode, two_x):
    x_ref, w_ref = refs[0], refs[1]
    refs = refs[2:]
    if two_x:
        x2_ref, refs = refs[0], refs[1:]
    if mode in ("headnorm", "residual"):
        aux_ref, o_ref, scratch = refs[0], refs[1], refs[2:]
    else:
        aux_ref, o_ref, scratch = None, refs[0], refs[1:]

    def finish(acc):
        if mode == "headnorm":
            g = aux_ref[...]
            for c in range(acc.shape[1] // HEAD_DIM):
                sl = slice(c * HEAD_DIM, (c + 1) * HEAD_DIM)
                y = acc[:, sl]
                ms = jnp.mean(y * y, axis=-1, keepdims=True)
                o_ref[:, sl] = (y * lax.rsqrt(ms + EPS) * g).astype(o_ref.dtype)
        elif mode == "residual":
            o_ref[...] = aux_ref[...] + acc
        else:
            o_ref[...] = acc.astype(o_ref.dtype)

    if two_x:
        k1 = x_ref.shape[1]
        finish(_dot(x_ref[...], w_ref[:k1, :]) + _dot(x2_ref[...], w_ref[k1:, :]))
    elif nk == 1:
        finish(_dot(x_ref[...], w_ref[...]))
    else:
        acc_ref = scratch[0]
        k = pl.program_id(2)

        @pl.when(k == 0)
        def _():
            acc_ref[...] = jnp.zeros_like(acc_ref)

        acc_ref[...] += _dot(x_ref[...], w_ref[...])

        @pl.when(k == nk - 1)
        def _():
            finish(acc_ref[...])


def _matmul(x, w, layer, *, x2=None, col0=0, n=None, out_dtype=F32, mode="plain", aux=None,
            tm=1024, tn=512, tk=None, name="matmul"):
    m, kdim = x.shape
    if x2 is not None:
        assert tk is None
        kdim += x2.shape[1]
    n = w.shape[2] if n is None else n
    tm, tn = _tile(m, tm), _tile(n, tn)
    tk = kdim if tk is None else _tile(kdim, tk)
    nk = kdim // tk
    cb = col0 // tn
    assert col0 % tn == 0
    in_specs = [pl.BlockSpec((tm, x.shape[1] if x2 is not None else tk), lambda i, j, k: (i, k)),
                pl.BlockSpec((None, tk, tn), lambda i, j, k: (layer, k, j + cb))]
    args = [x, w]
    if x2 is not None:
        in_specs.append(pl.BlockSpec((tm, x2.shape[1]), lambda i, j, k: (i, 0)))
        args.append(x2)
    if mode == "headnorm":
        in_specs.append(pl.BlockSpec((1, HEAD_DIM), lambda i, j, k: (0, 0)))
        args.append(aux.reshape(1, HEAD_DIM))
    elif mode == "residual":
        in_specs.append(pl.BlockSpec((tm, tn), lambda i, j, k: (i, j)))
        args.append(aux)
    scratch = [pltpu.VMEM((tm, tn), F32)] if nk > 1 else []
    return pl.pallas_call(
        functools.partial(_mm_kernel, nk=nk, mode=mode, two_x=x2 is not None),
        grid=(m // tm, n // tn, nk),
        in_specs=in_specs,
        out_specs=pl.BlockSpec((tm, tn), lambda i, j, k: (i, j)),
        out_shape=jax.ShapeDtypeStruct((m, n), out_dtype),
        scratch_shapes=scratch,
        compiler_params=_params("parallel", "parallel", "arbitrary"),
        name=name,
    )(*args)


def _band_head(q, ka, kb, va, vb, bias_a, bias_b, past_visible):
    sa = _dot_nt(q, ka.astype(BF16)) * ATTN_SCALE + bias_a
    sb = _dot_nt(q, kb.astype(BF16)) * ATTN_SCALE + bias_b
    if past_visible is not None:
        sa = jnp.where(past_visible, sa, NEG_INF)
    m = jnp.maximum(jnp.max(sa, axis=-1, keepdims=True), jnp.max(sb, axis=-1, keepdims=True))
    pa = jnp.exp(sa - m)
    pb = jnp.exp(sb - m)
    l = jnp.sum(pa, axis=-1, keepdims=True) + jnp.sum(pb, axis=-1, keepdims=True)
    o = _dot(pa.astype(BF16), va.astype(BF16)) + _dot(pb.astype(BF16), vb.astype(BF16))
    return o / l


def _band_prompt_kernel(q_ref, ka_ref, kb_ref, va_ref, vb_ref, bias_ref, o_ref):
    na = ka_ref.shape[0]
    o = _band_head(q_ref[...], ka_ref[...], kb_ref[...], va_ref[...], vb_ref[...],
                   bias_ref[:, :na], bias_ref[:, na:], pl.program_id(1) > 0)
    o_ref[...] = o.astype(o_ref.dtype)


def _band_sample_kernel(q_ref, kn_ref, vn_ref, kc_ref, vc_ref, bias_ref, o_ref):
    win, heads = kc_ref.shape[0], kc_ref.shape[1]
    kc = pltpu.einshape("phd->hpd", kc_ref[...])
    vc = pltpu.einshape("phd->hpd", vc_ref[...])
    for hh in range(heads):
        sl = slice(hh * HEAD_DIM, (hh + 1) * HEAD_DIM)
        o = _band_head(q_ref[:, sl], kc[hh], kn_ref[:, sl], vc[hh], vn_ref[:, sl],
                       bias_ref[hh, :, :win], bias_ref[hh, :, win:], None)
        o_ref[:, sl] = o.astype(o_ref.dtype)


def _band_bias(rel_bias, q_pos, k_pos):
    lq, lk = q_pos.shape[0], k_pos.shape[0]
    d_min = q_pos[0] - k_pos[lk - 1]
    diag = jnp.clip(d_min + jnp.arange(lq + lk - 1, dtype=jnp.int32), -REL_CLIP, REL_CLIP) + REL_CLIP
    g = rel_bias.astype(F32)[:, diag]
    period = lq + lk
    u = jnp.pad(g[:, ::-1], ((0, 0), (0, 1)))
    skew = jnp.tile(u, (1, lq))[:, :lq * (period - 1)].reshape(-1, lq, period - 1)
    bias = skew[:, :, lq - 1:lq - 1 + lk]
    qc = q_pos[:, None] // CHUNK
    kc = k_pos[None, :] // CHUNK
    mask = (kc <= qc) & (kc >= qc - A_PAST_CHUNKS)
    return jnp.where(mask[None], bias, NEG_INF)


def _band_prompt(q, k, v, rel_bias, sp):
    heads = rel_bias.shape[0]
    tq = A_PAST_CHUNKS * CHUNK
    assert sp % tq == 0
    pos = jnp.arange(tq, dtype=jnp.int32)
    bias = _band_bias(rel_bias, tq + pos, jnp.arange(2 * tq, dtype=jnp.int32))
    prev = lambda h, i: (jnp.maximum(i - 1, 0), h)
    cur = lambda h, i: (i, h)
    blk = lambda im: pl.BlockSpec((tq, HEAD_DIM), im)
    return pl.pallas_call(
        _band_prompt_kernel,
        grid=(heads, sp // tq),
        in_specs=[blk(cur), blk(prev), blk(cur), blk(prev), blk(cur),
                  pl.BlockSpec((None, tq, 2 * tq), lambda h, i: (h, 0, 0))],
        out_specs=blk(cur),
        out_shape=jax.ShapeDtypeStruct((q.shape[0], heads * HEAD_DIM), BF16),
        compiler_params=_params("parallel", "arbitrary"),
        name="band_prompt",
    )(q, k, k, v, v, bias)


def _band_sample(q, k, v, cache_k, cache_v, layer, rel_bias, sp, ls, past_len, out):
    _, bs, win, heads, _ = cache_k.shape
    width = heads * HEAD_DIM
    assert sp % ls == 0
    row0 = sp // ls
    q_pos = past_len + jnp.arange(ls, dtype=jnp.int32)
    k_pos = past_len - win + jnp.arange(win + ls, dtype=jnp.int32)
    bias = _band_bias(rel_bias, q_pos, k_pos)
    blk_new = pl.BlockSpec((ls, width), lambda b: (row0 + b, 0))
    blk_old = pl.BlockSpec((None, None, win, heads, HEAD_DIM), lambda b: (layer, b, 0, 0, 0))
    return pl.pallas_call(
        _ignore_input(_band_sample_kernel, 6),
        grid=(bs,),
        in_specs=[blk_new, blk_new, blk_new, blk_old, blk_old,
                  pl.BlockSpec((heads, ls, win + ls), lambda b: (0, 0, 0)),
                  pl.BlockSpec(memory_space=pl.ANY)],
        out_specs=blk_new,
        out_shape=jax.ShapeDtypeStruct(out.shape, out.dtype),
        input_output_aliases={6: 0},
        compiler_params=_params("parallel"),
        name="band_sample",
    )(q, k, v, cache_k, cache_v, bias, out)


def _pool_kernel(cur_ref, hist_ref, w_ref, sc_ref, o_ref, ext_ref, *, pos_base, pos_step):
    tr = cur_ref.shape[0]
    ext_ref[0:POOL_HIST_PAD, :] = hist_ref[...]
    ext_ref[POOL_HIST_PAD:POOL_HIST_PAD + tr, :] = cur_ref[...]
    pos = pos_base + pl.program_id(0) * pos_step + lax.broadcasted_iota(jnp.int32, (tr, 1), 0)
    for g, w in enumerate(POOL_WINDOWS):
        sl = slice(g * B_GROUP_WIDTH, (g + 1) * B_GROUP_WIDTH)
        cur = cur_ref[:, sl]
        tot = cur
        for j in range(1, w):
            tot = tot + ext_ref[POOL_HIST_PAD - j:POOL_HIST_PAD - j + tr, sl]
        cnt = jnp.minimum(pos + 1, w).astype(F32)
        d = tot / cnt - cur
        y = _dot(d.astype(BF16), w_ref[g]) * sc_ref[:, sl]
        o_ref[:, sl] = y.astype(o_ref.dtype)


def _pool(p, hist, pool_w, pool_scale, *, row_block0, nblk, tr, pos_base, pos_step, name, out=None):
    width = p.shape[1]
    rows = pl.BlockSpec((tr, width), lambda i: (row_block0 + i, 0))
    body = functools.partial(_pool_kernel, pos_base=pos_base, pos_step=pos_step)
    in_specs = [rows,
                pl.BlockSpec((None, POOL_HIST_PAD, width), lambda i: (i, 0, 0)),
                pl.BlockSpec(pool_w.shape, lambda i: (0, 0, 0)),
                pl.BlockSpec((1, width), lambda i: (0, 0))]
    args = [p, hist, pool_w, pool_scale.reshape(1, width)]
    aliases = {}
    if out is not None:
        body = _ignore_input(body, len(args))
        aliases = {len(args): 0}
        in_specs.append(pl.BlockSpec(memory_space=pl.ANY))
        args.append(out)
    return pl.pallas_call(
        body,
        grid=(nblk,),
        in_specs=in_specs,
        out_specs=rows,
        out_shape=jax.ShapeDtypeStruct(p.shape, BF16),
        input_output_aliases=aliases,
        scratch_shapes=[pltpu.VMEM((POOL_HIST_PAD + tr, width), F32)],
        compiler_params=_params("parallel"),
        name=name,
    )(*args)


def _sb_steps(qs, kvs, tri, carries, accs, visibles):
    nk = kvs[0][0].shape[0]
    zs = [_dot_nt(q, k) * ATTN_SCALE for q, (k, _) in zip(qs, kvs)]
    log_keeps = [jnp.where(vis, -(jnp.maximum(z, 0.0) + jnp.log1p(jnp.exp(-jnp.abs(z)))), 0.0)
                 for z, vis in zip(zs, visibles)]
    his = [lk.astype(BF16) for lk in log_keeps]
    los = [(lk - hi.astype(F32)).astype(BF16) for lk, hi in zip(log_keeps, his)]
    sums = [_dot(hi, tri) + _dot(lo, tri) for hi, lo in zip(his, los)]
    weights = [jnp.where(vis, jnp.exp(z + lk + s[:, :nk] + carry[:, :nk]), 0.0).astype(BF16)
               for z, lk, s, carry, vis in zip(zs, log_keeps, sums, carries, visibles)]
    new_accs = [acc + _dot(a, v) for acc, a, (_, v) in zip(accs, weights, kvs)]
    new_carries = [carry + s[:, nk:] for carry, s in zip(carries, sums)]
    return new_carries, new_accs


def _sb_sweep(chains, load_kv, tri, carries, accs):
    rows = chains[0][0].shape[0]
    lane = lax.broadcasted_iota(jnp.int32, (rows, SB_KEYS), 1)
    beyond = jnp.int32(2 ** 30)

    def cond(st):
        return st[1] == 0

    def body(st):
        n, _, carries, accs = st
        kbs = [kb0 - n for _, kb0, _ in chains]
        starts = [pl.multiple_of(jnp.maximum(kb, 0) * SB_KEYS, SB_KEYS) for kb in kbs]
        visibles = [(jnp.where(kb >= 0, start, beyond) + lane) < qpos
                    for (_, _, qpos), kb, start in zip(chains, kbs, starts)]
        carries, accs = _sb_steps([q for q, _, _ in chains], load_kv(starts), tri, carries, accs, visibles)
        pending = jnp.full((rows, HEAD_DIM), -jnp.inf, F32)
        for kb, carry in zip(kbs, carries):
            pending = jnp.maximum(pending, jnp.where(kb > 0, carry, -jnp.inf))
        done = (jnp.max(pending) < SB_EXIT).astype(jnp.int32)
        return n + 1, done, carries, accs

    return lax.while_loop(cond, body, (jnp.int32(0), jnp.int32(0), list(carries), list(accs)))[3]


def _sb_prompt_kernel(q_ref, k_ref, v_ref, tri_ref, o_ref, *, tq):
    n_chains = q_ref.shape[0] // tq
    row0 = pl.program_id(1) * q_ref.shape[0]
    row = lax.broadcasted_iota(jnp.int32, (tq, SB_KEYS), 0)

    def load_kv(starts):
        return [(k_ref[pl.ds(s, SB_KEYS), :].astype(BF16), v_ref[pl.ds(s, SB_KEYS), :].astype(BF16))
                for s in starts]

    chains = [(q_ref[c * tq:(c + 1) * tq, :], (row0 + (c + 1) * tq) // SB_KEYS - 1, row0 + c * tq + row)
              for c in range(n_chains)]
    zeros = [jnp.zeros((tq, HEAD_DIM), F32)] * n_chains
    accs = _sb_sweep(chains, load_kv, tri_ref[...], zeros, zeros)
    for c in range(n_chains):
        o_ref[c * tq:(c + 1) * tq, :] = accs[c].astype(o_ref.dtype)


def _sb_sample_kernel(q_ref, kn_ref, vn_ref, kc_ref, vc_ref, trin_ref, tri_ref, o_ref):
    ls = q_ref.shape[0]
    past, heads = kc_ref.shape[0], kc_ref.shape[1]
    causal = lax.broadcasted_iota(jnp.int32, (ls, ls), 1) < lax.broadcasted_iota(jnp.int32, (ls, ls), 0)
    all_rows = jnp.full((ls, SB_KEYS), 2 ** 30 - SB_KEYS, jnp.int32)
    zeros = jnp.zeros((ls, HEAD_DIM), F32)
    head_cols = [slice(hh * HEAD_DIM, (hh + 1) * HEAD_DIM) for hh in range(heads)]
    chains = [(q_ref[:, sl], past // SB_KEYS - 1, all_rows) for sl in head_cols]
    carries, accs = _sb_steps([q for q, _, _ in chains],
                              [(kn_ref[:, sl].astype(BF16), vn_ref[:, sl].astype(BF16)) for sl in head_cols],
                              trin_ref[...], [zeros] * heads, [zeros] * heads, [causal] * heads)

    def load_kv(starts):
        k = pltpu.einshape("phd->hpd", kc_ref[pl.ds(starts[0], SB_KEYS), :, :])
        v = pltpu.einshape("phd->hpd", vc_ref[pl.ds(starts[0], SB_KEYS), :, :])
        return [(k[hh].astype(BF16), v[hh].astype(BF16)) for hh in range(heads)]

    accs = _sb_sweep(chains, load_kv, tri_ref[...], carries, accs)
    for hh in range(heads):
        o_ref[:, hh * HEAD_DIM:(hh + 1) * HEAD_DIM] = accs[hh].astype(o_ref.dtype)


def _sb_tri(nk):
    j = jnp.arange(nk)[:, None]
    s = jnp.arange(nk + HEAD_DIM)[None, :]
    return ((s >= nk) | (j > s)).astype(BF16)


def _sb_prompt(q, k, v, sp):
    heads = q.shape[1] // HEAD_DIM
    tq = SB_KEYS
    tb = _tile(sp, 8 * tq)
    assert sp % SB_KEYS == 0 and tb % tq == 0
    return pl.pallas_call(
        functools.partial(_sb_prompt_kernel, tq=tq),
        grid=(heads, sp // tb),
        in_specs=[pl.BlockSpec((tb, HEAD_DIM), lambda h, i: (i, h)),
                  pl.BlockSpec((sp, HEAD_DIM), lambda h, i: (0, h)),
                  pl.BlockSpec((sp, HEAD_DIM), lambda h, i: (0, h)),
                  pl.BlockSpec((SB_KEYS, SB_KEYS + HEAD_DIM), lambda h, i: (0, 0))],
        out_specs=pl.BlockSpec((tb, HEAD_DIM), lambda h, i: (i, h)),
        out_shape=jax.ShapeDtypeStruct(q.shape, BF16),
        compiler_params=_params("parallel", "arbitrary"),
        name="stickbreak_prompt",
    )(q, k, v, _sb_tri(SB_KEYS))


def _sb_sample(q, k, v, cache_k, cache_v, layer, sp, ls, out):
    n_layers, bs, past, heads, _ = cache_k.shape
    hg = 8
    assert past % SB_KEYS == 0 and past >= SB_KEYS and sp % ls == 0 and heads % hg == 0
    row0 = sp // ls
    gw = hg * HEAD_DIM
    blk_new = pl.BlockSpec((ls, gw), lambda b, g: (row0 + b, g))
    blk_old = pl.BlockSpec((None, None, past, None, hg, HEAD_DIM), lambda b, g: (layer, b, 0, g, 0, 0))
    grouped = (n_layers, bs, past, heads // hg, hg, HEAD_DIM)
    return pl.pallas_call(
        _ignore_input(_sb_sample_kernel, 7),
        grid=(bs, heads // hg),
        in_specs=[blk_new, blk_new, blk_new, blk_old, blk_old,
                  pl.BlockSpec((ls, ls + HEAD_DIM), lambda b, g: (0, 0)),
                  pl.BlockSpec((SB_KEYS, SB_KEYS + HEAD_DIM), lambda b, g: (0, 0)),
                  pl.BlockSpec(memory_space=pl.ANY)],
        out_specs=blk_new,
        out_shape=jax.ShapeDtypeStruct(out.shape, out.dtype),
        input_output_aliases={7: 0},
        compiler_params=_params("parallel", "arbitrary"),
        name="stickbreak_sample",
    )(q, k, v, cache_k.reshape(grouped), cache_v.reshape(grouped), _sb_tri(ls), _sb_tri(SB_KEYS), out)


def _peer_scores_kernel(key_ref, q_ref, o_ref):
    o_ref[...] = _dot_nt(key_ref[...], q_ref[...])


def _peer_scores(pq, sub_keys, layer):
    t = pq.shape[0]
    _, nhc, nkeys, half = sub_keys.shape
    tt = _tile(t, 1024)
    return pl.pallas_call(
        _peer_scores_kernel,
        grid=(nhc, t // tt),
        in_specs=[pl.BlockSpec((None, None, nkeys, half), lambda c, i: (layer, c, 0, 0)),
                  pl.BlockSpec((tt, half), lambda c, i: (i, c))],
        out_specs=pl.BlockSpec((None, nkeys, tt), lambda c, i: (c, 0, i)),
        out_shape=jax.ShapeDtypeStruct((nhc, nkeys, t), F32),
        compiler_params=_params("parallel", "parallel"),
        name="peer_scores",
    )(sub_keys, pq)


def _extract_top(s, k, tie_safe):
    rows = lax.broadcasted_iota(jnp.int32, s.shape, 0)
    rank = jnp.full(s.shape, float(k), F32)
    vals = []
    for r in range(k):
        m = jnp.max(s, axis=0, keepdims=True)
        vals.append(m)
        hit = s == m
        if tie_safe:
            hit = rows == jnp.min(jnp.where(hit, rows, s.shape[0]), axis=0, keepdims=True)
        rank = jnp.where(hit, float(r), rank)
        s = jnp.where(hit, -jnp.inf, s)
    return vals, rank, s


def _peer_select_kernel(s_ref, m1x_ref, e1x_ref, rank2_ref, e2_ref, *, ne):
    def select(h, tie_safe):
        s1 = s_ref[2 * h]
        s2 = s_ref[2 * h + 1]
        a, rank1, _ = _extract_top(s1, PEER_TOPK, tie_safe)
        b, rank2, _ = _extract_top(s2, PEER_TOPK, tie_safe)
        b = jnp.concatenate(b, axis=0)
        sub = lax.broadcasted_iota(jnp.int32, (8, b.shape[1]), 0)
        cand = [a[0] + b, a[1] + b[:8]]
        for i in range(2, 8):
            cand.append(jnp.where(sub < PEER_TOPK // (i + 1), a[i] + b[:8], -jnp.inf))
        cand.append(jnp.concatenate(a[8:], axis=0) + b[0:1])
        cand = jnp.concatenate(cand, axis=0)
        g, _, left = _extract_top(cand, PEER_TOPK, tie_safe)
        z = jnp.ones_like(g[0])
        for gk in g[1:]:
            z = z + jnp.exp(gk - g[0])
        took = (left != cand).astype(F32)
        taken = [jnp.sum(took[0:16], axis=0, keepdims=True), jnp.sum(took[16:24], axis=0, keepdims=True)]
        taken += [jnp.sum(took[8 * i + 8:8 * i + 16], axis=0, keepdims=True) for i in range(2, 8)]
        taken += [took[72 + i:73 + i] for i in range(8)]
        m1 = jnp.zeros_like(s1)
        for i in range(PEER_TOPK):
            m1 = jnp.where(rank1 == float(i), taken[i], m1)
        e1 = jnp.exp(s1 - a[0]) / z
        rank2_ref[h] = rank2.astype(rank2_ref.dtype)
        e2_ref[h] = jnp.exp(s2 - b[0:1]).astype(e2_ref.dtype)
        row = pl.multiple_of(h * ne, ne)
        for j in range(N_KEYS // ne):
            m1x_ref[j, pl.ds(row, ne), :] = m1[j * ne:(j + 1) * ne]
            e1x_ref[j, pl.ds(row, ne), :] = e1[j * ne:(j + 1) * ne]
        removed = (jnp.sum((rank1 < PEER_TOPK).astype(F32), axis=0, keepdims=True)
                   + jnp.sum((rank2 < PEER_TOPK).astype(F32), axis=0, keepdims=True)
                   + jnp.sum(took, axis=0, keepdims=True))
        return jnp.max(removed) - 3.0 * PEER_TOPK

    def one_head(h, _):
        extra = select(h, tie_safe=False)

        @pl.when(extra > 0.0)
        def _():
            select(h, tie_safe=True)

        return 0

    lax.fori_loop(0, PEER_HEADS, one_head, 0)


def _peer_select(scores, ne):
    nhc, nkeys, t = scores.shape
    tt = _tile(t, 256)
    x_shape = jax.ShapeDtypeStruct((nkeys // ne, PEER_HEADS * ne, t), F32)
    x_spec = pl.BlockSpec((nkeys // ne, PEER_HEADS * ne, tt), lambda i: (0, 0, i))
    y_shape = jax.ShapeDtypeStruct((PEER_HEADS, nkeys, t), BF16)
    y_spec = pl.BlockSpec((PEER_HEADS, nkeys, tt), lambda i: (0, 0, i))
    return pl.pallas_call(
        functools.partial(_peer_select_kernel, ne=ne),
        grid=(t // tt,),
        in_specs=[pl.BlockSpec((nhc, nkeys, tt), lambda i: (0, 0, i))],
        out_specs=[x_spec, x_spec, y_spec, y_spec],
        out_shape=[x_shape, x_shape, y_shape, y_shape],
        compiler_params=_params("parallel"),
        name="peer_select",
    )(scores)


def _peer_act_kernel(*refs, ne, n_parts):
    u_ref, x_ref = refs[0], refs[1]
    part_refs = refs[2:2 + 4 * n_parts]
    o_ref, gate_ref = refs[2 + 4 * n_parts], refs[3 + 4 * n_parts]
    first_row = (pl.program_id(1) % (ROW_BLOCK // ne)) * ne
    for part in range(n_parts):
        m1x_ref, e1x_ref, rank2_ref, e2_ref = part_refs[4 * part:4 * part + 4]
        tp = rank2_ref.shape[2]
        for e in range(ne):
            gate = jnp.zeros((N_KEYS, tp), BF16)
            for h in range(PEER_HEADS):
                r = pl.ds(h * ROW_BLOCK + first_row + e, 1)
                hit = rank2_ref[h] < m1x_ref[r, :].astype(BF16)
                w = e2_ref[h] * e1x_ref[r, :].astype(BF16)
                gate = gate + jnp.where(hit, w, jnp.zeros_like(w))
            gate_ref[part * tp:(part + 1) * tp, e * N_KEYS:(e + 1) * N_KEYS] = gate.T
    x = x_ref[...]
    for p in range(ne // 2):
        cols = slice(2 * p * N_KEYS, 2 * (p + 1) * N_KEYS)
        hp = _dot_nt(x, u_ref[cols, :])
        gelu = 0.5 * hp * (1.0 + lax.erf(hp * (0.5 ** 0.5)))
        o_ref[:, cols] = gate_ref[:, cols] * gelu.astype(BF16)


def _peer_act(xn, u, layer, m1x, e1x, rank2, e2, ne):
    t, d = xn.shape
    n_exp = u.shape[1]
    tp = _tile(t, 512)
    n_parts = 2 if t % (2 * tp) == 0 else 1
    tt = n_parts * tp
    te = ne * N_KEYS
    once = pl.Buffered(1)
    part_specs, part_args = [], []
    for part in range(n_parts):
        row_spec = pl.BlockSpec((None, PEER_HEADS * ROW_BLOCK, tp),
                                lambda i, j, part=part: (j // (ROW_BLOCK // ne), 0, n_parts * i + part))
        tile_spec = pl.BlockSpec((PEER_HEADS, N_KEYS, tp), lambda i, j, part=part: (0, 0, n_parts * i + part),
                                 pipeline_mode=once)
        part_specs += [row_spec, row_spec, tile_spec, tile_spec]
        part_args += [m1x, e1x, rank2, e2]
    return pl.pallas_call(
        functools.partial(_peer_act_kernel, ne=ne, n_parts=n_parts),
        grid=(t // tt, n_exp // te),
        in_specs=[pl.BlockSpec((None, te, d), lambda i, j: (layer, j, 0)),
                  pl.BlockSpec((tt, d), lambda i, j: (i, 0), pipeline_mode=once)] + part_specs,
        out_specs=pl.BlockSpec((tt, te), lambda i, j: (i, j)),
        out_shape=jax.ShapeDtypeStruct((t, n_exp), BF16),
        scratch_shapes=[pltpu.VMEM((tt, te), BF16)],
        compiler_params=_params("parallel", "arbitrary"),
        name="peer_act",
    )(u, xn, *part_args)


def _peer_ffn(h, layer, norm_g, w_q, sub_keys, u, v):
    ne = 8
    xn = _rmsnorm(h, norm_g)
    pq = _matmul(xn, w_q, layer, out_dtype=BF16, name="peer_query")
    scores = _peer_scores(pq, sub_keys, layer)
    m1x, e1x, rank2, e2 = _peer_select(scores, ROW_BLOCK)
    act = _peer_act(xn, u, layer, m1x, e1x, rank2, e2, ne)
    return _matmul(act, v, layer, mode="residual", aux=h, tm=1024, tn=1024, tk=2048, name="peer_down")


def kernel(x_prompt, x_sample, cache_a_k, cache_a_v, state_b_pool, cache_c_k, cache_c_v, norm_mix, norm_ffn, ab_w_in, ab_q_gain, ab_k_gain, ab_rel_bias, ab_pool_w, ab_pool_scale, ab_w_out, c_w_in, c_w_out, peer_w_q, peer_sub_keys, peer_u, peer_v):
    bp, sp, d = x_prompt.shape
    bs, ls, _ = x_sample.shape
    assert bp == 1
    depth = norm_mix.shape[0]
    past_len = cache_c_k.shape[2]
    a_heads = ab_rel_bias.shape[1]
    a_width = a_heads * HEAD_DIM
    b_width = ab_pool_scale.shape[1]
    c_width = c_w_out.shape[1]
    win_p = min(A_PAST_CHUNKS * CHUNK, sp)
    n_s = bs * ls

    h = jnp.concatenate([x_prompt.reshape(sp, d), x_sample.reshape(n_s, d)], axis=0)

    ab_w_in_b, ab_w_out_b = ab_w_in.astype(BF16), ab_w_out.astype(BF16)
    c_w_in_b, c_w_out_b = c_w_in.astype(BF16), c_w_out.astype(BF16)
    peer_w_q_b, peer_u_b, peer_v_b = peer_w_q.astype(BF16), peer_u.astype(BF16), peer_v.astype(BF16)
    sub_keys_b = peer_sub_keys.astype(BF16).reshape(depth, PEER_HEADS * 2, N_KEYS, -1)

    outs = {name: [] for name in ("a_k_p", "a_v_p", "b_p", "c_k_p", "c_v_p",
                                  "a_k_s", "a_v_s", "b_s", "c_k_s", "c_v_s")}
    for layer in range(depth):
        xn = _rmsnorm(h, norm_mix[layer])
        if layer % 2 == 0:
            i = layer // 2
            q = _matmul(xn, ab_w_in_b, i, col0=0, n=a_width, out_dtype=BF16, mode="headnorm",
                        aux=ab_q_gain[i], name="a_q")
            k = _matmul(xn, ab_w_in_b, i, col0=a_width, n=a_width, mode="headnorm", aux=ab_k_gain[i], name="a_k")
            v = _matmul(xn, ab_w_in_b, i, col0=2 * a_width, n=a_width, name="a_v")
            p = _matmul(xn, ab_w_in_b, i, col0=3 * a_width, n=b_width, name="b_in")

            att = _band_prompt(q, k, v, ab_rel_bias[i], sp)
            att = _band_sample(q, k, v, cache_a_k, cache_a_v, i, ab_rel_bias[i], sp, ls, past_len, att)

            pool_w = ab_pool_w[i].astype(BF16)
            tr = _tile(sp, 256)
            pp = p[:sp].reshape(sp // tr, tr, b_width)
            hist_p = jnp.concatenate([jnp.zeros((1, POOL_HIST_PAD, b_width), F32),
                                      pp[:-1, tr - POOL_HIST_PAD:]], axis=0)
            pool = _pool(p, hist_p, pool_w, ab_pool_scale[i], row_block0=0, nblk=sp // tr, tr=tr,
                         pos_base=0, pos_step=tr, name="pool_prompt")
            hist_s = jnp.pad(state_b_pool[i], ((0, 0), (POOL_HIST_PAD - POOL_HIST, 0), (0, 0)))
            pool = _pool(p, hist_s, pool_w, ab_pool_scale[i], row_block0=sp // ls, nblk=bs, tr=ls,
                         pos_base=past_len, pos_step=0, name="pool_sample", out=pool)

            h = _matmul(att, ab_w_out_b, i, x2=pool, mode="residual", aux=h, name="a_out")

            ps = p[sp:].reshape(bs, ls, b_width)
            outs["a_k_p"].append(k[sp - win_p:sp].reshape(bp, win_p, a_heads, HEAD_DIM))
            outs["a_v_p"].append(v[sp - win_p:sp].reshape(bp, win_p, a_heads, HEAD_DIM))
            outs["b_p"].append(p[sp - POOL_HIST:sp].reshape(bp, POOL_HIST, b_width))
            outs["a_k_s"].append(k[sp:].reshape(bs, ls, a_heads, HEAD_DIM))
            outs["a_v_s"].append(v[sp:].reshape(bs, ls, a_heads, HEAD_DIM))
            outs["b_s"].append(jnp.concatenate([state_b_pool[i], ps], axis=1)[:, ls:])
        else:
            j = layer // 2
            c_heads = c_width // HEAD_DIM
            q = _matmul(xn, c_w_in_b, j, col0=0, n=c_width, out_dtype=BF16, name="c_q")
            k = _matmul(xn, c_w_in_b, j, col0=c_width, n=c_width, name="c_k")
            v = _matmul(xn, c_w_in_b, j, col0=2 * c_width, n=c_width, name="c_v")
            att = _sb_prompt(q, k, v, sp)
            att = _sb_sample(q, k, v, cache_c_k, cache_c_v, j, sp, ls, att)
            h = _matmul(att, c_w_out_b, j, mode="residual", aux=h, name="c_out")
            outs["c_k_p"].append(k[:sp].reshape(bp, sp, c_heads, HEAD_DIM))
            outs["c_v_p"].append(v[:sp].reshape(bp, sp, c_heads, HEAD_DIM))
            outs["c_k_s"].append(k[sp:].reshape(bs, ls, c_heads, HEAD_DIM))
            outs["c_v_s"].append(v[sp:].reshape(bs, ls, c_heads, HEAD_DIM))

        h = _peer_ffn(h, layer, norm_ffn[layer], peer_w_q_b, sub_keys_b, peer_u_b, peer_v_b)

    st = {name: jnp.stack(vals) for name, vals in outs.items()}
    return (h[:sp].reshape(bp, sp, d), h[sp:].reshape(bs, ls, d),
            st["a_k_p"], st["a_v_p"], st["b_p"], st["c_k_p"], st["c_v_p"],
            st["a_k_s"], st["a_v_s"], st["b_s"], st["c_k_s"], st["c_v_s"])
```

```python
import functools

import jax
import jax.numpy as jnp
from jax import lax
from jax.experimental import pallas as pl
from jax.experimental.pallas import tpu as pltpu

F32 = jnp.float32
BF16 = jnp.bfloat16

HEAD_DIM = 128
CHUNK = 64
A_PAST_CHUNKS = 8
REL_CLIP = 128
POOL_WINDOWS = (2, 4, 8, 16)
POOL_HIST = max(POOL_WINDOWS) - 1
POOL_HIST_PAD = 16
B_GROUP_WIDTH = 512
PEER_HEADS = 8
N_KEYS = 128
PEER_TOPK = 16
ROW_BLOCK = 8
EPS = 1e-6
NEG_INF = -1e30
ATTN_SCALE = HEAD_DIM ** -0.5
SB_EXIT = -104.0
SB_KEYS = 128
VMEM_LIMIT_BYTES = 56 * 1024 * 1024


def _params(*sem):
    return pltpu.CompilerParams(dimension_semantics=sem, vmem_limit_bytes=VMEM_LIMIT_BYTES)


def _tile(dim, pref):
    return pref if dim % pref == 0 else dim


def _ignore_input(body, index):
    def wrapped(*refs):
        return body(*refs[:index], *refs[index + 1:])
    return wrapped


def _dot(a, b):
    return jnp.dot(a, b, preferred_element_type=F32)


def _dot_nt(a, b):
    return lax.dot_general(a, b, (((1,), (1,)), ((), ())), preferred_element_type=F32)


def _rmsnorm_kernel(x_ref, g_ref, o_ref):
    x = x_ref[...]
    ms = jnp.mean(x * x, axis=-1, keepdims=True)
    o_ref[...] = (x * lax.rsqrt(ms + EPS) * g_ref[...]).astype(o_ref.dtype)


def _rmsnorm(x, g):
    t, d = x.shape
    tr = _tile(t, 256)
    return pl.pallas_call(
        _rmsnorm_kernel,
        grid=(t // tr,),
        in_specs=[pl.BlockSpec((tr, d), lambda i: (i, 0)),
                  pl.BlockSpec((1, d), lambda i: (0, 0))],
        out_specs=pl.BlockSpec((tr, d), lambda i: (i, 0)),
        out_shape=jax.ShapeDtypeStruct((t, d), BF16),
        compiler_params=_params("parallel"),
        name="rmsnorm",
    )(x, g.reshape(1, d))


def _mm_kernel(*refs, nk, mode, two_x):
    x_ref, w_ref = refs[0], refs[1]
    refs = refs[2:]
    if two_x:
        x2_ref, refs = refs[0], refs[1:]
    if mode in ("headnorm", "residual"):
        aux_ref, o_ref, scratch = refs[0], refs[1], refs[2:]
    else:
        aux_ref, o_ref, scratch = None, refs[0], refs[1:]

    def finish(acc):
        if mode == "headnorm":
            g = aux_ref[...]
            for c in range(acc.shape[1] // HEAD_DIM):
                sl = slice(c * HEAD_DIM, (c + 1) * HEAD_DIM)
                y = acc[:, sl]
                ms = jnp.mean(y * y, axis=-1, keepdims=True)
                o_ref[:, sl] = (y * lax.rsqrt(ms + EPS) * g).astype(o_ref.dtype)
        elif mode == "residual":
            o_ref[...] = aux_ref[...] + acc
        else:
            o_ref[...] = acc.astype(o_ref.dtype)

    if two_x:
        k1 = x_ref.shape[1]
        finish(_dot(x_ref[...], w_ref[:k1, :]) + _dot(x2_ref[...], w_ref[k1:, :]))
    elif nk == 1:
        finish(_dot(x_ref[...], w_ref[...]))
    else:
        acc_ref = scratch[0]
        k = pl.program_id(2)

        @pl.when(k == 0)
        def _():
            acc_ref[...] = jnp.zeros_like(acc_ref)

        acc_ref[...] += _dot(x_ref[...], w_ref[...])

        @pl.when(k == nk - 1)
        def _():
            finish(acc_ref[...])


def _matmul(x, w, layer, *, x2=None, col0=0, n=None, out_dtype=F32, mode="plain", aux=None,
            tm=1024, tn=512, tk=None, name="matmul"):
    m, kdim = x.shape
    if x2 is not None:
        assert tk is None
        kdim += x2.shape[1]
    n = w.shape[2] if n is None else n
    tm, tn = _tile(m, tm), _tile(n, tn)
    tk = kdim if tk is None else _tile(kdim, tk)
    nk = kdim // tk
    cb = col0 // tn
    assert col0 % tn == 0
    in_specs = [pl.BlockSpec((tm, x.shape[1] if x2 is not None else tk), lambda i, j, k: (i, k)),
                pl.BlockSpec((None, tk, tn), lambda i, j, k: (layer, k, j + cb))]
    args = [x, w]
    if x2 is not None:
        in_specs.append(pl.BlockSpec((tm, x2.shape[1]), lambda i, j, k: (i, 0)))
        args.append(x2)
    if mode == "headnorm":
        in_specs.append(pl.BlockSpec((1, HEAD_DIM), lambda i, j, k: (0, 0)))
        args.append(aux.reshape(1, HEAD_DIM))
    elif mode == "residual":
        in_specs.append(pl.BlockSpec((tm, tn), lambda i, j, k: (i, j)))
        args.append(aux)
    scratch = [pltpu.VMEM((tm, tn), F32)] if nk > 1 else []
    return pl.pallas_call(
        functools.partial(_mm_kernel, nk=nk, mode=mode, two_x=x2 is not None),
        grid=(m // tm, n // tn, nk),
        in_specs=in_specs,
        out_specs=pl.BlockSpec((tm, tn), lambda i, j, k: (i, j)),
        out_shape=jax.ShapeDtypeStruct((m, n), out_dtype),
        scratch_shapes=scratch,
        compiler_params=_params("parallel", "parallel", "arbitrary"),
        name=name,
    )(*args)


def _proj_heads_kernel(x_ref, w_ref, o2_ref, o3_ref):
    acc = _dot(x_ref[...], w_ref[...])
    o2_ref[...] = acc.astype(o2_ref.dtype)
    o3_ref[...] = pltpu.einshape("m(hd)->mhd", acc, h=o3_ref.shape[1])


def _proj_heads(x, w, layer, *, col0, n, row0, m, name):
    kdim = x.shape[1]
    tm, tn = _tile(m, 1024), 8 * HEAD_DIM
    assert row0 % tm == 0 and col0 % tn == 0 and n % tn == 0
    rb, cb = row0 // tm, col0 // tn
    return pl.pallas_call(
        _proj_heads_kernel,
        grid=(m // tm, n // tn),
        in_specs=[pl.BlockSpec((tm, kdim), lambda i, j: (i + rb, 0)),
                  pl.BlockSpec((None, kdim, tn), lambda i, j: (layer, 0, j + cb))],
        out_specs=[pl.BlockSpec((tm, tn), lambda i, j: (i, j)),
                   pl.BlockSpec((tm, tn // HEAD_DIM, HEAD_DIM), lambda i, j: (i, j, 0))],
        out_shape=[jax.ShapeDtypeStruct((m, n), BF16),
                   jax.ShapeDtypeStruct((m, n // HEAD_DIM, HEAD_DIM), F32)],
        compiler_params=_params("parallel", "parallel"),
        name=name,
    )(x, w)


def _band_head(q, ka, kb, va, vb, bias_a, bias_b, past_visible):
    sa = _dot_nt(q, ka.astype(BF16)) * ATTN_SCALE + bias_a
    sb = _dot_nt(q, kb.astype(BF16)) * ATTN_SCALE + bias_b
    if past_visible is not None:
        sa = jnp.where(past_visible, sa, NEG_INF)
    m = jnp.maximum(jnp.max(sa, axis=-1, keepdims=True), jnp.max(sb, axis=-1, keepdims=True))
    pa = jnp.exp(sa - m)
    pb = jnp.exp(sb - m)
    l = jnp.sum(pa, axis=-1, keepdims=True) + jnp.sum(pb, axis=-1, keepdims=True)
    o = _dot(pa.astype(BF16), va.astype(BF16)) + _dot(pb.astype(BF16), vb.astype(BF16))
    return o / l


def _band_prompt_kernel(q_ref, ka_ref, kb_ref, va_ref, vb_ref, bias_ref, o_ref):
    na = ka_ref.shape[0]
    o = _band_head(q_ref[...], ka_ref[...], kb_ref[...], va_ref[...], vb_ref[...],
                   bias_ref[:, :na], bias_ref[:, na:], pl.program_id(1) > 0)
    o_ref[...] = o.astype(o_ref.dtype)


def _band_sample_kernel(q_ref, kn_ref, vn_ref, kc_ref, vc_ref, bias_ref, o_ref):
    win, heads = kc_ref.shape[0], kc_ref.shape[1]
    kc = pltpu.einshape("phd->hpd", kc_ref[...])
    vc = pltpu.einshape("phd->hpd", vc_ref[...])
    for hh in range(heads):
        sl = slice(hh * HEAD_DIM, (hh + 1) * HEAD_DIM)
        o = _band_head(q_ref[:, sl], kc[hh], kn_ref[:, sl], vc[hh], vn_ref[:, sl],
                       bias_ref[hh, :, :win], bias_ref[hh, :, win:], None)
        o_ref[:, sl] = o.astype(o_ref.dtype)


def _band_bias(rel_bias, q_pos, k_pos):
    lq, lk = q_pos.shape[0], k_pos.shape[0]
    d_min = q_pos[0] - k_pos[lk - 1]
    diag = jnp.clip(d_min + jnp.arange(lq + lk - 1, dtype=jnp.int32), -REL_CLIP, REL_CLIP) + REL_CLIP
    g = rel_bias.astype(F32)[:, diag]
    period = lq + lk
    u = jnp.pad(g[:, ::-1], ((0, 0), (0, 1)))
    skew = jnp.tile(u, (1, lq))[:, :lq * (period - 1)].reshape(-1, lq, period - 1)
    bias = skew[:, :, lq - 1:lq - 1 + lk]
    qc = q_pos[:, None] // CHUNK
    kc = k_pos[None, :] // CHUNK
    mask = (kc <= qc) & (kc >= qc - A_PAST_CHUNKS)
    return jnp.where(mask[None], bias, NEG_INF)


def _band_prompt(q, k, v, rel_bias, sp):
    heads = rel_bias.shape[0]
    tq = A_PAST_CHUNKS * CHUNK
    assert sp % tq == 0
    pos = jnp.arange(tq, dtype=jnp.int32)
    bias = _band_bias(rel_bias, tq + pos, jnp.arange(2 * tq, dtype=jnp.int32))
    prev = lambda h, i: (jnp.maximum(i - 1, 0), h)
    cur = lambda h, i: (i, h)
    blk = lambda im: pl.BlockSpec((tq, HEAD_DIM), im)
    return pl.pallas_call(
        _band_prompt_kernel,
        grid=(heads, sp // tq),
        in_specs=[blk(cur), blk(prev), blk(cur), blk(prev), blk(cur),
                  pl.BlockSpec((None, tq, 2 * tq), lambda h, i: (h, 0, 0))],
        out_specs=blk(cur),
        out_shape=jax.ShapeDtypeStruct((q.shape[0], heads * HEAD_DIM), BF16),
        compiler_params=_params("parallel", "arbitrary"),
        name="band_prompt",
    )(q, k, k, v, v, bias)


def _band_sample(q, k, v, cache_k, cache_v, layer, rel_bias, sp, ls, past_len, out):
    _, bs, win, heads, _ = cache_k.shape
    width = heads * HEAD_DIM
    assert sp % ls == 0
    row0 = sp // ls
    q_pos = past_len + jnp.arange(ls, dtype=jnp.int32)
    k_pos = past_len - win + jnp.arange(win + ls, dtype=jnp.int32)
    bias = _band_bias(rel_bias, q_pos, k_pos)
    blk_new = pl.BlockSpec((ls, width), lambda b: (row0 + b, 0))
    blk_old = pl.BlockSpec((None, None, win, heads, HEAD_DIM), lambda b: (layer, b, 0, 0, 0))
    return pl.pallas_call(
        _ignore_input(_band_sample_kernel, 6),
        grid=(bs,),
        in_specs=[blk_new, blk_new, blk_new, blk_old, blk_old,
                  pl.BlockSpec((heads, ls, win + ls), lambda b: (0, 0, 0)),
                  pl.BlockSpec(memory_space=pl.ANY)],
        out_specs=blk_new,
        out_shape=jax.ShapeDtypeStruct(out.shape, out.dtype),
        input_output_aliases={6: 0},
        compiler_params=_params("parallel"),
        name="band_sample",
    )(q, k, v, cache_k, cache_v, bias, out)


def _pool_kernel(cur_ref, hist_ref, w_ref, sc_ref, o_ref, ext_ref, *, pos_base, pos_step):
    tr = cur_ref.shape[0]
    ext_ref[0:POOL_HIST_PAD, :] = hist_ref[...]
    ext_ref[POOL_HIST_PAD:POOL_HIST_PAD + tr, :] = cur_ref[...]
    pos = pos_base + pl.program_id(0) * pos_step + lax.broadcasted_iota(jnp.int32, (tr, 1), 0)
    for g, w in enumerate(POOL_WINDOWS):
        sl = slice(g * B_GROUP_WIDTH, (g + 1) * B_GROUP_WIDTH)
        cur = cur_ref[:, sl]
        tot = cur
        for j in range(1, w):
            tot = tot + ext_ref[POOL_HIST_PAD - j:POOL_HIST_PAD - j + tr, sl]
        cnt = jnp.minimum(pos + 1, w).astype(F32)
        d = tot / cnt - cur
        y = _dot(d.astype(BF16), w_ref[g]) * sc_ref[:, sl]
        o_ref[:, sl] = y.astype(o_ref.dtype)


def _pool(p, hist, pool_w, pool_scale, *, row_block0, nblk, tr, pos_base, pos_step, name, out=None):
    width = p.shape[1]
    rows = pl.BlockSpec((tr, width), lambda i: (row_block0 + i, 0))
    body = functools.partial(_pool_kernel, pos_base=pos_base, pos_step=pos_step)
    in_specs = [rows,
                pl.BlockSpec((None, POOL_HIST_PAD, width), lambda i: (i, 0, 0)),
                pl.BlockSpec(pool_w.shape, lambda i: (0, 0, 0)),
                pl.BlockSpec((1, width), lambda i: (0, 0))]
    args = [p, hist, pool_w, pool_scale.reshape(1, width)]
    aliases = {}
    if out is not None:
        body = _ignore_input(body, len(args))
        aliases = {len(args): 0}
        in_specs.append(pl.BlockSpec(memory_space=pl.ANY))
        args.append(out)
    return pl.pallas_call(
        body,
        grid=(nblk,),
        in_specs=in_specs,
        out_specs=rows,
        out_shape=jax.ShapeDtypeStruct(p.shape, BF16),
        input_output_aliases=aliases,
        scratch_shapes=[pltpu.VMEM((POOL_HIST_PAD + tr, width), F32)],
        compiler_params=_params("parallel"),
        name=name,
    )(*args)


def _sb_steps(qs, kvs, tri, carries, accs, visibles):
    nk = kvs[0][0].shape[0]
    zs = [_dot_nt(q, k) * ATTN_SCALE for q, (k, _) in zip(qs, kvs)]
    log_keeps = [jnp.where(vis, -(jnp.maximum(z, 0.0) + jnp.log1p(jnp.exp(-jnp.abs(z)))), 0.0)
                 for z, vis in zip(zs, visibles)]
    his = [lk.astype(BF16) for lk in log_keeps]
    los = [(lk - hi.astype(F32)).astype(BF16) for lk, hi in zip(log_keeps, his)]
    sums = [_dot(hi, tri) + _dot(lo, tri) for hi, lo in zip(his, los)]
    weights = [jnp.where(vis, jnp.exp(z + lk + s[:, :nk] + carry[:, :nk]), 0.0).astype(BF16)
               for z, lk, s, carry, vis in zip(zs, log_keeps, sums, carries, visibles)]
    new_accs = [acc + _dot(a, v) for acc, a, (_, v) in zip(accs, weights, kvs)]
    new_carries = [carry + s[:, nk:] for carry, s in zip(carries, sums)]
    return new_carries, new_accs


def _sb_sweep(chains, load_kv, tri, carries, accs):
    rows = chains[0][0].shape[0]
    lane = lax.broadcasted_iota(jnp.int32, (rows, SB_KEYS), 1)
    beyond = jnp.int32(2 ** 30)

    def cond(st):
        return st[1] == 0

    def body(st):
        n, _, carries, accs = st
        kbs = [kb0 - n for _, kb0, _ in chains]
        starts = [pl.multiple_of(jnp.maximum(kb, 0) * SB_KEYS, SB_KEYS) for kb in kbs]
        visibles = [(jnp.where(kb >= 0, start, beyond) + lane) < qpos
                    for (_, _, qpos), kb, start in zip(chains, kbs, starts)]
        carries, accs = _sb_steps([q for q, _, _ in chains], load_kv(starts), tri, carries, accs, visibles)
        pending = jnp.full((rows, HEAD_DIM), -jnp.inf, F32)
        for kb, carry in zip(kbs, carries):
            pending = jnp.maximum(pending, jnp.where(kb > 0, carry, -jnp.inf))
        done = (jnp.max(pending) < SB_EXIT).astype(jnp.int32)
        return n + 1, done, carries, accs

    return lax.while_loop(cond, body, (jnp.int32(0), jnp.int32(0), list(carries), list(accs)))[3]


def _sb_prompt_kernel(q_ref, k_ref, v_ref, tri_ref, o_ref, *, tq):
    n_chains = q_ref.shape[0] // tq
    row0 = pl.program_id(1) * q_ref.shape[0]
    row = lax.broadcasted_iota(jnp.int32, (tq, SB_KEYS), 0)

    def load_kv(starts):
        return [(k_ref[pl.ds(s, SB_KEYS), :].astype(BF16), v_ref[pl.ds(s, SB_KEYS), :].astype(BF16))
                for s in starts]

    chains = [(q_ref[c * tq:(c + 1) * tq, :], (row0 + (c + 1) * tq) // SB_KEYS - 1, row0 + c * tq + row)
              for c in range(n_chains)]
    zeros = [jnp.zeros((tq, HEAD_DIM), F32)] * n_chains
    accs = _sb_sweep(chains, load_kv, tri_ref[...], zeros, zeros)
    for c in range(n_chains):
        o_ref[c * tq:(c + 1) * tq, :] = accs[c].astype(o_ref.dtype)


def _sb_sample_kernel(q_ref, kn_ref, vn_ref, kc_ref, vc_ref, trin_ref, tri_ref, o_ref):
    ls = q_ref.shape[0]
    past, heads = kc_ref.shape[0], kc_ref.shape[1]
    causal = lax.broadcasted_iota(jnp.int32, (ls, ls), 1) < lax.broadcasted_iota(jnp.int32, (ls, ls), 0)
    all_rows = jnp.full((ls, SB_KEYS), 2 ** 30 - SB_KEYS, jnp.int32)
    zeros = jnp.zeros((ls, HEAD_DIM), F32)
    head_cols = [slice(hh * HEAD_DIM, (hh + 1) * HEAD_DIM) for hh in range(heads)]
    chains = [(q_ref[:, sl], past // SB_KEYS - 1, all_rows) for sl in head_cols]
    carries, accs = _sb_steps([q for q, _, _ in chains],
                              [(kn_ref[:, sl].astype(BF16), vn_ref[:, sl].astype(BF16)) for sl in head_cols],
                              trin_ref[...], [zeros] * heads, [zeros] * heads, [causal] * heads)

    def load_kv(starts):
        k = pltpu.einshape("phd->hpd", kc_ref[pl.ds(starts[0], SB_KEYS), :, :])
        v = pltpu.einshape("phd->hpd", vc_ref[pl.ds(starts[0], SB_KEYS), :, :])
        return [(k[hh].astype(BF16), v[hh].astype(BF16)) for hh in range(heads)]

    accs = _sb_sweep(chains, load_kv, tri_ref[...], carries, accs)
    for hh in range(heads):
        o_ref[:, hh * HEAD_DIM:(hh + 1) * HEAD_DIM] = accs[hh].astype(o_ref.dtype)


def _sb_tri(nk):
    j = jnp.arange(nk)[:, None]
    s = jnp.arange(nk + HEAD_DIM)[None, :]
    return ((s >= nk) | (j > s)).astype(BF16)


def _sb_prompt(q, k, v, sp):
    heads = q.shape[1] // HEAD_DIM
    tq = SB_KEYS
    tb = _tile(sp, 8 * tq)
    assert sp % SB_KEYS == 0 and tb % tq == 0
    return pl.pallas_call(
        functools.partial(_sb_prompt_kernel, tq=tq),
        grid=(heads, sp // tb),
        in_specs=[pl.BlockSpec((tb, HEAD_DIM), lambda h, i: (i, h)),
                  pl.BlockSpec((sp, HEAD_DIM), lambda h, i: (0, h)),
                  pl.BlockSpec((sp, HEAD_DIM), lambda h, i: (0, h)),
                  pl.BlockSpec((SB_KEYS, SB_KEYS + HEAD_DIM), lambda h, i: (0, 0))],
        out_specs=pl.BlockSpec((tb, HEAD_DIM), lambda h, i: (i, h)),
        out_shape=jax.ShapeDtypeStruct(q.shape, BF16),
        compiler_params=_params("parallel", "arbitrary"),
        name="stickbreak_prompt",
    )(q, k, v, _sb_tri(SB_KEYS))


def _sb_sample(q, k, v, cache_k, cache_v, layer, sp, ls, out):
    n_layers, bs, past, heads, _ = cache_k.shape
    hg = 8
    assert past % SB_KEYS == 0 and past >= SB_KEYS and sp % ls == 0 and heads % hg == 0
    row0 = sp // ls
    gw = hg * HEAD_DIM
    blk_new = pl.BlockSpec((ls, gw), lambda b, g: (row0 + b, g))
    blk_kv = pl.BlockSpec((ls, gw), lambda b, g: (b, g))
    blk_old = pl.BlockSpec((None, None, past, None, hg, HEAD_DIM), lambda b, g: (layer, b, 0, g, 0, 0))
    grouped = (n_layers, bs, past, heads // hg, hg, HEAD_DIM)
    return pl.pallas_call(
        _ignore_input(_sb_sample_kernel, 7),
        grid=(bs, heads // hg),
        in_specs=[blk_new, blk_kv, blk_kv, blk_old, blk_old,
                  pl.BlockSpec((ls, ls + HEAD_DIM), lambda b, g: (0, 0)),
                  pl.BlockSpec((SB_KEYS, SB_KEYS + HEAD_DIM), lambda b, g: (0, 0)),
                  pl.BlockSpec(memory_space=pl.ANY)],
        out_specs=blk_new,
        out_shape=jax.ShapeDtypeStruct(out.shape, out.dtype),
        input_output_aliases={7: 0},
        compiler_params=_params("parallel", "arbitrary"),
        name="stickbreak_sample",
    )(q, k, v, cache_k.reshape(grouped), cache_v.reshape(grouped), _sb_tri(ls), _sb_tri(SB_KEYS), out)


def _peer_scores_kernel(key_ref, q_ref, o_ref):
    o_ref[...] = _dot_nt(key_ref[...], q_ref[...])


def _peer_scores(pq, sub_keys, layer):
    t = pq.shape[0]
    _, nhc, nkeys, half = sub_keys.shape
    tt = _tile(t, 1024)
    return pl.pallas_call(
        _peer_scores_kernel,
        grid=(nhc, t // tt),
        in_specs=[pl.BlockSpec((None, None, nkeys, half), lambda c, i: (layer, c, 0, 0)),
                  pl.BlockSpec((tt, half), lambda c, i: (i, c))],
        out_specs=pl.BlockSpec((None, nkeys, tt), lambda c, i: (c, 0, i)),
        out_shape=jax.ShapeDtypeStruct((nhc, nkeys, t), F32),
        compiler_params=_params("parallel", "parallel"),
        name="peer_scores",
    )(sub_keys, pq)


def _extract_top(s, k, tie_safe):
    rows = lax.broadcasted_iota(jnp.int32, s.shape, 0)
    rank = jnp.full(s.shape, float(k), F32)
    vals = []
    for r in range(k):
        m = jnp.max(s, axis=0, keepdims=True)
        vals.append(m)
        hit = s == m
        if tie_safe:
            hit = rows == jnp.min(jnp.where(hit, rows, s.shape[0]), axis=0, keepdims=True)
        rank = jnp.where(hit, float(r), rank)
        s = jnp.where(hit, -jnp.inf, s)
    return vals, rank, s


def _peer_select_kernel(s_ref, m1x_ref, e1x_ref, rank2_ref, e2_ref, *, ne):
    def select(h, tie_safe):
        s1 = s_ref[2 * h]
        s2 = s_ref[2 * h + 1]
        a, rank1, _ = _extract_top(s1, PEER_TOPK, tie_safe)
        b, rank2, _ = _extract_top(s2, PEER_TOPK, tie_safe)
        b = jnp.concatenate(b, axis=0)
        sub = lax.broadcasted_iota(jnp.int32, (8, b.shape[1]), 0)
        cand = [a[0] + b, a[1] + b[:8]]
        for i in range(2, 8):
            cand.append(jnp.where(sub < PEER_TOPK // (i + 1), a[i] + b[:8], -jnp.inf))
        cand.append(jnp.concatenate(a[8:], axis=0) + b[0:1])
        cand = jnp.concatenate(cand, axis=0)
        g, _, left = _extract_top(cand, PEER_TOPK, tie_safe)
        z = jnp.ones_like(g[0])
        for gk in g[1:]:
            z = z + jnp.exp(gk - g[0])
        took = (left != cand).astype(F32)
        taken = [jnp.sum(took[0:16], axis=0, keepdims=True), jnp.sum(took[16:24], axis=0, keepdims=True)]
        taken += [jnp.sum(took[8 * i + 8:8 * i + 16], axis=0, keepdims=True) for i in range(2, 8)]
        taken += [took[72 + i:73 + i] for i in range(8)]
        m1 = jnp.zeros_like(s1)
        for i in range(PEER_TOPK):
            m1 = jnp.where(rank1 == float(i), taken[i], m1)
        e1 = jnp.exp(s1 - a[0]) / z
        rank2_ref[h] = rank2.astype(rank2_ref.dtype)
        e2_ref[h] = jnp.exp(s2 - b[0:1]).astype(e2_ref.dtype)
        row = pl.multiple_of(h * ne, ne)
        for j in range(N_KEYS // ne):
            m1x_ref[j, pl.ds(row, ne), :] = m1[j * ne:(j + 1) * ne]
            e1x_ref[j, pl.ds(row, ne), :] = e1[j * ne:(j + 1) * ne]
        removed = (jnp.sum((rank1 < PEER_TOPK).astype(F32), axis=0, keepdims=True)
                   + jnp.sum((rank2 < PEER_TOPK).astype(F32), axis=0, keepdims=True)
                   + jnp.sum(took, axis=0, keepdims=True))
        return jnp.max(removed) - 3.0 * PEER_TOPK

    def one_head(h, _):
        extra = select(h, tie_safe=False)

        @pl.when(extra > 0.0)
        def _():
            select(h, tie_safe=True)

        return 0

    lax.fori_loop(0, PEER_HEADS, one_head, 0)


def _peer_select(scores, ne):
    nhc, nkeys, t = scores.shape
    tt = _tile(t, 256)
    x_shape = jax.ShapeDtypeStruct((nkeys // ne, PEER_HEADS * ne, t), F32)
    x_spec = pl.BlockSpec((nkeys // ne, PEER_HEADS * ne, tt), lambda i: (0, 0, i))
    y_shape = jax.ShapeDtypeStruct((PEER_HEADS, nkeys, t), BF16)
    y_spec = pl.BlockSpec((PEER_HEADS, nkeys, tt), lambda i: (0, 0, i))
    return pl.pallas_call(
        functools.partial(_peer_select_kernel, ne=ne),
        grid=(t // tt,),
        in_specs=[pl.BlockSpec((nhc, nkeys, tt), lambda i: (0, 0, i))],
        out_specs=[x_spec, x_spec, y_spec, y_spec],
        out_shape=[x_shape, x_shape, y_shape, y_shape],
        compiler_params=_params("parallel"),
        name="peer_select",
    )(scores)


def _peer_act_kernel(*refs, ne, n_parts):
    u_ref, x_ref = refs[0], refs[1]
    part_refs = refs[2:2 + 4 * n_parts]
    o_ref, gate_ref = refs[2 + 4 * n_parts], refs[3 + 4 * n_parts]
    first_row = (pl.program_id(1) % (ROW_BLOCK // ne)) * ne
    for part in range(n_parts):
        m1x_ref, e1x_ref, rank2_ref, e2_ref = part_refs[4 * part:4 * part + 4]
        tp = rank2_ref.shape[2]
        for e in range(ne):
            gate = jnp.zeros((N_KEYS, tp), BF16)
            for h in range(PEER_HEADS):
                r = pl.ds(h * ROW_BLOCK + first_row + e, 1)
                hit = rank2_ref[h] < m1x_ref[r, :].astype(BF16)
                w = e2_ref[h] * e1x_ref[r, :].astype(BF16)
                gate = gate + jnp.where(hit, w, jnp.zeros_like(w))
            gate_ref[part * tp:(part + 1) * tp, e * N_KEYS:(e + 1) * N_KEYS] = gate.T
    x = x_ref[...]
    for p in range(ne // 2):
        cols = slice(2 * p * N_KEYS, 2 * (p + 1) * N_KEYS)
        hp = _dot_nt(x, u_ref[cols, :].astype(BF16))
        gelu = 0.5 * hp * (1.0 + lax.erf(hp * (0.5 ** 0.5)))
        o_ref[:, cols] = gate_ref[:, cols] * gelu.astype(BF16)


def _peer_act(xn, u, layer, m1x, e1x, rank2, e2, ne):
    t, d = xn.shape
    n_exp = u.shape[1]
    tp = _tile(t, 512)
    n_parts = 2 if t % (2 * tp) == 0 else 1
    tt = n_parts * tp
    te = ne * N_KEYS
    once = pl.Buffered(1)
    part_specs, part_args = [], []
    for part in range(n_parts):
        row_spec = pl.BlockSpec((None, PEER_HEADS * ROW_BLOCK, tp),
                                lambda i, j, part=part: (j // (ROW_BLOCK // ne), 0, n_parts * i + part))
        tile_spec = pl.BlockSpec((PEER_HEADS, N_KEYS, tp), lambda i, j, part=part: (0, 0, n_parts * i + part),
                                 pipeline_mode=once)
        part_specs += [row_spec, row_spec, tile_spec, tile_spec]
        part_args += [m1x, e1x, rank2, e2]
    return pl.pallas_call(
        functools.partial(_peer_act_kernel, ne=ne, n_parts=n_parts),
        grid=(t // tt, n_exp // te),
        in_specs=[pl.BlockSpec((None, te, d), lambda i, j: (layer, j, 0)),
                  pl.BlockSpec((tt, d), lambda i, j: (i, 0), pipeline_mode=once)] + part_specs,
        out_specs=pl.BlockSpec((tt, te), lambda i, j: (i, j)),
        out_shape=jax.ShapeDtypeStruct((t, n_exp), BF16),
        scratch_shapes=[pltpu.VMEM((tt, te), BF16)],
        compiler_params=_params("parallel", "arbitrary"),
        name="peer_act",
    )(u, xn, *part_args)


def _peer_ffn(h, layer, norm_g, w_q, sub_keys, u, v):
    ne = 4
    xn = _rmsnorm(h, norm_g)
    pq = _matmul(xn, w_q, layer, out_dtype=BF16, name="peer_query")
    scores = _peer_scores(pq, sub_keys, layer)
    m1x, e1x, rank2, e2 = _peer_select(scores, ROW_BLOCK)
    act = _peer_act(xn, u, layer, m1x, e1x, rank2, e2, ne)
    return _matmul(act, v, layer, mode="residual", aux=h, tm=1024, tn=1024, tk=2048, name="peer_down")


def kernel(x_prompt, x_sample, cache_a_k, cache_a_v, state_b_pool, cache_c_k, cache_c_v, norm_mix, norm_ffn, ab_w_in, ab_q_gain, ab_k_gain, ab_rel_bias, ab_pool_w, ab_pool_scale, ab_w_out, c_w_in, c_w_out, peer_w_q, peer_sub_keys, peer_u, peer_v):
    bp, sp, d = x_prompt.shape
    bs, ls, _ = x_sample.shape
    assert bp == 1
    depth = norm_mix.shape[0]
    past_len = cache_c_k.shape[2]
    a_heads = ab_rel_bias.shape[1]
    a_width = a_heads * HEAD_DIM
    b_width = ab_pool_scale.shape[1]
    c_width = c_w_out.shape[1]
    win_p = min(A_PAST_CHUNKS * CHUNK, sp)
    n_s = bs * ls

    h = jnp.concatenate([x_prompt.reshape(sp, d), x_sample.reshape(n_s, d)], axis=0)

    ab_w_in_b, ab_w_out_b = ab_w_in.astype(BF16), ab_w_out.astype(BF16)
    c_w_in_b, c_w_out_b = c_w_in.astype(BF16), c_w_out.astype(BF16)
    peer_w_q_b, peer_v_b = peer_w_q.astype(BF16), peer_v.astype(BF16)
    sub_keys_b = peer_sub_keys.astype(BF16).reshape(depth, PEER_HEADS * 2, N_KEYS, -1)

    outs = {name: [] for name in ("a_k_p", "a_v_p", "b_p", "c_k_p", "c_v_p",
                                  "a_k_s", "a_v_s", "b_s", "c_k_s", "c_v_s")}
    for layer in range(depth):
        xn = _rmsnorm(h, norm_mix[layer])
        if layer % 2 == 0:
            i = layer // 2
            q = _matmul(xn, ab_w_in_b, i, col0=0, n=a_width, out_dtype=BF16, mode="headnorm",
                        aux=ab_q_gain[i], name="a_q")
            k = _matmul(xn, ab_w_in_b, i, col0=a_width, n=a_width, mode="headnorm", aux=ab_k_gain[i], name="a_k")
            v = _matmul(xn, ab_w_in_b, i, col0=2 * a_width, n=a_width, name="a_v")
            p = _matmul(xn, ab_w_in_b, i, col0=3 * a_width, n=b_width, name="b_in")

            att = _band_prompt(q, k, v, ab_rel_bias[i], sp)
            att = _band_sample(q, k, v, cache_a_k, cache_a_v, i, ab_rel_bias[i], sp, ls, past_len, att)

            pool_w = ab_pool_w[i].astype(BF16)
            tr = _tile(sp, 256)
            pp = p[:sp].reshape(sp // tr, tr, b_width)
            hist_p = jnp.concatenate([jnp.zeros((1, POOL_HIST_PAD, b_width), F32),
                                      pp[:-1, tr - POOL_HIST_PAD:]], axis=0)
            pool = _pool(p, hist_p, pool_w, ab_pool_scale[i], row_block0=0, nblk=sp // tr, tr=tr,
                         pos_base=0, pos_step=tr, name="pool_prompt")
            hist_s = jnp.pad(state_b_pool[i], ((0, 0), (POOL_HIST_PAD - POOL_HIST, 0), (0, 0)))
            pool = _pool(p, hist_s, pool_w, ab_pool_scale[i], row_block0=sp // ls, nblk=bs, tr=ls,
                         pos_base=past_len, pos_step=0, name="pool_sample", out=pool)

            h = _matmul(att, ab_w_out_b, i, x2=pool, mode="residual", aux=h, name="a_out")

            ps = p[sp:].reshape(bs, ls, b_width)
            outs["a_k_p"].append(k[sp - win_p:sp].reshape(bp, win_p, a_heads, HEAD_DIM))
            outs["a_v_p"].append(v[sp - win_p:sp].reshape(bp, win_p, a_heads, HEAD_DIM))
            outs["b_p"].append(p[sp - POOL_HIST:sp].reshape(bp, POOL_HIST, b_width))
            outs["a_k_s"].append(k[sp:].reshape(bs, ls, a_heads, HEAD_DIM))
            outs["a_v_s"].append(v[sp:].reshape(bs, ls, a_heads, HEAD_DIM))
            outs["b_s"].append(jnp.concatenate([state_b_pool[i], ps], axis=1)[:, ls:])
        else:
            j = layer // 2
            c_heads = c_width // HEAD_DIM
            q = _matmul(xn, c_w_in_b, j, col0=0, n=c_width, out_dtype=BF16, name="c_q")
            k_p, k_p3 = _proj_heads(xn, c_w_in_b, j, col0=c_width, n=c_width, row0=0, m=sp, name="c_k_prompt")
            v_p, v_p3 = _proj_heads(xn, c_w_in_b, j, col0=2 * c_width, n=c_width, row0=0, m=sp, name="c_v_prompt")
            k_s, k_s3 = _proj_heads(xn, c_w_in_b, j, col0=c_width, n=c_width, row0=sp, m=n_s, name="c_k_sample")
            v_s, v_s3 = _proj_heads(xn, c_w_in_b, j, col0=2 * c_width, n=c_width, row0=sp, m=n_s, name="c_v_sample")
            att = _sb_prompt(q, k_p, v_p, sp)
            att = _sb_sample(q, k_s, v_s, cache_c_k, cache_c_v, j, sp, ls, att)
            h = _matmul(att, c_w_out_b, j, mode="residual", aux=h, name="c_out")
            outs["c_k_p"].append(k_p3.reshape(bp, sp, c_heads, HEAD_DIM))
            outs["c_v_p"].append(v_p3.reshape(bp, sp, c_heads, HEAD_DIM))
            outs["c_k_s"].append(k_s3.reshape(bs, ls, c_heads, HEAD_DIM))
            outs["c_v_s"].append(v_s3.reshape(bs, ls, c_heads, HEAD_DIM))

        h = _peer_ffn(h, layer, norm_ffn[layer], peer_w_q_b, sub_keys_b, peer_u, peer_v_b)

    st = {name: jnp.stack(vals) for name, vals in outs.items()}
    return (h[:sp].reshape(bp, sp, d), h[sp:].reshape(bs, ls, d),
            st["a_k_p"], st["a_v_p"], st["b_p"], st["c_k_p"], st["c_v_p"],
            st["a_k_s"], st["a_v_s"], st["b_s"], st["c_k_s"], st["c_v_s"])
```

```python
import functools

import jax
import jax.numpy as jnp
from jax import lax
from jax.experimental import pallas as pl
from jax.experimental.pallas import tpu as pltpu

F32 = jnp.float32
BF16 = jnp.bfloat16

HEAD_DIM = 128
CHUNK = 64
A_PAST_CHUNKS = 8
REL_CLIP = 128
POOL_WINDOWS = (2, 4, 8, 16)
POOL_HIST = max(POOL_WINDOWS) - 1
POOL_HIST_PAD = 16
B_GROUP_WIDTH = 512
PEER_HEADS = 8
N_KEYS = 128
PEER_TOPK = 16
ROW_BLOCK = 8
EPS = 1e-6
NEG_INF = -1e30
ATTN_SCALE = HEAD_DIM ** -0.5
SB_EXIT = -104.0
SB_KEYS = 128
VMEM_LIMIT_BYTES = 56 * 1024 * 1024


def _params(*sem):
    return pltpu.CompilerParams(dimension_semantics=sem, vmem_limit_bytes=VMEM_LIMIT_BYTES)


def _tile(dim, pref):
    return pref if dim % pref == 0 else dim


def _ignore_input(body, index):
    def wrapped(*refs):
        return body(*refs[:index], *refs[index + 1:])
    return wrapped


def _dot(a, b):
    return jnp.dot(a, b, preferred_element_type=F32)


def _dot_nt(a, b):
    return lax.dot_general(a, b, (((1,), (1,)), ((), ())), preferred_element_type=F32)


def _rmsnorm_kernel(x_ref, g_ref, o_ref):
    x = x_ref[...]
    ms = jnp.mean(x * x, axis=-1, keepdims=True)
    o_ref[...] = (x * lax.rsqrt(ms + EPS) * g_ref[...]).astype(o_ref.dtype)


def _rmsnorm(x, g):
    t, d = x.shape
    tr = _tile(t, 256)
    return pl.pallas_call(
        _rmsnorm_kernel,
        grid=(t // tr,),
        in_specs=[pl.BlockSpec((tr, d), lambda i: (i, 0)),
                  pl.BlockSpec((1, d), lambda i: (0, 0))],
        out_specs=pl.BlockSpec((tr, d), lambda i: (i, 0)),
        out_shape=jax.ShapeDtypeStruct((t, d), BF16),
        compiler_params=_params("parallel"),
        name="rmsnorm",
    )(x, g.reshape(1, d))


def _mm_kernel(*refs, nk, mode, two_x):
    x_ref, w_ref = refs[0], refs[1]
    refs = refs[2:]
    if two_x:
        x2_ref, refs = refs[0], refs[1:]
    if mode in ("headnorm", "residual"):
        aux_ref, o_ref, scratch = refs[0], refs[1], refs[2:]
    else:
        aux_ref, o_ref, scratch = None, refs[0], refs[1:]

    def finish(acc):
        if mode == "headnorm":
            g = aux_ref[...]
            for c in range(acc.shape[1] // HEAD_DIM):
                sl = slice(c * HEAD_DIM, (c + 1) * HEAD_DIM)
                y = acc[:, sl]
                ms = jnp.mean(y * y, axis=-1, keepdims=True)
                o_ref[:, sl] = (y * lax.rsqrt(ms + EPS) * g).astype(o_ref.dtype)
        elif mode == "residual":
            o_ref[...] = aux_ref[...] + acc
        else:
            o_ref[...] = acc.astype(o_ref.dtype)

    if two_x:
        k1 = x_ref.shape[1]
        finish(_dot(x_ref[...], w_ref[:k1, :]) + _dot(x2_ref[...], w_ref[k1:, :]))
    elif nk == 1:
        finish(_dot(x_ref[...], w_ref[...]))
    else:
        acc_ref = scratch[0]
        k = pl.program_id(2)

        @pl.when(k == 0)
        def _():
            acc_ref[...] = jnp.zeros_like(acc_ref)

        acc_ref[...] += _dot(x_ref[...], w_ref[...])

        @pl.when(k == nk - 1)
        def _():
            finish(acc_ref[...])


def _matmul(x, w, layer, *, x2=None, col0=0, n=None, out_dtype=F32, mode="plain", aux=None,
            tm=1024, tn=512, tk=None, name="matmul"):
    m, kdim = x.shape
    if x2 is not None:
        assert tk is None
        kdim += x2.shape[1]
    n = w.shape[2] if n is None else n
    tm, tn = _tile(m, tm), _tile(n, tn)
    tk = kdim if tk is None else _tile(kdim, tk)
    nk = kdim // tk
    cb = col0 // tn
    assert col0 % tn == 0
    in_specs = [pl.BlockSpec((tm, x.shape[1] if x2 is not None else tk), lambda i, j, k: (i, k)),
                pl.BlockSpec((None, tk, tn), lambda i, j, k: (layer, k, j + cb))]
    args = [x, w]
    if x2 is not None:
        in_specs.append(pl.BlockSpec((tm, x2.shape[1]), lambda i, j, k: (i, 0)))
        args.append(x2)
    if mode == "headnorm":
        in_specs.append(pl.BlockSpec((1, HEAD_DIM), lambda i, j, k: (0, 0)))
        args.append(aux.reshape(1, HEAD_DIM))
    elif mode == "residual":
        in_specs.append(pl.BlockSpec((tm, tn), lambda i, j, k: (i, j)))
        args.append(aux)
    scratch = [pltpu.VMEM((tm, tn), F32)] if nk > 1 else []
    return pl.pallas_call(
        functools.partial(_mm_kernel, nk=nk, mode=mode, two_x=x2 is not None),
        grid=(m // tm, n // tn, nk),
        in_specs=in_specs,
        out_specs=pl.BlockSpec((tm, tn), lambda i, j, k: (i, j)),
        out_shape=jax.ShapeDtypeStruct((m, n), out_dtype),
        scratch_shapes=scratch,
        compiler_params=_params("parallel", "parallel", "arbitrary"),
        name=name,
    )(*args)


def _proj_heads_kernel(x_ref, w_ref, o2_ref, o3_ref):
    acc = _dot(x_ref[...], w_ref[...])
    o2_ref[...] = acc.astype(o2_ref.dtype)
    o3_ref[...] = pltpu.einshape("m(hd)->mhd", acc, h=o3_ref.shape[1])


def _proj_heads(x, w, layer, *, col0, n, row0, m, name):
    kdim = x.shape[1]
    tm, tn = _tile(m, 1024), 8 * HEAD_DIM
    assert row0 % tm == 0 and col0 % tn == 0 and n % tn == 0
    rb, cb = row0 // tm, col0 // tn
    return pl.pallas_call(
        _proj_heads_kernel,
        grid=(m // tm, n // tn),
        in_specs=[pl.BlockSpec((tm, kdim), lambda i, j: (i + rb, 0)),
                  pl.BlockSpec((None, kdim, tn), lambda i, j: (layer, 0, j + cb))],
        out_specs=[pl.BlockSpec((tm, tn), lambda i, j: (i, j)),
                   pl.BlockSpec((tm, tn // HEAD_DIM, HEAD_DIM), lambda i, j: (i, j, 0))],
        out_shape=[jax.ShapeDtypeStruct((m, n), BF16),
                   jax.ShapeDtypeStruct((m, n // HEAD_DIM, HEAD_DIM), F32)],
        compiler_params=_params("parallel", "parallel"),
        name=name,
    )(x, w)


def _band_head(q, ka, kb, va, vb, bias_a, bias_b, past_visible):
    sa = _dot_nt(q, ka.astype(BF16)) * ATTN_SCALE + bias_a
    sb = _dot_nt(q, kb.astype(BF16)) * ATTN_SCALE + bias_b
    if past_visible is not None:
        sa = jnp.where(past_visible, sa, NEG_INF)
    m = jnp.maximum(jnp.max(sa, axis=-1, keepdims=True), jnp.max(sb, axis=-1, keepdims=True))
    pa = jnp.exp(sa - m)
    pb = jnp.exp(sb - m)
    l = jnp.sum(pa, axis=-1, keepdims=True) + jnp.sum(pb, axis=-1, keepdims=True)
    o = _dot(pa.astype(BF16), va.astype(BF16)) + _dot(pb.astype(BF16), vb.astype(BF16))
    return o / l


def _band_prompt_kernel(q_ref, ka_ref, kb_ref, va_ref, vb_ref, bias_ref, o_ref):
    na = ka_ref.shape[0]
    o = _band_head(q_ref[...], ka_ref[...], kb_ref[...], va_ref[...], vb_ref[...],
                   bias_ref[:, :na], bias_ref[:, na:], pl.program_id(1) > 0)
    o_ref[...] = o.astype(o_ref.dtype)


def _band_sample_kernel(q_ref, kn_ref, vn_ref, kc_ref, vc_ref, bias_ref, o_ref):
    win, heads = kc_ref.shape[0], kc_ref.shape[1]
    kc = pltpu.einshape("phd->hpd", kc_ref[...])
    vc = pltpu.einshape("phd->hpd", vc_ref[...])
    for hh in range(heads):
        sl = slice(hh * HEAD_DIM, (hh + 1) * HEAD_DIM)
        o = _band_head(q_ref[:, sl], kc[hh], kn_ref[:, sl], vc[hh], vn_ref[:, sl],
                       bias_ref[hh, :, :win], bias_ref[hh, :, win:], None)
        o_ref[:, sl] = o.astype(o_ref.dtype)


def _band_bias(rel_bias, q_pos, k_pos):
    lq, lk = q_pos.shape[0], k_pos.shape[0]
    d_min = q_pos[0] - k_pos[lk - 1]
    n_diag = lq + lk - 1
    diag = jnp.clip(d_min + jnp.arange(n_diag, dtype=jnp.int32), -REL_CLIP, REL_CLIP) + REL_CLIP
    g = rel_bias.astype(F32)[:, diag]
    row_len = -(-n_diag // HEAD_DIM) * HEAD_DIM
    period = row_len + 1
    u = jnp.pad(g[:, ::-1], ((0, 0), (0, period - n_diag)))
    skew = jnp.tile(u, (1, lq))[:, :lq * row_len].reshape(-1, lq, row_len)
    bias = skew[:, :, lq - 1:lq - 1 + lk]
    qc = q_pos[:, None] // CHUNK
    kc = k_pos[None, :] // CHUNK
    mask = (kc <= qc) & (kc >= qc - A_PAST_CHUNKS)
    return jnp.where(mask[None], bias, NEG_INF)


def _band_prompt(q, k, v, rel_bias, sp):
    heads = rel_bias.shape[0]
    tq = A_PAST_CHUNKS * CHUNK
    assert sp % tq == 0
    pos = jnp.arange(tq, dtype=jnp.int32)
    bias = _band_bias(rel_bias, tq + pos, jnp.arange(2 * tq, dtype=jnp.int32))
    prev = lambda h, i: (jnp.maximum(i - 1, 0), h)
    cur = lambda h, i: (i, h)
    blk = lambda im: pl.BlockSpec((tq, HEAD_DIM), im)
    return pl.pallas_call(
        _band_prompt_kernel,
        grid=(heads, sp // tq),
        in_specs=[blk(cur), blk(prev), blk(cur), blk(prev), blk(cur),
                  pl.BlockSpec((None, tq, 2 * tq), lambda h, i: (h, 0, 0))],
        out_specs=blk(cur),
        out_shape=jax.ShapeDtypeStruct((q.shape[0], heads * HEAD_DIM), BF16),
        compiler_params=_params("parallel", "arbitrary"),
        name="band_prompt",
    )(q, k, k, v, v, bias)


def _band_sample(q, k, v, cache_k, cache_v, layer, rel_bias, sp, ls, past_len, out):
    _, bs, win, heads, _ = cache_k.shape
    width = heads * HEAD_DIM
    assert sp % ls == 0
    row0 = sp // ls
    q_pos = past_len + jnp.arange(ls, dtype=jnp.int32)
    k_pos = past_len - win + jnp.arange(win + ls, dtype=jnp.int32)
    bias = _band_bias(rel_bias, q_pos, k_pos)
    blk_new = pl.BlockSpec((ls, width), lambda b: (row0 + b, 0))
    blk_old = pl.BlockSpec((None, None, win, heads, HEAD_DIM), lambda b: (layer, b, 0, 0, 0))
    return pl.pallas_call(
        _ignore_input(_band_sample_kernel, 6),
        grid=(bs,),
        in_specs=[blk_new, blk_new, blk_new, blk_old, blk_old,
                  pl.BlockSpec((heads, ls, win + ls), lambda b: (0, 0, 0)),
                  pl.BlockSpec(memory_space=pl.ANY)],
        out_specs=blk_new,
        out_shape=jax.ShapeDtypeStruct(out.shape, out.dtype),
        input_output_aliases={6: 0},
        compiler_params=_params("parallel"),
        name="band_sample",
    )(q, k, v, cache_k, cache_v, bias, out)


def _pool_kernel(cur_ref, hist_ref, w_ref, sc_ref, o_ref, ext_ref, *, pos_base, pos_step, first_has_no_past):
    tr = cur_ref.shape[0]
    hist = hist_ref[...]
    if first_has_no_past:
        hist = jnp.where(pl.program_id(0) > 0, hist, 0.0)
    ext_ref[0:POOL_HIST_PAD, :] = hist
    ext_ref[POOL_HIST_PAD:POOL_HIST_PAD + tr, :] = cur_ref[...]
    pos = pos_base + pl.program_id(0) * pos_step + lax.broadcasted_iota(jnp.int32, (tr, 1), 0)
    for g, w in enumerate(POOL_WINDOWS):
        sl = slice(g * B_GROUP_WIDTH, (g + 1) * B_GROUP_WIDTH)
        cur = cur_ref[:, sl]
        tot = cur
        for j in range(1, w):
            tot = tot + ext_ref[POOL_HIST_PAD - j:POOL_HIST_PAD - j + tr, sl]
        cnt = jnp.minimum(pos + 1, w).astype(F32)
        d = tot / cnt - cur
        y = _dot(d.astype(BF16), w_ref[g]) * sc_ref[:, sl]
        o_ref[:, sl] = y.astype(o_ref.dtype)


def _pool(p, hist, pool_w, pool_scale, *, row_block0, nblk, tr, pos_base, pos_step, name, out=None):
    width = p.shape[1]
    rows = pl.BlockSpec((tr, width), lambda i: (row_block0 + i, 0))
    body = functools.partial(_pool_kernel, pos_base=pos_base, pos_step=pos_step, first_has_no_past=hist is None)
    if hist is None:
        assert row_block0 == 0 and tr % POOL_HIST_PAD == 0
        per = tr // POOL_HIST_PAD
        hist_spec = pl.BlockSpec((POOL_HIST_PAD, width), lambda i: (jnp.maximum(i * per - 1, 0), 0))
        hist = p
    else:
        hist_spec = pl.BlockSpec((None, POOL_HIST_PAD, width), lambda i: (i, 0, 0))
    in_specs = [rows, hist_spec,
                pl.BlockSpec(pool_w.shape, lambda i: (0, 0, 0)),
                pl.BlockSpec((1, width), lambda i: (0, 0))]
    args = [p, hist, pool_w, pool_scale.reshape(1, width)]
    aliases = {}
    if out is not None:
        body = _ignore_input(body, len(args))
        aliases = {len(args): 0}
        in_specs.append(pl.BlockSpec(memory_space=pl.ANY))
        args.append(out)
    return pl.pallas_call(
        body,
        grid=(nblk,),
        in_specs=in_specs,
        out_specs=rows,
        out_shape=jax.ShapeDtypeStruct(p.shape, BF16),
        input_output_aliases=aliases,
        scratch_shapes=[pltpu.VMEM((POOL_HIST_PAD + tr, width), F32)],
        compiler_params=_params("parallel"),
        name=name,
    )(*args)


def _sb_steps(qs, kvs, tri, carries, accs, visibles):
    nk = kvs[0][0].shape[0]
    zs = [_dot_nt(q, k) * ATTN_SCALE for q, (k, _) in zip(qs, kvs)]
    log_keeps = [jnp.where(vis, -(jnp.maximum(z, 0.0) + jnp.log(1.0 + jnp.exp(-jnp.abs(z)))), 0.0)
                 for z, vis in zip(zs, visibles)]
    his = [lk.astype(BF16) for lk in log_keeps]
    los = [(lk - hi.astype(F32)).astype(BF16) for lk, hi in zip(log_keeps, his)]
    sums = [_dot(hi, tri) + _dot(lo, tri) for hi, lo in zip(his, los)]
    weights = [jnp.where(vis, jnp.exp(z + lk + s[:, :nk] + carry[:, :nk]), 0.0).astype(BF16)
               for z, lk, s, carry, vis in zip(zs, log_keeps, sums, carries, visibles)]
    new_accs = [acc + _dot(a, v) for acc, a, (_, v) in zip(accs, weights, kvs)]
    new_carries = [carry + s[:, nk:] for carry, s in zip(carries, sums)]
    return new_carries, new_accs


def _sb_sweep(chains, load_kv, tri, carries, accs):
    rows = chains[0][0].shape[0]
    lane = lax.broadcasted_iota(jnp.int32, (rows, SB_KEYS), 1)
    beyond = jnp.int32(2 ** 30)

    def cond(st):
        return st[1] == 0

    def body(st):
        n, _, carries, accs = st
        kbs = [kb0 - n for _, kb0, _ in chains]
        starts = [pl.multiple_of(jnp.maximum(kb, 0) * SB_KEYS, SB_KEYS) for kb in kbs]
        visibles = [(jnp.where(kb >= 0, start, beyond) + lane) < qpos
                    for (_, _, qpos), kb, start in zip(chains, kbs, starts)]
        carries, accs = _sb_steps([q for q, _, _ in chains], load_kv(starts), tri, carries, accs, visibles)
        pending = jnp.full((rows, HEAD_DIM), -jnp.inf, F32)
        for kb, carry in zip(kbs, carries):
            pending = jnp.maximum(pending, jnp.where(kb > 0, carry, -jnp.inf))
        done = (jnp.max(pending) < SB_EXIT).astype(jnp.int32)
        return n + 1, done, carries, accs

    return lax.while_loop(cond, body, (jnp.int32(0), jnp.int32(0), list(carries), list(accs)))[3]


def _sb_prompt_kernel(q_ref, k_ref, v_ref, tri_ref, o_ref, *, tq):
    n_chains = q_ref.shape[0] // tq
    row0 = pl.program_id(1) * q_ref.shape[0]
    row = lax.broadcasted_iota(jnp.int32, (tq, SB_KEYS), 0)

    def load_kv(starts):
        return [(k_ref[pl.ds(s, SB_KEYS), :].astype(BF16), v_ref[pl.ds(s, SB_KEYS), :].astype(BF16))
                for s in starts]

    chains = [(q_ref[c * tq:(c + 1) * tq, :], (row0 + (c + 1) * tq) // SB_KEYS - 1, row0 + c * tq + row)
              for c in range(n_chains)]
    zeros = [jnp.zeros((tq, HEAD_DIM), F32)] * n_chains
    accs = _sb_sweep(chains, load_kv, tri_ref[...], zeros, zeros)
    for c in range(n_chains):
        o_ref[c * tq:(c + 1) * tq, :] = accs[c].astype(o_ref.dtype)


def _sb_sample_kernel(q_ref, kn_ref, vn_ref, kc_ref, vc_ref, trin_ref, tri_ref, o_ref):
    ls = q_ref.shape[0]
    past, heads = kc_ref.shape[0], kc_ref.shape[1]
    causal = lax.broadcasted_iota(jnp.int32, (ls, ls), 1) < lax.broadcasted_iota(jnp.int32, (ls, ls), 0)
    all_rows = jnp.full((ls, SB_KEYS), 2 ** 30 - SB_KEYS, jnp.int32)
    zeros = jnp.zeros((ls, HEAD_DIM), F32)
    head_cols = [slice(hh * HEAD_DIM, (hh + 1) * HEAD_DIM) for hh in range(heads)]
    chains = [(q_ref[:, sl], past // SB_KEYS - 1, all_rows) for sl in head_cols]
    carries, accs = _sb_steps([q for q, _, _ in chains],
                              [(kn_ref[:, sl].astype(BF16), vn_ref[:, sl].astype(BF16)) for sl in head_cols],
                              trin_ref[...], [zeros] * heads, [zeros] * heads, [causal] * heads)

    def load_kv(starts):
        k = pltpu.einshape("phd->hpd", kc_ref[pl.ds(starts[0], SB_KEYS), :, :])
        v = pltpu.einshape("phd->hpd", vc_ref[pl.ds(starts[0], SB_KEYS), :, :])
        return [(k[hh].astype(BF16), v[hh].astype(BF16)) for hh in range(heads)]

    accs = _sb_sweep(chains, load_kv, tri_ref[...], carries, accs)
    for hh in range(heads):
        o_ref[:, hh * HEAD_DIM:(hh + 1) * HEAD_DIM] = accs[hh].astype(o_ref.dtype)


def _sb_tri(nk):
    j = jnp.arange(nk)[:, None]
    s = jnp.arange(nk + HEAD_DIM)[None, :]
    return ((s >= nk) | (j > s)).astype(BF16)


def _sb_prompt(q, k, v, sp):
    heads = q.shape[1] // HEAD_DIM
    tq = SB_KEYS
    tb = _tile(sp, 8 * tq)
    assert sp % SB_KEYS == 0 and tb % tq == 0
    return pl.pallas_call(
        functools.partial(_sb_prompt_kernel, tq=tq),
        grid=(heads, sp // tb),
        in_specs=[pl.BlockSpec((tb, HEAD_DIM), lambda h, i: (i, h)),
                  pl.BlockSpec((sp, HEAD_DIM), lambda h, i: (0, h)),
                  pl.BlockSpec((sp, HEAD_DIM), lambda h, i: (0, h)),
                  pl.BlockSpec((SB_KEYS, SB_KEYS + HEAD_DIM), lambda h, i: (0, 0))],
        out_specs=pl.BlockSpec((tb, HEAD_DIM), lambda h, i: (i, h)),
        out_shape=jax.ShapeDtypeStruct(q.shape, BF16),
        compiler_params=_params("parallel", "arbitrary"),
        name="stickbreak_prompt",
    )(q, k, v, _sb_tri(SB_KEYS))


def _sb_sample(q, k, v, cache_k, cache_v, layer, sp, ls, out):
    n_layers, bs, past, heads, _ = cache_k.shape
    hg = 8
    assert past % SB_KEYS == 0 and past >= SB_KEYS and sp % ls == 0 and heads % hg == 0
    row0 = sp // ls
    gw = hg * HEAD_DIM
    blk_new = pl.BlockSpec((ls, gw), lambda b, g: (row0 + b, g))
    blk_kv = pl.BlockSpec((ls, gw), lambda b, g: (b, g))
    blk_old = pl.BlockSpec((None, None, past, None, hg, HEAD_DIM), lambda b, g: (layer, b, 0, g, 0, 0))
    grouped = (n_layers, bs, past, heads // hg, hg, HEAD_DIM)
    return pl.pallas_call(
        _ignore_input(_sb_sample_kernel, 7),
        grid=(bs, heads // hg),
        in_specs=[blk_new, blk_kv, blk_kv, blk_old, blk_old,
                  pl.BlockSpec((ls, ls + HEAD_DIM), lambda b, g: (0, 0)),
                  pl.BlockSpec((SB_KEYS, SB_KEYS + HEAD_DIM), lambda b, g: (0, 0)),
                  pl.BlockSpec(memory_space=pl.ANY)],
        out_specs=blk_new,
        out_shape=jax.ShapeDtypeStruct(out.shape, out.dtype),
        input_output_aliases={7: 0},
        compiler_params=_params("parallel", "arbitrary"),
        name="stickbreak_sample",
    )(q, k, v, cache_k.reshape(grouped), cache_v.reshape(grouped), _sb_tri(ls), _sb_tri(SB_KEYS), out)


def _peer_scores_kernel(key_ref, q_ref, o_ref):
    o_ref[...] = _dot_nt(key_ref[...], q_ref[...])


def _peer_scores(pq, sub_keys, layer):
    t = pq.shape[0]
    _, nhc, nkeys, half = sub_keys.shape
    tt = _tile(t, 1024)
    return pl.pallas_call(
        _peer_scores_kernel,
        grid=(nhc, t // tt),
        in_specs=[pl.BlockSpec((None, None, nkeys, half), lambda c, i: (layer, c, 0, 0)),
                  pl.BlockSpec((tt, half), lambda c, i: (i, c))],
        out_specs=pl.BlockSpec((None, nkeys, tt), lambda c, i: (c, 0, i)),
        out_shape=jax.ShapeDtypeStruct((nhc, nkeys, t), F32),
        compiler_params=_params("parallel", "parallel"),
        name="peer_scores",
    )(sub_keys, pq)


def _extract_top(s, k, tie_safe):
    rows = lax.broadcasted_iota(jnp.int32, s.shape, 0)
    rank = jnp.full(s.shape, float(k), F32)
    vals = []
    for r in range(k):
        m = jnp.max(s, axis=0, keepdims=True)
        vals.append(m)
        hit = s == m
        if tie_safe:
            hit = rows == jnp.min(jnp.where(hit, rows, s.shape[0]), axis=0, keepdims=True)
        rank = jnp.where(hit, float(r), rank)
        s = jnp.where(hit, -jnp.inf, s)
    return vals, rank, s


def _peer_select_kernel(s_ref, m1x_ref, e1x_ref, rank2_ref, e2_ref, *, ne):
    def select(h, tie_safe):
        s1 = s_ref[2 * h]
        s2 = s_ref[2 * h + 1]
        a, rank1, _ = _extract_top(s1, PEER_TOPK, tie_safe)
        b, rank2, _ = _extract_top(s2, PEER_TOPK, tie_safe)
        b = jnp.concatenate(b, axis=0)
        sub = lax.broadcasted_iota(jnp.int32, (8, b.shape[1]), 0)
        cand = [a[0] + b, a[1] + b[:8]]
        for i in range(2, 8):
            cand.append(jnp.where(sub < PEER_TOPK // (i + 1), a[i] + b[:8], -jnp.inf))
        cand.append(jnp.concatenate(a[8:], axis=0) + b[0:1])
        cand = jnp.concatenate(cand, axis=0)
        g, _, left = _extract_top(cand, PEER_TOPK, tie_safe)
        z = jnp.ones_like(g[0])
        for gk in g[1:]:
            z = z + jnp.exp(gk - g[0])
        took = (left != cand).astype(F32)
        taken = [jnp.sum(took[0:16], axis=0, keepdims=True), jnp.sum(took[16:24], axis=0, keepdims=True)]
        taken += [jnp.sum(took[8 * i + 8:8 * i + 16], axis=0, keepdims=True) for i in range(2, 8)]
        taken += [took[72 + i:73 + i] for i in range(8)]
        m1 = jnp.zeros_like(s1)
        for i in range(PEER_TOPK):
            m1 = jnp.where(rank1 == float(i), taken[i], m1)
        e1 = jnp.exp(s1 - a[0]) / z
        rank2_ref[h] = rank2.astype(rank2_ref.dtype)
        e2_ref[h] = jnp.exp(s2 - b[0:1]).astype(e2_ref.dtype)
        row = pl.multiple_of(h * ne, ne)
        for j in range(N_KEYS // ne):
            m1x_ref[j, pl.ds(row, ne), :] = m1[j * ne:(j + 1) * ne]
            e1x_ref[j, pl.ds(row, ne), :] = e1[j * ne:(j + 1) * ne]
        removed = (jnp.sum((rank1 < PEER_TOPK).astype(F32), axis=0, keepdims=True)
                   + jnp.sum((rank2 < PEER_TOPK).astype(F32), axis=0, keepdims=True)
                   + jnp.sum(took, axis=0, keepdims=True))
        return jnp.max(removed) - 3.0 * PEER_TOPK

    def one_head(h, _):
        extra = select(h, tie_safe=False)

        @pl.when(extra > 0.0)
        def _():
            select(h, tie_safe=True)

        return 0

    lax.fori_loop(0, PEER_HEADS, one_head, 0)


def _peer_select(scores, ne):
    nhc, nkeys, t = scores.shape
    tt = _tile(t, 256)
    x_shape = jax.ShapeDtypeStruct((nkeys // ne, PEER_HEADS * ne, t), F32)
    x_spec = pl.BlockSpec((nkeys // ne, PEER_HEADS * ne, tt), lambda i: (0, 0, i))
    y_shape = jax.ShapeDtypeStruct((PEER_HEADS, nkeys, t), BF16)
    y_spec = pl.BlockSpec((PEER_HEADS, nkeys, tt), lambda i: (0, 0, i))
    return pl.pallas_call(
        functools.partial(_peer_select_kernel, ne=ne),
        grid=(t // tt,),
        in_specs=[pl.BlockSpec((nhc, nkeys, tt), lambda i: (0, 0, i))],
        out_specs=[x_spec, x_spec, y_spec, y_spec],
        out_shape=[x_shape, x_shape, y_shape, y_shape],
        compiler_params=_params("parallel"),
        name="peer_select",
    )(scores)


def _peer_act_kernel(*refs, ne, n_parts):
    u_ref, x_ref = refs[0], refs[1]
    part_refs = refs[2:2 + 4 * n_parts]
    o_ref, gate_ref = refs[2 + 4 * n_parts], refs[3 + 4 * n_parts]
    first_row = (pl.program_id(1) % (ROW_BLOCK // ne)) * ne
    for part in range(n_parts):
        m1x_ref, e1x_ref, rank2_ref, e2_ref = part_refs[4 * part:4 * part + 4]
        tp = rank2_ref.shape[2]
        for e in range(ne):
            gate = jnp.zeros((N_KEYS, tp), BF16)
            for h in range(PEER_HEADS):
                r = pl.ds(h * ROW_BLOCK + first_row + e, 1)
                hit = rank2_ref[h] < m1x_ref[r, :].astype(BF16)
                w = e2_ref[h] * e1x_ref[r, :].astype(BF16)
                gate = gate + jnp.where(hit, w, jnp.zeros_like(w))
            gate_ref[part * tp:(part + 1) * tp, e * N_KEYS:(e + 1) * N_KEYS] = gate.T
    x = x_ref[...]
    for p in range(ne // 2):
        cols = slice(2 * p * N_KEYS, 2 * (p + 1) * N_KEYS)
        hp = _dot_nt(x, u_ref[cols, :].astype(BF16))
        gelu = 0.5 * hp * (1.0 + lax.erf(hp * (0.5 ** 0.5)))
        o_ref[:, cols] = gate_ref[:, cols] * gelu.astype(BF16)


def _peer_act(xn, u, layer, m1x, e1x, rank2, e2, ne):
    t, d = xn.shape
    n_exp = u.shape[1]
    tp = _tile(t, 512)
    n_parts = 2 if t % (2 * tp) == 0 else 1
    tt = n_parts * tp
    te = ne * N_KEYS
    once = pl.Buffered(1)
    part_specs, part_args = [], []
    for part in range(n_parts):
        row_spec = pl.BlockSpec((None, PEER_HEADS * ROW_BLOCK, tp),
                                lambda i, j, part=part: (j // (ROW_BLOCK // ne), 0, n_parts * i + part))
        tile_spec = pl.BlockSpec((PEER_HEADS, N_KEYS, tp), lambda i, j, part=part: (0, 0, n_parts * i + part),
                                 pipeline_mode=once)
        part_specs += [row_spec, row_spec, tile_spec, tile_spec]
        part_args += [m1x, e1x, rank2, e2]
    return pl.pallas_call(
        functools.partial(_peer_act_kernel, ne=ne, n_parts=n_parts),
        grid=(t // tt, n_exp // te),
        in_specs=[pl.BlockSpec((None, te, d), lambda i, j: (layer, j, 0)),
                  pl.BlockSpec((tt, d), lambda i, j: (i, 0), pipeline_mode=once)] + part_specs,
        out_specs=pl.BlockSpec((tt, te), lambda i, j: (i, j)),
        out_shape=jax.ShapeDtypeStruct((t, n_exp), BF16),
        scratch_shapes=[pltpu.VMEM((tt, te), BF16)],
        compiler_params=_params("parallel", "arbitrary"),
        name="peer_act",
    )(u, xn, *part_args)


def _peer_ffn(h, layer, norm_g, w_q, sub_keys, u, v):
    ne = 4
    xn = _rmsnorm(h, norm_g)
    pq = _matmul(xn, w_q, layer, out_dtype=BF16, name="peer_query")
    scores = _peer_scores(pq, sub_keys, layer)
    m1x, e1x, rank2, e2 = _peer_select(scores, ROW_BLOCK)
    act = _peer_act(xn, u, layer, m1x, e1x, rank2, e2, ne)
    return _matmul(act, v, layer, mode="residual", aux=h, tm=1024, tn=1024, tk=2048, name="peer_down")


def kernel(x_prompt, x_sample, cache_a_k, cache_a_v, state_b_pool, cache_c_k, cache_c_v, norm_mix, norm_ffn, ab_w_in, ab_q_gain, ab_k_gain, ab_rel_bias, ab_pool_w, ab_pool_scale, ab_w_out, c_w_in, c_w_out, peer_w_q, peer_sub_keys, peer_u, peer_v):
    bp, sp, d = x_prompt.shape
    bs, ls, _ = x_sample.shape
    assert bp == 1
    depth = norm_mix.shape[0]
    past_len = cache_c_k.shape[2]
    a_heads = ab_rel_bias.shape[1]
    a_width = a_heads * HEAD_DIM
    b_width = ab_pool_scale.shape[1]
    c_width = c_w_out.shape[1]
    win_p = min(A_PAST_CHUNKS * CHUNK, sp)
    n_s = bs * ls

    h = jnp.concatenate([x_prompt.reshape(sp, d), x_sample.reshape(n_s, d)], axis=0)

    ab_w_in_b, ab_w_out_b = ab_w_in.astype(BF16), ab_w_out.astype(BF16)
    c_w_in_b, c_w_out_b = c_w_in.astype(BF16), c_w_out.astype(BF16)
    peer_w_q_b, peer_v_b = peer_w_q.astype(BF16), peer_v.astype(BF16)
    sub_keys_b = peer_sub_keys.astype(BF16).reshape(depth, PEER_HEADS * 2, N_KEYS, -1)

    outs = {name: [] for name in ("a_k_p", "a_v_p", "b_p", "c_k_p", "c_v_p",
                                  "a_k_s", "a_v_s", "b_s", "c_k_s", "c_v_s")}
    for layer in range(depth):
        xn = _rmsnorm(h, norm_mix[layer])
        if layer % 2 == 0:
            i = layer // 2
            q = _matmul(xn, ab_w_in_b, i, col0=0, n=a_width, out_dtype=BF16, mode="headnorm",
                        aux=ab_q_gain[i], name="a_q")
            k = _matmul(xn, ab_w_in_b, i, col0=a_width, n=a_width, mode="headnorm", aux=ab_k_gain[i], name="a_k")
            v = _matmul(xn, ab_w_in_b, i, col0=2 * a_width, n=a_width, name="a_v")
            p = _matmul(xn, ab_w_in_b, i, col0=3 * a_width, n=b_width, name="b_in")

            att = _band_prompt(q, k, v, ab_rel_bias[i], sp)
            att = _band_sample(q, k, v, cache_a_k, cache_a_v, i, ab_rel_bias[i], sp, ls, past_len, att)

            pool_w = ab_pool_w[i].astype(BF16)
            tr = _tile(sp, 256)
            pool = _pool(p, None, pool_w, ab_pool_scale[i], row_block0=0, nblk=sp // tr, tr=tr,
                         pos_base=0, pos_step=tr, name="pool_prompt")
            hist_s = jnp.pad(state_b_pool[i], ((0, 0), (POOL_HIST_PAD - POOL_HIST, 0), (0, 0)))
            pool = _pool(p, hist_s, pool_w, ab_pool_scale[i], row_block0=sp // ls, nblk=bs, tr=ls,
                         pos_base=past_len, pos_step=0, name="pool_sample", out=pool)

            h = _matmul(att, ab_w_out_b, i, x2=pool, mode="residual", aux=h, name="a_out")

            ps = p[sp:].reshape(bs, ls, b_width)
            outs["a_k_p"].append(k[sp - win_p:sp].reshape(bp, win_p, a_heads, HEAD_DIM))
            outs["a_v_p"].append(v[sp - win_p:sp].reshape(bp, win_p, a_heads, HEAD_DIM))
            outs["b_p"].append(p[sp - POOL_HIST:sp].reshape(bp, POOL_HIST, b_width))
            outs["a_k_s"].append(k[sp:].reshape(bs, ls, a_heads, HEAD_DIM))
            outs["a_v_s"].append(v[sp:].reshape(bs, ls, a_heads, HEAD_DIM))
            outs["b_s"].append(jnp.concatenate([state_b_pool[i], ps], axis=1)[:, ls:])
        else:
            j = layer // 2
            c_heads = c_width // HEAD_DIM
            q = _matmul(xn, c_w_in_b, j, col0=0, n=c_width, out_dtype=BF16, name="c_q")
            k_p, k_p3 = _proj_heads(xn, c_w_in_b, j, col0=c_width, n=c_width, row0=0, m=sp, name="c_k_prompt")
            v_p, v_p3 = _proj_heads(xn, c_w_in_b, j, col0=2 * c_width, n=c_width, row0=0, m=sp, name="c_v_prompt")
            k_s, k_s3 = _proj_heads(xn, c_w_in_b, j, col0=c_width, n=c_width, row0=sp, m=n_s, name="c_k_sample")
            v_s, v_s3 = _proj_heads(xn, c_w_in_b, j, col0=2 * c_width, n=c_width, row0=sp, m=n_s, name="c_v_sample")
            att = _sb_prompt(q, k_p, v_p, sp)
            att = _sb_sample(q, k_s, v_s, cache_c_k, cache_c_v, j, sp, ls, att)
            h = _matmul(att, c_w_out_b, j, mode="residual", aux=h, name="c_out")
            outs["c_k_p"].append(k_p3.reshape(bp, sp, c_heads, HEAD_DIM))
            outs["c_v_p"].append(v_p3.reshape(bp, sp, c_heads, HEAD_DIM))
            outs["c_k_s"].append(k_s3.reshape(bs, ls, c_heads, HEAD_DIM))
            outs["c_v_s"].append(v_s3.reshape(bs, ls, c_heads, HEAD_DIM))

        h = _peer_ffn(h, layer, norm_ffn[layer], peer_w_q_b, sub_keys_b, peer_u, peer_v_b)

    st = {name: jnp.stack(vals) for name, vals in outs.items()}
    return (h[:sp].reshape(bp, sp, d), h[sp:].reshape(bs, ls, d),
            st["a_k_p"], st["a_v_p"], st["b_p"], st["c_k_p"], st["c_v_p"],
            st["a_k_s"], st["a_v_s"], st["b_s"], st["c_k_s"], st["c_v_s"])
```

```python
import functools

import jax
import jax.numpy as jnp
from jax import lax
from jax.experimental import pallas as pl
from jax.experimental.pallas import tpu as pltpu

F32 = jnp.float32
BF16 = jnp.bfloat16

HEAD_DIM = 128
CHUNK = 64
A_PAST_CHUNKS = 8
REL_CLIP = 128
POOL_WINDOWS = (2, 4, 8, 16)
POOL_HIST = max(POOL_WINDOWS) - 1
POOL_HIST_PAD = 16
B_GROUP_WIDTH = 512
PEER_HEADS = 8
N_KEYS = 128
PEER_TOPK = 16
ROW_BLOCK = 8
EPS = 1e-6
NEG_INF = -1e30
ATTN_SCALE = HEAD_DIM ** -0.5
SB_EXIT = -104.0
SB_KEYS = 128
VMEM_LIMIT_BYTES = 56 * 1024 * 1024


def _params(*sem):
    return pltpu.CompilerParams(dimension_semantics=sem, vmem_limit_bytes=VMEM_LIMIT_BYTES)


def _tile(dim, pref):
    return pref if dim % pref == 0 else dim


def _ignore_input(body, index):
    def wrapped(*refs):
        return body(*refs[:index], *refs[index + 1:])
    return wrapped


def _dot(a, b):
    return jnp.dot(a, b, preferred_element_type=F32)


def _dot_nt(a, b):
    return lax.dot_general(a, b, (((1,), (1,)), ((), ())), preferred_element_type=F32)


def _rmsnorm_kernel(x_ref, g_ref, o_ref):
    x = x_ref[...]
    ms = jnp.mean(x * x, axis=-1, keepdims=True)
    o_ref[...] = (x * lax.rsqrt(ms + EPS) * g_ref[...]).astype(o_ref.dtype)


def _rmsnorm(x, g):
    t, d = x.shape
    tr = _tile(t, 256)
    return pl.pallas_call(
        _rmsnorm_kernel,
        grid=(t // tr,),
        in_specs=[pl.BlockSpec((tr, d), lambda i: (i, 0)),
                  pl.BlockSpec((1, d), lambda i: (0, 0))],
        out_specs=pl.BlockSpec((tr, d), lambda i: (i, 0)),
        out_shape=jax.ShapeDtypeStruct((t, d), BF16),
        compiler_params=_params("parallel"),
        name="rmsnorm",
    )(x, g.reshape(1, d))


def _mm_kernel(*refs, nk, mode, two_x):
    x_ref, w_ref = refs[0], refs[1]
    refs = refs[2:]
    if two_x:
        x2_ref, refs = refs[0], refs[1:]
    if mode in ("headnorm", "residual"):
        aux_ref, o_ref, scratch = refs[0], refs[1], refs[2:]
    else:
        aux_ref, o_ref, scratch = None, refs[0], refs[1:]

    def finish(acc):
        if mode == "headnorm":
            g = aux_ref[...]
            for c in range(acc.shape[1] // HEAD_DIM):
                sl = slice(c * HEAD_DIM, (c + 1) * HEAD_DIM)
                y = acc[:, sl]
                ms = jnp.mean(y * y, axis=-1, keepdims=True)
                o_ref[:, sl] = (y * lax.rsqrt(ms + EPS) * g).astype(o_ref.dtype)
        elif mode == "residual":
            o_ref[...] = aux_ref[...] + acc
        else:
            o_ref[...] = acc.astype(o_ref.dtype)

    if two_x:
        k1 = x_ref.shape[1]
        finish(_dot(x_ref[...], w_ref[:k1, :]) + _dot(x2_ref[...], w_ref[k1:, :]))
    elif nk == 1:
        finish(_dot(x_ref[...], w_ref[...]))
    else:
        acc_ref = scratch[0]
        k = pl.program_id(2)

        @pl.when(k == 0)
        def _():
            acc_ref[...] = jnp.zeros_like(acc_ref)

        acc_ref[...] += _dot(x_ref[...], w_ref[...])

        @pl.when(k == nk - 1)
        def _():
            finish(acc_ref[...])


def _matmul(x, w, layer, *, x2=None, col0=0, n=None, row0=0, m=None, out_dtype=F32, mode="plain", aux=None,
            tm=1024, tn=512, tk=None, name="matmul"):
    kdim = x.shape[1]
    m = x.shape[0] if m is None else m
    if x2 is not None:
        assert tk is None
        kdim += x2.shape[1]
    n = w.shape[2] if n is None else n
    tm, tn = _tile(m, tm), _tile(n, tn)
    tk = kdim if tk is None else _tile(kdim, tk)
    nk = kdim // tk
    rb, cb = row0 // tm, col0 // tn
    assert row0 % tm == 0 and col0 % tn == 0
    in_specs = [pl.BlockSpec((tm, x.shape[1] if x2 is not None else tk), lambda i, j, k: (i + rb, k)),
                pl.BlockSpec((None, tk, tn), lambda i, j, k: (layer, k, j + cb))]
    args = [x, w]
    if x2 is not None:
        in_specs.append(pl.BlockSpec((tm, x2.shape[1]), lambda i, j, k: (i + rb, 0)))
        args.append(x2)
    if mode == "headnorm":
        in_specs.append(pl.BlockSpec((1, HEAD_DIM), lambda i, j, k: (0, 0)))
        args.append(aux.reshape(1, HEAD_DIM))
    elif mode == "residual":
        in_specs.append(pl.BlockSpec((tm, tn), lambda i, j, k: (i + rb, j)))
        args.append(aux)
    scratch = [pltpu.VMEM((tm, tn), F32)] if nk > 1 else []
    return pl.pallas_call(
        functools.partial(_mm_kernel, nk=nk, mode=mode, two_x=x2 is not None),
        grid=(m // tm, n // tn, nk),
        in_specs=in_specs,
        out_specs=pl.BlockSpec((tm, tn), lambda i, j, k: (i, j)),
        out_shape=jax.ShapeDtypeStruct((m, n), out_dtype),
        scratch_shapes=scratch,
        compiler_params=_params("parallel", "parallel", "arbitrary"),
        name=name,
    )(*args)


def _proj_heads_kernel(x_ref, w_ref, o2_ref, o3_ref):
    acc = _dot(x_ref[...], w_ref[...])
    o2_ref[...] = acc.astype(o2_ref.dtype)
    o3_ref[...] = pltpu.einshape("m(hd)->mhd", acc, h=o3_ref.shape[1])


def _proj_heads(x, w, layer, *, col0, n, row0, m, name):
    kdim = x.shape[1]
    tm, tn = _tile(m, 1024), 8 * HEAD_DIM
    assert row0 % tm == 0 and col0 % tn == 0 and n % tn == 0
    rb, cb = row0 // tm, col0 // tn
    return pl.pallas_call(
        _proj_heads_kernel,
        grid=(m // tm, n // tn),
        in_specs=[pl.BlockSpec((tm, kdim), lambda i, j: (i + rb, 0)),
                  pl.BlockSpec((None, kdim, tn), lambda i, j: (layer, 0, j + cb))],
        out_specs=[pl.BlockSpec((tm, tn), lambda i, j: (i, j)),
                   pl.BlockSpec((tm, tn // HEAD_DIM, HEAD_DIM), lambda i, j: (i, j, 0))],
        out_shape=[jax.ShapeDtypeStruct((m, n), BF16),
                   jax.ShapeDtypeStruct((m, n // HEAD_DIM, HEAD_DIM), F32)],
        compiler_params=_params("parallel", "parallel"),
        name=name,
    )(x, w)


def _band_head(q, ka, kb, va, vb, bias_a, bias_b, past_visible):
    sa = _dot_nt(q, ka.astype(BF16)) * ATTN_SCALE + bias_a
    sb = _dot_nt(q, kb.astype(BF16)) * ATTN_SCALE + bias_b
    if past_visible is not None:
        sa = jnp.where(past_visible, sa, NEG_INF)
    m = jnp.maximum(jnp.max(sa, axis=-1, keepdims=True), jnp.max(sb, axis=-1, keepdims=True))
    pa = jnp.exp(sa - m)
    pb = jnp.exp(sb - m)
    l = jnp.sum(pa, axis=-1, keepdims=True) + jnp.sum(pb, axis=-1, keepdims=True)
    o = _dot(pa.astype(BF16), va.astype(BF16)) + _dot(pb.astype(BF16), vb.astype(BF16))
    return o / l


def _band_prompt_kernel(q_ref, ka_ref, kb_ref, va_ref, vb_ref, bias_ref, o_ref, *, n_own):
    i = pl.program_id(1)

    @pl.when(i < n_own)
    def _():
        na = ka_ref.shape[0]
        o = _band_head(q_ref[...], ka_ref[...], kb_ref[...], va_ref[...], vb_ref[...],
                       bias_ref[:, :na], bias_ref[:, na:], i > 0)
        o_ref[...] = o.astype(o_ref.dtype)

    @pl.when(i >= n_own)
    def _():
        o_ref[...] = jnp.zeros_like(o_ref)


def _band_sample_kernel(q_ref, kn_ref, vn_ref, kc_ref, vc_ref, bias_ref, o_ref):
    win, heads = kc_ref.shape[0], kc_ref.shape[1]
    kc = pltpu.einshape("phd->hpd", kc_ref[...])
    vc = pltpu.einshape("phd->hpd", vc_ref[...])
    for hh in range(heads):
        sl = slice(hh * HEAD_DIM, (hh + 1) * HEAD_DIM)
        o = _band_head(q_ref[:, sl], kc[hh], kn_ref[:, sl], vc[hh], vn_ref[:, sl],
                       bias_ref[hh, :, :win], bias_ref[hh, :, win:], None)
        o_ref[:, sl] = o.astype(o_ref.dtype)


def _band_bias(rel_bias, q_pos, k_pos):
    lq, lk = q_pos.shape[0], k_pos.shape[0]
    d_min = q_pos[0] - k_pos[lk - 1]
    n_diag = lq + lk - 1
    diag = jnp.clip(d_min + jnp.arange(n_diag, dtype=jnp.int32), -REL_CLIP, REL_CLIP) + REL_CLIP
    g = rel_bias.astype(F32)[:, diag]
    row_len = -(-n_diag // HEAD_DIM) * HEAD_DIM
    period = row_len + 1
    u = jnp.pad(g[:, ::-1], ((0, 0), (0, period - n_diag)))
    skew = jnp.tile(u, (1, lq))[:, :lq * row_len].reshape(-1, lq, row_len)
    bias = skew[:, :, lq - 1:lq - 1 + lk]
    qc = q_pos[:, None] // CHUNK
    kc = k_pos[None, :] // CHUNK
    mask = (kc <= qc) & (kc >= qc - A_PAST_CHUNKS)
    return jnp.where(mask[None], bias, NEG_INF)


def _band_prompt(q, k, v, rel_bias, sp):
    heads = rel_bias.shape[0]
    tq = A_PAST_CHUNKS * CHUNK
    total = q.shape[0]
    assert sp % tq == 0 and total % tq == 0
    n_own = sp // tq
    pos = jnp.arange(tq, dtype=jnp.int32)
    bias = _band_bias(rel_bias, tq + pos, jnp.arange(2 * tq, dtype=jnp.int32))
    prev = lambda h, i: (jnp.clip(i - 1, 0, n_own - 1), h)
    cur = lambda h, i: (jnp.minimum(i, n_own - 1), h)
    blk = lambda im: pl.BlockSpec((tq, HEAD_DIM), im)
    return pl.pallas_call(
        functools.partial(_band_prompt_kernel, n_own=n_own),
        grid=(heads, total // tq),
        in_specs=[blk(cur), blk(prev), blk(cur), blk(prev), blk(cur),
                  pl.BlockSpec((None, tq, 2 * tq), lambda h, i: (h, 0, 0))],
        out_specs=blk(lambda h, i: (i, h)),
        out_shape=jax.ShapeDtypeStruct((q.shape[0], heads * HEAD_DIM), BF16),
        compiler_params=_params("parallel", "arbitrary"),
        name="band_prompt",
    )(q, k, k, v, v, bias)


def _band_sample(q, k, v, cache_k, cache_v, layer, rel_bias, sp, ls, past_len, out):
    _, bs, win, heads, _ = cache_k.shape
    width = heads * HEAD_DIM
    assert sp % ls == 0
    row0 = sp // ls
    q_pos = past_len + jnp.arange(ls, dtype=jnp.int32)
    k_pos = past_len - win + jnp.arange(win + ls, dtype=jnp.int32)
    bias = _band_bias(rel_bias, q_pos, k_pos)
    blk_new = pl.BlockSpec((ls, width), lambda b: (row0 + b, 0))
    blk_old = pl.BlockSpec((None, None, win, heads, HEAD_DIM), lambda b: (layer, b, 0, 0, 0))
    return pl.pallas_call(
        _ignore_input(_band_sample_kernel, 6),
        grid=(bs,),
        in_specs=[blk_new, blk_new, blk_new, blk_old, blk_old,
                  pl.BlockSpec((heads, ls, win + ls), lambda b: (0, 0, 0)),
                  pl.BlockSpec(memory_space=pl.ANY)],
        out_specs=blk_new,
        out_shape=jax.ShapeDtypeStruct(out.shape, out.dtype),
        input_output_aliases={6: 0},
        compiler_params=_params("parallel"),
        name="band_sample",
    )(q, k, v, cache_k, cache_v, bias, out)


def _pool_kernel(cur_ref, hist_ref, w_ref, sc_ref, o_ref, ext_ref, *, pos_base, pos_step, first_has_no_past, n_own):
    tr = cur_ref.shape[0]
    i = pl.program_id(0)

    @pl.when(i < n_own)
    def _():
        hist = hist_ref[...]
        if first_has_no_past:
            hist = jnp.where(i > 0, hist, 0.0)
        ext_ref[0:POOL_HIST_PAD, :] = hist
        ext_ref[POOL_HIST_PAD:POOL_HIST_PAD + tr, :] = cur_ref[...]
        pos = pos_base + i * pos_step + lax.broadcasted_iota(jnp.int32, (tr, 1), 0)
        for g, w in enumerate(POOL_WINDOWS):
            sl = slice(g * B_GROUP_WIDTH, (g + 1) * B_GROUP_WIDTH)
            cur = cur_ref[:, sl]
            tot = cur
            for j in range(1, w):
                tot = tot + ext_ref[POOL_HIST_PAD - j:POOL_HIST_PAD - j + tr, sl]
            cnt = jnp.minimum(pos + 1, w).astype(F32)
            d = tot / cnt - cur
            y = _dot(d.astype(BF16), w_ref[g]) * sc_ref[:, sl]
            o_ref[:, sl] = y.astype(o_ref.dtype)

    @pl.when(i >= n_own)
    def _():
        o_ref[...] = jnp.zeros_like(o_ref)


def _pool(p, hist, pool_w, pool_scale, *, row_block0, nblk, tr, pos_base, pos_step, name, out=None):
    width = p.shape[1]
    rows = pl.BlockSpec((tr, width), lambda i: (row_block0 + i, 0))
    body = functools.partial(_pool_kernel, pos_base=pos_base, pos_step=pos_step, first_has_no_past=hist is None,
                             n_own=nblk)
    steps, cur_rows = nblk, rows
    if hist is None:
        assert row_block0 == 0 and tr % POOL_HIST_PAD == 0 and p.shape[0] % tr == 0
        per = tr // POOL_HIST_PAD
        steps = p.shape[0] // tr
        cur_rows = pl.BlockSpec((tr, width), lambda i: (jnp.minimum(i, nblk - 1), 0))
        hist_spec = pl.BlockSpec((POOL_HIST_PAD, width),
                                 lambda i: (jnp.maximum(jnp.minimum(i, nblk - 1) * per - 1, 0), 0))
        hist = p
    else:
        hist_spec = pl.BlockSpec((None, POOL_HIST_PAD, width), lambda i: (i, 0, 0))
    in_specs = [cur_rows, hist_spec,
                pl.BlockSpec(pool_w.shape, lambda i: (0, 0, 0)),
                pl.BlockSpec((1, width), lambda i: (0, 0))]
    args = [p, hist, pool_w, pool_scale.reshape(1, width)]
    aliases = {}
    if out is not None:
        body = _ignore_input(body, len(args))
        aliases = {len(args): 0}
        in_specs.append(pl.BlockSpec(memory_space=pl.ANY))
        args.append(out)
    return pl.pallas_call(
        body,
        grid=(steps,),
        in_specs=in_specs,
        out_specs=rows,
        out_shape=jax.ShapeDtypeStruct(p.shape, BF16),
        input_output_aliases=aliases,
        scratch_shapes=[pltpu.VMEM((POOL_HIST_PAD + tr, width), F32)],
        compiler_params=_params("parallel"),
        name=name,
    )(*args)


def _sb_steps(qs, kvs, tri, carries, accs, visibles):
    nk = kvs[0][0].shape[0]
    zs = [_dot_nt(q, k) * ATTN_SCALE for q, (k, _) in zip(qs, kvs)]
    log_keeps = [jnp.where(vis, -(jnp.maximum(z, 0.0) + jnp.log(1.0 + jnp.exp(-jnp.abs(z)))), 0.0)
                 for z, vis in zip(zs, visibles)]
    his = [lk.astype(BF16) for lk in log_keeps]
    los = [(lk - hi.astype(F32)).astype(BF16) for lk, hi in zip(log_keeps, his)]
    sums = [_dot(hi, tri) + _dot(lo, tri) for hi, lo in zip(his, los)]
    weights = [jnp.where(vis, jnp.exp(z + lk + s[:, :nk] + carry[:, :nk]), 0.0).astype(BF16)
               for z, lk, s, carry, vis in zip(zs, log_keeps, sums, carries, visibles)]
    new_accs = [acc + _dot(a, v) for acc, a, (_, v) in zip(accs, weights, kvs)]
    new_carries = [carry + s[:, nk:] for carry, s in zip(carries, sums)]
    return new_carries, new_accs


def _sb_sweep(chains, load_kv, tri, carries, accs):
    rows = chains[0][0].shape[0]
    lane = lax.broadcasted_iota(jnp.int32, (rows, SB_KEYS), 1)
    beyond = jnp.int32(2 ** 30)

    def cond(st):
        return st[1] == 0

    def body(st):
        n, _, carries, accs = st
        kbs = [kb0 - n for _, kb0, _ in chains]
        starts = [pl.multiple_of(jnp.maximum(kb, 0) * SB_KEYS, SB_KEYS) for kb in kbs]
        visibles = [(jnp.where(kb >= 0, start, beyond) + lane) < qpos
                    for (_, _, qpos), kb, start in zip(chains, kbs, starts)]
        carries, accs = _sb_steps([q for q, _, _ in chains], load_kv(starts), tri, carries, accs, visibles)
        pending = jnp.full((rows, HEAD_DIM), -jnp.inf, F32)
        for kb, carry in zip(kbs, carries):
            pending = jnp.maximum(pending, jnp.where(kb > 0, carry, -jnp.inf))
        done = (jnp.max(pending) < SB_EXIT).astype(jnp.int32)
        return n + 1, done, carries, accs

    return lax.while_loop(cond, body, (jnp.int32(0), jnp.int32(0), list(carries), list(accs)))[3]


def _sb_prompt_kernel(q_ref, k_ref, v_ref, tri_ref, o_ref, *, tq, n_own):
    n_chains = q_ref.shape[0] // tq
    i = pl.program_id(1)

    @pl.when(i < n_own)
    def _():
        row0 = i * q_ref.shape[0]
        row = lax.broadcasted_iota(jnp.int32, (tq, SB_KEYS), 0)

        def load_kv(starts):
            return [(k_ref[pl.ds(s, SB_KEYS), :].astype(BF16), v_ref[pl.ds(s, SB_KEYS), :].astype(BF16))
                    for s in starts]

        chains = [(q_ref[c * tq:(c + 1) * tq, :], (row0 + (c + 1) * tq) // SB_KEYS - 1, row0 + c * tq + row)
                  for c in range(n_chains)]
        zeros = [jnp.zeros((tq, HEAD_DIM), F32)] * n_chains
        accs = _sb_sweep(chains, load_kv, tri_ref[...], zeros, zeros)
        for c in range(n_chains):
            o_ref[c * tq:(c + 1) * tq, :] = accs[c].astype(o_ref.dtype)

    @pl.when(i >= n_own)
    def _():
        o_ref[...] = jnp.zeros_like(o_ref)


def _sb_sample_kernel(q_ref, kn_ref, vn_ref, kc_ref, vc_ref, trin_ref, tri_ref, o_ref):
    ls = q_ref.shape[0]
    past, heads = kc_ref.shape[0], kc_ref.shape[1]
    causal = lax.broadcasted_iota(jnp.int32, (ls, ls), 1) < lax.broadcasted_iota(jnp.int32, (ls, ls), 0)
    all_rows = jnp.full((ls, SB_KEYS), 2 ** 30 - SB_KEYS, jnp.int32)
    zeros = jnp.zeros((ls, HEAD_DIM), F32)
    head_cols = [slice(hh * HEAD_DIM, (hh + 1) * HEAD_DIM) for hh in range(heads)]
    chains = [(q_ref[:, sl], past // SB_KEYS - 1, all_rows) for sl in head_cols]
    carries, accs = _sb_steps([q for q, _, _ in chains],
                              [(kn_ref[:, sl].astype(BF16), vn_ref[:, sl].astype(BF16)) for sl in head_cols],
                              trin_ref[...], [zeros] * heads, [zeros] * heads, [causal] * heads)

    def load_kv(starts):
        k = pltpu.einshape("phd->hpd", kc_ref[pl.ds(starts[0], SB_KEYS), :, :])
        v = pltpu.einshape("phd->hpd", vc_ref[pl.ds(starts[0], SB_KEYS), :, :])
        return [(k[hh].astype(BF16), v[hh].astype(BF16)) for hh in range(heads)]

    accs = _sb_sweep(chains, load_kv, tri_ref[...], carries, accs)
    for hh in range(heads):
        o_ref[:, hh * HEAD_DIM:(hh + 1) * HEAD_DIM] = accs[hh].astype(o_ref.dtype)


def _sb_tri(nk):
    j = jnp.arange(nk)[:, None]
    s = jnp.arange(nk + HEAD_DIM)[None, :]
    return ((s >= nk) | (j > s)).astype(BF16)


def _sb_prompt(q, k, v, sp):
    heads = q.shape[1] // HEAD_DIM
    tq = SB_KEYS
    tb = _tile(sp, 8 * tq)
    total = q.shape[0]
    assert sp % SB_KEYS == 0 and tb % tq == 0 and total % tb == 0
    n_own = sp // tb
    return pl.pallas_call(
        functools.partial(_sb_prompt_kernel, tq=tq, n_own=n_own),
        grid=(heads, total // tb),
        in_specs=[pl.BlockSpec((tb, HEAD_DIM), lambda h, i: (jnp.minimum(i, n_own - 1), h)),
                  pl.BlockSpec((sp, HEAD_DIM), lambda h, i: (0, h)),
                  pl.BlockSpec((sp, HEAD_DIM), lambda h, i: (0, h)),
                  pl.BlockSpec((SB_KEYS, SB_KEYS + HEAD_DIM), lambda h, i: (0, 0))],
        out_specs=pl.BlockSpec((tb, HEAD_DIM), lambda h, i: (i, h)),
        out_shape=jax.ShapeDtypeStruct(q.shape, BF16),
        compiler_params=_params("parallel", "arbitrary"),
        name="stickbreak_prompt",
    )(q, k, v, _sb_tri(SB_KEYS))


def _sb_sample(q, k, v, cache_k, cache_v, layer, sp, ls, out):
    n_layers, bs, past, heads, _ = cache_k.shape
    hg = 8
    assert past % SB_KEYS == 0 and past >= SB_KEYS and sp % ls == 0 and heads % hg == 0
    row0 = sp // ls
    gw = hg * HEAD_DIM
    blk_new = pl.BlockSpec((ls, gw), lambda b, g: (row0 + b, g))
    blk_kv = pl.BlockSpec((ls, gw), lambda b, g: (b, g))
    blk_old = pl.BlockSpec((None, None, past, None, hg, HEAD_DIM), lambda b, g: (layer, b, 0, g, 0, 0))
    grouped = (n_layers, bs, past, heads // hg, hg, HEAD_DIM)
    return pl.pallas_call(
        _ignore_input(_sb_sample_kernel, 7),
        grid=(bs, heads // hg),
        in_specs=[blk_new, blk_kv, blk_kv, blk_old, blk_old,
                  pl.BlockSpec((ls, ls + HEAD_DIM), lambda b, g: (0, 0)),
                  pl.BlockSpec((SB_KEYS, SB_KEYS + HEAD_DIM), lambda b, g: (0, 0)),
                  pl.BlockSpec(memory_space=pl.ANY)],
        out_specs=blk_new,
        out_shape=jax.ShapeDtypeStruct(out.shape, out.dtype),
        input_output_aliases={7: 0},
        compiler_params=_params("parallel", "arbitrary"),
        name="stickbreak_sample",
    )(q, k, v, cache_k.reshape(grouped), cache_v.reshape(grouped), _sb_tri(ls), _sb_tri(SB_KEYS), out)


def _peer_scores_kernel(key_ref, q_ref, o_ref):
    o_ref[...] = _dot_nt(key_ref[...], q_ref[...])


def _peer_scores(pq, sub_keys, layer):
    t = pq.shape[0]
    _, nhc, nkeys, half = sub_keys.shape
    tt = _tile(t, 1024)
    return pl.pallas_call(
        _peer_scores_kernel,
        grid=(nhc, t // tt),
        in_specs=[pl.BlockSpec((None, None, nkeys, half), lambda c, i: (layer, c, 0, 0)),
                  pl.BlockSpec((tt, half), lambda c, i: (i, c))],
        out_specs=pl.BlockSpec((None, nkeys, tt), lambda c, i: (c, 0, i)),
        out_shape=jax.ShapeDtypeStruct((nhc, nkeys, t), F32),
        compiler_params=_params("parallel", "parallel"),
        name="peer_scores",
    )(sub_keys, pq)


def _extract_top(s, k, tie_safe):
    rows = lax.broadcasted_iota(jnp.int32, s.shape, 0)
    rank = jnp.full(s.shape, float(k), F32)
    vals = []
    for r in range(k):
        m = jnp.max(s, axis=0, keepdims=True)
        vals.append(m)
        hit = s == m
        if tie_safe:
            hit = rows == jnp.min(jnp.where(hit, rows, s.shape[0]), axis=0, keepdims=True)
        rank = jnp.where(hit, float(r), rank)
        s = jnp.where(hit, -jnp.inf, s)
    return vals, rank, s


def _peer_select_kernel(s_ref, m1x_ref, e1x_ref, rank2_ref, e2_ref, *, ne):
    def select(h, tie_safe):
        s1 = s_ref[2 * h]
        s2 = s_ref[2 * h + 1]
        a, rank1, _ = _extract_top(s1, PEER_TOPK, tie_safe)
        b, rank2, _ = _extract_top(s2, PEER_TOPK, tie_safe)
        b = jnp.concatenate(b, axis=0)
        sub = lax.broadcasted_iota(jnp.int32, (8, b.shape[1]), 0)
        cand = [a[0] + b, a[1] + b[:8]]
        for i in range(2, 8):
            cand.append(jnp.where(sub < PEER_TOPK // (i + 1), a[i] + b[:8], -jnp.inf))
        cand.append(jnp.concatenate(a[8:], axis=0) + b[0:1])
        cand = jnp.concatenate(cand, axis=0)
        g, _, left = _extract_top(cand, PEER_TOPK, tie_safe)
        z = jnp.ones_like(g[0])
        for gk in g[1:]:
            z = z + jnp.exp(gk - g[0])
        took = (left != cand).astype(F32)
        taken = [jnp.sum(took[0:16], axis=0, keepdims=True), jnp.sum(took[16:24], axis=0, keepdims=True)]
        taken += [jnp.sum(took[8 * i + 8:8 * i + 16], axis=0, keepdims=True) for i in range(2, 8)]
        taken += [took[72 + i:73 + i] for i in range(8)]
        m1 = jnp.zeros_like(s1)
        for i in range(PEER_TOPK):
            m1 = jnp.where(rank1 == float(i), taken[i], m1)
        e1 = jnp.exp(s1 - a[0]) / z
        rank2_ref[h] = rank2.astype(rank2_ref.dtype)
        e2_ref[h] = jnp.exp(s2 - b[0:1]).astype(e2_ref.dtype)
        row = pl.multiple_of(h * ne, ne)
        for j in range(N_KEYS // ne):
            m1x_ref[j, pl.ds(row, ne), :] = m1[j * ne:(j + 1) * ne]
            e1x_ref[j, pl.ds(row, ne), :] = e1[j * ne:(j + 1) * ne]
        removed = (jnp.sum((rank1 < PEER_TOPK).astype(F32), axis=0, keepdims=True)
                   + jnp.sum((rank2 < PEER_TOPK).astype(F32), axis=0, keepdims=True)
                   + jnp.sum(took, axis=0, keepdims=True))
        return jnp.max(removed) - 3.0 * PEER_TOPK

    def one_head(h, _):
        extra = select(h, tie_safe=False)

        @pl.when(extra > 0.0)
        def _():
            select(h, tie_safe=True)

        return 0

    lax.fori_loop(0, PEER_HEADS, one_head, 0)


def _peer_select(scores, ne):
    nhc, nkeys, t = scores.shape
    tt = _tile(t, 256)
    x_shape = jax.ShapeDtypeStruct((nkeys // ne, PEER_HEADS * ne, t), F32)
    x_spec = pl.BlockSpec((nkeys // ne, PEER_HEADS * ne, tt), lambda i: (0, 0, i))
    y_shape = jax.ShapeDtypeStruct((PEER_HEADS, nkeys, t), BF16)
    y_spec = pl.BlockSpec((PEER_HEADS, nkeys, tt), lambda i: (0, 0, i))
    return pl.pallas_call(
        functools.partial(_peer_select_kernel, ne=ne),
        grid=(t // tt,),
        in_specs=[pl.BlockSpec((nhc, nkeys, tt), lambda i: (0, 0, i))],
        out_specs=[x_spec, x_spec, y_spec, y_spec],
        out_shape=[x_shape, x_shape, y_shape, y_shape],
        compiler_params=_params("parallel"),
        name="peer_select",
    )(scores)


def _peer_act_kernel(*refs, ne, n_parts):
    u_ref, x_ref = refs[0], refs[1]
    part_refs = refs[2:2 + 4 * n_parts]
    o_ref, gate_ref = refs[2 + 4 * n_parts], refs[3 + 4 * n_parts]
    first_row = (pl.program_id(1) % (ROW_BLOCK // ne)) * ne
    for part in range(n_parts):
        m1x_ref, e1x_ref, rank2_ref, e2_ref = part_refs[4 * part:4 * part + 4]
        tp = rank2_ref.shape[2]
        for e in range(ne):
            gate = jnp.zeros((N_KEYS, tp), BF16)
            for h in range(PEER_HEADS):
                r = pl.ds(h * ROW_BLOCK + first_row + e, 1)
                hit = rank2_ref[h] < m1x_ref[r, :].astype(BF16)
                w = e2_ref[h] * e1x_ref[r, :].astype(BF16)
                gate = gate + jnp.where(hit, w, jnp.zeros_like(w))
            gate_ref[part * tp:(part + 1) * tp, e * N_KEYS:(e + 1) * N_KEYS] = gate.T
    x = x_ref[...]
    for p in range(ne // 2):
        cols = slice(2 * p * N_KEYS, 2 * (p + 1) * N_KEYS)
        hp = _dot_nt(x, u_ref[cols, :].astype(BF16))
        gelu = 0.5 * hp * (1.0 + lax.erf(hp * (0.5 ** 0.5)))
        o_ref[:, cols] = gate_ref[:, cols] * gelu.astype(BF16)


def _peer_act(xn, u, layer, m1x, e1x, rank2, e2, ne):
    t, d = xn.shape
    n_exp = u.shape[1]
    tp = _tile(t, 512)
    n_parts = 2 if t % (2 * tp) == 0 else 1
    tt = n_parts * tp
    te = ne * N_KEYS
    once = pl.Buffered(1)
    part_specs, part_args = [], []
    for part in range(n_parts):
        row_spec = pl.BlockSpec((None, PEER_HEADS * ROW_BLOCK, tp),
                                lambda i, j, part=part: (j // (ROW_BLOCK // ne), 0, n_parts * i + part))
        tile_spec = pl.BlockSpec((PEER_HEADS, N_KEYS, tp), lambda i, j, part=part: (0, 0, n_parts * i + part),
                                 pipeline_mode=once)
        part_specs += [row_spec, row_spec, tile_spec, tile_spec]
        part_args += [m1x, e1x, rank2, e2]
    return pl.pallas_call(
        functools.partial(_peer_act_kernel, ne=ne, n_parts=n_parts),
        grid=(t // tt, n_exp // te),
        in_specs=[pl.BlockSpec((None, te, d), lambda i, j: (layer, j, 0)),
                  pl.BlockSpec((tt, d), lambda i, j: (i, 0), pipeline_mode=once)] + part_specs,
        out_specs=pl.BlockSpec((tt, te), lambda i, j: (i, j)),
        out_shape=jax.ShapeDtypeStruct((t, n_exp), BF16),
        scratch_shapes=[pltpu.VMEM((tt, te), BF16)],
        compiler_params=_params("parallel", "arbitrary"),
        name="peer_act",
    )(u, xn, *part_args)


def _peer_ffn(h, layer, norm_g, w_q, sub_keys, u, v, split_at=None):
    ne = 4
    xn = _rmsnorm(h, norm_g)
    pq = _matmul(xn, w_q, layer, out_dtype=BF16, name="peer_query")
    scores = _peer_scores(pq, sub_keys, layer)
    m1x, e1x, rank2, e2 = _peer_select(scores, ROW_BLOCK)
    act = _peer_act(xn, u, layer, m1x, e1x, rank2, e2, ne)
    down = functools.partial(_matmul, act, v, layer, mode="residual", aux=h, tm=1024, tn=1024, tk=2048)
    if split_at is None:
        return down(name="peer_down")
    return (down(row0=0, m=split_at, name="peer_down_head"),
            down(row0=split_at, m=h.shape[0] - split_at, name="peer_down_tail"))


def kernel(x_prompt, x_sample, cache_a_k, cache_a_v, state_b_pool, cache_c_k, cache_c_v, norm_mix, norm_ffn, ab_w_in, ab_q_gain, ab_k_gain, ab_rel_bias, ab_pool_w, ab_pool_scale, ab_w_out, c_w_in, c_w_out, peer_w_q, peer_sub_keys, peer_u, peer_v):
    bp, sp, d = x_prompt.shape
    bs, ls, _ = x_sample.shape
    assert bp == 1
    depth = norm_mix.shape[0]
    past_len = cache_c_k.shape[2]
    a_heads = ab_rel_bias.shape[1]
    a_width = a_heads * HEAD_DIM
    b_width = ab_pool_scale.shape[1]
    c_width = c_w_out.shape[1]
    win_p = min(A_PAST_CHUNKS * CHUNK, sp)
    n_s = bs * ls

    h = jnp.concatenate([x_prompt.reshape(sp, d), x_sample.reshape(n_s, d)], axis=0)

    ab_w_in_b, ab_w_out_b = ab_w_in.astype(BF16), ab_w_out.astype(BF16)
    c_w_in_b, c_w_out_b = c_w_in.astype(BF16), c_w_out.astype(BF16)
    peer_w_q_b, peer_v_b = peer_w_q.astype(BF16), peer_v.astype(BF16)
    sub_keys_b = peer_sub_keys.astype(BF16).reshape(depth, PEER_HEADS * 2, N_KEYS, -1)

    outs = {name: [] for name in ("a_k_p", "a_v_p", "b_p", "c_k_p", "c_v_p",
                                  "a_k_s", "a_v_s", "b_s", "c_k_s", "c_v_s")}
    for layer in range(depth):
        xn = _rmsnorm(h, norm_mix[layer])
        if layer % 2 == 0:
            i = layer // 2
            q = _matmul(xn, ab_w_in_b, i, col0=0, n=a_width, out_dtype=BF16, mode="headnorm",
                        aux=ab_q_gain[i], name="a_q")
            k = _matmul(xn, ab_w_in_b, i, col0=a_width, n=a_width, mode="headnorm", aux=ab_k_gain[i], name="a_k")
            v = _matmul(xn, ab_w_in_b, i, col0=2 * a_width, n=a_width, name="a_v")
            p = _matmul(xn, ab_w_in_b, i, col0=3 * a_width, n=b_width, name="b_in")

            att = _band_prompt(q, k, v, ab_rel_bias[i], sp)
            att = _band_sample(q, k, v, cache_a_k, cache_a_v, i, ab_rel_bias[i], sp, ls, past_len, att)

            pool_w = ab_pool_w[i].astype(BF16)
            tr = _tile(sp, 256)
            pool = _pool(p, None, pool_w, ab_pool_scale[i], row_block0=0, nblk=sp // tr, tr=tr,
                         pos_base=0, pos_step=tr, name="pool_prompt")
            hist_s = jnp.pad(state_b_pool[i], ((0, 0), (POOL_HIST_PAD - POOL_HIST, 0), (0, 0)))
            pool = _pool(p, hist_s, pool_w, ab_pool_scale[i], row_block0=sp // ls, nblk=bs, tr=ls,
                         pos_base=past_len, pos_step=0, name="pool_sample", out=pool)

            h = _matmul(att, ab_w_out_b, i, x2=pool, mode="residual", aux=h, name="a_out")

            ps = p[sp:].reshape(bs, ls, b_width)
            outs["a_k_p"].append(k[sp - win_p:sp].reshape(bp, win_p, a_heads, HEAD_DIM))
            outs["a_v_p"].append(v[sp - win_p:sp].reshape(bp, win_p, a_heads, HEAD_DIM))
            outs["b_p"].append(p[sp - POOL_HIST:sp].reshape(bp, POOL_HIST, b_width))
            outs["a_k_s"].append(k[sp:].reshape(bs, ls, a_heads, HEAD_DIM))
            outs["a_v_s"].append(v[sp:].reshape(bs, ls, a_heads, HEAD_DIM))
            outs["b_s"].append(jnp.concatenate([state_b_pool[i], ps], axis=1)[:, ls:])
        else:
            j = layer // 2
            c_heads = c_width // HEAD_DIM
            q = _matmul(xn, c_w_in_b, j, col0=0, n=c_width, out_dtype=BF16, name="c_q")
            k_p, k_p3 = _proj_heads(xn, c_w_in_b, j, col0=c_width, n=c_width, row0=0, m=sp, name="c_k_prompt")
            v_p, v_p3 = _proj_heads(xn, c_w_in_b, j, col0=2 * c_width, n=c_width, row0=0, m=sp, name="c_v_prompt")
            k_s, k_s3 = _proj_heads(xn, c_w_in_b, j, col0=c_width, n=c_width, row0=sp, m=n_s, name="c_k_sample")
            v_s, v_s3 = _proj_heads(xn, c_w_in_b, j, col0=2 * c_width, n=c_width, row0=sp, m=n_s, name="c_v_sample")
            att = _sb_prompt(q, k_p, v_p, sp)
            att = _sb_sample(q, k_s, v_s, cache_c_k, cache_c_v, j, sp, ls, att)
            h = _matmul(att, c_w_out_b, j, mode="residual", aux=h, name="c_out")
            outs["c_k_p"].append(k_p3.reshape(bp, sp, c_heads, HEAD_DIM))
            outs["c_v_p"].append(v_p3.reshape(bp, sp, c_heads, HEAD_DIM))
            outs["c_k_s"].append(k_s3.reshape(bs, ls, c_heads, HEAD_DIM))
            outs["c_v_s"].append(v_s3.reshape(bs, ls, c_heads, HEAD_DIM))

        h = _peer_ffn(h, layer, norm_ffn[layer], peer_w_q_b, sub_keys_b, peer_u, peer_v_b,
                      split_at=sp if layer == depth - 1 else None)

    h_prompt, h_sample = h
    st = {name: jnp.stack(vals) for name, vals in outs.items()}
    return (h_prompt.reshape(bp, sp, d), h_sample.reshape(bs, ls, d),
            st["a_k_p"], st["a_v_p"], st["b_p"], st["c_k_p"], st["c_v_p"],
            st["a_k_s"], st["a_v_s"], st["b_s"], st["c_k_s"], st["c_v_s"])
```

```python
import functools

import jax
import jax.numpy as jnp
from jax import lax
from jax.experimental import pallas as pl
from jax.experimental.pallas import tpu as pltpu

F32 = jnp.float32
BF16 = jnp.bfloat16

HEAD_DIM = 128
CHUNK = 64
A_PAST_CHUNKS = 8
REL_CLIP = 128
POOL_WINDOWS = (2, 4, 8, 16)
POOL_HIST = max(POOL_WINDOWS) - 1
POOL_HIST_PAD = 16
B_GROUP_WIDTH = 512
PEER_HEADS = 8
N_KEYS = 128
PEER_TOPK = 16
ROW_BLOCK = 8
EPS = 1e-6
NEG_INF = -1e30
ATTN_SCALE = HEAD_DIM ** -0.5
SB_EXIT = -104.0
SB_KEYS = 128
VMEM_LIMIT_BYTES = 56 * 1024 * 1024


def _params(*sem):
    return pltpu.CompilerParams(dimension_semantics=sem, vmem_limit_bytes=VMEM_LIMIT_BYTES)


def _tile(dim, pref):
    return pref if dim % pref == 0 else dim


def _ignore_input(body, index):
    def wrapped(*refs):
        return body(*refs[:index], *refs[index + 1:])
    return wrapped


def _dot(a, b):
    return jnp.dot(a, b, preferred_element_type=F32)


def _dot_nt(a, b):
    return lax.dot_general(a, b, (((1,), (1,)), ((), ())), preferred_element_type=F32)


def _rmsnorm_kernel(x_ref, g_ref, o_ref):
    x = x_ref[...]
    ms = jnp.mean(x * x, axis=-1, keepdims=True)
    o_ref[...] = (x * lax.rsqrt(ms + EPS) * g_ref[...]).astype(o_ref.dtype)


def _rmsnorm(x, g):
    t, d = x.shape
    tr = _tile(t, 256)
    return pl.pallas_call(
        _rmsnorm_kernel,
        grid=(t // tr,),
        in_specs=[pl.BlockSpec((tr, d), lambda i: (i, 0)),
                  pl.BlockSpec((1, d), lambda i: (0, 0))],
        out_specs=pl.BlockSpec((tr, d), lambda i: (i, 0)),
        out_shape=jax.ShapeDtypeStruct((t, d), BF16),
        compiler_params=_params("parallel"),
        name="rmsnorm",
    )(x, g.reshape(1, d))


def _mm_kernel(*refs, nk, mode, two_x):
    x_ref, w_ref = refs[0], refs[1]
    refs = refs[2:]
    if two_x:
        x2_ref, refs = refs[0], refs[1:]
    if mode in ("headnorm", "residual"):
        aux_ref, o_ref, scratch = refs[0], refs[1], refs[2:]
    else:
        aux_ref, o_ref, scratch = None, refs[0], refs[1:]

    def finish(acc):
        if mode == "headnorm":
            g = aux_ref[...]
            for c in range(acc.shape[1] // HEAD_DIM):
                sl = slice(c * HEAD_DIM, (c + 1) * HEAD_DIM)
                y = acc[:, sl]
                ms = jnp.mean(y * y, axis=-1, keepdims=True)
                o_ref[:, sl] = (y * lax.rsqrt(ms + EPS) * g).astype(o_ref.dtype)
        elif mode == "residual":
            o_ref[...] = aux_ref[...] + acc
        else:
            o_ref[...] = acc.astype(o_ref.dtype)

    if two_x:
        k1 = x_ref.shape[1]
        finish(_dot(x_ref[...], w_ref[:k1, :]) + _dot(x2_ref[...], w_ref[k1:, :]))
    elif nk == 1:
        finish(_dot(x_ref[...], w_ref[...]))
    else:
        acc_ref = scratch[0]
        k = pl.program_id(2)

        @pl.when(k == 0)
        def _():
            acc_ref[...] = jnp.zeros_like(acc_ref)

        acc_ref[...] += _dot(x_ref[...], w_ref[...])

        @pl.when(k == nk - 1)
        def _():
            finish(acc_ref[...])


def _matmul(x, w, layer, *, x2=None, col0=0, n=None, row0=0, m=None, out_dtype=F32, mode="plain", aux=None,
            tm=1024, tn=512, tk=None, name="matmul"):
    kdim = x.shape[1]
    m = x.shape[0] if m is None else m
    if x2 is not None:
        assert tk is None
        kdim += x2.shape[1]
    n = w.shape[2] if n is None else n
    tm, tn = _tile(m, tm), _tile(n, tn)
    tk = kdim if tk is None else _tile(kdim, tk)
    nk = kdim // tk
    rb, cb = row0 // tm, col0 // tn
    assert row0 % tm == 0 and col0 % tn == 0
    in_specs = [pl.BlockSpec((tm, x.shape[1] if x2 is not None else tk), lambda i, j, k: (i + rb, k)),
                pl.BlockSpec((None, tk, tn), lambda i, j, k: (layer, k, j + cb))]
    args = [x, w]
    if x2 is not None:
        in_specs.append(pl.BlockSpec((tm, x2.shape[1]), lambda i, j, k: (i + rb, 0)))
        args.append(x2)
    if mode == "headnorm":
        in_specs.append(pl.BlockSpec((1, HEAD_DIM), lambda i, j, k: (0, 0)))
        args.append(aux.reshape(1, HEAD_DIM))
    elif mode == "residual":
        in_specs.append(pl.BlockSpec((tm, tn), lambda i, j, k: (i + rb, j)))
        args.append(aux)
    scratch = [pltpu.VMEM((tm, tn), F32)] if nk > 1 else []
    return pl.pallas_call(
        functools.partial(_mm_kernel, nk=nk, mode=mode, two_x=x2 is not None),
        grid=(m // tm, n // tn, nk),
        in_specs=in_specs,
        out_specs=pl.BlockSpec((tm, tn), lambda i, j, k: (i, j)),
        out_shape=jax.ShapeDtypeStruct((m, n), out_dtype),
        scratch_shapes=scratch,
        compiler_params=_params("parallel", "parallel", "arbitrary"),
        name=name,
    )(*args)


def _proj_heads_kernel(x_ref, w_ref, o2_ref, o3_ref):
    acc = _dot(x_ref[...], w_ref[...])
    o2_ref[...] = acc.astype(o2_ref.dtype)
    o3_ref[...] = pltpu.einshape("m(hd)->mhd", acc, h=o3_ref.shape[1])


def _proj_heads(x, w, layer, *, col0, n, row0, m, name):
    kdim = x.shape[1]
    tm, tn = _tile(m, 1024), 8 * HEAD_DIM
    assert row0 % tm == 0 and col0 % tn == 0 and n % tn == 0
    rb, cb = row0 // tm, col0 // tn
    return pl.pallas_call(
        _proj_heads_kernel,
        grid=(m // tm, n // tn),
        in_specs=[pl.BlockSpec((tm, kdim), lambda i, j: (i + rb, 0)),
                  pl.BlockSpec((None, kdim, tn), lambda i, j: (layer, 0, j + cb))],
        out_specs=[pl.BlockSpec((tm, tn), lambda i, j: (i, j)),
                   pl.BlockSpec((tm, tn // HEAD_DIM, HEAD_DIM), lambda i, j: (i, j, 0))],
        out_shape=[jax.ShapeDtypeStruct((m, n), BF16),
                   jax.ShapeDtypeStruct((m, n // HEAD_DIM, HEAD_DIM), F32)],
        compiler_params=_params("parallel", "parallel"),
        name=name,
    )(x, w)


def _band_head(q, ka, kb, va, vb, bias_a, bias_b, past_visible):
    sa = _dot_nt(q, ka.astype(BF16)) * ATTN_SCALE + bias_a
    sb = _dot_nt(q, kb.astype(BF16)) * ATTN_SCALE + bias_b
    if past_visible is not None:
        sa = jnp.where(past_visible, sa, NEG_INF)
    m = jnp.maximum(jnp.max(sa, axis=-1, keepdims=True), jnp.max(sb, axis=-1, keepdims=True))
    pa = jnp.exp(sa - m)
    pb = jnp.exp(sb - m)
    l = jnp.sum(pa, axis=-1, keepdims=True) + jnp.sum(pb, axis=-1, keepdims=True)
    o = _dot(pa.astype(BF16), va.astype(BF16)) + _dot(pb.astype(BF16), vb.astype(BF16))
    return o / l


def _band_prompt_kernel(q_ref, ka_ref, kb_ref, va_ref, vb_ref, bias_ref, o_ref):
    na = ka_ref.shape[0]
    o = _band_head(q_ref[...], ka_ref[...], kb_ref[...], va_ref[...], vb_ref[...],
                   bias_ref[:, :na], bias_ref[:, na:], pl.program_id(1) > 0)
    o_ref[...] = o.astype(o_ref.dtype)


def _band_sample_kernel(q_ref, kn_ref, vn_ref, kc_ref, vc_ref, bias_ref, o_ref):
    win, heads = kc_ref.shape[0], kc_ref.shape[1]
    kc = pltpu.einshape("phd->hpd", kc_ref[...])
    vc = pltpu.einshape("phd->hpd", vc_ref[...])
    for hh in range(heads):
        sl = slice(hh * HEAD_DIM, (hh + 1) * HEAD_DIM)
        o = _band_head(q_ref[:, sl], kc[hh], kn_ref[:, sl], vc[hh], vn_ref[:, sl],
                       bias_ref[hh, :, :win], bias_ref[hh, :, win:], None)
        o_ref[:, sl] = o.astype(o_ref.dtype)


def _band_bias(rel_bias, q_pos, k_pos):
    lq, lk = q_pos.shape[0], k_pos.shape[0]
    d_min = q_pos[0] - k_pos[lk - 1]
    n_diag = lq + lk - 1
    diag = jnp.clip(d_min + jnp.arange(n_diag, dtype=jnp.int32), -REL_CLIP, REL_CLIP) + REL_CLIP
    g = rel_bias.astype(F32)[:, diag]
    row_len = -(-n_diag // HEAD_DIM) * HEAD_DIM
    period = row_len + 1
    u = jnp.pad(g[:, ::-1], ((0, 0), (0, period - n_diag)))
    skew = jnp.tile(u, (1, lq))[:, :lq * row_len].reshape(-1, lq, row_len)
    bias = skew[:, :, lq - 1:lq - 1 + lk]
    qc = q_pos[:, None] // CHUNK
    kc = k_pos[None, :] // CHUNK
    mask = (kc <= qc) & (kc >= qc - A_PAST_CHUNKS)
    return jnp.where(mask[None], bias, NEG_INF)


def _band_prompt(q, k, v, rel_bias, sp, out):
    heads = rel_bias.shape[0]
    tq = A_PAST_CHUNKS * CHUNK
    assert sp % tq == 0
    pos = jnp.arange(tq, dtype=jnp.int32)
    bias = _band_bias(rel_bias, tq + pos, jnp.arange(2 * tq, dtype=jnp.int32))
    prev = lambda h, i: (jnp.maximum(i - 1, 0), h)
    cur = lambda h, i: (i, h)
    blk = lambda im: pl.BlockSpec((tq, HEAD_DIM), im)
    return pl.pallas_call(
        _ignore_input(_band_prompt_kernel, 6),
        grid=(heads, sp // tq),
        in_specs=[blk(cur), blk(prev), blk(cur), blk(prev), blk(cur),
                  pl.BlockSpec((None, tq, 2 * tq), lambda h, i: (h, 0, 0)),
                  pl.BlockSpec(memory_space=pl.ANY)],
        out_specs=blk(cur),
        out_shape=jax.ShapeDtypeStruct(out.shape, out.dtype),
        input_output_aliases={6: 0},
        compiler_params=_params("parallel", "arbitrary"),
        name="band_prompt",
    )(q, k, k, v, v, bias, out)


def _band_sample(q, k, v, cache_k, cache_v, layer, rel_bias, sp, ls, past_len, out):
    _, bs, win, heads, _ = cache_k.shape
    width = heads * HEAD_DIM
    assert sp % ls == 0
    row0 = sp // ls
    q_pos = past_len + jnp.arange(ls, dtype=jnp.int32)
    k_pos = past_len - win + jnp.arange(win + ls, dtype=jnp.int32)
    bias = _band_bias(rel_bias, q_pos, k_pos)
    blk_new = pl.BlockSpec((ls, width), lambda b: (row0 + b, 0))
    blk_old = pl.BlockSpec((None, None, win, heads, HEAD_DIM), lambda b: (layer, b, 0, 0, 0))
    return pl.pallas_call(
        _ignore_input(_band_sample_kernel, 6),
        grid=(bs,),
        in_specs=[blk_new, blk_new, blk_new, blk_old, blk_old,
                  pl.BlockSpec((heads, ls, win + ls), lambda b: (0, 0, 0)),
                  pl.BlockSpec(memory_space=pl.ANY)],
        out_specs=blk_new,
        out_shape=jax.ShapeDtypeStruct(out.shape, out.dtype),
        input_output_aliases={6: 0},
        compiler_params=_params("parallel"),
        name="band_sample",
    )(q, k, v, cache_k, cache_v, bias, out)


def _pool_kernel(cur_ref, hist_ref, w_ref, sc_ref, o_ref, ext_ref, *, pos_base, pos_step, first_has_no_past):
    tr = cur_ref.shape[0]
    hist = hist_ref[...]
    if first_has_no_past:
        hist = jnp.where(pl.program_id(0) > 0, hist, 0.0)
    ext_ref[0:POOL_HIST_PAD, :] = hist
    ext_ref[POOL_HIST_PAD:POOL_HIST_PAD + tr, :] = cur_ref[...]
    pos = pos_base + pl.program_id(0) * pos_step + lax.broadcasted_iota(jnp.int32, (tr, 1), 0)
    for g, w in enumerate(POOL_WINDOWS):
        sl = slice(g * B_GROUP_WIDTH, (g + 1) * B_GROUP_WIDTH)
        cur = cur_ref[:, sl]
        tot = cur
        for j in range(1, w):
            tot = tot + ext_ref[POOL_HIST_PAD - j:POOL_HIST_PAD - j + tr, sl]
        cnt = jnp.minimum(pos + 1, w).astype(F32)
        d = tot / cnt - cur
        y = _dot(d.astype(BF16), w_ref[g]) * sc_ref[:, sl]
        o_ref[:, sl] = y.astype(o_ref.dtype)


def _pool(p, hist, pool_w, pool_scale, out, *, row_block0, nblk, tr, pos_base, pos_step, name):
    width = p.shape[1]
    rows = pl.BlockSpec((tr, width), lambda i: (row_block0 + i, 0))
    body = functools.partial(_pool_kernel, pos_base=pos_base, pos_step=pos_step, first_has_no_past=hist is None)
    if hist is None:
        assert row_block0 == 0 and tr % POOL_HIST_PAD == 0
        per = tr // POOL_HIST_PAD
        hist_spec = pl.BlockSpec((POOL_HIST_PAD, width), lambda i: (jnp.maximum(i * per - 1, 0), 0))
        hist = p
    else:
        hist_spec = pl.BlockSpec((None, POOL_HIST_PAD, width), lambda i: (i, 0, 0))
    return pl.pallas_call(
        _ignore_input(body, 4),
        grid=(nblk,),
        in_specs=[rows, hist_spec,
                  pl.BlockSpec(pool_w.shape, lambda i: (0, 0, 0)),
                  pl.BlockSpec((1, width), lambda i: (0, 0)),
                  pl.BlockSpec(memory_space=pl.ANY)],
        out_specs=rows,
        out_shape=jax.ShapeDtypeStruct(out.shape, out.dtype),
        input_output_aliases={4: 0},
        scratch_shapes=[pltpu.VMEM((POOL_HIST_PAD + tr, width), F32)],
        compiler_params=_params("parallel"),
        name=name,
    )(p, hist, pool_w, pool_scale.reshape(1, width), out)


def _sb_steps(qs, kvs, tri, carries, accs, visibles):
    nk = kvs[0][0].shape[0]
    zs = [_dot_nt(q, k) * ATTN_SCALE for q, (k, _) in zip(qs, kvs)]
    log_keeps = [jnp.where(vis, -(jnp.maximum(z, 0.0) + jnp.log(1.0 + jnp.exp(-jnp.abs(z)))), 0.0)
                 for z, vis in zip(zs, visibles)]
    his = [lk.astype(BF16) for lk in log_keeps]
    los = [(lk - hi.astype(F32)).astype(BF16) for lk, hi in zip(log_keeps, his)]
    sums = [_dot(hi, tri) + _dot(lo, tri) for hi, lo in zip(his, los)]
    weights = [jnp.where(vis, jnp.exp(z + lk + s[:, :nk] + carry[:, :nk]), 0.0).astype(BF16)
               for z, lk, s, carry, vis in zip(zs, log_keeps, sums, carries, visibles)]
    new_accs = [acc + _dot(a, v) for acc, a, (_, v) in zip(accs, weights, kvs)]
    new_carries = [carry + s[:, nk:] for carry, s in zip(carries, sums)]
    return new_carries, new_accs


def _sb_sweep(chains, load_kv, tri, carries, accs):
    rows = chains[0][0].shape[0]
    lane = lax.broadcasted_iota(jnp.int32, (rows, SB_KEYS), 1)
    beyond = jnp.int32(2 ** 30)

    def cond(st):
        return st[1] == 0

    def body(st):
        n, _, carries, accs = st
        kbs = [kb0 - n for _, kb0, _ in chains]
        starts = [pl.multiple_of(jnp.maximum(kb, 0) * SB_KEYS, SB_KEYS) for kb in kbs]
        visibles = [(jnp.where(kb >= 0, start, beyond) + lane) < qpos
                    for (_, _, qpos), kb, start in zip(chains, kbs, starts)]
        carries, accs = _sb_steps([q for q, _, _ in chains], load_kv(starts), tri, carries, accs, visibles)
        pending = jnp.full((rows, HEAD_DIM), -jnp.inf, F32)
        for kb, carry in zip(kbs, carries):
            pending = jnp.maximum(pending, jnp.where(kb > 0, carry, -jnp.inf))
        done = (jnp.max(pending) < SB_EXIT).astype(jnp.int32)
        return n + 1, done, carries, accs

    return lax.while_loop(cond, body, (jnp.int32(0), jnp.int32(0), list(carries), list(accs)))[3]


def _sb_prompt_kernel(q_ref, k_ref, v_ref, tri_ref, o_ref, *, tq):
    n_chains = q_ref.shape[0] // tq
    row0 = pl.program_id(1) * q_ref.shape[0]
    row = lax.broadcasted_iota(jnp.int32, (tq, SB_KEYS), 0)

    def load_kv(starts):
        return [(k_ref[pl.ds(s, SB_KEYS), :].astype(BF16), v_ref[pl.ds(s, SB_KEYS), :].astype(BF16))
                for s in starts]

    chains = [(q_ref[c * tq:(c + 1) * tq, :], (row0 + (c + 1) * tq) // SB_KEYS - 1, row0 + c * tq + row)
              for c in range(n_chains)]
    zeros = [jnp.zeros((tq, HEAD_DIM), F32)] * n_chains
    accs = _sb_sweep(chains, load_kv, tri_ref[...], zeros, zeros)
    for c in range(n_chains):
        o_ref[c * tq:(c + 1) * tq, :] = accs[c].astype(o_ref.dtype)


def _sb_sample_kernel(q_ref, kn_ref, vn_ref, kc_ref, vc_ref, trin_ref, tri_ref, o_ref):
    ls = q_ref.shape[0]
    past, heads = kc_ref.shape[0], kc_ref.shape[1]
    causal = lax.broadcasted_iota(jnp.int32, (ls, ls), 1) < lax.broadcasted_iota(jnp.int32, (ls, ls), 0)
    all_rows = jnp.full((ls, SB_KEYS), 2 ** 30 - SB_KEYS, jnp.int32)
    zeros = jnp.zeros((ls, HEAD_DIM), F32)
    head_cols = [slice(hh * HEAD_DIM, (hh + 1) * HEAD_DIM) for hh in range(heads)]
    chains = [(q_ref[:, sl], past // SB_KEYS - 1, all_rows) for sl in head_cols]
    carries, accs = _sb_steps([q for q, _, _ in chains],
                              [(kn_ref[:, sl].astype(BF16), vn_ref[:, sl].astype(BF16)) for sl in head_cols],
                              trin_ref[...], [zeros] * heads, [zeros] * heads, [causal] * heads)

    def load_kv(starts):
        k = pltpu.einshape("phd->hpd", kc_ref[pl.ds(starts[0], SB_KEYS), :, :])
        v = pltpu.einshape("phd->hpd", vc_ref[pl.ds(starts[0], SB_KEYS), :, :])
        return [(k[hh].astype(BF16), v[hh].astype(BF16)) for hh in range(heads)]

    accs = _sb_sweep(chains, load_kv, tri_ref[...], carries, accs)
    for hh in range(heads):
        o_ref[:, hh * HEAD_DIM:(hh + 1) * HEAD_DIM] = accs[hh].astype(o_ref.dtype)


def _sb_tri(nk):
    j = jnp.arange(nk)[:, None]
    s = jnp.arange(nk + HEAD_DIM)[None, :]
    return ((s >= nk) | (j > s)).astype(BF16)


def _sb_prompt(q, k, v, sp, out):
    heads = q.shape[1] // HEAD_DIM
    tq = SB_KEYS
    tb = _tile(sp, 8 * tq)
    assert sp % SB_KEYS == 0 and tb % tq == 0
    rows = pl.BlockSpec((tb, HEAD_DIM), lambda h, i: (i, h))
    return pl.pallas_call(
        _ignore_input(functools.partial(_sb_prompt_kernel, tq=tq), 4),
        grid=(heads, sp // tb),
        in_specs=[rows,
                  pl.BlockSpec((sp, HEAD_DIM), lambda h, i: (0, h)),
                  pl.BlockSpec((sp, HEAD_DIM), lambda h, i: (0, h)),
                  pl.BlockSpec((SB_KEYS, SB_KEYS + HEAD_DIM), lambda h, i: (0, 0)),
                  pl.BlockSpec(memory_space=pl.ANY)],
        out_specs=rows,
        out_shape=jax.ShapeDtypeStruct(out.shape, out.dtype),
        input_output_aliases={4: 0},
        compiler_params=_params("parallel", "arbitrary"),
        name="stickbreak_prompt",
    )(q, k, v, _sb_tri(SB_KEYS), out)


def _sb_sample(q, k, v, cache_k, cache_v, layer, sp, ls, out):
    n_layers, bs, past, heads, _ = cache_k.shape
    hg = 8
    assert past % SB_KEYS == 0 and past >= SB_KEYS and sp % ls == 0 and heads % hg == 0
    row0 = sp // ls
    gw = hg * HEAD_DIM
    blk_new = pl.BlockSpec((ls, gw), lambda b, g: (row0 + b, g))
    blk_kv = pl.BlockSpec((ls, gw), lambda b, g: (b, g))
    blk_old = pl.BlockSpec((None, None, past, None, hg, HEAD_DIM), lambda b, g: (layer, b, 0, g, 0, 0))
    grouped = (n_layers, bs, past, heads // hg, hg, HEAD_DIM)
    return pl.pallas_call(
        _ignore_input(_sb_sample_kernel, 7),
        grid=(bs, heads // hg),
        in_specs=[blk_new, blk_kv, blk_kv, blk_old, blk_old,
                  pl.BlockSpec((ls, ls + HEAD_DIM), lambda b, g: (0, 0)),
                  pl.BlockSpec((SB_KEYS, SB_KEYS + HEAD_DIM), lambda b, g: (0, 0)),
                  pl.BlockSpec(memory_space=pl.ANY)],
        out_specs=blk_new,
        out_shape=jax.ShapeDtypeStruct(out.shape, out.dtype),
        input_output_aliases={7: 0},
        compiler_params=_params("parallel", "arbitrary"),
        name="stickbreak_sample",
    )(q, k, v, cache_k.reshape(grouped), cache_v.reshape(grouped), _sb_tri(ls), _sb_tri(SB_KEYS), out)


def _peer_scores_kernel(key_ref, q_ref, o_ref):
    o_ref[...] = _dot_nt(key_ref[...], q_ref[...])


def _peer_scores(pq, sub_keys, layer):
    t = pq.shape[0]
    _, nhc, nkeys, half = sub_keys.shape
    tt = _tile(t, 1024)
    return pl.pallas_call(
        _peer_scores_kernel,
        grid=(nhc, t // tt),
        in_specs=[pl.BlockSpec((None, None, nkeys, half), lambda c, i: (layer, c, 0, 0)),
                  pl.BlockSpec((tt, half), lambda c, i: (i, c))],
        out_specs=pl.BlockSpec((None, nkeys, tt), lambda c, i: (c, 0, i)),
        out_shape=jax.ShapeDtypeStruct((nhc, nkeys, t), F32),
        compiler_params=_params("parallel", "parallel"),
        name="peer_scores",
    )(sub_keys, pq)


def _extract_top(s, k, tie_safe):
    rows = lax.broadcasted_iota(jnp.int32, s.shape, 0)
    rank = jnp.full(s.shape, float(k), F32)
    vals = []
    for r in range(k):
        m = jnp.max(s, axis=0, keepdims=True)
        vals.append(m)
        hit = s == m
        if tie_safe:
            hit = rows == jnp.min(jnp.where(hit, rows, s.shape[0]), axis=0, keepdims=True)
        rank = jnp.where(hit, float(r), rank)
        s = jnp.where(hit, -jnp.inf, s)
    return vals, rank, s


def _peer_select_kernel(s_ref, m1x_ref, e1x_ref, rank2_ref, e2_ref, *, ne):
    def select(h, tie_safe):
        s1 = s_ref[2 * h]
        s2 = s_ref[2 * h + 1]
        a, rank1, _ = _extract_top(s1, PEER_TOPK, tie_safe)
        b, rank2, _ = _extract_top(s2, PEER_TOPK, tie_safe)
        b = jnp.concatenate(b, axis=0)
        sub = lax.broadcasted_iota(jnp.int32, (8, b.shape[1]), 0)
        cand = [a[0] + b, a[1] + b[:8]]
        for i in range(2, 8):
            cand.append(jnp.where(sub < PEER_TOPK // (i + 1), a[i] + b[:8], -jnp.inf))
        cand.append(jnp.concatenate(a[8:], axis=0) + b[0:1])
        cand = jnp.concatenate(cand, axis=0)
        g, _, left = _extract_top(cand, PEER_TOPK, tie_safe)
        z = jnp.ones_like(g[0])
        for gk in g[1:]:
            z = z + jnp.exp(gk - g[0])
        took = (left != cand).astype(F32)
        taken = [jnp.sum(took[0:16], axis=0, keepdims=True), jnp.sum(took[16:24], axis=0, keepdims=True)]
        taken += [jnp.sum(took[8 * i + 8:8 * i + 16], axis=0, keepdims=True) for i in range(2, 8)]
        taken += [took[72 + i:73 + i] for i in range(8)]
        m1 = jnp.zeros_like(s1)
        for i in range(PEER_TOPK):
            m1 = jnp.where(rank1 == float(i), taken[i], m1)
        e1 = jnp.exp(s1 - a[0]) / z
        rank2_ref[h] = rank2.astype(rank2_ref.dtype)
        e2_ref[h] = jnp.exp(s2 - b[0:1]).astype(e2_ref.dtype)
        row = pl.multiple_of(h * ne, ne)
        for j in range(N_KEYS // ne):
            m1x_ref[j, pl.ds(row, ne), :] = m1[j * ne:(j + 1) * ne]
            e1x_ref[j, pl.ds(row, ne), :] = e1[j * ne:(j + 1) * ne]
        removed = (jnp.sum((rank1 < PEER_TOPK).astype(F32), axis=0, keepdims=True)
                   + jnp.sum((rank2 < PEER_TOPK).astype(F32), axis=0, keepdims=True)
                   + jnp.sum(took, axis=0, keepdims=True))
        return jnp.max(removed) - 3.0 * PEER_TOPK

    def one_head(h, _):
        extra = select(h, tie_safe=False)

        @pl.when(extra > 0.0)
        def _():
            select(h, tie_safe=True)

        return 0

    lax.fori_loop(0, PEER_HEADS, one_head, 0)


def _peer_select(scores, ne):
    nhc, nkeys, t = scores.shape
    tt = _tile(t, 256)
    x_shape = jax.ShapeDtypeStruct((nkeys // ne, PEER_HEADS * ne, t), F32)
    x_spec = pl.BlockSpec((nkeys // ne, PEER_HEADS * ne, tt), lambda i: (0, 0, i))
    y_shape = jax.ShapeDtypeStruct((PEER_HEADS, nkeys, t), BF16)
    y_spec = pl.BlockSpec((PEER_HEADS, nkeys, tt), lambda i: (0, 0, i))
    return pl.pallas_call(
        functools.partial(_peer_select_kernel, ne=ne),
        grid=(t // tt,),
        in_specs=[pl.BlockSpec((nhc, nkeys, tt), lambda i: (0, 0, i))],
        out_specs=[x_spec, x_spec, y_spec, y_spec],
        out_shape=[x_shape, x_shape, y_shape, y_shape],
        compiler_params=_params("parallel"),
        name="peer_select",
    )(scores)


def _peer_act_kernel(*refs, ne, n_parts):
    u_ref, x_ref = refs[0], refs[1]
    part_refs = refs[2:2 + 4 * n_parts]
    o_ref, gate_ref = refs[2 + 4 * n_parts], refs[3 + 4 * n_parts]
    first_row = (pl.program_id(1) % (ROW_BLOCK // ne)) * ne
    for part in range(n_parts):
        m1x_ref, e1x_ref, rank2_ref, e2_ref = part_refs[4 * part:4 * part + 4]
        tp = rank2_ref.shape[2]
        for e in range(ne):
            gate = jnp.zeros((N_KEYS, tp), BF16)
            for h in range(PEER_HEADS):
                r = pl.ds(h * ROW_BLOCK + first_row + e, 1)
                hit = rank2_ref[h] < m1x_ref[r, :].astype(BF16)
                w = e2_ref[h] * e1x_ref[r, :].astype(BF16)
                gate = gate + jnp.where(hit, w, jnp.zeros_like(w))
            gate_ref[part * tp:(part + 1) * tp, e * N_KEYS:(e + 1) * N_KEYS] = gate.T
    x = x_ref[...]
    for p in range(ne // 2):
        cols = slice(2 * p * N_KEYS, 2 * (p + 1) * N_KEYS)
        hp = _dot_nt(x, u_ref[cols, :].astype(BF16))
        gelu = 0.5 * hp * (1.0 + lax.erf(hp * (0.5 ** 0.5)))
        o_ref[:, cols] = gate_ref[:, cols] * gelu.astype(BF16)


def _peer_act(xn, u, layer, m1x, e1x, rank2, e2, ne):
    t, d = xn.shape
    n_exp = u.shape[1]
    tp = _tile(t, 512)
    n_parts = 2 if t % (2 * tp) == 0 else 1
    tt = n_parts * tp
    te = ne * N_KEYS
    once = pl.Buffered(1)
    part_specs, part_args = [], []
    for part in range(n_parts):
        row_spec = pl.BlockSpec((None, PEER_HEADS * ROW_BLOCK, tp),
                                lambda i, j, part=part: (j // (ROW_BLOCK // ne), 0, n_parts * i + part))
        tile_spec = pl.BlockSpec((PEER_HEADS, N_KEYS, tp), lambda i, j, part=part: (0, 0, n_parts * i + part),
                                 pipeline_mode=once)
        part_specs += [row_spec, row_spec, tile_spec, tile_spec]
        part_args += [m1x, e1x, rank2, e2]
    return pl.pallas_call(
        functools.partial(_peer_act_kernel, ne=ne, n_parts=n_parts),
        grid=(t // tt, n_exp // te),
        in_specs=[pl.BlockSpec((None, te, d), lambda i, j: (layer, j, 0)),
                  pl.BlockSpec((tt, d), lambda i, j: (i, 0), pipeline_mode=once)] + part_specs,
        out_specs=pl.BlockSpec((tt, te), lambda i, j: (i, j)),
        out_shape=jax.ShapeDtypeStruct((t, n_exp), BF16),
        scratch_shapes=[pltpu.VMEM((tt, te), BF16)],
        compiler_params=_params("parallel", "arbitrary"),
        name="peer_act",
    )(u, xn, *part_args)


def _peer_ffn(h, layer, norm_g, w_q, sub_keys, u, v, split_at=None):
    ne = 4
    xn = _rmsnorm(h, norm_g)
    pq = _matmul(xn, w_q, layer, out_dtype=BF16, name="peer_query")
    scores = _peer_scores(pq, sub_keys, layer)
    m1x, e1x, rank2, e2 = _peer_select(scores, ROW_BLOCK)
    act = _peer_act(xn, u, layer, m1x, e1x, rank2, e2, ne)
    down = functools.partial(_matmul, act, v, layer, mode="residual", aux=h, tm=1024, tn=1024, tk=2048)
    if split_at is None:
        return down(name="peer_down")
    return (down(row0=0, m=split_at, name="peer_down_head"),
            down(row0=split_at, m=h.shape[0] - split_at, name="peer_down_tail"))


def kernel(x_prompt, x_sample, cache_a_k, cache_a_v, state_b_pool, cache_c_k, cache_c_v, norm_mix, norm_ffn, ab_w_in, ab_q_gain, ab_k_gain, ab_rel_bias, ab_pool_w, ab_pool_scale, ab_w_out, c_w_in, c_w_out, peer_w_q, peer_sub_keys, peer_u, peer_v):
    bp, sp, d = x_prompt.shape
    bs, ls, _ = x_sample.shape
    assert bp == 1
    depth = norm_mix.shape[0]
    past_len = cache_c_k.shape[2]
    a_heads = ab_rel_bias.shape[1]
    a_width = a_heads * HEAD_DIM
    b_width = ab_pool_scale.shape[1]
    c_width = c_w_out.shape[1]
    win_p = min(A_PAST_CHUNKS * CHUNK, sp)
    n_s = bs * ls

    h = jnp.concatenate([x_prompt.reshape(sp, d), x_sample.reshape(n_s, d)], axis=0)

    ab_w_in_b, ab_w_out_b = ab_w_in.astype(BF16), ab_w_out.astype(BF16)
    c_w_in_b, c_w_out_b = c_w_in.astype(BF16), c_w_out.astype(BF16)
    peer_w_q_b, peer_v_b = peer_w_q.astype(BF16), peer_v.astype(BF16)
    sub_keys_b = peer_sub_keys.astype(BF16).reshape(depth, PEER_HEADS * 2, N_KEYS, -1)

    outs = {name: [] for name in ("a_k_p", "a_v_p", "b_p", "c_k_p", "c_v_p",
                                  "a_k_s", "a_v_s", "b_s", "c_k_s", "c_v_s")}
    for layer in range(depth):
        xn = _rmsnorm(h, norm_mix[layer])
        if layer % 2 == 0:
            i = layer // 2
            q = _matmul(xn, ab_w_in_b, i, col0=0, n=a_width, out_dtype=BF16, mode="headnorm",
                        aux=ab_q_gain[i], name="a_q")
            k = _matmul(xn, ab_w_in_b, i, col0=a_width, n=a_width, mode="headnorm", aux=ab_k_gain[i], name="a_k")
            v = _matmul(xn, ab_w_in_b, i, col0=2 * a_width, n=a_width, name="a_v")
            p = _matmul(xn, ab_w_in_b, i, col0=3 * a_width, n=b_width, name="b_in")

            att = jnp.zeros((sp + n_s, a_width), BF16)
            att = _band_prompt(q, k, v, ab_rel_bias[i], sp, att)
            att = _band_sample(q, k, v, cache_a_k, cache_a_v, i, ab_rel_bias[i], sp, ls, past_len, att)

            pool_w = ab_pool_w[i].astype(BF16)
            tr = _tile(sp, 256)
            pool = jnp.zeros((sp + n_s, b_width), BF16)
            pool = _pool(p, None, pool_w, ab_pool_scale[i], pool, row_block0=0, nblk=sp // tr, tr=tr,
                         pos_base=0, pos_step=tr, name="pool_prompt")
            hist_s = jnp.pad(state_b_pool[i], ((0, 0), (POOL_HIST_PAD - POOL_HIST, 0), (0, 0)))
            pool = _pool(p, hist_s, pool_w, ab_pool_scale[i], pool, row_block0=sp // ls, nblk=bs, tr=ls,
                         pos_base=past_len, pos_step=0, name="pool_sample")

            h = _matmul(att, ab_w_out_b, i, x2=pool, mode="residual", aux=h, name="a_out")

            ps = p[sp:].reshape(bs, ls, b_width)
            outs["a_k_p"].append(k[sp - win_p:sp].reshape(bp, win_p, a_heads, HEAD_DIM))
            outs["a_v_p"].append(v[sp - win_p:sp].reshape(bp, win_p, a_heads, HEAD_DIM))
            outs["b_p"].append(p[sp - POOL_HIST:sp].reshape(bp, POOL_HIST, b_width))
            outs["a_k_s"].append(k[sp:].reshape(bs, ls, a_heads, HEAD_DIM))
            outs["a_v_s"].append(v[sp:].reshape(bs, ls, a_heads, HEAD_DIM))
            outs["b_s"].append(jnp.concatenate([state_b_pool[i], ps], axis=1)[:, ls:])
        else:
            j = layer // 2
            c_heads = c_width // HEAD_DIM
            q = _matmul(xn, c_w_in_b, j, col0=0, n=c_width, out_dtype=BF16, name="c_q")
            k_p, k_p3 = _proj_heads(xn, c_w_in_b, j, col0=c_width, n=c_width, row0=0, m=sp, name="c_k_prompt")
            v_p, v_p3 = _proj_heads(xn, c_w_in_b, j, col0=2 * c_width, n=c_width, row0=0, m=sp, name="c_v_prompt")
            k_s, k_s3 = _proj_heads(xn, c_w_in_b, j, col0=c_width, n=c_width, row0=sp, m=n_s, name="c_k_sample")
            v_s, v_s3 = _proj_heads(xn, c_w_in_b, j, col0=2 * c_width, n=c_width, row0=sp, m=n_s, name="c_v_sample")
            att = jnp.zeros((sp + n_s, c_width), BF16)
            att = _sb_prompt(q, k_p, v_p, sp, att)
            att = _sb_sample(q, k_s, v_s, cache_c_k, cache_c_v, j, sp, ls, att)
            h = _matmul(att, c_w_out_b, j, mode="residual", aux=h, name="c_out")
            outs["c_k_p"].append(k_p3.reshape(bp, sp, c_heads, HEAD_DIM))
            outs["c_v_p"].append(v_p3.reshape(bp, sp, c_heads, HEAD_DIM))
            outs["c_k_s"].append(k_s3.reshape(bs, ls, c_heads, HEAD_DIM))
            outs["c_v_s"].append(v_s3.reshape(bs, ls, c_heads, HEAD_DIM))

        h = _peer_ffn(h, layer, norm_ffn[layer], peer_w_q_b, sub_keys_b, peer_u, peer_v_b,
                      split_at=sp if layer == depth - 1 else None)

    h_prompt, h_sample = h
    st = {name: jnp.stack(vals) for name, vals in outs.items()}
    return (h_prompt.reshape(bp, sp, d), h_sample.reshape(bs, ls, d),
            st["a_k_p"], st["a_v_p"], st["b_p"], st["c_k_p"], st["c_v_p"],
            st["a_k_s"], st["a_v_s"], st["b_s"], st["c_k_s"], st["c_v_s"])
```

```python
import functools

import jax
import jax.numpy as jnp
from jax import lax
from jax.experimental import pallas as pl
from jax.experimental.pallas import tpu as pltpu

F32 = jnp.float32
BF16 = jnp.bfloat16

HEAD_DIM = 128
CHUNK = 64
A_PAST_CHUNKS = 8
REL_CLIP = 128
POOL_WINDOWS = (2, 4, 8, 16)
POOL_HIST = max(POOL_WINDOWS) - 1
POOL_HIST_PAD = 16
B_GROUP_WIDTH = 512
PEER_HEADS = 8
N_KEYS = 128
PEER_TOPK = 16
ROW_BLOCK = 8
EPS = 1e-6
NEG_INF = -1e30
ATTN_SCALE = HEAD_DIM ** -0.5
SB_EXIT = -104.0
SB_KEYS = 128
VMEM_LIMIT_BYTES = 56 * 1024 * 1024


def _params(*sem):
    return pltpu.CompilerParams(dimension_semantics=sem, vmem_limit_bytes=VMEM_LIMIT_BYTES)


def _tile(dim, pref):
    return pref if dim % pref == 0 else dim


def _ignore_input(body, index):
    def wrapped(*refs):
        return body(*refs[:index], *refs[index + 1:])
    return wrapped


def _dot(a, b):
    return jnp.dot(a, b, preferred_element_type=F32)


def _dot_nt(a, b):
    return lax.dot_general(a, b, (((1,), (1,)), ((), ())), preferred_element_type=F32)


def _rmsnorm_kernel(x_ref, g_ref, o_ref, *transposed_ref):
    x = x_ref[...]
    ms = jnp.mean(x * x, axis=-1, keepdims=True)
    y = (x * lax.rsqrt(ms + EPS) * g_ref[...]).astype(o_ref.dtype)
    o_ref[...] = y
    for ot_ref in transposed_ref:
        ot_ref[...] = y.T


def _rmsnorm(x, g, with_transpose=False):
    t, d = x.shape
    tr = _tile(t, 256)
    out_specs = [pl.BlockSpec((tr, d), lambda i: (i, 0))]
    out_shape = [jax.ShapeDtypeStruct((t, d), BF16)]
    if with_transpose:
        out_specs.append(pl.BlockSpec((d, tr), lambda i: (0, i)))
        out_shape.append(jax.ShapeDtypeStruct((d, t), BF16))
    outs = pl.pallas_call(
        _rmsnorm_kernel,
        grid=(t // tr,),
        in_specs=[pl.BlockSpec((tr, d), lambda i: (i, 0)),
                  pl.BlockSpec((1, d), lambda i: (0, 0))],
        out_specs=out_specs,
        out_shape=out_shape,
        compiler_params=_params("parallel"),
        name="rmsnorm",
    )(x, g.reshape(1, d))
    return outs if with_transpose else outs[0]


def _mm_kernel(*refs, nk, mode, two_x):
    x_ref, w_ref = refs[0], refs[1]
    refs = refs[2:]
    if two_x:
        x2_ref, refs = refs[0], refs[1:]
    if mode in ("headnorm", "residual"):
        aux_ref, o_ref, scratch = refs[0], refs[1], refs[2:]
    else:
        aux_ref, o_ref, scratch = None, refs[0], refs[1:]

    def finish(acc):
        if mode == "headnorm":
            g = aux_ref[...]
            for c in range(acc.shape[1] // HEAD_DIM):
                sl = slice(c * HEAD_DIM, (c + 1) * HEAD_DIM)
                y = acc[:, sl]
                ms = jnp.mean(y * y, axis=-1, keepdims=True)
                o_ref[:, sl] = (y * lax.rsqrt(ms + EPS) * g).astype(o_ref.dtype)
        elif mode == "residual":
            o_ref[...] = aux_ref[...] + acc
        else:
            o_ref[...] = acc.astype(o_ref.dtype)

    if two_x:
        k1 = x_ref.shape[1]
        finish(_dot(x_ref[...], w_ref[:k1, :]) + _dot(x2_ref[...], w_ref[k1:, :]))
    elif nk == 1:
        finish(_dot(x_ref[...], w_ref[...]))
    else:
        acc_ref = scratch[0]
        k = pl.program_id(2)

        @pl.when(k == 0)
        def _():
            acc_ref[...] = jnp.zeros_like(acc_ref)

        acc_ref[...] += _dot(x_ref[...], w_ref[...])

        @pl.when(k == nk - 1)
        def _():
            finish(acc_ref[...])


def _matmul(x, w, layer, *, x2=None, col0=0, n=None, row0=0, m=None, out_dtype=F32, mode="plain", aux=None,
            tm=1024, tn=512, tk=None, name="matmul"):
    kdim = x.shape[1]
    m = x.shape[0] if m is None else m
    if x2 is not None:
        assert tk is None
        kdim += x2.shape[1]
    n = w.shape[2] if n is None else n
    tm, tn = _tile(m, tm), _tile(n, tn)
    tk = kdim if tk is None else _tile(kdim, tk)
    nk = kdim // tk
    rb, cb = row0 // tm, col0 // tn
    assert row0 % tm == 0 and col0 % tn == 0
    in_specs = [pl.BlockSpec((tm, x.shape[1] if x2 is not None else tk), lambda i, j, k: (i + rb, k)),
                pl.BlockSpec((None, tk, tn), lambda i, j, k: (layer, k, j + cb))]
    args = [x, w]
    if x2 is not None:
        in_specs.append(pl.BlockSpec((tm, x2.shape[1]), lambda i, j, k: (i + rb, 0)))
        args.append(x2)
    if mode == "headnorm":
        in_specs.append(pl.BlockSpec((1, HEAD_DIM), lambda i, j, k: (0, 0)))
        args.append(aux.reshape(1, HEAD_DIM))
    elif mode == "residual":
        in_specs.append(pl.BlockSpec((tm, tn), lambda i, j, k: (i + rb, j)))
        args.append(aux)
    scratch = [pltpu.VMEM((tm, tn), F32)] if nk > 1 else []
    return pl.pallas_call(
        functools.partial(_mm_kernel, nk=nk, mode=mode, two_x=x2 is not None),
        grid=(m // tm, n // tn, nk),
        in_specs=in_specs,
        out_specs=pl.BlockSpec((tm, tn), lambda i, j, k: (i, j)),
        out_shape=jax.ShapeDtypeStruct((m, n), out_dtype),
        scratch_shapes=scratch,
        compiler_params=_params("parallel", "parallel", "arbitrary"),
        name=name,
    )(*args)


def _proj_heads_kernel(x_ref, w_ref, o2_ref, o3_ref):
    acc = _dot(x_ref[...], w_ref[...])
    o2_ref[...] = acc.astype(o2_ref.dtype)
    o3_ref[...] = pltpu.einshape("m(hd)->mhd", acc, h=o3_ref.shape[1])


def _proj_heads(x, w, layer, *, col0, n, row0, m, name):
    kdim = x.shape[1]
    tm, tn = _tile(m, 1024), 8 * HEAD_DIM
    assert row0 % tm == 0 and col0 % tn == 0 and n % tn == 0
    rb, cb = row0 // tm, col0 // tn
    return pl.pallas_call(
        _proj_heads_kernel,
        grid=(m // tm, n // tn),
        in_specs=[pl.BlockSpec((tm, kdim), lambda i, j: (i + rb, 0)),
                  pl.BlockSpec((None, kdim, tn), lambda i, j: (layer, 0, j + cb))],
        out_specs=[pl.BlockSpec((tm, tn), lambda i, j: (i, j)),
                   pl.BlockSpec((tm, tn // HEAD_DIM, HEAD_DIM), lambda i, j: (i, j, 0))],
        out_shape=[jax.ShapeDtypeStruct((m, n), BF16),
                   jax.ShapeDtypeStruct((m, n // HEAD_DIM, HEAD_DIM), F32)],
        compiler_params=_params("parallel", "parallel"),
        name=name,
    )(x, w)


def _band_head(q, ka, kb, va, vb, bias_a, bias_b, past_visible):
    sa = _dot_nt(q, ka.astype(BF16)) * ATTN_SCALE + bias_a
    sb = _dot_nt(q, kb.astype(BF16)) * ATTN_SCALE + bias_b
    if past_visible is not None:
        sa = jnp.where(past_visible, sa, NEG_INF)
    m = jnp.maximum(jnp.max(sa, axis=-1, keepdims=True), jnp.max(sb, axis=-1, keepdims=True))
    pa = jnp.exp(sa - m)
    pb = jnp.exp(sb - m)
    l = jnp.sum(pa, axis=-1, keepdims=True) + jnp.sum(pb, axis=-1, keepdims=True)
    o = _dot(pa.astype(BF16), va.astype(BF16)) + _dot(pb.astype(BF16), vb.astype(BF16))
    return o / l


def _band_prompt_kernel(q_ref, ka_ref, kb_ref, va_ref, vb_ref, bias_ref, o_ref):
    na = ka_ref.shape[0]
    o = _band_head(q_ref[...], ka_ref[...], kb_ref[...], va_ref[...], vb_ref[...],
                   bias_ref[:, :na], bias_ref[:, na:], pl.program_id(1) > 0)
    o_ref[...] = o.astype(o_ref.dtype)


def _band_sample_kernel(q_ref, kn_ref, vn_ref, kc_ref, vc_ref, bias_ref, o_ref):
    win, heads = kc_ref.shape[0], kc_ref.shape[1]
    kc = pltpu.einshape("phd->hpd", kc_ref[...])
    vc = pltpu.einshape("phd->hpd", vc_ref[...])
    for hh in range(heads):
        sl = slice(hh * HEAD_DIM, (hh + 1) * HEAD_DIM)
        o = _band_head(q_ref[:, sl], kc[hh], kn_ref[:, sl], vc[hh], vn_ref[:, sl],
                       bias_ref[hh, :, :win], bias_ref[hh, :, win:], None)
        o_ref[:, sl] = o.astype(o_ref.dtype)


def _band_bias(rel_bias, q_pos, k_pos):
    lq, lk = q_pos.shape[0], k_pos.shape[0]
    d_min = q_pos[0] - k_pos[lk - 1]
    n_diag = lq + lk - 1
    diag = jnp.clip(d_min + jnp.arange(n_diag, dtype=jnp.int32), -REL_CLIP, REL_CLIP) + REL_CLIP
    g = rel_bias.astype(F32)[:, diag]
    row_len = -(-n_diag // HEAD_DIM) * HEAD_DIM
    period = row_len + 1
    u = jnp.pad(g[:, ::-1], ((0, 0), (0, period - n_diag)))
    skew = jnp.tile(u, (1, lq))[:, :lq * row_len].reshape(-1, lq, row_len)
    bias = skew[:, :, lq - 1:lq - 1 + lk]
    qc = q_pos[:, None] // CHUNK
    kc = k_pos[None, :] // CHUNK
    mask = (kc <= qc) & (kc >= qc - A_PAST_CHUNKS)
    return jnp.where(mask[None], bias, NEG_INF)


def _band_prompt(q, k, v, rel_bias, sp, out):
    heads = rel_bias.shape[0]
    tq = A_PAST_CHUNKS * CHUNK
    assert sp % tq == 0
    pos = jnp.arange(tq, dtype=jnp.int32)
    bias = _band_bias(rel_bias, tq + pos, jnp.arange(2 * tq, dtype=jnp.int32))
    prev = lambda h, i: (jnp.maximum(i - 1, 0), h)
    cur = lambda h, i: (i, h)
    blk = lambda im: pl.BlockSpec((tq, HEAD_DIM), im)
    return pl.pallas_call(
        _ignore_input(_band_prompt_kernel, 6),
        grid=(heads, sp // tq),
        in_specs=[blk(cur), blk(prev), blk(cur), blk(prev), blk(cur),
                  pl.BlockSpec((None, tq, 2 * tq), lambda h, i: (h, 0, 0)),
                  pl.BlockSpec(memory_space=pl.ANY)],
        out_specs=blk(cur),
        out_shape=jax.ShapeDtypeStruct(out.shape, out.dtype),
        input_output_aliases={6: 0},
        compiler_params=_params("parallel", "arbitrary"),
        name="band_prompt",
    )(q, k, k, v, v, bias, out)


def _band_sample(q, k, v, cache_k, cache_v, layer, rel_bias, sp, ls, past_len, out):
    _, bs, win, heads, _ = cache_k.shape
    width = heads * HEAD_DIM
    assert sp % ls == 0
    row0 = sp // ls
    q_pos = past_len + jnp.arange(ls, dtype=jnp.int32)
    k_pos = past_len - win + jnp.arange(win + ls, dtype=jnp.int32)
    bias = _band_bias(rel_bias, q_pos, k_pos)
    blk_new = pl.BlockSpec((ls, width), lambda b: (row0 + b, 0))
    blk_old = pl.BlockSpec((None, None, win, heads, HEAD_DIM), lambda b: (layer, b, 0, 0, 0))
    return pl.pallas_call(
        _ignore_input(_band_sample_kernel, 6),
        grid=(bs,),
        in_specs=[blk_new, blk_new, blk_new, blk_old, blk_old,
                  pl.BlockSpec((heads, ls, win + ls), lambda b: (0, 0, 0)),
                  pl.BlockSpec(memory_space=pl.ANY)],
        out_specs=blk_new,
        out_shape=jax.ShapeDtypeStruct(out.shape, out.dtype),
        input_output_aliases={6: 0},
        compiler_params=_params("parallel"),
        name="band_sample",
    )(q, k, v, cache_k, cache_v, bias, out)


def _pool_kernel(cur_ref, hist_ref, w_ref, sc_ref, o_ref, ext_ref, *, pos_base, pos_step, first_has_no_past):
    tr = cur_ref.shape[0]
    hist = hist_ref[...]
    if first_has_no_past:
        hist = jnp.where(pl.program_id(0) > 0, hist, 0.0)
    ext_ref[0:POOL_HIST_PAD, :] = hist
    ext_ref[POOL_HIST_PAD:POOL_HIST_PAD + tr, :] = cur_ref[...]
    pos = pos_base + pl.program_id(0) * pos_step + lax.broadcasted_iota(jnp.int32, (tr, 1), 0)
    for g, w in enumerate(POOL_WINDOWS):
        sl = slice(g * B_GROUP_WIDTH, (g + 1) * B_GROUP_WIDTH)
        cur = cur_ref[:, sl]
        tot = cur
        for j in range(1, w):
            tot = tot + ext_ref[POOL_HIST_PAD - j:POOL_HIST_PAD - j + tr, sl]
        cnt = jnp.minimum(pos + 1, w).astype(F32)
        d = tot / cnt - cur
        y = _dot(d.astype(BF16), w_ref[g]) * sc_ref[:, sl]
        o_ref[:, sl] = y.astype(o_ref.dtype)


def _pool(p, hist, pool_w, pool_scale, out, *, row_block0, nblk, tr, pos_base, pos_step, name):
    width = p.shape[1]
    rows = pl.BlockSpec((tr, width), lambda i: (row_block0 + i, 0))
    body = functools.partial(_pool_kernel, pos_base=pos_base, pos_step=pos_step, first_has_no_past=hist is None)
    if hist is None:
        assert row_block0 == 0 and tr % POOL_HIST_PAD == 0
        per = tr // POOL_HIST_PAD
        hist_spec = pl.BlockSpec((POOL_HIST_PAD, width), lambda i: (jnp.maximum(i * per - 1, 0), 0))
        hist = p
    else:
        hist_spec = pl.BlockSpec((None, POOL_HIST_PAD, width), lambda i: (i, 0, 0))
    return pl.pallas_call(
        _ignore_input(body, 4),
        grid=(nblk,),
        in_specs=[rows, hist_spec,
                  pl.BlockSpec(pool_w.shape, lambda i: (0, 0, 0)),
                  pl.BlockSpec((1, width), lambda i: (0, 0)),
                  pl.BlockSpec(memory_space=pl.ANY)],
        out_specs=rows,
        out_shape=jax.ShapeDtypeStruct(out.shape, out.dtype),
        input_output_aliases={4: 0},
        scratch_shapes=[pltpu.VMEM((POOL_HIST_PAD + tr, width), F32)],
        compiler_params=_params("parallel"),
        name=name,
    )(p, hist, pool_w, pool_scale.reshape(1, width), out)


def _sb_steps(qs, kvs, tri, carries, accs, visibles):
    nk = kvs[0][0].shape[0]
    zs = [_dot_nt(q, k) * ATTN_SCALE for q, (k, _) in zip(qs, kvs)]
    log_keeps = [jnp.where(vis, -(jnp.maximum(z, 0.0) + jnp.log(1.0 + jnp.exp(-jnp.abs(z)))), 0.0)
                 for z, vis in zip(zs, visibles)]
    his = [lk.astype(BF16) for lk in log_keeps]
    los = [(lk - hi.astype(F32)).astype(BF16) for lk, hi in zip(log_keeps, his)]
    sums = [_dot(hi, tri) + _dot(lo, tri) for hi, lo in zip(his, los)]
    weights = [jnp.where(vis, jnp.exp(z + lk + s[:, :nk] + carry[:, :nk]), 0.0).astype(BF16)
               for z, lk, s, carry, vis in zip(zs, log_keeps, sums, carries, visibles)]
    new_accs = [acc + _dot(a, v) for acc, a, (_, v) in zip(accs, weights, kvs)]
    new_carries = [carry + s[:, nk:] for carry, s in zip(carries, sums)]
    return new_carries, new_accs


def _sb_sweep(chains, load_kv, tri, carries, accs):
    rows = chains[0][0].shape[0]
    lane = lax.broadcasted_iota(jnp.int32, (rows, SB_KEYS), 1)
    beyond = jnp.int32(2 ** 30)

    def cond(st):
        return st[1] == 0

    def body(st):
        n, _, carries, accs = st
        kbs = [kb0 - n for _, kb0, _ in chains]
        starts = [pl.multiple_of(jnp.maximum(kb, 0) * SB_KEYS, SB_KEYS) for kb in kbs]
        visibles = [(jnp.where(kb >= 0, start, beyond) + lane) < qpos
                    for (_, _, qpos), kb, start in zip(chains, kbs, starts)]
        carries, accs = _sb_steps([q for q, _, _ in chains], load_kv(starts), tri, carries, accs, visibles)
        pending = jnp.full((rows, HEAD_DIM), -jnp.inf, F32)
        for kb, carry in zip(kbs, carries):
            pending = jnp.maximum(pending, jnp.where(kb > 0, carry, -jnp.inf))
        done = (jnp.max(pending) < SB_EXIT).astype(jnp.int32)
        return n + 1, done, carries, accs

    return lax.while_loop(cond, body, (jnp.int32(0), jnp.int32(0), list(carries), list(accs)))[3]


def _sb_prompt_kernel(q_ref, k_ref, v_ref, tri_ref, o_ref, *, tq):
    n_chains = q_ref.shape[0] // tq
    row0 = pl.program_id(1) * q_ref.shape[0]
    row = lax.broadcasted_iota(jnp.int32, (tq, SB_KEYS), 0)

    def load_kv(starts):
        return [(k_ref[pl.ds(s, SB_KEYS), :].astype(BF16), v_ref[pl.ds(s, SB_KEYS), :].astype(BF16))
                for s in starts]

    chains = [(q_ref[c * tq:(c + 1) * tq, :], (row0 + (c + 1) * tq) // SB_KEYS - 1, row0 + c * tq + row)
              for c in range(n_chains)]
    zeros = [jnp.zeros((tq, HEAD_DIM), F32)] * n_chains
    accs = _sb_sweep(chains, load_kv, tri_ref[...], zeros, zeros)
    for c in range(n_chains):
        o_ref[c * tq:(c + 1) * tq, :] = accs[c].astype(o_ref.dtype)


def _sb_sample_kernel(q_ref, kn_ref, vn_ref, kc_ref, vc_ref, trin_ref, tri_ref, o_ref):
    ls = q_ref.shape[0]
    past, heads = kc_ref.shape[0], kc_ref.shape[1]
    causal = lax.broadcasted_iota(jnp.int32, (ls, ls), 1) < lax.broadcasted_iota(jnp.int32, (ls, ls), 0)
    all_rows = jnp.full((ls, SB_KEYS), 2 ** 30 - SB_KEYS, jnp.int32)
    zeros = jnp.zeros((ls, HEAD_DIM), F32)
    head_cols = [slice(hh * HEAD_DIM, (hh + 1) * HEAD_DIM) for hh in range(heads)]
    chains = [(q_ref[:, sl], past // SB_KEYS - 1, all_rows) for sl in head_cols]
    carries, accs = _sb_steps([q for q, _, _ in chains],
                              [(kn_ref[:, sl].astype(BF16), vn_ref[:, sl].astype(BF16)) for sl in head_cols],
                              trin_ref[...], [zeros] * heads, [zeros] * heads, [causal] * heads)

    def load_kv(starts):
        k = pltpu.einshape("phd->hpd", kc_ref[pl.ds(starts[0], SB_KEYS), :, :])
        v = pltpu.einshape("phd->hpd", vc_ref[pl.ds(starts[0], SB_KEYS), :, :])
        return [(k[hh].astype(BF16), v[hh].astype(BF16)) for hh in range(heads)]

    accs = _sb_sweep(chains, load_kv, tri_ref[...], carries, accs)
    for hh in range(heads):
        o_ref[:, hh * HEAD_DIM:(hh + 1) * HEAD_DIM] = accs[hh].astype(o_ref.dtype)


def _sb_tri(nk):
    j = jnp.arange(nk)[:, None]
    s = jnp.arange(nk + HEAD_DIM)[None, :]
    return ((s >= nk) | (j > s)).astype(BF16)


def _sb_prompt(q, k, v, sp, out):
    heads = q.shape[1] // HEAD_DIM
    tq = SB_KEYS
    tb = _tile(sp, 8 * tq)
    assert sp % SB_KEYS == 0 and tb % tq == 0
    rows = pl.BlockSpec((tb, HEAD_DIM), lambda h, i: (i, h))
    return pl.pallas_call(
        _ignore_input(functools.partial(_sb_prompt_kernel, tq=tq), 4),
        grid=(heads, sp // tb),
        in_specs=[rows,
                  pl.BlockSpec((sp, HEAD_DIM), lambda h, i: (0, h)),
                  pl.BlockSpec((sp, HEAD_DIM), lambda h, i: (0, h)),
                  pl.BlockSpec((SB_KEYS, SB_KEYS + HEAD_DIM), lambda h, i: (0, 0)),
                  pl.BlockSpec(memory_space=pl.ANY)],
        out_specs=rows,
        out_shape=jax.ShapeDtypeStruct(out.shape, out.dtype),
        input_output_aliases={4: 0},
        compiler_params=_params("parallel", "arbitrary"),
        name="stickbreak_prompt",
    )(q, k, v, _sb_tri(SB_KEYS), out)


def _sb_sample(q, k, v, cache_k, cache_v, layer, sp, ls, out):
    n_layers, bs, past, heads, _ = cache_k.shape
    hg = 8
    assert past % SB_KEYS == 0 and past >= SB_KEYS and sp % ls == 0 and heads % hg == 0
    row0 = sp // ls
    gw = hg * HEAD_DIM
    blk_new = pl.BlockSpec((ls, gw), lambda b, g: (row0 + b, g))
    blk_kv = pl.BlockSpec((ls, gw), lambda b, g: (b, g))
    blk_old = pl.BlockSpec((None, None, past, None, hg, HEAD_DIM), lambda b, g: (layer, b, 0, g, 0, 0))
    grouped = (n_layers, bs, past, heads // hg, hg, HEAD_DIM)
    return pl.pallas_call(
        _ignore_input(_sb_sample_kernel, 7),
        grid=(bs, heads // hg),
        in_specs=[blk_new, blk_kv, blk_kv, blk_old, blk_old,
                  pl.BlockSpec((ls, ls + HEAD_DIM), lambda b, g: (0, 0)),
                  pl.BlockSpec((SB_KEYS, SB_KEYS + HEAD_DIM), lambda b, g: (0, 0)),
                  pl.BlockSpec(memory_space=pl.ANY)],
        out_specs=blk_new,
        out_shape=jax.ShapeDtypeStruct(out.shape, out.dtype),
        input_output_aliases={7: 0},
        compiler_params=_params("parallel", "arbitrary"),
        name="stickbreak_sample",
    )(q, k, v, cache_k.reshape(grouped), cache_v.reshape(grouped), _sb_tri(ls), _sb_tri(SB_KEYS), out)


def _peer_scores_kernel(key_ref, q_ref, o_ref):
    o_ref[...] = _dot_nt(key_ref[...], q_ref[...])


def _peer_scores(pq, sub_keys, layer):
    t = pq.shape[0]
    _, nhc, nkeys, half = sub_keys.shape
    tt = _tile(t, 1024)
    return pl.pallas_call(
        _peer_scores_kernel,
        grid=(nhc, t // tt),
        in_specs=[pl.BlockSpec((None, None, nkeys, half), lambda c, i: (layer, c, 0, 0)),
                  pl.BlockSpec((tt, half), lambda c, i: (i, c))],
        out_specs=pl.BlockSpec((None, nkeys, tt), lambda c, i: (c, 0, i)),
        out_shape=jax.ShapeDtypeStruct((nhc, nkeys, t), F32),
        compiler_params=_params("parallel", "parallel"),
        name="peer_scores",
    )(sub_keys, pq)


def _extract_top(s, k, tie_safe):
    rows = lax.broadcasted_iota(jnp.int32, s.shape, 0)
    rank = jnp.full(s.shape, float(k), F32)
    vals = []
    for r in range(k):
        m = jnp.max(s, axis=0, keepdims=True)
        vals.append(m)
        hit = s == m
        if tie_safe:
            hit = rows == jnp.min(jnp.where(hit, rows, s.shape[0]), axis=0, keepdims=True)
        rank = jnp.where(hit, float(r), rank)
        s = jnp.where(hit, -jnp.inf, s)
    return vals, rank, s


def _peer_select_kernel(s_ref, m1x_ref, e1x_ref, rank2_ref, e2_ref, *, ne):
    def select(h, tie_safe):
        s1 = s_ref[2 * h]
        s2 = s_ref[2 * h + 1]
        a, rank1, _ = _extract_top(s1, PEER_TOPK, tie_safe)
        b, rank2, _ = _extract_top(s2, PEER_TOPK, tie_safe)
        b = jnp.concatenate(b, axis=0)
        sub = lax.broadcasted_iota(jnp.int32, (8, b.shape[1]), 0)
        cand = [a[0] + b, a[1] + b[:8]]
        for i in range(2, 8):
            cand.append(jnp.where(sub < PEER_TOPK // (i + 1), a[i] + b[:8], -jnp.inf))
        cand.append(jnp.concatenate(a[8:], axis=0) + b[0:1])
        cand = jnp.concatenate(cand, axis=0)
        g, _, left = _extract_top(cand, PEER_TOPK, tie_safe)
        z = jnp.ones_like(g[0])
        for gk in g[1:]:
            z = z + jnp.exp(gk - g[0])
        took = (left != cand).astype(F32)
        taken = [jnp.sum(took[0:16], axis=0, keepdims=True), jnp.sum(took[16:24], axis=0, keepdims=True)]
        taken += [jnp.sum(took[8 * i + 8:8 * i + 16], axis=0, keepdims=True) for i in range(2, 8)]
        taken += [took[72 + i:73 + i] for i in range(8)]
        m1 = jnp.zeros_like(s1)
        for i in range(PEER_TOPK):
            m1 = jnp.where(rank1 == float(i), taken[i], m1)
        e1 = jnp.exp(s1 - a[0]) / z
        rank2_ref[h] = rank2.astype(rank2_ref.dtype)
        e2_ref[h] = jnp.exp(s2 - b[0:1]).astype(e2_ref.dtype)
        row = pl.multiple_of(h * ne, ne)
        for j in range(N_KEYS // ne):
            m1x_ref[j, pl.ds(row, ne), :] = m1[j * ne:(j + 1) * ne]
            e1x_ref[j, pl.ds(row, ne), :] = e1[j * ne:(j + 1) * ne]
        removed = (jnp.sum((rank1 < PEER_TOPK).astype(F32), axis=0, keepdims=True)
                   + jnp.sum((rank2 < PEER_TOPK).astype(F32), axis=0, keepdims=True)
                   + jnp.sum(took, axis=0, keepdims=True))
        return jnp.max(removed) - 3.0 * PEER_TOPK

    def one_head(h, _):
        extra = select(h, tie_safe=False)

        @pl.when(extra > 0.0)
        def _():
            select(h, tie_safe=True)

        return 0

    lax.fori_loop(0, PEER_HEADS, one_head, 0)


def _peer_select(scores, ne):
    nhc, nkeys, t = scores.shape
    tt = _tile(t, 256)
    x_shape = jax.ShapeDtypeStruct((nkeys // ne, PEER_HEADS * ne, t), F32)
    x_spec = pl.BlockSpec((nkeys // ne, PEER_HEADS * ne, tt), lambda i: (0, 0, i))
    y_shape = jax.ShapeDtypeStruct((PEER_HEADS, nkeys, t), BF16)
    y_spec = pl.BlockSpec((PEER_HEADS, nkeys, tt), lambda i: (0, 0, i))
    return pl.pallas_call(
        functools.partial(_peer_select_kernel, ne=ne),
        grid=(t // tt,),
        in_specs=[pl.BlockSpec((nhc, nkeys, tt), lambda i: (0, 0, i))],
        out_specs=[x_spec, x_spec, y_spec, y_spec],
        out_shape=[x_shape, x_shape, y_shape, y_shape],
        compiler_params=_params("parallel"),
        name="peer_select",
    )(scores)


def _peer_act_kernel(*refs, ne, n_parts):
    u_ref, xt_ref = refs[0], refs[1]
    part_refs = refs[2:2 + 4 * n_parts]
    o_ref, gate_ref = refs[2 + 4 * n_parts], refs[3 + 4 * n_parts]
    first_row = (pl.program_id(1) % (ROW_BLOCK // ne)) * ne
    for part in range(n_parts):
        m1x_ref, e1x_ref, rank2_ref, e2_ref = part_refs[4 * part:4 * part + 4]
        tp = rank2_ref.shape[2]
        for e in range(ne):
            gate = jnp.zeros((N_KEYS, tp), BF16)
            for h in range(PEER_HEADS):
                r = pl.ds(h * ROW_BLOCK + first_row + e, 1)
                hit = rank2_ref[h] < m1x_ref[r, :].astype(BF16)
                w = e2_ref[h] * e1x_ref[r, :].astype(BF16)
                gate = gate + jnp.where(hit, w, jnp.zeros_like(w))
            gate_ref[e * N_KEYS:(e + 1) * N_KEYS, part * tp:(part + 1) * tp] = gate
    xt = xt_ref[...]
    for p in range(ne // 2):
        rows = slice(2 * p * N_KEYS, 2 * (p + 1) * N_KEYS)
        ht = _dot(u_ref[rows, :].astype(BF16), xt)
        gelu = 0.5 * ht * (1.0 + lax.erf(ht * (0.5 ** 0.5)))
        act_t = gate_ref[rows, :] * gelu.astype(BF16)
        o_ref[:, rows] = act_t.T


def _peer_act(xt, u, layer, m1x, e1x, rank2, e2, ne):
    d, t = xt.shape
    n_exp = u.shape[1]
    tp = _tile(t, 512)
    n_parts = 2 if t % (2 * tp) == 0 else 1
    tt = n_parts * tp
    te = ne * N_KEYS
    once = pl.Buffered(1)
    part_specs, part_args = [], []
    for part in range(n_parts):
        row_spec = pl.BlockSpec((None, PEER_HEADS * ROW_BLOCK, tp),
                                lambda i, j, part=part: (j // (ROW_BLOCK // ne), 0, n_parts * i + part))
        tile_spec = pl.BlockSpec((PEER_HEADS, N_KEYS, tp), lambda i, j, part=part: (0, 0, n_parts * i + part),
                                 pipeline_mode=once)
        part_specs += [row_spec, row_spec, tile_spec, tile_spec]
        part_args += [m1x, e1x, rank2, e2]
    return pl.pallas_call(
        functools.partial(_peer_act_kernel, ne=ne, n_parts=n_parts),
        grid=(t // tt, n_exp // te),
        in_specs=[pl.BlockSpec((None, te, d), lambda i, j: (layer, j, 0)),
                  pl.BlockSpec((d, tt), lambda i, j: (0, i), pipeline_mode=once)] + part_specs,
        out_specs=pl.BlockSpec((tt, te), lambda i, j: (i, j)),
        out_shape=jax.ShapeDtypeStruct((t, n_exp), BF16),
        scratch_shapes=[pltpu.VMEM((te, tt), BF16)],
        compiler_params=_params("parallel", "arbitrary"),
        name="peer_act",
    )(u, xt, *part_args)


def _peer_ffn(h, layer, norm_g, w_q, sub_keys, u, v, split_at=None):
    ne = 4
    xn, xn_t = _rmsnorm(h, norm_g, with_transpose=True)
    pq = _matmul(xn, w_q, layer, out_dtype=BF16, name="peer_query")
    scores = _peer_scores(pq, sub_keys, layer)
    m1x, e1x, rank2, e2 = _peer_select(scores, ROW_BLOCK)
    act = _peer_act(xn_t, u, layer, m1x, e1x, rank2, e2, ne)
    down = functools.partial(_matmul, act, v, layer, mode="residual", aux=h, tm=1024, tn=1024, tk=2048)
    if split_at is None:
        return down(name="peer_down")
    return (down(row0=0, m=split_at, name="peer_down_head"),
            down(row0=split_at, m=h.shape[0] - split_at, name="peer_down_tail"))


def kernel(x_prompt, x_sample, cache_a_k, cache_a_v, state_b_pool, cache_c_k, cache_c_v, norm_mix, norm_ffn, ab_w_in, ab_q_gain, ab_k_gain, ab_rel_bias, ab_pool_w, ab_pool_scale, ab_w_out, c_w_in, c_w_out, peer_w_q, peer_sub_keys, peer_u, peer_v):
    bp, sp, d = x_prompt.shape
    bs, ls, _ = x_sample.shape
    assert bp == 1
    depth = norm_mix.shape[0]
    past_len = cache_c_k.shape[2]
    a_heads = ab_rel_bias.shape[1]
    a_width = a_heads * HEAD_DIM
    b_width = ab_pool_scale.shape[1]
    c_width = c_w_out.shape[1]
    win_p = min(A_PAST_CHUNKS * CHUNK, sp)
    n_s = bs * ls

    h = jnp.concatenate([x_prompt.reshape(sp, d), x_sample.reshape(n_s, d)], axis=0)

    ab_w_in_b, ab_w_out_b = ab_w_in.astype(BF16), ab_w_out.astype(BF16)
    c_w_in_b, c_w_out_b = c_w_in.astype(BF16), c_w_out.astype(BF16)
    peer_w_q_b, peer_v_b = peer_w_q.astype(BF16), peer_v.astype(BF16)
    sub_keys_b = peer_sub_keys.astype(BF16).reshape(depth, PEER_HEADS * 2, N_KEYS, -1)

    outs = {name: [] for name in ("a_k_p", "a_v_p", "b_p", "c_k_p", "c_v_p",
                                  "a_k_s", "a_v_s", "b_s", "c_k_s", "c_v_s")}
    for layer in range(depth):
        xn = _rmsnorm(h, norm_mix[layer])
        if layer % 2 == 0:
            i = layer // 2
            q = _matmul(xn, ab_w_in_b, i, col0=0, n=a_width, out_dtype=BF16, mode="headnorm",
                        aux=ab_q_gain[i], name="a_q")
            k = _matmul(xn, ab_w_in_b, i, col0=a_width, n=a_width, mode="headnorm", aux=ab_k_gain[i], name="a_k")
            v = _matmul(xn, ab_w_in_b, i, col0=2 * a_width, n=a_width, name="a_v")
            p = _matmul(xn, ab_w_in_b, i, col0=3 * a_width, n=b_width, name="b_in")

            att = jnp.zeros((sp + n_s, a_width), BF16)
            att = _band_prompt(q, k, v, ab_rel_bias[i], sp, att)
            att = _band_sample(q, k, v, cache_a_k, cache_a_v, i, ab_rel_bias[i], sp, ls, past_len, att)

            pool_w = ab_pool_w[i].astype(BF16)
            tr = _tile(sp, 256)
            pool = jnp.zeros((sp + n_s, b_width), BF16)
            pool = _pool(p, None, pool_w, ab_pool_scale[i], pool, row_block0=0, nblk=sp // tr, tr=tr,
                         pos_base=0, pos_step=tr, name="pool_prompt")
            hist_s = jnp.pad(state_b_pool[i], ((0, 0), (POOL_HIST_PAD - POOL_HIST, 0), (0, 0)))
            pool = _pool(p, hist_s, pool_w, ab_pool_scale[i], pool, row_block0=sp // ls, nblk=bs, tr=ls,
                         pos_base=past_len, pos_step=0, name="pool_sample")

            h = _matmul(att, ab_w_out_b, i, x2=pool, mode="residual", aux=h, name="a_out")

            ps = p[sp:].reshape(bs, ls, b_width)
            outs["a_k_p"].append(k[sp - win_p:sp].reshape(bp, win_p, a_heads, HEAD_DIM))
            outs["a_v_p"].append(v[sp - win_p:sp].reshape(bp, win_p, a_heads, HEAD_DIM))
            outs["b_p"].append(p[sp - POOL_HIST:sp].reshape(bp, POOL_HIST, b_width))
            outs["a_k_s"].append(k[sp:].reshape(bs, ls, a_heads, HEAD_DIM))
            outs["a_v_s"].append(v[sp:].reshape(bs, ls, a_heads, HEAD_DIM))
            outs["b_s"].append(jnp.concatenate([state_b_pool[i], ps], axis=1)[:, ls:])
        else:
            j = layer // 2
            c_heads = c_width // HEAD_DIM
            q = _matmul(xn, c_w_in_b, j, col0=0, n=c_width, out_dtype=BF16, name="c_q")
            k_p, k_p3 = _proj_heads(xn, c_w_in_b, j, col0=c_width, n=c_width, row0=0, m=sp, name="c_k_prompt")
            v_p, v_p3 = _proj_heads(xn, c_w_in_b, j, col0=2 * c_width, n=c_width, row0=0, m=sp, name="c_v_prompt")
            k_s, k_s3 = _proj_heads(xn, c_w_in_b, j, col0=c_width, n=c_width, row0=sp, m=n_s, name="c_k_sample")
            v_s, v_s3 = _proj_heads(xn, c_w_in_b, j, col0=2 * c_width, n=c_width, row0=sp, m=n_s, name="c_v_sample")
            att = jnp.zeros((sp + n_s, c_width), BF16)
            att = _sb_prompt(q, k_p, v_p, sp, att)
            att = _sb_sample(q, k_s, v_s, cache_c_k, cache_c_v, j, sp, ls, att)
            h = _matmul(att, c_w_out_b, j, mode="residual", aux=h, name="c_out")
            outs["c_k_p"].append(k_p3.reshape(bp, sp, c_heads, HEAD_DIM))
            outs["c_v_p"].append(v_p3.reshape(bp, sp, c_heads, HEAD_DIM))
            outs["c_k_s"].append(k_s3.reshape(bs, ls, c_heads, HEAD_DIM))
            outs["c_v_s"].append(v_s3.reshape(bs, ls, c_heads, HEAD_DIM))

        h = _peer_ffn(h, layer, norm_ffn[layer], peer_w_q_b, sub_keys_b, peer_u, peer_v_b,
                      split_at=sp if layer == depth - 1 else None)

    h_prompt, h_sample = h
    st = {name: jnp.stack(vals) for name, vals in outs.items()}
    return (h_prompt.reshape(bp, sp, d), h_sample.reshape(bs, ls, d),
            st["a_k_p"], st["a_v_p"], st["b_p"], st["c_k_p"], st["c_v_p"],
            st["a_k_s"], st["a_v_s"], st["b_s"], st["c_k_s"], st["c_v_s"])
```

```python
import functools

import jax
import jax.numpy as jnp
from jax import lax
from jax.experimental import pallas as pl
from jax.experimental.pallas import tpu as pltpu

F32 = jnp.float32
BF16 = jnp.bfloat16

HEAD_DIM = 128
CHUNK = 64
A_PAST_CHUNKS = 8
REL_CLIP = 128
POOL_WINDOWS = (2, 4, 8, 16)
POOL_HIST = max(POOL_WINDOWS) - 1
POOL_HIST_PAD = 16
B_GROUP_WIDTH = 512
PEER_HEADS = 8
N_KEYS = 128
PEER_TOPK = 16
ROW_BLOCK = 8
EPS = 1e-6
NEG_INF = -1e30
ATTN_SCALE = HEAD_DIM ** -0.5
SB_EXIT = -104.0
SB_KEYS = 128
VMEM_LIMIT_BYTES = 56 * 1024 * 1024


def _params(*sem):
    return pltpu.CompilerParams(dimension_semantics=sem, vmem_limit_bytes=VMEM_LIMIT_BYTES)


def _tile(dim, pref):
    return pref if dim % pref == 0 else dim


def _ignore_input(body, index):
    def wrapped(*refs):
        return body(*refs[:index], *refs[index + 1:])
    return wrapped


def _dot(a, b):
    return jnp.dot(a, b, preferred_element_type=F32)


def _dot_nt(a, b):
    return lax.dot_general(a, b, (((1,), (1,)), ((), ())), preferred_element_type=F32)


def _rmsnorm_kernel(x_ref, g_ref, o_ref, *transposed_ref):
    x = x_ref[...]
    ms = jnp.mean(x * x, axis=-1, keepdims=True)
    y = (x * lax.rsqrt(ms + EPS) * g_ref[...]).astype(o_ref.dtype)
    o_ref[...] = y
    for ot_ref in transposed_ref:
        ot_ref[...] = y.T


def _rmsnorm(x, g, with_transpose=False):
    t, d = x.shape
    tr = _tile(t, 256)
    out_specs = [pl.BlockSpec((tr, d), lambda i: (i, 0))]
    out_shape = [jax.ShapeDtypeStruct((t, d), BF16)]
    if with_transpose:
        out_specs.append(pl.BlockSpec((d, tr), lambda i: (0, i)))
        out_shape.append(jax.ShapeDtypeStruct((d, t), BF16))
    outs = pl.pallas_call(
        _rmsnorm_kernel,
        grid=(t // tr,),
        in_specs=[pl.BlockSpec((tr, d), lambda i: (i, 0)),
                  pl.BlockSpec((1, d), lambda i: (0, 0))],
        out_specs=out_specs,
        out_shape=out_shape,
        compiler_params=_params("parallel"),
        name="rmsnorm",
    )(x, g.reshape(1, d))
    return outs if with_transpose else outs[0]


def _mm_kernel(*refs, nk, mode, two_x):
    x_ref, w_ref = refs[0], refs[1]
    refs = refs[2:]
    if two_x:
        x2_ref, refs = refs[0], refs[1:]
    if mode in ("headnorm", "residual"):
        aux_ref, o_ref, scratch = refs[0], refs[1], refs[2:]
    else:
        aux_ref, o_ref, scratch = None, refs[0], refs[1:]

    def finish(acc):
        if mode == "headnorm":
            g = aux_ref[...]
            for c in range(acc.shape[1] // HEAD_DIM):
                sl = slice(c * HEAD_DIM, (c + 1) * HEAD_DIM)
                y = acc[:, sl]
                ms = jnp.mean(y * y, axis=-1, keepdims=True)
                o_ref[:, sl] = (y * lax.rsqrt(ms + EPS) * g).astype(o_ref.dtype)
        elif mode == "residual":
            o_ref[...] = aux_ref[...] + acc
        else:
            o_ref[...] = acc.astype(o_ref.dtype)

    if two_x:
        k1 = x_ref.shape[1]
        finish(_dot(x_ref[...], w_ref[:k1, :]) + _dot(x2_ref[...], w_ref[k1:, :]))
    elif nk == 1:
        finish(_dot(x_ref[...], w_ref[...]))
    else:
        acc_ref = scratch[0]
        k = pl.program_id(2)

        @pl.when(k == 0)
        def _():
            acc_ref[...] = jnp.zeros_like(acc_ref)

        acc_ref[...] += _dot(x_ref[...], w_ref[...])

        @pl.when(k == nk - 1)
        def _():
            finish(acc_ref[...])


def _matmul(x, w, layer, *, x2=None, col0=0, n=None, row0=0, m=None, out_dtype=F32, mode="plain", aux=None,
            tm=1024, tn=512, tk=None, name="matmul"):
    kdim = x.shape[1]
    m = x.shape[0] if m is None else m
    if x2 is not None:
        assert tk is None
        kdim += x2.shape[1]
    n = w.shape[2] if n is None else n
    tm, tn = _tile(m, tm), _tile(n, tn)
    tk = kdim if tk is None else _tile(kdim, tk)
    nk = kdim // tk
    rb, cb = row0 // tm, col0 // tn
    assert row0 % tm == 0 and col0 % tn == 0
    in_specs = [pl.BlockSpec((tm, x.shape[1] if x2 is not None else tk), lambda i, j, k: (i + rb, k)),
                pl.BlockSpec((None, tk, tn), lambda i, j, k: (layer, k, j + cb))]
    args = [x, w]
    if x2 is not None:
        in_specs.append(pl.BlockSpec((tm, x2.shape[1]), lambda i, j, k: (i + rb, 0)))
        args.append(x2)
    if mode == "headnorm":
        in_specs.append(pl.BlockSpec((1, HEAD_DIM), lambda i, j, k: (0, 0)))
        args.append(aux.reshape(1, HEAD_DIM))
    elif mode == "residual":
        in_specs.append(pl.BlockSpec((tm, tn), lambda i, j, k: (i + rb, j)))
        args.append(aux)
    scratch = [pltpu.VMEM((tm, tn), F32)] if nk > 1 else []
    return pl.pallas_call(
        functools.partial(_mm_kernel, nk=nk, mode=mode, two_x=x2 is not None),
        grid=(m // tm, n // tn, nk),
        in_specs=in_specs,
        out_specs=pl.BlockSpec((tm, tn), lambda i, j, k: (i, j)),
        out_shape=jax.ShapeDtypeStruct((m, n), out_dtype),
        scratch_shapes=scratch,
        compiler_params=_params("parallel", "parallel", "arbitrary"),
        name=name,
    )(*args)


def _proj_heads_kernel(x_ref, w_ref, o2_ref, o3_ref):
    acc = _dot(x_ref[...], w_ref[...])
    o2_ref[...] = acc.astype(o2_ref.dtype)
    o3_ref[...] = pltpu.einshape("m(hd)->mhd", acc, h=o3_ref.shape[1])


def _proj_heads(x, w, layer, *, col0, n, row0, m, name):
    kdim = x.shape[1]
    tm, tn = _tile(m, 1024), 8 * HEAD_DIM
    assert row0 % tm == 0 and col0 % tn == 0 and n % tn == 0
    rb, cb = row0 // tm, col0 // tn
    return pl.pallas_call(
        _proj_heads_kernel,
        grid=(m // tm, n // tn),
        in_specs=[pl.BlockSpec((tm, kdim), lambda i, j: (i + rb, 0)),
                  pl.BlockSpec((None, kdim, tn), lambda i, j: (layer, 0, j + cb))],
        out_specs=[pl.BlockSpec((tm, tn), lambda i, j: (i, j)),
                   pl.BlockSpec((tm, tn // HEAD_DIM, HEAD_DIM), lambda i, j: (i, j, 0))],
        out_shape=[jax.ShapeDtypeStruct((m, n), BF16),
                   jax.ShapeDtypeStruct((m, n // HEAD_DIM, HEAD_DIM), F32)],
        compiler_params=_params("parallel", "parallel"),
        name=name,
    )(x, w)


def _band_head(q, ka, kb, va, vb, bias_a, bias_b, past_visible):
    sa = _dot_nt(q, ka.astype(BF16)) * ATTN_SCALE + bias_a
    sb = _dot_nt(q, kb.astype(BF16)) * ATTN_SCALE + bias_b
    if past_visible is not None:
        sa = jnp.where(past_visible, sa, NEG_INF)
    m = jnp.maximum(jnp.max(sa, axis=-1, keepdims=True), jnp.max(sb, axis=-1, keepdims=True))
    pa = jnp.exp(sa - m)
    pb = jnp.exp(sb - m)
    l = jnp.sum(pa, axis=-1, keepdims=True) + jnp.sum(pb, axis=-1, keepdims=True)
    o = _dot(pa.astype(BF16), va.astype(BF16)) + _dot(pb.astype(BF16), vb.astype(BF16))
    return o / l


def _band_prompt_kernel(q_ref, ka_ref, kb_ref, va_ref, vb_ref, bias_ref, o_ref):
    na = ka_ref.shape[0]
    o = _band_head(q_ref[...], ka_ref[...], kb_ref[...], va_ref[...], vb_ref[...],
                   bias_ref[:, :na], bias_ref[:, na:], pl.program_id(1) > 0)
    o_ref[...] = o.astype(o_ref.dtype)


def _band_sample_kernel(q_ref, kn_ref, vn_ref, kc_ref, vc_ref, bias_ref, o_ref):
    win, heads = kc_ref.shape[0], kc_ref.shape[1]
    kc = pltpu.einshape("phd->hpd", kc_ref[...])
    vc = pltpu.einshape("phd->hpd", vc_ref[...])
    for hh in range(heads):
        sl = slice(hh * HEAD_DIM, (hh + 1) * HEAD_DIM)
        o = _band_head(q_ref[:, sl], kc[hh], kn_ref[:, sl], vc[hh], vn_ref[:, sl],
                       bias_ref[hh, :, :win], bias_ref[hh, :, win:], None)
        o_ref[:, sl] = o.astype(o_ref.dtype)


def _band_bias(rel_bias, q_pos, k_pos):
    lq, lk = q_pos.shape[0], k_pos.shape[0]
    d_min = q_pos[0] - k_pos[lk - 1]
    n_diag = lq + lk - 1
    diag = jnp.clip(d_min + jnp.arange(n_diag, dtype=jnp.int32), -REL_CLIP, REL_CLIP) + REL_CLIP
    g = rel_bias.astype(F32)[:, diag]
    row_len = -(-n_diag // HEAD_DIM) * HEAD_DIM
    period = row_len + 1
    u = jnp.pad(g[:, ::-1], ((0, 0), (0, period - n_diag)))
    skew = jnp.tile(u, (1, lq))[:, :lq * row_len].reshape(-1, lq, row_len)
    bias = skew[:, :, lq - 1:lq - 1 + lk]
    qc = q_pos[:, None] // CHUNK
    kc = k_pos[None, :] // CHUNK
    mask = (kc <= qc) & (kc >= qc - A_PAST_CHUNKS)
    return jnp.where(mask[None], bias, NEG_INF)


def _band_prompt(q, k, v, rel_bias, sp, out):
    heads = rel_bias.shape[0]
    tq = A_PAST_CHUNKS * CHUNK
    assert sp % tq == 0
    pos = jnp.arange(tq, dtype=jnp.int32)
    bias = _band_bias(rel_bias, tq + pos, jnp.arange(2 * tq, dtype=jnp.int32))
    prev = lambda h, i: (jnp.maximum(i - 1, 0), h)
    cur = lambda h, i: (i, h)
    blk = lambda im: pl.BlockSpec((tq, HEAD_DIM), im)
    return pl.pallas_call(
        _ignore_input(_band_prompt_kernel, 6),
        grid=(heads, sp // tq),
        in_specs=[blk(cur), blk(prev), blk(cur), blk(prev), blk(cur),
                  pl.BlockSpec((None, tq, 2 * tq), lambda h, i: (h, 0, 0)),
                  pl.BlockSpec(memory_space=pl.ANY)],
        out_specs=blk(cur),
        out_shape=jax.ShapeDtypeStruct(out.shape, out.dtype),
        input_output_aliases={6: 0},
        compiler_params=_params("parallel", "arbitrary"),
        name="band_prompt",
    )(q, k, k, v, v, bias, out)


def _band_sample(q, k, v, cache_k, cache_v, layer, rel_bias, sp, ls, past_len, out):
    _, bs, win, heads, _ = cache_k.shape
    width = heads * HEAD_DIM
    assert sp % ls == 0
    row0 = sp // ls
    q_pos = past_len + jnp.arange(ls, dtype=jnp.int32)
    k_pos = past_len - win + jnp.arange(win + ls, dtype=jnp.int32)
    bias = _band_bias(rel_bias, q_pos, k_pos)
    blk_new = pl.BlockSpec((ls, width), lambda b: (row0 + b, 0))
    blk_old = pl.BlockSpec((None, None, win, heads, HEAD_DIM), lambda b: (layer, b, 0, 0, 0))
    return pl.pallas_call(
        _ignore_input(_band_sample_kernel, 6),
        grid=(bs,),
        in_specs=[blk_new, blk_new, blk_new, blk_old, blk_old,
                  pl.BlockSpec((heads, ls, win + ls), lambda b: (0, 0, 0)),
                  pl.BlockSpec(memory_space=pl.ANY)],
        out_specs=blk_new,
        out_shape=jax.ShapeDtypeStruct(out.shape, out.dtype),
        input_output_aliases={6: 0},
        compiler_params=_params("parallel"),
        name="band_sample",
    )(q, k, v, cache_k, cache_v, bias, out)


def _pool_kernel(cur_ref, hist_ref, w_ref, sc_ref, o_ref, ext_ref, *, pos_base, pos_step, first_has_no_past):
    tr = cur_ref.shape[0]
    hist = hist_ref[...]
    if first_has_no_past:
        hist = jnp.where(pl.program_id(0) > 0, hist, 0.0)
    ext_ref[0:POOL_HIST_PAD, :] = hist
    ext_ref[POOL_HIST_PAD:POOL_HIST_PAD + tr, :] = cur_ref[...]
    pos = pos_base + pl.program_id(0) * pos_step + lax.broadcasted_iota(jnp.int32, (tr, 1), 0)
    for g, w in enumerate(POOL_WINDOWS):
        sl = slice(g * B_GROUP_WIDTH, (g + 1) * B_GROUP_WIDTH)
        cur = cur_ref[:, sl]
        tot = cur
        for j in range(1, w):
            tot = tot + ext_ref[POOL_HIST_PAD - j:POOL_HIST_PAD - j + tr, sl]
        cnt = jnp.minimum(pos + 1, w).astype(F32)
        d = tot / cnt - cur
        y = _dot(d.astype(BF16), w_ref[g]) * sc_ref[:, sl]
        o_ref[:, sl] = y.astype(o_ref.dtype)


def _pool(p, hist, pool_w, pool_scale, out, *, row_block0, nblk, tr, pos_base, pos_step, name):
    width = p.shape[1]
    rows = pl.BlockSpec((tr, width), lambda i: (row_block0 + i, 0))
    body = functools.partial(_pool_kernel, pos_base=pos_base, pos_step=pos_step, first_has_no_past=hist is None)
    if hist is None:
        assert row_block0 == 0 and tr % POOL_HIST_PAD == 0
        per = tr // POOL_HIST_PAD
        hist_spec = pl.BlockSpec((POOL_HIST_PAD, width), lambda i: (jnp.maximum(i * per - 1, 0), 0))
        hist = p
    else:
        hist_spec = pl.BlockSpec((None, POOL_HIST_PAD, width), lambda i: (i, 0, 0))
    return pl.pallas_call(
        _ignore_input(body, 4),
        grid=(nblk,),
        in_specs=[rows, hist_spec,
                  pl.BlockSpec(pool_w.shape, lambda i: (0, 0, 0)),
                  pl.BlockSpec((1, width), lambda i: (0, 0)),
                  pl.BlockSpec(memory_space=pl.ANY)],
        out_specs=rows,
        out_shape=jax.ShapeDtypeStruct(out.shape, out.dtype),
        input_output_aliases={4: 0},
        scratch_shapes=[pltpu.VMEM((POOL_HIST_PAD + tr, width), F32)],
        compiler_params=_params("parallel"),
        name=name,
    )(p, hist, pool_w, pool_scale.reshape(1, width), out)


def _sb_steps(qs, kvs, tri, carries, accs, visibles):
    nk = kvs[0][0].shape[0]
    zs = [_dot_nt(q, k) * ATTN_SCALE for q, (k, _) in zip(qs, kvs)]
    log_keeps = [jnp.where(vis, -(jnp.maximum(z, 0.0) + jnp.log(1.0 + jnp.exp(-jnp.abs(z)))), 0.0)
                 for z, vis in zip(zs, visibles)]
    his = [lk.astype(BF16) for lk in log_keeps]
    los = [(lk - hi.astype(F32)).astype(BF16) for lk, hi in zip(log_keeps, his)]
    sums = [_dot(hi, tri) + _dot(lo, tri) for hi, lo in zip(his, los)]
    weights = [jnp.where(vis, jnp.exp(z + lk + s[:, :nk] + carry[:, :nk]), 0.0).astype(BF16)
               for z, lk, s, carry, vis in zip(zs, log_keeps, sums, carries, visibles)]
    new_accs = [acc + _dot(a, v) for acc, a, (_, v) in zip(accs, weights, kvs)]
    new_carries = [carry + s[:, nk:] for carry, s in zip(carries, sums)]
    return new_carries, new_accs


def _sb_sweep(chains, load_kv, tri, carries, accs):
    rows = chains[0][0].shape[0]
    lane = lax.broadcasted_iota(jnp.int32, (rows, SB_KEYS), 1)
    beyond = jnp.int32(2 ** 30)

    def cond(st):
        return st[1] == 0

    def body(st):
        n, _, carries, accs = st
        kbs = [kb0 - n for _, kb0, _ in chains]
        starts = [pl.multiple_of(jnp.maximum(kb, 0) * SB_KEYS, SB_KEYS) for kb in kbs]
        visibles = [(jnp.where(kb >= 0, start, beyond) + lane) < qpos
                    for (_, _, qpos), kb, start in zip(chains, kbs, starts)]
        carries, accs = _sb_steps([q for q, _, _ in chains], load_kv(starts), tri, carries, accs, visibles)
        pending = jnp.full((rows, HEAD_DIM), -jnp.inf, F32)
        for kb, carry in zip(kbs, carries):
            pending = jnp.maximum(pending, jnp.where(kb > 0, carry, -jnp.inf))
        done = (jnp.max(pending) < SB_EXIT).astype(jnp.int32)
        return n + 1, done, carries, accs

    return lax.while_loop(cond, body, (jnp.int32(0), jnp.int32(0), list(carries), list(accs)))[3]


def _sb_prompt_kernel(q_ref, k_ref, v_ref, tri_ref, o_ref, *, tq):
    n_chains = q_ref.shape[0] // tq
    row0 = pl.program_id(1) * q_ref.shape[0]
    row = lax.broadcasted_iota(jnp.int32, (tq, SB_KEYS), 0)

    def load_kv(starts):
        return [(k_ref[pl.ds(s, SB_KEYS), :].astype(BF16), v_ref[pl.ds(s, SB_KEYS), :].astype(BF16))
                for s in starts]

    chains = [(q_ref[c * tq:(c + 1) * tq, :], (row0 + (c + 1) * tq) // SB_KEYS - 1, row0 + c * tq + row)
              for c in range(n_chains)]
    zeros = [jnp.zeros((tq, HEAD_DIM), F32)] * n_chains
    accs = _sb_sweep(chains, load_kv, tri_ref[...], zeros, zeros)
    for c in range(n_chains):
        o_ref[c * tq:(c + 1) * tq, :] = accs[c].astype(o_ref.dtype)


def _sb_sample_kernel(q_ref, kn_ref, vn_ref, kc_ref, vc_ref, trin_ref, tri_ref, o_ref):
    ls = q_ref.shape[0]
    past, heads = kc_ref.shape[0], kc_ref.shape[1]
    causal = lax.broadcasted_iota(jnp.int32, (ls, ls), 1) < lax.broadcasted_iota(jnp.int32, (ls, ls), 0)
    all_rows = jnp.full((ls, SB_KEYS), 2 ** 30 - SB_KEYS, jnp.int32)
    zeros = jnp.zeros((ls, HEAD_DIM), F32)
    head_cols = [slice(hh * HEAD_DIM, (hh + 1) * HEAD_DIM) for hh in range(heads)]
    chains = [(q_ref[:, sl], past // SB_KEYS - 1, all_rows) for sl in head_cols]
    carries, accs = _sb_steps([q for q, _, _ in chains],
                              [(kn_ref[:, sl].astype(BF16), vn_ref[:, sl].astype(BF16)) for sl in head_cols],
                              trin_ref[...], [zeros] * heads, [zeros] * heads, [causal] * heads)

    def load_kv(starts):
        k = pltpu.einshape("phd->hpd", kc_ref[pl.ds(starts[0], SB_KEYS), :, :])
        v = pltpu.einshape("phd->hpd", vc_ref[pl.ds(starts[0], SB_KEYS), :, :])
        return [(k[hh].astype(BF16), v[hh].astype(BF16)) for hh in range(heads)]

    accs = _sb_sweep(chains, load_kv, tri_ref[...], carries, accs)
    for hh in range(heads):
        o_ref[:, hh * HEAD_DIM:(hh + 1) * HEAD_DIM] = accs[hh].astype(o_ref.dtype)


def _sb_tri(nk):
    j = jnp.arange(nk)[:, None]
    s = jnp.arange(nk + HEAD_DIM)[None, :]
    return ((s >= nk) | (j > s)).astype(BF16)


def _sb_prompt(q, k, v, sp, out):
    heads = q.shape[1] // HEAD_DIM
    tq = SB_KEYS
    tb = _tile(sp, 8 * tq)
    assert sp % SB_KEYS == 0 and tb % tq == 0
    rows = pl.BlockSpec((tb, HEAD_DIM), lambda h, i: (i, h))
    return pl.pallas_call(
        _ignore_input(functools.partial(_sb_prompt_kernel, tq=tq), 4),
        grid=(heads, sp // tb),
        in_specs=[rows,
                  pl.BlockSpec((sp, HEAD_DIM), lambda h, i: (0, h)),
                  pl.BlockSpec((sp, HEAD_DIM), lambda h, i: (0, h)),
                  pl.BlockSpec((SB_KEYS, SB_KEYS + HEAD_DIM), lambda h, i: (0, 0)),
                  pl.BlockSpec(memory_space=pl.ANY)],
        out_specs=rows,
        out_shape=jax.ShapeDtypeStruct(out.shape, out.dtype),
        input_output_aliases={4: 0},
        compiler_params=_params("parallel", "arbitrary"),
        name="stickbreak_prompt",
    )(q, k, v, _sb_tri(SB_KEYS), out)


def _sb_sample(q, k, v, cache_k, cache_v, layer, sp, ls, out):
    n_layers, bs, past, heads, _ = cache_k.shape
    hg = 8
    assert past % SB_KEYS == 0 and past >= SB_KEYS and sp % ls == 0 and heads % hg == 0
    row0 = sp // ls
    gw = hg * HEAD_DIM
    blk_new = pl.BlockSpec((ls, gw), lambda b, g: (row0 + b, g))
    blk_kv = pl.BlockSpec((ls, gw), lambda b, g: (b, g))
    blk_old = pl.BlockSpec((None, None, past, None, hg, HEAD_DIM), lambda b, g: (layer, b, 0, g, 0, 0))
    grouped = (n_layers, bs, past, heads // hg, hg, HEAD_DIM)
    return pl.pallas_call(
        _ignore_input(_sb_sample_kernel, 7),
        grid=(bs, heads // hg),
        in_specs=[blk_new, blk_kv, blk_kv, blk_old, blk_old,
                  pl.BlockSpec((ls, ls + HEAD_DIM), lambda b, g: (0, 0)),
                  pl.BlockSpec((SB_KEYS, SB_KEYS + HEAD_DIM), lambda b, g: (0, 0)),
                  pl.BlockSpec(memory_space=pl.ANY)],
        out_specs=blk_new,
        out_shape=jax.ShapeDtypeStruct(out.shape, out.dtype),
        input_output_aliases={7: 0},
        compiler_params=_params("parallel", "arbitrary"),
        name="stickbreak_sample",
    )(q, k, v, cache_k.reshape(grouped), cache_v.reshape(grouped), _sb_tri(ls), _sb_tri(SB_KEYS), out)


def _peer_scores_kernel(x_ref, w_ref, key_ref, o_ref):
    half = key_ref.shape[2]
    pq = _dot(x_ref[...], w_ref[...]).astype(BF16)
    for c in range(key_ref.shape[0]):
        o_ref[c] = _dot_nt(key_ref[c], pq[:, c * half:(c + 1) * half])


def _peer_scores(xn, w_q, sub_keys, layer):
    t, d = xn.shape
    _, nhc, nkeys, half = sub_keys.shape
    tm = _tile(t, 1024)
    per = 4
    assert nhc % per == 0
    return pl.pallas_call(
        _peer_scores_kernel,
        grid=(t // tm, nhc // per),
        in_specs=[pl.BlockSpec((tm, d), lambda i, j: (i, 0)),
                  pl.BlockSpec((None, d, per * half), lambda i, j: (layer, 0, j)),
                  pl.BlockSpec((None, per, nkeys, half), lambda i, j: (layer, j, 0, 0))],
        out_specs=pl.BlockSpec((per, nkeys, tm), lambda i, j: (j, 0, i)),
        out_shape=jax.ShapeDtypeStruct((nhc, nkeys, t), F32),
        compiler_params=_params("parallel", "parallel"),
        name="peer_scores",
    )(xn, w_q, sub_keys)


def _extract_top(s, k, tie_safe):
    rows = lax.broadcasted_iota(jnp.int32, s.shape, 0)
    rank = jnp.full(s.shape, float(k), F32)
    vals = []
    for r in range(k):
        m = jnp.max(s, axis=0, keepdims=True)
        vals.append(m)
        hit = s == m
        if tie_safe:
            hit = rows == jnp.min(jnp.where(hit, rows, s.shape[0]), axis=0, keepdims=True)
        rank = jnp.where(hit, float(r), rank)
        s = jnp.where(hit, -jnp.inf, s)
    return vals, rank, s


def _peer_select_kernel(s_ref, m1x_ref, e1x_ref, rank2_ref, e2_ref, *, ne):
    def select(h, tie_safe):
        s1 = s_ref[2 * h]
        s2 = s_ref[2 * h + 1]
        a, rank1, _ = _extract_top(s1, PEER_TOPK, tie_safe)
        b, rank2, _ = _extract_top(s2, PEER_TOPK, tie_safe)
        b = jnp.concatenate(b, axis=0)
        sub = lax.broadcasted_iota(jnp.int32, (8, b.shape[1]), 0)
        cand = [a[0] + b, a[1] + b[:8]]
        for i in range(2, 8):
            cand.append(jnp.where(sub < PEER_TOPK // (i + 1), a[i] + b[:8], -jnp.inf))
        cand.append(jnp.concatenate(a[8:], axis=0) + b[0:1])
        cand = jnp.concatenate(cand, axis=0)
        g, _, left = _extract_top(cand, PEER_TOPK, tie_safe)
        z = jnp.ones_like(g[0])
        for gk in g[1:]:
            z = z + jnp.exp(gk - g[0])
        took = (left != cand).astype(F32)
        taken = [jnp.sum(took[0:16], axis=0, keepdims=True), jnp.sum(took[16:24], axis=0, keepdims=True)]
        taken += [jnp.sum(took[8 * i + 8:8 * i + 16], axis=0, keepdims=True) for i in range(2, 8)]
        taken += [took[72 + i:73 + i] for i in range(8)]
        m1 = jnp.zeros_like(s1)
        for i in range(PEER_TOPK):
            m1 = jnp.where(rank1 == float(i), taken[i], m1)
        e1 = jnp.exp(s1 - a[0]) / z
        rank2_ref[h] = rank2.astype(rank2_ref.dtype)
        e2_ref[h] = jnp.exp(s2 - b[0:1]).astype(e2_ref.dtype)
        row = pl.multiple_of(h * ne, ne)
        for j in range(N_KEYS // ne):
            m1x_ref[j, pl.ds(row, ne), :] = m1[j * ne:(j + 1) * ne]
            e1x_ref[j, pl.ds(row, ne), :] = e1[j * ne:(j + 1) * ne]
        removed = (jnp.sum((rank1 < PEER_TOPK).astype(F32), axis=0, keepdims=True)
                   + jnp.sum((rank2 < PEER_TOPK).astype(F32), axis=0, keepdims=True)
                   + jnp.sum(took, axis=0, keepdims=True))
        return jnp.max(removed) - 3.0 * PEER_TOPK

    def one_head(h, _):
        extra = select(h, tie_safe=False)

        @pl.when(extra > 0.0)
        def _():
            select(h, tie_safe=True)

        return 0

    lax.fori_loop(0, PEER_HEADS, one_head, 0)


def _peer_select(scores, ne):
    nhc, nkeys, t = scores.shape
    tt = _tile(t, 512)
    x_shape = jax.ShapeDtypeStruct((nkeys // ne, PEER_HEADS * ne, t), F32)
    x_spec = pl.BlockSpec((nkeys // ne, PEER_HEADS * ne, tt), lambda i: (0, 0, i))
    y_shape = jax.ShapeDtypeStruct((PEER_HEADS, nkeys, t), BF16)
    y_spec = pl.BlockSpec((PEER_HEADS, nkeys, tt), lambda i: (0, 0, i))
    return pl.pallas_call(
        functools.partial(_peer_select_kernel, ne=ne),
        grid=(t // tt,),
        in_specs=[pl.BlockSpec((nhc, nkeys, tt), lambda i: (0, 0, i))],
        out_specs=[x_spec, x_spec, y_spec, y_spec],
        out_shape=[x_shape, x_shape, y_shape, y_shape],
        compiler_params=_params("parallel"),
        name="peer_select",
    )(scores)


def _peer_act_kernel(*refs, ne, n_parts):
    u_ref, xt_ref = refs[0], refs[1]
    part_refs = refs[2:2 + 4 * n_parts]
    o_ref, gate_ref = refs[2 + 4 * n_parts], refs[3 + 4 * n_parts]
    first_row = (pl.program_id(1) % (ROW_BLOCK // ne)) * ne
    for part in range(n_parts):
        m1x_ref, e1x_ref, rank2_ref, e2_ref = part_refs[4 * part:4 * part + 4]
        tp = rank2_ref.shape[2]
        for e in range(ne):
            gate = jnp.zeros((N_KEYS, tp), BF16)
            for h in range(PEER_HEADS):
                r = pl.ds(h * ROW_BLOCK + first_row + e, 1)
                hit = rank2_ref[h] < m1x_ref[r, :].astype(BF16)
                w = e2_ref[h] * e1x_ref[r, :].astype(BF16)
                gate = gate + jnp.where(hit, w, jnp.zeros_like(w))
            gate_ref[e * N_KEYS:(e + 1) * N_KEYS, part * tp:(part + 1) * tp] = gate
    xt = xt_ref[...]
    for p in range(ne // 2):
        rows = slice(2 * p * N_KEYS, 2 * (p + 1) * N_KEYS)
        ht = _dot(u_ref[rows, :].astype(BF16), xt)
        gelu = 0.5 * ht * (1.0 + lax.erf(ht * (0.5 ** 0.5)))
        act_t = gate_ref[rows, :] * gelu.astype(BF16)
        o_ref[:, rows] = act_t.T


def _peer_act(xt, u, layer, m1x, e1x, rank2, e2, ne):
    d, t = xt.shape
    n_exp = u.shape[1]
    tp = _tile(t, 512)
    n_parts = 2 if t % (2 * tp) == 0 else 1
    tt = n_parts * tp
    te = ne * N_KEYS
    once = pl.Buffered(1)
    part_specs, part_args = [], []
    for part in range(n_parts):
        row_spec = pl.BlockSpec((None, PEER_HEADS * ROW_BLOCK, tp),
                                lambda i, j, part=part: (j // (ROW_BLOCK // ne), 0, n_parts * i + part))
        tile_spec = pl.BlockSpec((PEER_HEADS, N_KEYS, tp), lambda i, j, part=part: (0, 0, n_parts * i + part),
                                 pipeline_mode=once)
        part_specs += [row_spec, row_spec, tile_spec, tile_spec]
        part_args += [m1x, e1x, rank2, e2]
    return pl.pallas_call(
        functools.partial(_peer_act_kernel, ne=ne, n_parts=n_parts),
        grid=(t // tt, n_exp // te),
        in_specs=[pl.BlockSpec((None, te, d), lambda i, j: (layer, j, 0)),
                  pl.BlockSpec((d, tt), lambda i, j: (0, i), pipeline_mode=once)] + part_specs,
        out_specs=pl.BlockSpec((tt, te), lambda i, j: (i, j)),
        out_shape=jax.ShapeDtypeStruct((t, n_exp), BF16),
        scratch_shapes=[pltpu.VMEM((te, tt), BF16)],
        compiler_params=_params("parallel", "arbitrary"),
        name="peer_act",
    )(u, xt, *part_args)


def _peer_ffn(h, layer, norm_g, w_q, sub_keys, u, v, split_at=None):
    ne = 4
    xn, xn_t = _rmsnorm(h, norm_g, with_transpose=True)
    scores = _peer_scores(xn, w_q, sub_keys, layer)
    m1x, e1x, rank2, e2 = _peer_select(scores, ROW_BLOCK)
    act = _peer_act(xn_t, u, layer, m1x, e1x, rank2, e2, ne)
    down = functools.partial(_matmul, act, v, layer, mode="residual", aux=h, tm=1024, tn=1024, tk=2048)
    if split_at is None:
        return down(name="peer_down")
    return (down(row0=0, m=split_at, name="peer_down_head"),
            down(row0=split_at, m=h.shape[0] - split_at, name="peer_down_tail"))


def kernel(x_prompt, x_sample, cache_a_k, cache_a_v, state_b_pool, cache_c_k, cache_c_v, norm_mix, norm_ffn, ab_w_in, ab_q_gain, ab_k_gain, ab_rel_bias, ab_pool_w, ab_pool_scale, ab_w_out, c_w_in, c_w_out, peer_w_q, peer_sub_keys, peer_u, peer_v):
    bp, sp, d = x_prompt.shape
    bs, ls, _ = x_sample.shape
    assert bp == 1
    depth = norm_mix.shape[0]
    past_len = cache_c_k.shape[2]
    a_heads = ab_rel_bias.shape[1]
    a_width = a_heads * HEAD_DIM
    b_width = ab_pool_scale.shape[1]
    c_width = c_w_out.shape[1]
    win_p = min(A_PAST_CHUNKS * CHUNK, sp)
    n_s = bs * ls

    h = jnp.concatenate([x_prompt.reshape(sp, d), x_sample.reshape(n_s, d)], axis=0)

    ab_w_in_b, ab_w_out_b = ab_w_in.astype(BF16), ab_w_out.astype(BF16)
    c_w_in_b, c_w_out_b = c_w_in.astype(BF16), c_w_out.astype(BF16)
    peer_w_q_b, peer_v_b = peer_w_q.astype(BF16), peer_v.astype(BF16)
    sub_keys_b = peer_sub_keys.astype(BF16).reshape(depth, PEER_HEADS * 2, N_KEYS, -1)

    outs = {name: [] for name in ("a_k_p", "a_v_p", "b_p", "c_k_p", "c_v_p",
                                  "a_k_s", "a_v_s", "b_s", "c_k_s", "c_v_s")}
    for layer in range(depth):
        xn = _rmsnorm(h, norm_mix[layer])
        if layer % 2 == 0:
            i = layer // 2
            q = _matmul(xn, ab_w_in_b, i, col0=0, n=a_width, out_dtype=BF16, mode="headnorm",
                        aux=ab_q_gain[i], name="a_q")
            k = _matmul(xn, ab_w_in_b, i, col0=a_width, n=a_width, mode="headnorm", aux=ab_k_gain[i], name="a_k")
            v = _matmul(xn, ab_w_in_b, i, col0=2 * a_width, n=a_width, name="a_v")
            p = _matmul(xn, ab_w_in_b, i, col0=3 * a_width, n=b_width, name="b_in")

            att = jnp.zeros((sp + n_s, a_width), BF16)
            att = _band_prompt(q, k, v, ab_rel_bias[i], sp, att)
            att = _band_sample(q, k, v, cache_a_k, cache_a_v, i, ab_rel_bias[i], sp, ls, past_len, att)

            pool_w = ab_pool_w[i].astype(BF16)
            tr = _tile(sp, 256)
            pool = jnp.zeros((sp + n_s, b_width), BF16)
            pool = _pool(p, None, pool_w, ab_pool_scale[i], pool, row_block0=0, nblk=sp // tr, tr=tr,
                         pos_base=0, pos_step=tr, name="pool_prompt")
            hist_s = jnp.pad(state_b_pool[i], ((0, 0), (POOL_HIST_PAD - POOL_HIST, 0), (0, 0)))
            pool = _pool(p, hist_s, pool_w, ab_pool_scale[i], pool, row_block0=sp // ls, nblk=bs, tr=ls,
                         pos_base=past_len, pos_step=0, name="pool_sample")

            h = _matmul(att, ab_w_out_b, i, x2=pool, mode="residual", aux=h, name="a_out")

            ps = p[sp:].reshape(bs, ls, b_width)
            outs["a_k_p"].append(k[sp - win_p:sp].reshape(bp, win_p, a_heads, HEAD_DIM))
            outs["a_v_p"].append(v[sp - win_p:sp].reshape(bp, win_p, a_heads, HEAD_DIM))
            outs["b_p"].append(p[sp - POOL_HIST:sp].reshape(bp, POOL_HIST, b_width))
            outs["a_k_s"].append(k[sp:].reshape(bs, ls, a_heads, HEAD_DIM))
            outs["a_v_s"].append(v[sp:].reshape(bs, ls, a_heads, HEAD_DIM))
            outs["b_s"].append(jnp.concatenate([state_b_pool[i], ps], axis=1)[:, ls:])
        else:
            j = layer // 2
            c_heads = c_width // HEAD_DIM
            q = _matmul(xn, c_w_in_b, j, col0=0, n=c_width, out_dtype=BF16, name="c_q")
            k_p, k_p3 = _proj_heads(xn, c_w_in_b, j, col0=c_width, n=c_width, row0=0, m=sp, name="c_k_prompt")
            v_p, v_p3 = _proj_heads(xn, c_w_in_b, j, col0=2 * c_width, n=c_width, row0=0, m=sp, name="c_v_prompt")
            k_s, k_s3 = _proj_heads(xn, c_w_in_b, j, col0=c_width, n=c_width, row0=sp, m=n_s, name="c_k_sample")
            v_s, v_s3 = _proj_heads(xn, c_w_in_b, j, col0=2 * c_width, n=c_width, row0=sp, m=n_s, name="c_v_sample")
            att = jnp.zeros((sp + n_s, c_width), BF16)
            att = _sb_prompt(q, k_p, v_p, sp, att)
            att = _sb_sample(q, k_s, v_s, cache_c_k, cache_c_v, j, sp, ls, att)
            h = _matmul(att, c_w_out_b, j, mode="residual", aux=h, name="c_out")
            outs["c_k_p"].append(k_p3.reshape(bp, sp, c_heads, HEAD_DIM))
            outs["c_v_p"].append(v_p3.reshape(bp, sp, c_heads, HEAD_DIM))
            outs["c_k_s"].append(k_s3.reshape(bs, ls, c_heads, HEAD_DIM))
            outs["c_v_s"].append(v_s3.reshape(bs, ls, c_heads, HEAD_DIM))

        h = _peer_ffn(h, layer, norm_ffn[layer], peer_w_q_b, sub_keys_b, peer_u, peer_v_b,
                      split_at=sp if layer == depth - 1 else None)

    h_prompt, h_sample = h
    st = {name: jnp.stack(vals) for name, vals in outs.items()}
    return (h_prompt.reshape(bp, sp, d), h_sample.reshape(bs, ls, d),
            st["a_k_p"], st["a_v_p"], st["b_p"], st["c_k_p"], st["c_v_p"],
            st["a_k_s"], st["a_v_s"], st["b_s"], st["c_k_s"], st["c_v_s"])
```

```python
import functools

import jax
import jax.numpy as jnp
from jax import lax
from jax.experimental import pallas as pl
from jax.experimental.pallas import tpu as pltpu

F32 = jnp.float32
BF16 = jnp.bfloat16

HEAD_DIM = 128
CHUNK = 64
A_PAST_CHUNKS = 8
REL_CLIP = 128
POOL_WINDOWS = (2, 4, 8, 16)
POOL_HIST = max(POOL_WINDOWS) - 1
POOL_HIST_PAD = 16
B_GROUP_WIDTH = 512
PEER_HEADS = 8
N_KEYS = 128
PEER_TOPK = 16
ROW_BLOCK = 8
EPS = 1e-6
NEG_INF = -1e30
ATTN_SCALE = HEAD_DIM ** -0.5
SB_EXIT = -104.0
SB_KEYS = 128
VMEM_LIMIT_BYTES = 56 * 1024 * 1024


def _params(*sem):
    return pltpu.CompilerParams(dimension_semantics=sem, vmem_limit_bytes=VMEM_LIMIT_BYTES)


def _tile(dim, pref):
    return pref if dim % pref == 0 else dim


def _ignore_input(body, index):
    def wrapped(*refs):
        return body(*refs[:index], *refs[index + 1:])
    return wrapped


def _dot(a, b):
    return jnp.dot(a, b, preferred_element_type=F32)


def _dot_nt(a, b):
    return lax.dot_general(a, b, (((1,), (1,)), ((), ())), preferred_element_type=F32)


def _rmsnorm_kernel(x_ref, g_ref, o_ref, *transposed_ref):
    x = x_ref[...]
    ms = jnp.mean(x * x, axis=-1, keepdims=True)
    y = (x * lax.rsqrt(ms + EPS) * g_ref[...]).astype(o_ref.dtype)
    o_ref[...] = y
    for ot_ref in transposed_ref:
        ot_ref[...] = y.T


def _rmsnorm(x, g, with_transpose=False):
    t, d = x.shape
    tr = _tile(t, 256)
    out_specs = [pl.BlockSpec((tr, d), lambda i: (i, 0))]
    out_shape = [jax.ShapeDtypeStruct((t, d), BF16)]
    if with_transpose:
        out_specs.append(pl.BlockSpec((d, tr), lambda i: (0, i)))
        out_shape.append(jax.ShapeDtypeStruct((d, t), BF16))
    outs = pl.pallas_call(
        _rmsnorm_kernel,
        grid=(t // tr,),
        in_specs=[pl.BlockSpec((tr, d), lambda i: (i, 0)),
                  pl.BlockSpec((1, d), lambda i: (0, 0))],
        out_specs=out_specs,
        out_shape=out_shape,
        compiler_params=_params("parallel"),
        name="rmsnorm",
    )(x, g.reshape(1, d))
    return outs if with_transpose else outs[0]


def _mm_kernel(*refs, nk, mode, two_x):
    x_ref, w_ref = refs[0], refs[1]
    refs = refs[2:]
    if two_x:
        x2_ref, refs = refs[0], refs[1:]
    if mode in ("headnorm", "residual"):
        aux_ref, o_ref, scratch = refs[0], refs[1], refs[2:]
    else:
        aux_ref, o_ref, scratch = None, refs[0], refs[1:]

    def finish(acc):
        if mode == "headnorm":
            g = aux_ref[...]
            for c in range(acc.shape[1] // HEAD_DIM):
                sl = slice(c * HEAD_DIM, (c + 1) * HEAD_DIM)
                y = acc[:, sl]
                ms = jnp.mean(y * y, axis=-1, keepdims=True)
                o_ref[:, sl] = (y * lax.rsqrt(ms + EPS) * g).astype(o_ref.dtype)
        elif mode == "residual":
            o_ref[...] = aux_ref[...] + acc
        else:
            o_ref[...] = acc.astype(o_ref.dtype)

    if two_x:
        k1 = x_ref.shape[1]
        finish(_dot(x_ref[...], w_ref[:k1, :]) + _dot(x2_ref[...], w_ref[k1:, :]))
    elif nk == 1:
        finish(_dot(x_ref[...], w_ref[...]))
    else:
        acc_ref = scratch[0]
        k = pl.program_id(2)

        @pl.when(k == 0)
        def _():
            acc_ref[...] = jnp.zeros_like(acc_ref)

        acc_ref[...] += _dot(x_ref[...], w_ref[...])

        @pl.when(k == nk - 1)
        def _():
            finish(acc_ref[...])


def _matmul(x, w, layer, *, x2=None, col0=0, n=None, row0=0, m=None, out_dtype=F32, mode="plain", aux=None,
            tm=1024, tn=512, tk=None, name="matmul"):
    kdim = x.shape[1]
    m = x.shape[0] if m is None else m
    if x2 is not None:
        assert tk is None
        kdim += x2.shape[1]
    n = w.shape[2] if n is None else n
    tm, tn = _tile(m, tm), _tile(n, tn)
    tk = kdim if tk is None else _tile(kdim, tk)
    nk = kdim // tk
    rb, cb = row0 // tm, col0 // tn
    assert row0 % tm == 0 and col0 % tn == 0
    in_specs = [pl.BlockSpec((tm, x.shape[1] if x2 is not None else tk), lambda i, j, k: (i + rb, k)),
                pl.BlockSpec((None, tk, tn), lambda i, j, k: (layer, k, j + cb))]
    args = [x, w]
    if x2 is not None:
        in_specs.append(pl.BlockSpec((tm, x2.shape[1]), lambda i, j, k: (i + rb, 0)))
        args.append(x2)
    if mode == "headnorm":
        in_specs.append(pl.BlockSpec((1, HEAD_DIM), lambda i, j, k: (0, 0)))
        args.append(aux.reshape(1, HEAD_DIM))
    elif mode == "residual":
        in_specs.append(pl.BlockSpec((tm, tn), lambda i, j, k: (i + rb, j)))
        args.append(aux)
    scratch = [pltpu.VMEM((tm, tn), F32)] if nk > 1 else []
    return pl.pallas_call(
        functools.partial(_mm_kernel, nk=nk, mode=mode, two_x=x2 is not None),
        grid=(m // tm, n // tn, nk),
        in_specs=in_specs,
        out_specs=pl.BlockSpec((tm, tn), lambda i, j, k: (i, j)),
        out_shape=jax.ShapeDtypeStruct((m, n), out_dtype),
        scratch_shapes=scratch,
        compiler_params=_params("parallel", "parallel", "arbitrary"),
        name=name,
    )(*args)


def _proj_heads_kernel(x_ref, w_ref, o2_ref, o3_ref):
    acc = _dot(x_ref[...], w_ref[...])
    o2_ref[...] = acc.astype(o2_ref.dtype)
    o3_ref[...] = pltpu.einshape("m(hd)->mhd", acc, h=o3_ref.shape[1])


def _proj_heads(x, w, layer, *, col0, n, row0, m, name):
    kdim = x.shape[1]
    tm, tn = _tile(m, 1024), 8 * HEAD_DIM
    assert row0 % tm == 0 and col0 % tn == 0 and n % tn == 0
    rb, cb = row0 // tm, col0 // tn
    return pl.pallas_call(
        _proj_heads_kernel,
        grid=(m // tm, n // tn),
        in_specs=[pl.BlockSpec((tm, kdim), lambda i, j: (i + rb, 0)),
                  pl.BlockSpec((None, kdim, tn), lambda i, j: (layer, 0, j + cb))],
        out_specs=[pl.BlockSpec((tm, tn), lambda i, j: (i, j)),
                   pl.BlockSpec((tm, tn // HEAD_DIM, HEAD_DIM), lambda i, j: (i, j, 0))],
        out_shape=[jax.ShapeDtypeStruct((m, n), BF16),
                   jax.ShapeDtypeStruct((m, n // HEAD_DIM, HEAD_DIM), F32)],
        compiler_params=_params("parallel", "parallel"),
        name=name,
    )(x, w)


def _band_head(q, ka, kb, va, vb, bias_a, bias_b, past_visible):
    sa = _dot_nt(q, ka.astype(BF16)) * ATTN_SCALE + bias_a
    sb = _dot_nt(q, kb.astype(BF16)) * ATTN_SCALE + bias_b
    if past_visible is not None:
        sa = jnp.where(past_visible, sa, NEG_INF)
    m = jnp.maximum(jnp.max(sa, axis=-1, keepdims=True), jnp.max(sb, axis=-1, keepdims=True))
    pa = jnp.exp(sa - m)
    pb = jnp.exp(sb - m)
    l = jnp.sum(pa, axis=-1, keepdims=True) + jnp.sum(pb, axis=-1, keepdims=True)
    o = _dot(pa.astype(BF16), va.astype(BF16)) + _dot(pb.astype(BF16), vb.astype(BF16))
    return o / l


def _band_prompt_kernel(q_ref, ka_ref, kb_ref, va_ref, vb_ref, bias_ref, o_ref):
    na = ka_ref.shape[0]
    o = _band_head(q_ref[...], ka_ref[...], kb_ref[...], va_ref[...], vb_ref[...],
                   bias_ref[:, :na], bias_ref[:, na:], pl.program_id(1) > 0)
    o_ref[...] = o.astype(o_ref.dtype)


def _band_sample_kernel(q_ref, kn_ref, vn_ref, kc_ref, vc_ref, bias_ref, o_ref):
    win, heads = kc_ref.shape[0], kc_ref.shape[1]
    kc = pltpu.einshape("phd->hpd", kc_ref[...])
    vc = pltpu.einshape("phd->hpd", vc_ref[...])
    for hh in range(heads):
        sl = slice(hh * HEAD_DIM, (hh + 1) * HEAD_DIM)
        o = _band_head(q_ref[:, sl], kc[hh], kn_ref[:, sl], vc[hh], vn_ref[:, sl],
                       bias_ref[hh, :, :win], bias_ref[hh, :, win:], None)
        o_ref[:, sl] = o.astype(o_ref.dtype)


def _band_bias(rel_bias, q_pos, k_pos):
    lq, lk = q_pos.shape[0], k_pos.shape[0]
    d_min = q_pos[0] - k_pos[lk - 1]
    n_diag = lq + lk - 1
    diag = jnp.clip(d_min + jnp.arange(n_diag, dtype=jnp.int32), -REL_CLIP, REL_CLIP) + REL_CLIP
    g = rel_bias.astype(F32)[:, diag]
    row_len = -(-n_diag // HEAD_DIM) * HEAD_DIM
    period = row_len + 1
    u = jnp.pad(g[:, ::-1], ((0, 0), (0, period - n_diag)))
    skew = jnp.tile(u, (1, lq))[:, :lq * row_len].reshape(-1, lq, row_len)
    bias = skew[:, :, lq - 1:lq - 1 + lk]
    qc = q_pos[:, None] // CHUNK
    kc = k_pos[None, :] // CHUNK
    mask = (kc <= qc) & (kc >= qc - A_PAST_CHUNKS)
    return jnp.where(mask[None], bias, NEG_INF)


def _band_prompt(q, k, v, rel_bias, sp, out):
    heads = rel_bias.shape[0]
    tq = A_PAST_CHUNKS * CHUNK
    assert sp % tq == 0
    pos = jnp.arange(tq, dtype=jnp.int32)
    bias = _band_bias(rel_bias, tq + pos, jnp.arange(2 * tq, dtype=jnp.int32))
    prev = lambda h, i: (jnp.maximum(i - 1, 0), h)
    cur = lambda h, i: (i, h)
    blk = lambda im: pl.BlockSpec((tq, HEAD_DIM), im)
    return pl.pallas_call(
        _ignore_input(_band_prompt_kernel, 6),
        grid=(heads, sp // tq),
        in_specs=[blk(cur), blk(prev), blk(cur), blk(prev), blk(cur),
                  pl.BlockSpec((None, tq, 2 * tq), lambda h, i: (h, 0, 0)),
                  pl.BlockSpec(memory_space=pl.ANY)],
        out_specs=blk(cur),
        out_shape=jax.ShapeDtypeStruct(out.shape, out.dtype),
        input_output_aliases={6: 0},
        compiler_params=_params("parallel", "arbitrary"),
        name="band_prompt",
    )(q, k, k, v, v, bias, out)


def _band_sample(q, k, v, cache_k, cache_v, layer, rel_bias, sp, ls, past_len, out):
    _, bs, win, heads, _ = cache_k.shape
    width = heads * HEAD_DIM
    assert sp % ls == 0
    row0 = sp // ls
    q_pos = past_len + jnp.arange(ls, dtype=jnp.int32)
    k_pos = past_len - win + jnp.arange(win + ls, dtype=jnp.int32)
    bias = _band_bias(rel_bias, q_pos, k_pos)
    blk_new = pl.BlockSpec((ls, width), lambda b: (row0 + b, 0))
    blk_old = pl.BlockSpec((None, None, win, heads, HEAD_DIM), lambda b: (layer, b, 0, 0, 0))
    return pl.pallas_call(
        _ignore_input(_band_sample_kernel, 6),
        grid=(bs,),
        in_specs=[blk_new, blk_new, blk_new, blk_old, blk_old,
                  pl.BlockSpec((heads, ls, win + ls), lambda b: (0, 0, 0)),
                  pl.BlockSpec(memory_space=pl.ANY)],
        out_specs=blk_new,
        out_shape=jax.ShapeDtypeStruct(out.shape, out.dtype),
        input_output_aliases={6: 0},
        compiler_params=_params("parallel"),
        name="band_sample",
    )(q, k, v, cache_k, cache_v, bias, out)


def _pool_kernel(cur_ref, hist_ref, w_ref, sc_ref, o_ref, ext_ref, *, pos_base, pos_step, first_has_no_past):
    tr = cur_ref.shape[0]
    hist = hist_ref[...]
    if first_has_no_past:
        hist = jnp.where(pl.program_id(0) > 0, hist, 0.0)
    ext_ref[0:POOL_HIST_PAD, :] = hist
    ext_ref[POOL_HIST_PAD:POOL_HIST_PAD + tr, :] = cur_ref[...]
    pos = pos_base + pl.program_id(0) * pos_step + lax.broadcasted_iota(jnp.int32, (tr, 1), 0)
    for g, w in enumerate(POOL_WINDOWS):
        sl = slice(g * B_GROUP_WIDTH, (g + 1) * B_GROUP_WIDTH)
        cur = cur_ref[:, sl]
        tot = cur
        for j in range(1, w):
            tot = tot + ext_ref[POOL_HIST_PAD - j:POOL_HIST_PAD - j + tr, sl]
        cnt = jnp.minimum(pos + 1, w).astype(F32)
        d = tot / cnt - cur
        y = _dot(d.astype(BF16), w_ref[g]) * sc_ref[:, sl]
        o_ref[:, sl] = y.astype(o_ref.dtype)


def _pool(p, hist, pool_w, pool_scale, out, *, row_block0, nblk, tr, pos_base, pos_step, name):
    width = p.shape[1]
    rows = pl.BlockSpec((tr, width), lambda i: (row_block0 + i, 0))
    body = functools.partial(_pool_kernel, pos_base=pos_base, pos_step=pos_step, first_has_no_past=hist is None)
    if hist is None:
        assert row_block0 == 0 and tr % POOL_HIST_PAD == 0
        per = tr // POOL_HIST_PAD
        hist_spec = pl.BlockSpec((POOL_HIST_PAD, width), lambda i: (jnp.maximum(i * per - 1, 0), 0))
        hist = p
    else:
        hist_spec = pl.BlockSpec((None, POOL_HIST_PAD, width), lambda i: (i, 0, 0))
    return pl.pallas_call(
        _ignore_input(body, 4),
        grid=(nblk,),
        in_specs=[rows, hist_spec,
                  pl.BlockSpec(pool_w.shape, lambda i: (0, 0, 0)),
                  pl.BlockSpec((1, width), lambda i: (0, 0)),
                  pl.BlockSpec(memory_space=pl.ANY)],
        out_specs=rows,
        out_shape=jax.ShapeDtypeStruct(out.shape, out.dtype),
        input_output_aliases={4: 0},
        scratch_shapes=[pltpu.VMEM((POOL_HIST_PAD + tr, width), F32)],
        compiler_params=_params("parallel"),
        name=name,
    )(p, hist, pool_w, pool_scale.reshape(1, width), out)


def _sb_steps(qs, kvs, tri, carries, accs, visibles):
    nk = kvs[0][0].shape[0]
    zs = [_dot_nt(q, k) * ATTN_SCALE for q, (k, _) in zip(qs, kvs)]
    log_keeps = [-(jnp.maximum(z, 0.0) + jnp.log(1.0 + jnp.exp(-jnp.abs(z)))) for z in zs]
    if visibles is not None:
        log_keeps = [jnp.where(vis, lk, 0.0) for lk, vis in zip(log_keeps, visibles)]
    his = [lk.astype(BF16) for lk in log_keeps]
    los = [(lk - hi.astype(F32)).astype(BF16) for lk, hi in zip(log_keeps, his)]
    sums = [_dot(hi, tri) + _dot(lo, tri) for hi, lo in zip(his, los)]
    weights = [jnp.exp(z + lk + s[:, :nk] + carry[:, :nk])
               for z, lk, s, carry in zip(zs, log_keeps, sums, carries)]
    if visibles is not None:
        weights = [jnp.where(vis, a, 0.0) for a, vis in zip(weights, visibles)]
    weights = [a.astype(BF16) for a in weights]
    new_accs = [acc + _dot(a, v) for acc, a, (_, v) in zip(accs, weights, kvs)]
    new_carries = [carry + s[:, nk:] for carry, s in zip(carries, sums)]
    return new_carries, new_accs


def _sb_sweep(chains, load_kv, tri, carries, accs):
    rows = chains[0][0].shape[0]

    def cond(st):
        return st[1] == 0

    def body(st):
        n, _, carries, accs = st
        kbs = [kb0 - n for _, kb0 in chains]
        starts = [pl.multiple_of(jnp.maximum(kb, 0) * SB_KEYS, SB_KEYS) for kb in kbs]
        carries = [jnp.where(kb >= 0, carry, -jnp.inf) for kb, carry in zip(kbs, carries)]
        carries, accs = _sb_steps([q for q, _ in chains], load_kv(starts), tri, carries, accs, None)
        pending = jnp.full((rows, HEAD_DIM), -jnp.inf, F32)
        for kb, carry in zip(kbs, carries):
            pending = jnp.maximum(pending, jnp.where(kb > 0, carry, -jnp.inf))
        done = (jnp.max(pending) < SB_EXIT).astype(jnp.int32)
        return n + 1, done, carries, accs

    return lax.while_loop(cond, body, (jnp.int32(0), jnp.int32(0), list(carries), list(accs)))[3]


def _sb_prompt_kernel(q_ref, k_ref, v_ref, tri_ref, o_ref, *, tq):
    n_chains = q_ref.shape[0] // tq
    block0 = pl.program_id(1) * n_chains
    causal = lax.broadcasted_iota(jnp.int32, (tq, tq), 1) < lax.broadcasted_iota(jnp.int32, (tq, tq), 0)

    def load_kv(starts):
        return [(k_ref[pl.ds(s, SB_KEYS), :].astype(BF16), v_ref[pl.ds(s, SB_KEYS), :].astype(BF16))
                for s in starts]

    qs = [q_ref[c * tq:(c + 1) * tq, :] for c in range(n_chains)]
    own = [pl.multiple_of((block0 + c) * SB_KEYS, SB_KEYS) for c in range(n_chains)]
    zeros = [jnp.zeros((tq, HEAD_DIM), F32)] * n_chains
    tri = tri_ref[...]
    carries, accs = _sb_steps(qs, load_kv(own), tri, zeros, zeros, [causal] * n_chains)
    chains = [(q, block0 + c - 1) for c, q in enumerate(qs)]
    accs = _sb_sweep(chains, load_kv, tri, carries, accs)
    for c in range(n_chains):
        o_ref[c * tq:(c + 1) * tq, :] = accs[c].astype(o_ref.dtype)


def _sb_sample_kernel(q_ref, kn_ref, vn_ref, kc_ref, vc_ref, trin_ref, tri_ref, o_ref):
    ls = q_ref.shape[0]
    past, heads = kc_ref.shape[0], kc_ref.shape[1]
    causal = lax.broadcasted_iota(jnp.int32, (ls, ls), 1) < lax.broadcasted_iota(jnp.int32, (ls, ls), 0)
    zeros = jnp.zeros((ls, HEAD_DIM), F32)
    head_cols = [slice(hh * HEAD_DIM, (hh + 1) * HEAD_DIM) for hh in range(heads)]
    chains = [(q_ref[:, sl], past // SB_KEYS - 1) for sl in head_cols]
    carries, accs = _sb_steps([q for q, _ in chains],
                              [(kn_ref[:, sl].astype(BF16), vn_ref[:, sl].astype(BF16)) for sl in head_cols],
                              trin_ref[...], [zeros] * heads, [zeros] * heads, [causal] * heads)

    def load_kv(starts):
        k = pltpu.einshape("phd->hpd", kc_ref[pl.ds(starts[0], SB_KEYS), :, :])
        v = pltpu.einshape("phd->hpd", vc_ref[pl.ds(starts[0], SB_KEYS), :, :])
        return [(k[hh].astype(BF16), v[hh].astype(BF16)) for hh in range(heads)]

    accs = _sb_sweep(chains, load_kv, tri_ref[...], carries, accs)
    for hh in range(heads):
        o_ref[:, hh * HEAD_DIM:(hh + 1) * HEAD_DIM] = accs[hh].astype(o_ref.dtype)


def _sb_tri(nk):
    j = jnp.arange(nk)[:, None]
    s = jnp.arange(nk + HEAD_DIM)[None, :]
    return ((s >= nk) | (j > s)).astype(BF16)


def _sb_prompt(q, k, v, sp, out):
    heads = q.shape[1] // HEAD_DIM
    tq = SB_KEYS
    tb = _tile(sp, 8 * tq)
    assert sp % SB_KEYS == 0 and tb % tq == 0
    rows = pl.BlockSpec((tb, HEAD_DIM), lambda h, i: (i, h))
    return pl.pallas_call(
        _ignore_input(functools.partial(_sb_prompt_kernel, tq=tq), 4),
        grid=(heads, sp // tb),
        in_specs=[rows,
                  pl.BlockSpec((sp, HEAD_DIM), lambda h, i: (0, h)),
                  pl.BlockSpec((sp, HEAD_DIM), lambda h, i: (0, h)),
                  pl.BlockSpec((SB_KEYS, SB_KEYS + HEAD_DIM), lambda h, i: (0, 0)),
                  pl.BlockSpec(memory_space=pl.ANY)],
        out_specs=rows,
        out_shape=jax.ShapeDtypeStruct(out.shape, out.dtype),
        input_output_aliases={4: 0},
        compiler_params=_params("parallel", "arbitrary"),
        name="stickbreak_prompt",
    )(q, k, v, _sb_tri(SB_KEYS), out)


def _sb_sample(q, k, v, cache_k, cache_v, layer, sp, ls, out):
    n_layers, bs, past, heads, _ = cache_k.shape
    hg = 8
    assert past % SB_KEYS == 0 and past >= SB_KEYS and sp % ls == 0 and heads % hg == 0
    row0 = sp // ls
    gw = hg * HEAD_DIM
    blk_new = pl.BlockSpec((ls, gw), lambda b, g: (row0 + b, g))
    blk_kv = pl.BlockSpec((ls, gw), lambda b, g: (b, g))
    blk_old = pl.BlockSpec((None, None, past, None, hg, HEAD_DIM), lambda b, g: (layer, b, 0, g, 0, 0))
    grouped = (n_layers, bs, past, heads // hg, hg, HEAD_DIM)
    return pl.pallas_call(
        _ignore_input(_sb_sample_kernel, 7),
        grid=(bs, heads // hg),
        in_specs=[blk_new, blk_kv, blk_kv, blk_old, blk_old,
                  pl.BlockSpec((ls, ls + HEAD_DIM), lambda b, g: (0, 0)),
                  pl.BlockSpec((SB_KEYS, SB_KEYS + HEAD_DIM), lambda b, g: (0, 0)),
                  pl.BlockSpec(memory_space=pl.ANY)],
        out_specs=blk_new,
        out_shape=jax.ShapeDtypeStruct(out.shape, out.dtype),
        input_output_aliases={7: 0},
        compiler_params=_params("parallel", "arbitrary"),
        name="stickbreak_sample",
    )(q, k, v, cache_k.reshape(grouped), cache_v.reshape(grouped), _sb_tri(ls), _sb_tri(SB_KEYS), out)


def _peer_scores_kernel(x_ref, w_ref, key_ref, o_ref):
    half = key_ref.shape[2]
    pq = _dot(x_ref[...], w_ref[...]).astype(BF16)
    for c in range(key_ref.shape[0]):
        o_ref[c] = _dot_nt(key_ref[c], pq[:, c * half:(c + 1) * half])


def _peer_scores(xn, w_q, sub_keys, layer):
    t, d = xn.shape
    _, nhc, nkeys, half = sub_keys.shape
    tm = _tile(t, 1024)
    per = 4
    assert nhc % per == 0
    return pl.pallas_call(
        _peer_scores_kernel,
        grid=(t // tm, nhc // per),
        in_specs=[pl.BlockSpec((tm, d), lambda i, j: (i, 0)),
                  pl.BlockSpec((None, d, per * half), lambda i, j: (layer, 0, j)),
                  pl.BlockSpec((None, per, nkeys, half), lambda i, j: (layer, j, 0, 0))],
        out_specs=pl.BlockSpec((per, nkeys, tm), lambda i, j: (j, 0, i)),
        out_shape=jax.ShapeDtypeStruct((nhc, nkeys, t), F32),
        compiler_params=_params("parallel", "parallel"),
        name="peer_scores",
    )(xn, w_q, sub_keys)


def _extract_top(s, k, tie_safe):
    rows = lax.broadcasted_iota(jnp.int32, s.shape, 0)
    rank = jnp.full(s.shape, float(k), F32)
    vals = []
    for r in range(k):
        m = jnp.max(s, axis=0, keepdims=True)
        vals.append(m)
        hit = s == m
        if tie_safe:
            hit = rows == jnp.min(jnp.where(hit, rows, s.shape[0]), axis=0, keepdims=True)
        rank = jnp.where(hit, float(r), rank)
        s = jnp.where(hit, -jnp.inf, s)
    return vals, rank, s


def _peer_select_kernel(s_ref, m1x_ref, e1x_ref, rank2_ref, e2_ref, *, ne):
    def select(h, tie_safe):
        s1 = s_ref[2 * h]
        s2 = s_ref[2 * h + 1]
        a, rank1, _ = _extract_top(s1, PEER_TOPK, tie_safe)
        b, rank2, _ = _extract_top(s2, PEER_TOPK, tie_safe)
        b = jnp.concatenate(b, axis=0)
        sub = lax.broadcasted_iota(jnp.int32, (8, b.shape[1]), 0)
        cand = [a[0] + b, a[1] + b[:8]]
        for i in range(2, 8):
            cand.append(jnp.where(sub < PEER_TOPK // (i + 1), a[i] + b[:8], -jnp.inf))
        cand.append(jnp.concatenate(a[8:], axis=0) + b[0:1])
        cand = jnp.concatenate(cand, axis=0)
        g, _, left = _extract_top(cand, PEER_TOPK, tie_safe)
        z = jnp.ones_like(g[0])
        for gk in g[1:]:
            z = z + jnp.exp(gk - g[0])
        took = (left != cand).astype(F32)
        taken = [jnp.sum(took[0:16], axis=0, keepdims=True), jnp.sum(took[16:24], axis=0, keepdims=True)]
        taken += [jnp.sum(took[8 * i + 8:8 * i + 16], axis=0, keepdims=True) for i in range(2, 8)]
        taken += [took[72 + i:73 + i] for i in range(8)]
        m1 = jnp.zeros_like(s1)
        for i in range(PEER_TOPK):
            m1 = jnp.where(rank1 == float(i), taken[i], m1)
        e1 = jnp.exp(s1 - a[0]) / z
        rank2_ref[h] = rank2.astype(rank2_ref.dtype)
        e2_ref[h] = jnp.exp(s2 - b[0:1]).astype(e2_ref.dtype)
        row = pl.multiple_of(h * ne, ne)
        for j in range(N_KEYS // ne):
            m1x_ref[j, pl.ds(row, ne), :] = m1[j * ne:(j + 1) * ne]
            e1x_ref[j, pl.ds(row, ne), :] = e1[j * ne:(j + 1) * ne]
        removed = (jnp.sum((rank1 < PEER_TOPK).astype(F32), axis=0, keepdims=True)
                   + jnp.sum((rank2 < PEER_TOPK).astype(F32), axis=0, keepdims=True)
                   + jnp.sum(took, axis=0, keepdims=True))
        return jnp.max(removed) - 3.0 * PEER_TOPK

    def one_head(h, _):
        extra = select(h, tie_safe=False)

        @pl.when(extra > 0.0)
        def _():
            select(h, tie_safe=True)

        return 0

    lax.fori_loop(0, PEER_HEADS, one_head, 0)


def _peer_select(scores, ne):
    nhc, nkeys, t = scores.shape
    tt = _tile(t, 512)
    x_shape = jax.ShapeDtypeStruct((nkeys // ne, PEER_HEADS * ne, t), F32)
    x_spec = pl.BlockSpec((nkeys // ne, PEER_HEADS * ne, tt), lambda i: (0, 0, i))
    y_shape = jax.ShapeDtypeStruct((PEER_HEADS, nkeys, t), BF16)
    y_spec = pl.BlockSpec((PEER_HEADS, nkeys, tt), lambda i: (0, 0, i))
    return pl.pallas_call(
        functools.partial(_peer_select_kernel, ne=ne),
        grid=(t // tt,),
        in_specs=[pl.BlockSpec((nhc, nkeys, tt), lambda i: (0, 0, i))],
        out_specs=[x_spec, x_spec, y_spec, y_spec],
        out_shape=[x_shape, x_shape, y_shape, y_shape],
        compiler_params=_params("parallel"),
        name="peer_select",
    )(scores)


def _peer_act_kernel(*refs, ne, n_parts):
    u_ref, xt_ref = refs[0], refs[1]
    part_refs = refs[2:2 + 4 * n_parts]
    o_ref, gate_ref = refs[2 + 4 * n_parts], refs[3 + 4 * n_parts]
    first_row = (pl.program_id(1) % (ROW_BLOCK // ne)) * ne
    for part in range(n_parts):
        m1x_ref, e1x_ref, rank2_ref, e2_ref = part_refs[4 * part:4 * part + 4]
        tp = rank2_ref.shape[2]
        for e in range(ne):
            gate = jnp.zeros((N_KEYS, tp), BF16)
            for h in range(PEER_HEADS):
                r = pl.ds(h * ROW_BLOCK + first_row + e, 1)
                hit = rank2_ref[h] < m1x_ref[r, :].astype(BF16)
                w = e2_ref[h] * e1x_ref[r, :].astype(BF16)
                gate = gate + jnp.where(hit, w, jnp.zeros_like(w))
            gate_ref[e * N_KEYS:(e + 1) * N_KEYS, part * tp:(part + 1) * tp] = gate
    xt = xt_ref[...]
    for p in range(ne // 2):
        rows = slice(2 * p * N_KEYS, 2 * (p + 1) * N_KEYS)
        ht = _dot(u_ref[rows, :].astype(BF16), xt)
        gelu = 0.5 * ht * (1.0 + lax.erf(ht * (0.5 ** 0.5)))
        act_t = gate_ref[rows, :] * gelu.astype(BF16)
        o_ref[:, rows] = act_t.T


def _peer_act(xt, u, layer, m1x, e1x, rank2, e2, ne):
    d, t = xt.shape
    n_exp = u.shape[1]
    tp = _tile(t, 512)
    n_parts = 2 if t % (2 * tp) == 0 else 1
    tt = n_parts * tp
    te = ne * N_KEYS
    once = pl.Buffered(1)
    part_specs, part_args = [], []
    for part in range(n_parts):
        row_spec = pl.BlockSpec((None, PEER_HEADS * ROW_BLOCK, tp),
                                lambda i, j, part=part: (j // (ROW_BLOCK // ne), 0, n_parts * i + part))
        tile_spec = pl.BlockSpec((PEER_HEADS, N_KEYS, tp), lambda i, j, part=part: (0, 0, n_parts * i + part),
                                 pipeline_mode=once)
        part_specs += [row_spec, row_spec, tile_spec, tile_spec]
        part_args += [m1x, e1x, rank2, e2]
    return pl.pallas_call(
        functools.partial(_peer_act_kernel, ne=ne, n_parts=n_parts),
        grid=(t // tt, n_exp // te),
        in_specs=[pl.BlockSpec((None, te, d), lambda i, j: (layer, j, 0)),
                  pl.BlockSpec((d, tt), lambda i, j: (0, i), pipeline_mode=once)] + part_specs,
        out_specs=pl.BlockSpec((tt, te), lambda i, j: (i, j)),
        out_shape=jax.ShapeDtypeStruct((t, n_exp), BF16),
        scratch_shapes=[pltpu.VMEM((te, tt), BF16)],
        compiler_params=_params("parallel", "arbitrary"),
        name="peer_act",
    )(u, xt, *part_args)


def _peer_ffn(h, layer, norm_g, w_q, sub_keys, u, v, split_at=None):
    ne = 4
    xn, xn_t = _rmsnorm(h, norm_g, with_transpose=True)
    scores = _peer_scores(xn, w_q, sub_keys, layer)
    m1x, e1x, rank2, e2 = _peer_select(scores, ROW_BLOCK)
    act = _peer_act(xn_t, u, layer, m1x, e1x, rank2, e2, ne)
    down = functools.partial(_matmul, act, v, layer, mode="residual", aux=h, tm=1024, tn=1024, tk=2048)
    if split_at is None:
        return down(name="peer_down")
    return (down(row0=0, m=split_at, name="peer_down_head"),
            down(row0=split_at, m=h.shape[0] - split_at, name="peer_down_tail"))


def kernel(x_prompt, x_sample, cache_a_k, cache_a_v, state_b_pool, cache_c_k, cache_c_v, norm_mix, norm_ffn, ab_w_in, ab_q_gain, ab_k_gain, ab_rel_bias, ab_pool_w, ab_pool_scale, ab_w_out, c_w_in, c_w_out, peer_w_q, peer_sub_keys, peer_u, peer_v):
    bp, sp, d = x_prompt.shape
    bs, ls, _ = x_sample.shape
    assert bp == 1
    depth = norm_mix.shape[0]
    past_len = cache_c_k.shape[2]
    a_heads = ab_rel_bias.shape[1]
    a_width = a_heads * HEAD_DIM
    b_width = ab_pool_scale.shape[1]
    c_width = c_w_out.shape[1]
    win_p = min(A_PAST_CHUNKS * CHUNK, sp)
    n_s = bs * ls

    h = jnp.concatenate([x_prompt.reshape(sp, d), x_sample.reshape(n_s, d)], axis=0)

    ab_w_in_b, ab_w_out_b = ab_w_in.astype(BF16), ab_w_out.astype(BF16)
    c_w_in_b, c_w_out_b = c_w_in.astype(BF16), c_w_out.astype(BF16)
    peer_w_q_b, peer_v_b = peer_w_q.astype(BF16), peer_v.astype(BF16)
    sub_keys_b = peer_sub_keys.astype(BF16).reshape(depth, PEER_HEADS * 2, N_KEYS, -1)

    outs = {name: [] for name in ("a_k_p", "a_v_p", "b_p", "c_k_p", "c_v_p",
                                  "a_k_s", "a_v_s", "b_s", "c_k_s", "c_v_s")}
    for layer in range(depth):
        xn = _rmsnorm(h, norm_mix[layer])
        if layer % 2 == 0:
            i = layer // 2
            q = _matmul(xn, ab_w_in_b, i, col0=0, n=a_width, out_dtype=BF16, mode="headnorm",
                        aux=ab_q_gain[i], name="a_q")
            k = _matmul(xn, ab_w_in_b, i, col0=a_width, n=a_width, mode="headnorm", aux=ab_k_gain[i], name="a_k")
            v = _matmul(xn, ab_w_in_b, i, col0=2 * a_width, n=a_width, name="a_v")
            p = _matmul(xn, ab_w_in_b, i, col0=3 * a_width, n=b_width, name="b_in")

            att = jnp.zeros((sp + n_s, a_width), BF16)
            att = _band_prompt(q, k, v, ab_rel_bias[i], sp, att)
            att = _band_sample(q, k, v, cache_a_k, cache_a_v, i, ab_rel_bias[i], sp, ls, past_len, att)

            pool_w = ab_pool_w[i].astype(BF16)
            tr = _tile(sp, 256)
            pool = jnp.zeros((sp + n_s, b_width), BF16)
            pool = _pool(p, None, pool_w, ab_pool_scale[i], pool, row_block0=0, nblk=sp // tr, tr=tr,
                         pos_base=0, pos_step=tr, name="pool_prompt")
            hist_s = jnp.pad(state_b_pool[i], ((0, 0), (POOL_HIST_PAD - POOL_HIST, 0), (0, 0)))
            pool = _pool(p, hist_s, pool_w, ab_pool_scale[i], pool, row_block0=sp // ls, nblk=bs, tr=ls,
                         pos_base=past_len, pos_step=0, name="pool_sample")

            h = _matmul(att, ab_w_out_b, i, x2=pool, mode="residual", aux=h, name="a_out")

            ps = p[sp:].reshape(bs, ls, b_width)
            outs["a_k_p"].append(k[sp - win_p:sp].reshape(bp, win_p, a_heads, HEAD_DIM))
            outs["a_v_p"].append(v[sp - win_p:sp].reshape(bp, win_p, a_heads, HEAD_DIM))
            outs["b_p"].append(p[sp - POOL_HIST:sp].reshape(bp, POOL_HIST, b_width))
            outs["a_k_s"].append(k[sp:].reshape(bs, ls, a_heads, HEAD_DIM))
            outs["a_v_s"].append(v[sp:].reshape(bs, ls, a_heads, HEAD_DIM))
            outs["b_s"].append(jnp.concatenate([state_b_pool[i], ps], axis=1)[:, ls:])
        else:
            j = layer // 2
            c_heads = c_width // HEAD_DIM
            q = _matmul(xn, c_w_in_b, j, col0=0, n=c_width, out_dtype=BF16, name="c_q")
            k_p, k_p3 = _proj_heads(xn, c_w_in_b, j, col0=c_width, n=c_width, row0=0, m=sp, name="c_k_prompt")
            v_p, v_p3 = _proj_heads(xn, c_w_in_b, j, col0=2 * c_width, n=c_width, row0=0, m=sp, name="c_v_prompt")
            k_s, k_s3 = _proj_heads(xn, c_w_in_b, j, col0=c_width, n=c_width, row0=sp, m=n_s, name="c_k_sample")
            v_s, v_s3 = _proj_heads(xn, c_w_in_b, j, col0=2 * c_width, n=c_width, row0=sp, m=n_s, name="c_v_sample")
            att = jnp.zeros((sp + n_s, c_width), BF16)
            att = _sb_prompt(q, k_p, v_p, sp, att)
            att = _sb_sample(q, k_s, v_s, cache_c_k, cache_c_v, j, sp, ls, att)
            h = _matmul(att, c_w_out_b, j, mode="residual", aux=h, name="c_out")
            outs["c_k_p"].append(k_p3.reshape(bp, sp, c_heads, HEAD_DIM))
            outs["c_v_p"].append(v_p3.reshape(bp, sp, c_heads, HEAD_DIM))
            outs["c_k_s"].append(k_s3.reshape(bs, ls, c_heads, HEAD_DIM))
            outs["c_v_s"].append(v_s3.reshape(bs, ls, c_heads, HEAD_DIM))

        h = _peer_ffn(h, layer, norm_ffn[layer], peer_w_q_b, sub_keys_b, peer_u, peer_v_b,
                      split_at=sp if layer == depth - 1 else None)

    h_prompt, h_sample = h
    st = {name: jnp.stack(vals) for name, vals in outs.items()}
    return (h_prompt.reshape(bp, sp, d), h_sample.reshape(bs, ls, d),
            st["a_k_p"], st["a_v_p"], st["b_p"], st["c_k_p"], st["c_v_p"],
            st["a_k_s"], st["a_v_s"], st["b_s"], st["c_k_s"], st["c_v_s"])
```

```python
import functools

import jax
import jax.numpy as jnp
from jax import lax
from jax.experimental import pallas as pl
from jax.experimental.pallas import tpu as pltpu

F32 = jnp.float32
BF16 = jnp.bfloat16

HEAD_DIM = 128
CHUNK = 64
A_PAST_CHUNKS = 8
REL_CLIP = 128
POOL_WINDOWS = (2, 4, 8, 16)
POOL_HIST = max(POOL_WINDOWS) - 1
POOL_HIST_PAD = 16
B_GROUP_WIDTH = 512
PEER_HEADS = 8
N_KEYS = 128
PEER_TOPK = 16
ROW_BLOCK = 8
EPS = 1e-6
NEG_INF = -1e30
ATTN_SCALE = HEAD_DIM ** -0.5
SB_EXIT = -104.0
SB_KEYS = 128
VMEM_LIMIT_BYTES = 56 * 1024 * 1024


def _params(*sem):
    return pltpu.CompilerParams(dimension_semantics=sem, vmem_limit_bytes=VMEM_LIMIT_BYTES)


def _tile(dim, pref):
    return pref if dim % pref == 0 else dim


def _ignore_input(body, index):
    def wrapped(*refs):
        return body(*refs[:index], *refs[index + 1:])
    return wrapped


def _dot(a, b):
    return jnp.dot(a, b, preferred_element_type=F32)


def _dot_nt(a, b):
    return lax.dot_general(a, b, (((1,), (1,)), ((), ())), preferred_element_type=F32)


def _rmsnorm_kernel(x_ref, g_ref, o_ref, *transposed_ref):
    x = x_ref[...]
    ms = jnp.mean(x * x, axis=-1, keepdims=True)
    y = (x * lax.rsqrt(ms + EPS) * g_ref[...]).astype(o_ref.dtype)
    o_ref[...] = y
    for ot_ref in transposed_ref:
        ot_ref[...] = y.T


def _rmsnorm(x, g, with_transpose=False):
    t, d = x.shape
    tr = _tile(t, 256)
    out_specs = [pl.BlockSpec((tr, d), lambda i: (i, 0))]
    out_shape = [jax.ShapeDtypeStruct((t, d), BF16)]
    if with_transpose:
        out_specs.append(pl.BlockSpec((d, tr), lambda i: (0, i)))
        out_shape.append(jax.ShapeDtypeStruct((d, t), BF16))
    outs = pl.pallas_call(
        _rmsnorm_kernel,
        grid=(t // tr,),
        in_specs=[pl.BlockSpec((tr, d), lambda i: (i, 0)),
                  pl.BlockSpec((1, d), lambda i: (0, 0))],
        out_specs=out_specs,
        out_shape=out_shape,
        compiler_params=_params("parallel"),
        name="rmsnorm",
    )(x, g.reshape(1, d))
    return outs if with_transpose else outs[0]


def _mm_kernel(*refs, nk, mode, two_x):
    x_ref, w_ref = refs[0], refs[1]
    refs = refs[2:]
    if two_x:
        x2_ref, refs = refs[0], refs[1:]
    if mode in ("headnorm", "residual"):
        aux_ref, o_ref, scratch = refs[0], refs[1], refs[2:]
    else:
        aux_ref, o_ref, scratch = None, refs[0], refs[1:]

    def finish(acc):
        if mode == "headnorm":
            g = aux_ref[...]
            for c in range(acc.shape[1] // HEAD_DIM):
                sl = slice(c * HEAD_DIM, (c + 1) * HEAD_DIM)
                y = acc[:, sl]
                ms = jnp.mean(y * y, axis=-1, keepdims=True)
                o_ref[:, sl] = (y * lax.rsqrt(ms + EPS) * g).astype(o_ref.dtype)
        elif mode == "residual":
            o_ref[...] = aux_ref[...] + acc
        else:
            o_ref[...] = acc.astype(o_ref.dtype)

    if two_x:
        k1 = x_ref.shape[1]
        finish(_dot(x_ref[...], w_ref[:k1, :]) + _dot(x2_ref[...], w_ref[k1:, :]))
    elif nk == 1:
        finish(_dot(x_ref[...], w_ref[...]))
    else:
        acc_ref = scratch[0]
        k = pl.program_id(2)

        @pl.when(k == 0)
        def _():
            acc_ref[...] = jnp.zeros_like(acc_ref)

        acc_ref[...] += _dot(x_ref[...], w_ref[...])

        @pl.when(k == nk - 1)
        def _():
            finish(acc_ref[...])


def _matmul(x, w, layer, *, x2=None, col0=0, n=None, row0=0, m=None, out_dtype=F32, mode="plain", aux=None,
            tm=1024, tn=512, tk=None, name="matmul"):
    kdim = x.shape[1]
    m = x.shape[0] if m is None else m
    if x2 is not None:
        assert tk is None
        kdim += x2.shape[1]
    n = w.shape[2] if n is None else n
    tm, tn = _tile(m, tm), _tile(n, tn)
    tk = kdim if tk is None else _tile(kdim, tk)
    nk = kdim // tk
    rb, cb = row0 // tm, col0 // tn
    assert row0 % tm == 0 and col0 % tn == 0
    in_specs = [pl.BlockSpec((tm, x.shape[1] if x2 is not None else tk), lambda i, j, k: (i + rb, k)),
                pl.BlockSpec((None, tk, tn), lambda i, j, k: (layer, k, j + cb))]
    args = [x, w]
    if x2 is not None:
        in_specs.append(pl.BlockSpec((tm, x2.shape[1]), lambda i, j, k: (i + rb, 0)))
        args.append(x2)
    if mode == "headnorm":
        in_specs.append(pl.BlockSpec((1, HEAD_DIM), lambda i, j, k: (0, 0)))
        args.append(aux.reshape(1, HEAD_DIM))
    elif mode == "residual":
        in_specs.append(pl.BlockSpec((tm, tn), lambda i, j, k: (i + rb, j)))
        args.append(aux)
    scratch = [pltpu.VMEM((tm, tn), F32)] if nk > 1 else []
    return pl.pallas_call(
        functools.partial(_mm_kernel, nk=nk, mode=mode, two_x=x2 is not None),
        grid=(m // tm, n // tn, nk),
        in_specs=in_specs,
        out_specs=pl.BlockSpec((tm, tn), lambda i, j, k: (i, j)),
        out_shape=jax.ShapeDtypeStruct((m, n), out_dtype),
        scratch_shapes=scratch,
        compiler_params=_params("parallel", "parallel", "arbitrary"),
        name=name,
    )(*args)


def _proj_heads_kernel(x_ref, w_ref, o2_ref, o3_ref):
    acc = _dot(x_ref[...], w_ref[...])
    o2_ref[...] = acc.astype(o2_ref.dtype)
    o3_ref[...] = pltpu.einshape("m(hd)->mhd", acc, h=o3_ref.shape[1])


def _proj_heads(x, w, layer, *, col0, n, row0, m, name):
    kdim = x.shape[1]
    tm, tn = _tile(m, 1024), 8 * HEAD_DIM
    assert row0 % tm == 0 and col0 % tn == 0 and n % tn == 0
    rb, cb = row0 // tm, col0 // tn
    return pl.pallas_call(
        _proj_heads_kernel,
        grid=(m // tm, n // tn),
        in_specs=[pl.BlockSpec((tm, kdim), lambda i, j: (i + rb, 0)),
                  pl.BlockSpec((None, kdim, tn), lambda i, j: (layer, 0, j + cb))],
        out_specs=[pl.BlockSpec((tm, tn), lambda i, j: (i, j)),
                   pl.BlockSpec((tm, tn // HEAD_DIM, HEAD_DIM), lambda i, j: (i, j, 0))],
        out_shape=[jax.ShapeDtypeStruct((m, n), BF16),
                   jax.ShapeDtypeStruct((m, n // HEAD_DIM, HEAD_DIM), F32)],
        compiler_params=_params("parallel", "parallel"),
        name=name,
    )(x, w)


def _band_head(q, ka, kb, va, vb, bias_a, bias_b, past_visible):
    sa = _dot_nt(q, ka.astype(BF16)) * ATTN_SCALE + bias_a
    sb = _dot_nt(q, kb.astype(BF16)) * ATTN_SCALE + bias_b
    if past_visible is not None:
        sa = jnp.where(past_visible, sa, NEG_INF)
    m = jnp.maximum(jnp.max(sa, axis=-1, keepdims=True), jnp.max(sb, axis=-1, keepdims=True))
    pa = jnp.exp(sa - m)
    pb = jnp.exp(sb - m)
    l = jnp.sum(pa, axis=-1, keepdims=True) + jnp.sum(pb, axis=-1, keepdims=True)
    o = _dot(pa.astype(BF16), va.astype(BF16)) + _dot(pb.astype(BF16), vb.astype(BF16))
    return o / l


def _band_prompt_kernel(q_ref, ka_ref, kb_ref, va_ref, vb_ref, bias_ref, o_ref):
    tq = q_ref.shape[0]
    half = tq // 2
    past = pl.program_id(1) > 0
    lo = _band_head(q_ref[:half, :], ka_ref[...], kb_ref[:half, :], va_ref[...], vb_ref[:half, :],
                    bias_ref[:half, :tq], bias_ref[:half, tq:tq + half], past)
    o_ref[:half, :] = lo.astype(o_ref.dtype)
    hi = _band_head(q_ref[half:, :], ka_ref[half:, :], kb_ref[...], va_ref[half:, :], vb_ref[...],
                    bias_ref[half:, half:tq], bias_ref[half:, tq:], past)
    o_ref[half:, :] = hi.astype(o_ref.dtype)


def _band_sample_kernel(q_ref, kn_ref, vn_ref, kc_ref, vc_ref, bias_ref, o_ref):
    win, heads = kc_ref.shape[0], kc_ref.shape[1]
    kc = pltpu.einshape("phd->hpd", kc_ref[...])
    vc = pltpu.einshape("phd->hpd", vc_ref[...])
    for hh in range(heads):
        sl = slice(hh * HEAD_DIM, (hh + 1) * HEAD_DIM)
        o = _band_head(q_ref[:, sl], kc[hh], kn_ref[:, sl], vc[hh], vn_ref[:, sl],
                       bias_ref[hh, :, :win], bias_ref[hh, :, win:], None)
        o_ref[:, sl] = o.astype(o_ref.dtype)


def _band_bias(rel_bias, q_pos, k_pos):
    lq, lk = q_pos.shape[0], k_pos.shape[0]
    d_min = q_pos[0] - k_pos[lk - 1]
    n_diag = lq + lk - 1
    diag = jnp.clip(d_min + jnp.arange(n_diag, dtype=jnp.int32), -REL_CLIP, REL_CLIP) + REL_CLIP
    g = rel_bias.astype(F32)[:, diag]
    row_len = -(-n_diag // HEAD_DIM) * HEAD_DIM
    period = row_len + 1
    u = jnp.pad(g[:, ::-1], ((0, 0), (0, period - n_diag)))
    skew = jnp.tile(u, (1, lq))[:, :lq * row_len].reshape(-1, lq, row_len)
    bias = skew[:, :, lq - 1:lq - 1 + lk]
    qc = q_pos[:, None] // CHUNK
    kc = k_pos[None, :] // CHUNK
    mask = (kc <= qc) & (kc >= qc - A_PAST_CHUNKS)
    return jnp.where(mask[None], bias, NEG_INF)


def _band_prompt(q, k, v, rel_bias, sp, out):
    heads = rel_bias.shape[0]
    tq = A_PAST_CHUNKS * CHUNK
    assert sp % tq == 0
    pos = jnp.arange(tq, dtype=jnp.int32)
    bias = _band_bias(rel_bias, tq + pos, jnp.arange(2 * tq, dtype=jnp.int32))
    prev = lambda h, i: (jnp.maximum(i - 1, 0), h)
    cur = lambda h, i: (i, h)
    blk = lambda im: pl.BlockSpec((tq, HEAD_DIM), im)
    return pl.pallas_call(
        _ignore_input(_band_prompt_kernel, 6),
        grid=(heads, sp // tq),
        in_specs=[blk(cur), blk(prev), blk(cur), blk(prev), blk(cur),
                  pl.BlockSpec((None, tq, 2 * tq), lambda h, i: (h, 0, 0)),
                  pl.BlockSpec(memory_space=pl.ANY)],
        out_specs=blk(cur),
        out_shape=jax.ShapeDtypeStruct(out.shape, out.dtype),
        input_output_aliases={6: 0},
        compiler_params=_params("parallel", "arbitrary"),
        name="band_prompt",
    )(q, k, k, v, v, bias, out)


def _band_sample(q, k, v, cache_k, cache_v, layer, rel_bias, sp, ls, past_len, out):
    _, bs, win, heads, _ = cache_k.shape
    width = heads * HEAD_DIM
    assert sp % ls == 0
    row0 = sp // ls
    q_pos = past_len + jnp.arange(ls, dtype=jnp.int32)
    k_pos = past_len - win + jnp.arange(win + ls, dtype=jnp.int32)
    bias = _band_bias(rel_bias, q_pos, k_pos)
    blk_new = pl.BlockSpec((ls, width), lambda b: (row0 + b, 0))
    blk_old = pl.BlockSpec((None, None, win, heads, HEAD_DIM), lambda b: (layer, b, 0, 0, 0))
    return pl.pallas_call(
        _ignore_input(_band_sample_kernel, 6),
        grid=(bs,),
        in_specs=[blk_new, blk_new, blk_new, blk_old, blk_old,
                  pl.BlockSpec((heads, ls, win + ls), lambda b: (0, 0, 0)),
                  pl.BlockSpec(memory_space=pl.ANY)],
        out_specs=blk_new,
        out_shape=jax.ShapeDtypeStruct(out.shape, out.dtype),
        input_output_aliases={6: 0},
        compiler_params=_params("parallel"),
        name="band_sample",
    )(q, k, v, cache_k, cache_v, bias, out)


def _pool_kernel(cur_ref, hist_ref, w_ref, sc_ref, o_ref, ext_ref, *, pos_base, pos_step, first_has_no_past):
    tr = cur_ref.shape[0]
    hist = hist_ref[...]
    if first_has_no_past:
        hist = jnp.where(pl.program_id(0) > 0, hist, 0.0)
    ext_ref[0:POOL_HIST_PAD, :] = hist
    ext_ref[POOL_HIST_PAD:POOL_HIST_PAD + tr, :] = cur_ref[...]
    pos = pos_base + pl.program_id(0) * pos_step + lax.broadcasted_iota(jnp.int32, (tr, 1), 0)
    for g, w in enumerate(POOL_WINDOWS):
        sl = slice(g * B_GROUP_WIDTH, (g + 1) * B_GROUP_WIDTH)
        cur = cur_ref[:, sl]
        tot = cur
        for j in range(1, w):
            tot = tot + ext_ref[POOL_HIST_PAD - j:POOL_HIST_PAD - j + tr, sl]
        cnt = jnp.minimum(pos + 1, w).astype(F32)
        d = tot / cnt - cur
        y = _dot(d.astype(BF16), w_ref[g]) * sc_ref[:, sl]
        o_ref[:, sl] = y.astype(o_ref.dtype)


def _pool(p, hist, pool_w, pool_scale, out, *, row_block0, nblk, tr, pos_base, pos_step, name):
    width = p.shape[1]
    rows = pl.BlockSpec((tr, width), lambda i: (row_block0 + i, 0))
    body = functools.partial(_pool_kernel, pos_base=pos_base, pos_step=pos_step, first_has_no_past=hist is None)
    if hist is None:
        assert row_block0 == 0 and tr % POOL_HIST_PAD == 0
        per = tr // POOL_HIST_PAD
        hist_spec = pl.BlockSpec((POOL_HIST_PAD, width), lambda i: (jnp.maximum(i * per - 1, 0), 0))
        hist = p
    else:
        hist_spec = pl.BlockSpec((None, POOL_HIST_PAD, width), lambda i: (i, 0, 0))
    return pl.pallas_call(
        _ignore_input(body, 4),
        grid=(nblk,),
        in_specs=[rows, hist_spec,
                  pl.BlockSpec(pool_w.shape, lambda i: (0, 0, 0)),
                  pl.BlockSpec((1, width), lambda i: (0, 0)),
                  pl.BlockSpec(memory_space=pl.ANY)],
        out_specs=rows,
        out_shape=jax.ShapeDtypeStruct(out.shape, out.dtype),
        input_output_aliases={4: 0},
        scratch_shapes=[pltpu.VMEM((POOL_HIST_PAD + tr, width), F32)],
        compiler_params=_params("parallel"),
        name=name,
    )(p, hist, pool_w, pool_scale.reshape(1, width), out)


def _sb_steps(qs, kvs, tri, carries, accs, visibles):
    nk = kvs[0][0].shape[0]
    zs = [_dot_nt(q, k) * ATTN_SCALE for q, (k, _) in zip(qs, kvs)]
    log_keeps = [-(jnp.maximum(z, 0.0) + jnp.log(1.0 + jnp.exp(-jnp.abs(z)))) for z in zs]
    if visibles is not None:
        log_keeps = [jnp.where(vis, lk, 0.0) for lk, vis in zip(log_keeps, visibles)]
    his = [lk.astype(BF16) for lk in log_keeps]
    los = [(lk - hi.astype(F32)).astype(BF16) for lk, hi in zip(log_keeps, his)]
    sums = [_dot(hi, tri) + _dot(lo, tri) for hi, lo in zip(his, los)]
    weights = [jnp.exp(z + lk + s[:, :nk] + carry[:, :nk])
               for z, lk, s, carry in zip(zs, log_keeps, sums, carries)]
    if visibles is not None:
        weights = [jnp.where(vis, a, 0.0) for a, vis in zip(weights, visibles)]
    weights = [a.astype(BF16) for a in weights]
    new_accs = [acc + _dot(a, v) for acc, a, (_, v) in zip(accs, weights, kvs)]
    new_carries = [carry + s[:, nk:] for carry, s in zip(carries, sums)]
    return new_carries, new_accs


def _sb_sweep(chains, load_kv, tri, carries, accs):
    rows = chains[0][0].shape[0]

    def cond(st):
        return st[1] == 0

    def body(st):
        n, _, carries, accs = st
        kbs = [kb0 - n for _, kb0 in chains]
        starts = [pl.multiple_of(jnp.maximum(kb, 0) * SB_KEYS, SB_KEYS) for kb in kbs]
        carries = [jnp.where(kb >= 0, carry, -jnp.inf) for kb, carry in zip(kbs, carries)]
        carries, accs = _sb_steps([q for q, _ in chains], load_kv(starts), tri, carries, accs, None)
        pending = jnp.full((rows, HEAD_DIM), -jnp.inf, F32)
        for kb, carry in zip(kbs, carries):
            pending = jnp.maximum(pending, jnp.where(kb > 0, carry, -jnp.inf))
        done = (jnp.max(pending) < SB_EXIT).astype(jnp.int32)
        return n + 1, done, carries, accs

    return lax.while_loop(cond, body, (jnp.int32(0), jnp.int32(0), list(carries), list(accs)))[3]


def _sb_prompt_kernel(q_ref, k_ref, v_ref, tri_ref, o_ref, *, tq):
    n_chains = q_ref.shape[0] // tq
    block0 = pl.program_id(1) * n_chains
    causal = lax.broadcasted_iota(jnp.int32, (tq, tq), 1) < lax.broadcasted_iota(jnp.int32, (tq, tq), 0)

    def load_kv(starts):
        return [(k_ref[pl.ds(s, SB_KEYS), :].astype(BF16), v_ref[pl.ds(s, SB_KEYS), :].astype(BF16))
                for s in starts]

    qs = [q_ref[c * tq:(c + 1) * tq, :] for c in range(n_chains)]
    own = [pl.multiple_of((block0 + c) * SB_KEYS, SB_KEYS) for c in range(n_chains)]
    zeros = [jnp.zeros((tq, HEAD_DIM), F32)] * n_chains
    tri = tri_ref[...]
    carries, accs = _sb_steps(qs, load_kv(own), tri, zeros, zeros, [causal] * n_chains)
    chains = [(q, block0 + c - 1) for c, q in enumerate(qs)]
    accs = _sb_sweep(chains, load_kv, tri, carries, accs)
    for c in range(n_chains):
        o_ref[c * tq:(c + 1) * tq, :] = accs[c].astype(o_ref.dtype)


def _sb_sample_kernel(q_ref, kn_ref, vn_ref, kc_ref, vc_ref, trin_ref, tri_ref, o_ref):
    ls = q_ref.shape[0]
    past, heads = kc_ref.shape[0], kc_ref.shape[1]
    causal = lax.broadcasted_iota(jnp.int32, (ls, ls), 1) < lax.broadcasted_iota(jnp.int32, (ls, ls), 0)
    zeros = jnp.zeros((ls, HEAD_DIM), F32)
    head_cols = [slice(hh * HEAD_DIM, (hh + 1) * HEAD_DIM) for hh in range(heads)]
    chains = [(q_ref[:, sl], past // SB_KEYS - 1) for sl in head_cols]
    carries, accs = _sb_steps([q for q, _ in chains],
                              [(kn_ref[:, sl].astype(BF16), vn_ref[:, sl].astype(BF16)) for sl in head_cols],
                              trin_ref[...], [zeros] * heads, [zeros] * heads, [causal] * heads)

    def load_kv(starts):
        k = pltpu.einshape("phd->hpd", kc_ref[pl.ds(starts[0], SB_KEYS), :, :])
        v = pltpu.einshape("phd->hpd", vc_ref[pl.ds(starts[0], SB_KEYS), :, :])
        return [(k[hh].astype(BF16), v[hh].astype(BF16)) for hh in range(heads)]

    accs = _sb_sweep(chains, load_kv, tri_ref[...], carries, accs)
    for hh in range(heads):
        o_ref[:, hh * HEAD_DIM:(hh + 1) * HEAD_DIM] = accs[hh].astype(o_ref.dtype)


def _sb_tri(nk):
    j = jnp.arange(nk)[:, None]
    s = jnp.arange(nk + HEAD_DIM)[None, :]
    return ((s >= nk) | (j > s)).astype(BF16)


def _sb_prompt(q, k, v, sp, out):
    heads = q.shape[1] // HEAD_DIM
    tq = SB_KEYS
    tb = _tile(sp, 8 * tq)
    assert sp % SB_KEYS == 0 and tb % tq == 0
    rows = pl.BlockSpec((tb, HEAD_DIM), lambda h, i: (i, h))
    return pl.pallas_call(
        _ignore_input(functools.partial(_sb_prompt_kernel, tq=tq), 4),
        grid=(heads, sp // tb),
        in_specs=[rows,
                  pl.BlockSpec((sp, HEAD_DIM), lambda h, i: (0, h)),
                  pl.BlockSpec((sp, HEAD_DIM), lambda h, i: (0, h)),
                  pl.BlockSpec((SB_KEYS, SB_KEYS + HEAD_DIM), lambda h, i: (0, 0)),
                  pl.BlockSpec(memory_space=pl.ANY)],
        out_specs=rows,
        out_shape=jax.ShapeDtypeStruct(out.shape, out.dtype),
        input_output_aliases={4: 0},
        compiler_params=_params("parallel", "arbitrary"),
        name="stickbreak_prompt",
    )(q, k, v, _sb_tri(SB_KEYS), out)


def _sb_sample(q, k, v, cache_k, cache_v, layer, sp, ls, out):
    n_layers, bs, past, heads, _ = cache_k.shape
    hg = 8
    assert past % SB_KEYS == 0 and past >= SB_KEYS and sp % ls == 0 and heads % hg == 0
    row0 = sp // ls
    gw = hg * HEAD_DIM
    blk_new = pl.BlockSpec((ls, gw), lambda b, g: (row0 + b, g))
    blk_kv = pl.BlockSpec((ls, gw), lambda b, g: (b, g))
    blk_old = pl.BlockSpec((None, None, past, None, hg, HEAD_DIM), lambda b, g: (layer, b, 0, g, 0, 0))
    grouped = (n_layers, bs, past, heads // hg, hg, HEAD_DIM)
    return pl.pallas_call(
        _ignore_input(_sb_sample_kernel, 7),
        grid=(bs, heads // hg),
        in_specs=[blk_new, blk_kv, blk_kv, blk_old, blk_old,
                  pl.BlockSpec((ls, ls + HEAD_DIM), lambda b, g: (0, 0)),
                  pl.BlockSpec((SB_KEYS, SB_KEYS + HEAD_DIM), lambda b, g: (0, 0)),
                  pl.BlockSpec(memory_space=pl.ANY)],
        out_specs=blk_new,
        out_shape=jax.ShapeDtypeStruct(out.shape, out.dtype),
        input_output_aliases={7: 0},
        compiler_params=_params("parallel", "arbitrary"),
        name="stickbreak_sample",
    )(q, k, v, cache_k.reshape(grouped), cache_v.reshape(grouped), _sb_tri(ls), _sb_tri(SB_KEYS), out)


def _peer_scores_kernel(x_ref, w_ref, key_ref, o_ref):
    half = key_ref.shape[2]
    pq = _dot(x_ref[...], w_ref[...]).astype(BF16)
    for c in range(key_ref.shape[0]):
        o_ref[c] = _dot_nt(key_ref[c], pq[:, c * half:(c + 1) * half])


def _peer_scores(xn, w_q, sub_keys, layer):
    t, d = xn.shape
    _, nhc, nkeys, half = sub_keys.shape
    tm = _tile(t, 1024)
    per = 4
    assert nhc % per == 0
    return pl.pallas_call(
        _peer_scores_kernel,
        grid=(t // tm, nhc // per),
        in_specs=[pl.BlockSpec((tm, d), lambda i, j: (i, 0)),
                  pl.BlockSpec((None, d, per * half), lambda i, j: (layer, 0, j)),
                  pl.BlockSpec((None, per, nkeys, half), lambda i, j: (layer, j, 0, 0))],
        out_specs=pl.BlockSpec((per, nkeys, tm), lambda i, j: (j, 0, i)),
        out_shape=jax.ShapeDtypeStruct((nhc, nkeys, t), F32),
        compiler_params=_params("parallel", "parallel"),
        name="peer_scores",
    )(xn, w_q, sub_keys)


def _extract_top(s, k, tie_safe):
    rows = lax.broadcasted_iota(jnp.int32, s.shape, 0)
    rank = jnp.full(s.shape, float(k), F32)
    vals = []
    for r in range(k):
        m = jnp.max(s, axis=0, keepdims=True)
        vals.append(m)
        hit = s == m
        if tie_safe:
            hit = rows == jnp.min(jnp.where(hit, rows, s.shape[0]), axis=0, keepdims=True)
        rank = jnp.where(hit, float(r), rank)
        s = jnp.where(hit, -jnp.inf, s)
    return vals, rank, s


def _peer_select_kernel(s_ref, m1x_ref, e1x_ref, rank2_ref, e2_ref, *, ne):
    def select(h, tie_safe):
        s1 = s_ref[2 * h]
        s2 = s_ref[2 * h + 1]
        a, rank1, _ = _extract_top(s1, PEER_TOPK, tie_safe)
        b, rank2, _ = _extract_top(s2, PEER_TOPK, tie_safe)
        b = jnp.concatenate(b, axis=0)
        sub = lax.broadcasted_iota(jnp.int32, (8, b.shape[1]), 0)
        cand = [a[0] + b, a[1] + b[:8]]
        for i in range(2, 8):
            cand.append(jnp.where(sub < PEER_TOPK // (i + 1), a[i] + b[:8], -jnp.inf))
        cand.append(jnp.concatenate(a[8:], axis=0) + b[0:1])
        cand = jnp.concatenate(cand, axis=0)
        g, _, left = _extract_top(cand, PEER_TOPK, tie_safe)
        z = jnp.ones_like(g[0])
        for gk in g[1:]:
            z = z + jnp.exp(gk - g[0])
        took = (left != cand).astype(F32)
        taken = [jnp.sum(took[0:16], axis=0, keepdims=True), jnp.sum(took[16:24], axis=0, keepdims=True)]
        taken += [jnp.sum(took[8 * i + 8:8 * i + 16], axis=0, keepdims=True) for i in range(2, 8)]
        taken += [took[72 + i:73 + i] for i in range(8)]
        m1 = jnp.zeros_like(s1)
        for i in range(PEER_TOPK):
            m1 = jnp.where(rank1 == float(i), taken[i], m1)
        e1 = jnp.exp(s1 - a[0]) / z
        rank2_ref[h] = rank2.astype(rank2_ref.dtype)
        e2_ref[h] = jnp.exp(s2 - b[0:1]).astype(e2_ref.dtype)
        row = pl.multiple_of(h * ne, ne)
        for j in range(N_KEYS // ne):
            m1x_ref[j, pl.ds(row, ne), :] = m1[j * ne:(j + 1) * ne]
            e1x_ref[j, pl.ds(row, ne), :] = e1[j * ne:(j + 1) * ne]
        removed = (jnp.sum((rank1 < PEER_TOPK).astype(F32), axis=0, keepdims=True)
                   + jnp.sum((rank2 < PEER_TOPK).astype(F32), axis=0, keepdims=True)
                   + jnp.sum(took, axis=0, keepdims=True))
        return jnp.max(removed) - 3.0 * PEER_TOPK

    def one_head(h, _):
        extra = select(h, tie_safe=False)

        @pl.when(extra > 0.0)
        def _():
            select(h, tie_safe=True)

        return 0

    lax.fori_loop(0, PEER_HEADS, one_head, 0)


def _peer_select(scores, ne):
    nhc, nkeys, t = scores.shape
    tt = _tile(t, 512)
    x_shape = jax.ShapeDtypeStruct((nkeys // ne, PEER_HEADS * ne, t), F32)
    x_spec = pl.BlockSpec((nkeys // ne, PEER_HEADS * ne, tt), lambda i: (0, 0, i))
    y_shape = jax.ShapeDtypeStruct((PEER_HEADS, nkeys, t), BF16)
    y_spec = pl.BlockSpec((PEER_HEADS, nkeys, tt), lambda i: (0, 0, i))
    return pl.pallas_call(
        functools.partial(_peer_select_kernel, ne=ne),
        grid=(t // tt,),
        in_specs=[pl.BlockSpec((nhc, nkeys, tt), lambda i: (0, 0, i))],
        out_specs=[x_spec, x_spec, y_spec, y_spec],
        out_shape=[x_shape, x_shape, y_shape, y_shape],
        compiler_params=_params("parallel"),
        name="peer_select",
    )(scores)


def _peer_act_kernel(*refs, ne, n_parts):
    u_ref, xt_ref = refs[0], refs[1]
    part_refs = refs[2:2 + 4 * n_parts]
    o_ref, gate_ref = refs[2 + 4 * n_parts], refs[3 + 4 * n_parts]
    first_row = (pl.program_id(1) % (ROW_BLOCK // ne)) * ne
    for part in range(n_parts):
        m1x_ref, e1x_ref, rank2_ref, e2_ref = part_refs[4 * part:4 * part + 4]
        tp = rank2_ref.shape[2]
        for e in range(ne):
            gate = jnp.zeros((N_KEYS, tp), BF16)
            for h in range(PEER_HEADS):
                r = pl.ds(h * ROW_BLOCK + first_row + e, 1)
                hit = rank2_ref[h] < m1x_ref[r, :].astype(BF16)
                w = e2_ref[h] * e1x_ref[r, :].astype(BF16)
                gate = gate + jnp.where(hit, w, jnp.zeros_like(w))
            gate_ref[e * N_KEYS:(e + 1) * N_KEYS, part * tp:(part + 1) * tp] = gate
    xt = xt_ref[...]
    for p in range(ne // 2):
        rows = slice(2 * p * N_KEYS, 2 * (p + 1) * N_KEYS)
        ht = _dot(u_ref[rows, :].astype(BF16), xt)
        gelu = 0.5 * ht * (1.0 + lax.erf(ht * (0.5 ** 0.5)))
        act_t = gate_ref[rows, :] * gelu.astype(BF16)
        o_ref[:, rows] = act_t.T


def _peer_act(xt, u, layer, m1x, e1x, rank2, e2, ne):
    d, t = xt.shape
    n_exp = u.shape[1]
    tp = _tile(t, 512)
    n_parts = 2 if t % (2 * tp) == 0 else 1
    tt = n_parts * tp
    te = ne * N_KEYS
    once = pl.Buffered(1)
    part_specs, part_args = [], []
    for part in range(n_parts):
        row_spec = pl.BlockSpec((None, PEER_HEADS * ROW_BLOCK, tp),
                                lambda i, j, part=part: (j // (ROW_BLOCK // ne), 0, n_parts * i + part))
        tile_spec = pl.BlockSpec((PEER_HEADS, N_KEYS, tp), lambda i, j, part=part: (0, 0, n_parts * i + part),
                                 pipeline_mode=once)
        part_specs += [row_spec, row_spec, tile_spec, tile_spec]
        part_args += [m1x, e1x, rank2, e2]
    return pl.pallas_call(
        functools.partial(_peer_act_kernel, ne=ne, n_parts=n_parts),
        grid=(t // tt, n_exp // te),
        in_specs=[pl.BlockSpec((None, te, d), lambda i, j: (layer, j, 0)),
                  pl.BlockSpec((d, tt), lambda i, j: (0, i), pipeline_mode=once)] + part_specs,
        out_specs=pl.BlockSpec((tt, te), lambda i, j: (i, j)),
        out_shape=jax.ShapeDtypeStruct((t, n_exp), BF16),
        scratch_shapes=[pltpu.VMEM((te, tt), BF16)],
        compiler_params=_params("parallel", "arbitrary"),
        name="peer_act",
    )(u, xt, *part_args)


def _peer_ffn(h, layer, norm_g, w_q, sub_keys, u, v, split_at=None):
    ne = 4
    xn, xn_t = _rmsnorm(h, norm_g, with_transpose=True)
    scores = _peer_scores(xn, w_q, sub_keys, layer)
    m1x, e1x, rank2, e2 = _peer_select(scores, ROW_BLOCK)
    act = _peer_act(xn_t, u, layer, m1x, e1x, rank2, e2, ne)
    down = functools.partial(_matmul, act, v, layer, mode="residual", aux=h, tm=1024, tn=1024, tk=2048)
    if split_at is None:
        return down(name="peer_down")
    return (down(row0=0, m=split_at, name="peer_down_head"),
            down(row0=split_at, m=h.shape[0] - split_at, name="peer_down_tail"))


def kernel(x_prompt, x_sample, cache_a_k, cache_a_v, state_b_pool, cache_c_k, cache_c_v, norm_mix, norm_ffn, ab_w_in, ab_q_gain, ab_k_gain, ab_rel_bias, ab_pool_w, ab_pool_scale, ab_w_out, c_w_in, c_w_out, peer_w_q, peer_sub_keys, peer_u, peer_v):
    bp, sp, d = x_prompt.shape
    bs, ls, _ = x_sample.shape
    assert bp == 1
    depth = norm_mix.shape[0]
    past_len = cache_c_k.shape[2]
    a_heads = ab_rel_bias.shape[1]
    a_width = a_heads * HEAD_DIM
    b_width = ab_pool_scale.shape[1]
    c_width = c_w_out.shape[1]
    win_p = min(A_PAST_CHUNKS * CHUNK, sp)
    n_s = bs * ls

    h = jnp.concatenate([x_prompt.reshape(sp, d), x_sample.reshape(n_s, d)], axis=0)

    ab_w_in_b, ab_w_out_b = ab_w_in.astype(BF16), ab_w_out.astype(BF16)
    c_w_in_b, c_w_out_b = c_w_in.astype(BF16), c_w_out.astype(BF16)
    peer_w_q_b, peer_v_b = peer_w_q.astype(BF16), peer_v.astype(BF16)
    sub_keys_b = peer_sub_keys.astype(BF16).reshape(depth, PEER_HEADS * 2, N_KEYS, -1)

    outs = {name: [] for name in ("a_k_p", "a_v_p", "b_p", "c_k_p", "c_v_p",
                                  "a_k_s", "a_v_s", "b_s", "c_k_s", "c_v_s")}
    for layer in range(depth):
        xn = _rmsnorm(h, norm_mix[layer])
        if layer % 2 == 0:
            i = layer // 2
            q = _matmul(xn, ab_w_in_b, i, col0=0, n=a_width, out_dtype=BF16, mode="headnorm",
                        aux=ab_q_gain[i], name="a_q")
            k = _matmul(xn, ab_w_in_b, i, col0=a_width, n=a_width, mode="headnorm", aux=ab_k_gain[i], name="a_k")
            v = _matmul(xn, ab_w_in_b, i, col0=2 * a_width, n=a_width, name="a_v")
            p = _matmul(xn, ab_w_in_b, i, col0=3 * a_width, n=b_width, name="b_in")

            att = jnp.zeros((sp + n_s, a_width), BF16)
            att = _band_prompt(q, k, v, ab_rel_bias[i], sp, att)
            att = _band_sample(q, k, v, cache_a_k, cache_a_v, i, ab_rel_bias[i], sp, ls, past_len, att)

            pool_w = ab_pool_w[i].astype(BF16)
            tr = _tile(sp, 256)
            pool = jnp.zeros((sp + n_s, b_width), BF16)
            pool = _pool(p, None, pool_w, ab_pool_scale[i], pool, row_block0=0, nblk=sp // tr, tr=tr,
                         pos_base=0, pos_step=tr, name="pool_prompt")
            hist_s = jnp.pad(state_b_pool[i], ((0, 0), (POOL_HIST_PAD - POOL_HIST, 0), (0, 0)))
            pool = _pool(p, hist_s, pool_w, ab_pool_scale[i], pool, row_block0=sp // ls, nblk=bs, tr=ls,
                         pos_base=past_len, pos_step=0, name="pool_sample")

            h = _matmul(att, ab_w_out_b, i, x2=pool, mode="residual", aux=h, name="a_out")

            ps = p[sp:].reshape(bs, ls, b_width)
            outs["a_k_p"].append(k[sp - win_p:sp].reshape(bp, win_p, a_heads, HEAD_DIM))
            outs["a_v_p"].append(v[sp - win_p:sp].reshape(bp, win_p, a_heads, HEAD_DIM))
            outs["b_p"].append(p[sp - POOL_HIST:sp].reshape(bp, POOL_HIST, b_width))
            outs["a_k_s"].append(k[sp:].reshape(bs, ls, a_heads, HEAD_DIM))
            outs["a_v_s"].append(v[sp:].reshape(bs, ls, a_heads, HEAD_DIM))
            outs["b_s"].append(jnp.concatenate([state_b_pool[i], ps], axis=1)[:, ls:])
        else:
            j = layer // 2
            c_heads = c_width // HEAD_DIM
            q = _matmul(xn, c_w_in_b, j, col0=0, n=c_width, out_dtype=BF16, name="c_q")
            k_p, k_p3 = _proj_heads(xn, c_w_in_b, j, col0=c_width, n=c_width, row0=0, m=sp, name="c_k_prompt")
            v_p, v_p3 = _proj_heads(xn, c_w_in_b, j, col0=2 * c_width, n=c_width, row0=0, m=sp, name="c_v_prompt")
            k_s, k_s3 = _proj_heads(xn, c_w_in_b, j, col0=c_width, n=c_width, row0=sp, m=n_s, name="c_k_sample")
            v_s, v_s3 = _proj_heads(xn, c_w_in_b, j, col0=2 * c_width, n=c_width, row0=sp, m=n_s, name="c_v_sample")
            att = jnp.zeros((sp + n_s, c_width), BF16)
            att = _sb_prompt(q, k_p, v_p, sp, att)
            att = _sb_sample(q, k_s, v_s, cache_c_k, cache_c_v, j, sp, ls, att)
            h = _matmul(att, c_w_out_b, j, mode="residual", aux=h, name="c_out")
            outs["c_k_p"].append(k_p3.reshape(bp, sp, c_heads, HEAD_DIM))
            outs["c_v_p"].append(v_p3.reshape(bp, sp, c_heads, HEAD_DIM))
            outs["c_k_s"].append(k_s3.reshape(bs, ls, c_heads, HEAD_DIM))
            outs["c_v_s"].append(v_s3.reshape(bs, ls, c_heads, HEAD_DIM))

        h = _peer_ffn(h, layer, norm_ffn[layer], peer_w_q_b, sub_keys_b, peer_u, peer_v_b,
                      split_at=sp if layer == depth - 1 else None)

    h_prompt, h_sample = h
    st = {name: jnp.stack(vals) for name, vals in outs.items()}
    return (h_prompt.reshape(bp, sp, d), h_sample.reshape(bs, ls, d),
            st["a_k_p"], st["a_v_p"], st["b_p"], st["c_k_p"], st["c_v_p"],
            st["a_k_s"], st["a_v_s"], st["b_s"], st["c_k_s"], st["c_v_s"])
```

```python
import functools

import jax
import jax.numpy as jnp
from jax import lax
from jax.experimental import pallas as pl
from jax.experimental.pallas import tpu as pltpu

F32 = jnp.float32
BF16 = jnp.bfloat16

HEAD_DIM = 128
CHUNK = 64
A_PAST_CHUNKS = 8
REL_CLIP = 128
POOL_WINDOWS = (2, 4, 8, 16)
POOL_HIST = max(POOL_WINDOWS) - 1
POOL_HIST_PAD = 16
B_GROUP_WIDTH = 512
PEER_HEADS = 8
N_KEYS = 128
PEER_TOPK = 16
ROW_BLOCK = 8
EPS = 1e-6
NEG_INF = -1e30
ATTN_SCALE = HEAD_DIM ** -0.5
SB_EXIT = -104.0
SB_KEYS = 128
VMEM_LIMIT_BYTES = 56 * 1024 * 1024


def _params(*sem):
    return pltpu.CompilerParams(dimension_semantics=sem, vmem_limit_bytes=VMEM_LIMIT_BYTES)


def _tile(dim, pref):
    return pref if dim % pref == 0 else dim


def _ignore_input(body, index):
    def wrapped(*refs):
        return body(*refs[:index], *refs[index + 1:])
    return wrapped


def _dot(a, b):
    return jnp.dot(a, b, preferred_element_type=F32)


def _dot_nt(a, b):
    return lax.dot_general(a, b, (((1,), (1,)), ((), ())), preferred_element_type=F32)


def _rmsnorm_kernel(x_ref, g_ref, o_ref, *transposed_ref):
    x = x_ref[...]
    ms = jnp.mean(x * x, axis=-1, keepdims=True)
    y = (x * lax.rsqrt(ms + EPS) * g_ref[...]).astype(o_ref.dtype)
    o_ref[...] = y
    for ot_ref in transposed_ref:
        ot_ref[...] = y.T


def _rmsnorm(x, g, with_transpose=False):
    t, d = x.shape
    tr = _tile(t, 256)
    out_specs = [pl.BlockSpec((tr, d), lambda i: (i, 0))]
    out_shape = [jax.ShapeDtypeStruct((t, d), BF16)]
    if with_transpose:
        out_specs.append(pl.BlockSpec((d, tr), lambda i: (0, i)))
        out_shape.append(jax.ShapeDtypeStruct((d, t), BF16))
    outs = pl.pallas_call(
        _rmsnorm_kernel,
        grid=(t // tr,),
        in_specs=[pl.BlockSpec((tr, d), lambda i: (i, 0)),
                  pl.BlockSpec((1, d), lambda i: (0, 0))],
        out_specs=out_specs,
        out_shape=out_shape,
        compiler_params=_params("parallel"),
        name="rmsnorm",
    )(x, g.reshape(1, d))
    return outs if with_transpose else outs[0]


def _mm_kernel(*refs, nk, mode, two_x):
    x_ref, w_ref = refs[0], refs[1]
    refs = refs[2:]
    if two_x:
        x2_ref, refs = refs[0], refs[1:]
    if mode in ("headnorm", "residual"):
        aux_ref, o_ref, scratch = refs[0], refs[1], refs[2:]
    else:
        aux_ref, o_ref, scratch = None, refs[0], refs[1:]

    def finish(acc):
        if mode == "headnorm":
            g = aux_ref[...]
            for c in range(acc.shape[1] // HEAD_DIM):
                sl = slice(c * HEAD_DIM, (c + 1) * HEAD_DIM)
                y = acc[:, sl]
                ms = jnp.mean(y * y, axis=-1, keepdims=True)
                o_ref[:, sl] = (y * lax.rsqrt(ms + EPS) * g).astype(o_ref.dtype)
        elif mode == "residual":
            o_ref[...] = aux_ref[...] + acc
        else:
            o_ref[...] = acc.astype(o_ref.dtype)

    if two_x:
        k1 = x_ref.shape[1]
        finish(_dot(x_ref[...], w_ref[:k1, :]) + _dot(x2_ref[...], w_ref[k1:, :]))
    elif nk == 1:
        finish(_dot(x_ref[...], w_ref[...]))
    else:
        acc_ref = scratch[0]
        k = pl.program_id(2)

        @pl.when(k == 0)
        def _():
            acc_ref[...] = jnp.zeros_like(acc_ref)

        acc_ref[...] += _dot(x_ref[...], w_ref[...])

        @pl.when(k == nk - 1)
        def _():
            finish(acc_ref[...])


def _matmul(x, w, layer, *, x2=None, col0=0, n=None, row0=0, m=None, out_dtype=F32, mode="plain", aux=None,
            tm=1024, tn=512, tk=None, name="matmul"):
    kdim = x.shape[1]
    m = x.shape[0] if m is None else m
    if x2 is not None:
        assert tk is None
        kdim += x2.shape[1]
    n = w.shape[2] if n is None else n
    tm, tn = _tile(m, tm), _tile(n, tn)
    tk = kdim if tk is None else _tile(kdim, tk)
    nk = kdim // tk
    rb, cb = row0 // tm, col0 // tn
    assert row0 % tm == 0 and col0 % tn == 0
    in_specs = [pl.BlockSpec((tm, x.shape[1] if x2 is not None else tk), lambda i, j, k: (i + rb, k)),
                pl.BlockSpec((None, tk, tn), lambda i, j, k: (layer, k, j + cb))]
    args = [x, w]
    if x2 is not None:
        in_specs.append(pl.BlockSpec((tm, x2.shape[1]), lambda i, j, k: (i + rb, 0)))
        args.append(x2)
    if mode == "headnorm":
        in_specs.append(pl.BlockSpec((1, HEAD_DIM), lambda i, j, k: (0, 0)))
        args.append(aux.reshape(1, HEAD_DIM))
    elif mode == "residual":
        in_specs.append(pl.BlockSpec((tm, tn), lambda i, j, k: (i + rb, j)))
        args.append(aux)
    scratch = [pltpu.VMEM((tm, tn), F32)] if nk > 1 else []
    return pl.pallas_call(
        functools.partial(_mm_kernel, nk=nk, mode=mode, two_x=x2 is not None),
        grid=(m // tm, n // tn, nk),
        in_specs=in_specs,
        out_specs=pl.BlockSpec((tm, tn), lambda i, j, k: (i, j)),
        out_shape=jax.ShapeDtypeStruct((m, n), out_dtype),
        scratch_shapes=scratch,
        compiler_params=_params("parallel", "parallel", "arbitrary"),
        name=name,
    )(*args)


def _proj_heads_kernel(x_ref, w_ref, o2_ref, o3_ref):
    acc = _dot(x_ref[...], w_ref[...])
    o2_ref[...] = acc.astype(o2_ref.dtype)
    o3_ref[...] = pltpu.einshape("m(hd)->mhd", acc, h=o3_ref.shape[1])


def _proj_heads(x, w, layer, *, col0, n, row0, m, name):
    kdim = x.shape[1]
    tm, tn = _tile(m, 1024), 8 * HEAD_DIM
    assert row0 % tm == 0 and col0 % tn == 0 and n % tn == 0
    rb, cb = row0 // tm, col0 // tn
    return pl.pallas_call(
        _proj_heads_kernel,
        grid=(m // tm, n // tn),
        in_specs=[pl.BlockSpec((tm, kdim), lambda i, j: (i + rb, 0)),
                  pl.BlockSpec((None, kdim, tn), lambda i, j: (layer, 0, j + cb))],
        out_specs=[pl.BlockSpec((tm, tn), lambda i, j: (i, j)),
                   pl.BlockSpec((tm, tn // HEAD_DIM, HEAD_DIM), lambda i, j: (i, j, 0))],
        out_shape=[jax.ShapeDtypeStruct((m, n), BF16),
                   jax.ShapeDtypeStruct((m, n // HEAD_DIM, HEAD_DIM), F32)],
        compiler_params=_params("parallel", "parallel"),
        name=name,
    )(x, w)


def _band_head(q, ka, kb, va, vb, bias_a, bias_b, past_visible):
    sa = _dot_nt(q, ka.astype(BF16)) * ATTN_SCALE + bias_a
    sb = _dot_nt(q, kb.astype(BF16)) * ATTN_SCALE + bias_b
    if past_visible is not None:
        sa = jnp.where(past_visible, sa, NEG_INF)
    m = jnp.maximum(jnp.max(sa, axis=-1, keepdims=True), jnp.max(sb, axis=-1, keepdims=True))
    pa = jnp.exp(sa - m)
    pb = jnp.exp(sb - m)
    l = jnp.sum(pa, axis=-1, keepdims=True) + jnp.sum(pb, axis=-1, keepdims=True)
    o = _dot(pa.astype(BF16), va.astype(BF16)) + _dot(pb.astype(BF16), vb.astype(BF16))
    return o / l


def _band_prompt_kernel(q_ref, ka_ref, kb_ref, va_ref, vb_ref, bias_ref, o_ref):
    tq = q_ref.shape[0]
    half = tq // 2
    past = pl.program_id(1) > 0
    lo = _band_head(q_ref[:half, :], ka_ref[...], kb_ref[:half, :], va_ref[...], vb_ref[:half, :],
                    bias_ref[:half, :tq], bias_ref[:half, tq:tq + half], past)
    o_ref[:half, :] = lo.astype(o_ref.dtype)
    hi = _band_head(q_ref[half:, :], ka_ref[half:, :], kb_ref[...], va_ref[half:, :], vb_ref[...],
                    bias_ref[half:, half:tq], bias_ref[half:, tq:], past)
    o_ref[half:, :] = hi.astype(o_ref.dtype)


def _band_sample_kernel(q_ref, kn_ref, vn_ref, kc_ref, vc_ref, bias_ref, o_ref):
    win, heads = kc_ref.shape[0], kc_ref.shape[1]
    kc = pltpu.einshape("phd->hpd", kc_ref[...])
    vc = pltpu.einshape("phd->hpd", vc_ref[...])
    for hh in range(heads):
        sl = slice(hh * HEAD_DIM, (hh + 1) * HEAD_DIM)
        o = _band_head(q_ref[:, sl], kc[hh], kn_ref[:, sl], vc[hh], vn_ref[:, sl],
                       bias_ref[hh, :, :win], bias_ref[hh, :, win:], None)
        o_ref[:, sl] = o.astype(o_ref.dtype)


def _band_bias(rel_bias, q_pos, k_pos):
    lq, lk = q_pos.shape[0], k_pos.shape[0]
    d_min = q_pos[0] - k_pos[lk - 1]
    n_diag = lq + lk - 1
    diag = jnp.clip(d_min + jnp.arange(n_diag, dtype=jnp.int32), -REL_CLIP, REL_CLIP) + REL_CLIP
    g = rel_bias.astype(F32)[:, diag]
    row_len = -(-n_diag // HEAD_DIM) * HEAD_DIM
    period = row_len + 1
    u = jnp.pad(g[:, ::-1], ((0, 0), (0, period - n_diag)))
    skew = jnp.tile(u, (1, lq))[:, :lq * row_len].reshape(-1, lq, row_len)
    bias = skew[:, :, lq - 1:lq - 1 + lk]
    qc = q_pos[:, None] // CHUNK
    kc = k_pos[None, :] // CHUNK
    mask = (kc <= qc) & (kc >= qc - A_PAST_CHUNKS)
    return jnp.where(mask[None], bias, NEG_INF)


def _band_prompt(q, k, v, rel_bias, sp, out):
    heads = rel_bias.shape[0]
    tq = A_PAST_CHUNKS * CHUNK
    assert sp % tq == 0
    pos = jnp.arange(tq, dtype=jnp.int32)
    bias = _band_bias(rel_bias, tq + pos, jnp.arange(2 * tq, dtype=jnp.int32))
    prev = lambda h, i: (jnp.maximum(i - 1, 0), h)
    cur = lambda h, i: (i, h)
    blk = lambda im: pl.BlockSpec((tq, HEAD_DIM), im)
    return pl.pallas_call(
        _ignore_input(_band_prompt_kernel, 6),
        grid=(heads, sp // tq),
        in_specs=[blk(cur), blk(prev), blk(cur), blk(prev), blk(cur),
                  pl.BlockSpec((None, tq, 2 * tq), lambda h, i: (h, 0, 0)),
                  pl.BlockSpec(memory_space=pl.ANY)],
        out_specs=blk(cur),
        out_shape=jax.ShapeDtypeStruct(out.shape, out.dtype),
        input_output_aliases={6: 0},
        compiler_params=_params("parallel", "arbitrary"),
        name="band_prompt",
    )(q, k, k, v, v, bias, out)


def _band_sample(q, k, v, cache_k, cache_v, layer, rel_bias, sp, ls, past_len, out):
    _, bs, win, heads, _ = cache_k.shape
    width = heads * HEAD_DIM
    assert sp % ls == 0
    row0 = sp // ls
    q_pos = past_len + jnp.arange(ls, dtype=jnp.int32)
    k_pos = past_len - win + jnp.arange(win + ls, dtype=jnp.int32)
    bias = _band_bias(rel_bias, q_pos, k_pos)
    blk_new = pl.BlockSpec((ls, width), lambda b: (row0 + b, 0))
    blk_old = pl.BlockSpec((None, None, win, heads, HEAD_DIM), lambda b: (layer, b, 0, 0, 0))
    return pl.pallas_call(
        _ignore_input(_band_sample_kernel, 6),
        grid=(bs,),
        in_specs=[blk_new, blk_new, blk_new, blk_old, blk_old,
                  pl.BlockSpec((heads, ls, win + ls), lambda b: (0, 0, 0)),
                  pl.BlockSpec(memory_space=pl.ANY)],
        out_specs=blk_new,
        out_shape=jax.ShapeDtypeStruct(out.shape, out.dtype),
        input_output_aliases={6: 0},
        compiler_params=_params("parallel"),
        name="band_sample",
    )(q, k, v, cache_k, cache_v, bias, out)


def _pool_kernel(cur_ref, hist_ref, w_ref, sc_ref, o_ref, ext_ref, *, pos_base, pos_step, first_has_no_past):
    tr = cur_ref.shape[0]
    hist = hist_ref[...]
    if first_has_no_past:
        hist = jnp.where(pl.program_id(0) > 0, hist, 0.0)
    ext_ref[0:POOL_HIST_PAD, :] = hist
    ext_ref[POOL_HIST_PAD:POOL_HIST_PAD + tr, :] = cur_ref[...]
    pos = pos_base + pl.program_id(0) * pos_step + lax.broadcasted_iota(jnp.int32, (tr, 1), 0)
    for g, w in enumerate(POOL_WINDOWS):
        sl = slice(g * B_GROUP_WIDTH, (g + 1) * B_GROUP_WIDTH)
        cur = cur_ref[:, sl]
        tot = cur
        for j in range(1, w):
            tot = tot + ext_ref[POOL_HIST_PAD - j:POOL_HIST_PAD - j + tr, sl]
        cnt = jnp.minimum(pos + 1, w).astype(F32)
        d = tot / cnt - cur
        y = _dot(d.astype(BF16), w_ref[g]) * sc_ref[:, sl]
        o_ref[:, sl] = y.astype(o_ref.dtype)


def _pool(p, hist, pool_w, pool_scale, out, *, row_block0, nblk, tr, pos_base, pos_step, name):
    width = p.shape[1]
    rows = pl.BlockSpec((tr, width), lambda i: (row_block0 + i, 0))
    body = functools.partial(_pool_kernel, pos_base=pos_base, pos_step=pos_step, first_has_no_past=hist is None)
    if hist is None:
        assert row_block0 == 0 and tr % POOL_HIST_PAD == 0
        per = tr // POOL_HIST_PAD
        hist_spec = pl.BlockSpec((POOL_HIST_PAD, width), lambda i: (jnp.maximum(i * per - 1, 0), 0))
        hist = p
    else:
        hist_spec = pl.BlockSpec((None, POOL_HIST_PAD, width), lambda i: (i, 0, 0))
    return pl.pallas_call(
        _ignore_input(body, 4),
        grid=(nblk,),
        in_specs=[rows, hist_spec,
                  pl.BlockSpec(pool_w.shape, lambda i: (0, 0, 0)),
                  pl.BlockSpec((1, width), lambda i: (0, 0)),
                  pl.BlockSpec(memory_space=pl.ANY)],
        out_specs=rows,
        out_shape=jax.ShapeDtypeStruct(out.shape, out.dtype),
        input_output_aliases={4: 0},
        scratch_shapes=[pltpu.VMEM((POOL_HIST_PAD + tr, width), F32)],
        compiler_params=_params("parallel"),
        name=name,
    )(p, hist, pool_w, pool_scale.reshape(1, width), out)


def _sb_steps(qs, kvs, tri, carries, accs, visibles):
    nk = kvs[0][0].shape[0]
    zs = [_dot_nt(q, k) * ATTN_SCALE for q, (k, _) in zip(qs, kvs)]
    log_keeps = [-(jnp.maximum(z, 0.0) + jnp.log(1.0 + jnp.exp(-jnp.abs(z)))) for z in zs]
    if visibles is not None:
        log_keeps = [jnp.where(vis, lk, 0.0) for lk, vis in zip(log_keeps, visibles)]
    his = [lk.astype(BF16) for lk in log_keeps]
    los = [(lk - hi.astype(F32)).astype(BF16) for lk, hi in zip(log_keeps, his)]
    sums = [_dot(hi, tri) + _dot(lo, tri) for hi, lo in zip(his, los)]
    weights = [jnp.exp(z + lk + s[:, :nk] + carry[:, :nk])
               for z, lk, s, carry in zip(zs, log_keeps, sums, carries)]
    if visibles is not None:
        weights = [jnp.where(vis, a, 0.0) for a, vis in zip(weights, visibles)]
    weights = [a.astype(BF16) for a in weights]
    new_accs = [acc + _dot(a, v) for acc, a, (_, v) in zip(accs, weights, kvs)]
    new_carries = [carry + s[:, nk:] for carry, s in zip(carries, sums)]
    return new_carries, new_accs


def _sb_sweep(chains, load_kv, tri, carries, accs):
    rows = chains[0][0].shape[0]

    def cond(st):
        return st[1] == 0

    def body(st):
        n, _, carries, accs = st
        kbs = [kb0 - n for _, kb0 in chains]
        starts = [pl.multiple_of(jnp.maximum(kb, 0) * SB_KEYS, SB_KEYS) for kb in kbs]
        carries = [jnp.where(kb >= 0, carry, -jnp.inf) for kb, carry in zip(kbs, carries)]
        carries, accs = _sb_steps([q for q, _ in chains], load_kv(starts), tri, carries, accs, None)
        pending = jnp.full((rows, HEAD_DIM), -jnp.inf, F32)
        for kb, carry in zip(kbs, carries):
            pending = jnp.maximum(pending, jnp.where(kb > 0, carry, -jnp.inf))
        done = (jnp.max(pending) < SB_EXIT).astype(jnp.int32)
        return n + 1, done, carries, accs

    return lax.while_loop(cond, body, (jnp.int32(0), jnp.int32(0), list(carries), list(accs)))[3]


def _sb_prompt_kernel(q_ref, k_ref, v_ref, tri_ref, o_ref, *, tq):
    n_chains = q_ref.shape[0] // tq
    block0 = pl.program_id(1) * n_chains
    causal = lax.broadcasted_iota(jnp.int32, (tq, tq), 1) < lax.broadcasted_iota(jnp.int32, (tq, tq), 0)

    def load_kv(starts):
        return [(k_ref[pl.ds(s, SB_KEYS), :].astype(BF16), v_ref[pl.ds(s, SB_KEYS), :].astype(BF16))
                for s in starts]

    qs = [q_ref[c * tq:(c + 1) * tq, :] for c in range(n_chains)]
    own = [pl.multiple_of((block0 + c) * SB_KEYS, SB_KEYS) for c in range(n_chains)]
    zeros = [jnp.zeros((tq, HEAD_DIM), F32)] * n_chains
    tri = tri_ref[...]
    carries, accs = _sb_steps(qs, load_kv(own), tri, zeros, zeros, [causal] * n_chains)
    chains = [(q, block0 + c - 1) for c, q in enumerate(qs)]
    accs = _sb_sweep(chains, load_kv, tri, carries, accs)
    for c in range(n_chains):
        o_ref[c * tq:(c + 1) * tq, :] = accs[c].astype(o_ref.dtype)


def _sb_sample_kernel(q_ref, kn_ref, vn_ref, kc_ref, vc_ref, trin_ref, tri_ref, o_ref):
    ls = q_ref.shape[0]
    past, heads = kc_ref.shape[0], kc_ref.shape[1]
    causal = lax.broadcasted_iota(jnp.int32, (ls, ls), 1) < lax.broadcasted_iota(jnp.int32, (ls, ls), 0)
    zeros = jnp.zeros((ls, HEAD_DIM), F32)
    head_cols = [slice(hh * HEAD_DIM, (hh + 1) * HEAD_DIM) for hh in range(heads)]
    chains = [(q_ref[:, sl], past // SB_KEYS - 1) for sl in head_cols]
    carries, accs = _sb_steps([q for q, _ in chains],
                              [(kn_ref[:, sl].astype(BF16), vn_ref[:, sl].astype(BF16)) for sl in head_cols],
                              trin_ref[...], [zeros] * heads, [zeros] * heads, [causal] * heads)

    def load_kv(starts):
        k = pltpu.einshape("phd->hpd", kc_ref[pl.ds(starts[0], SB_KEYS), :, :])
        v = pltpu.einshape("phd->hpd", vc_ref[pl.ds(starts[0], SB_KEYS), :, :])
        return [(k[hh].astype(BF16), v[hh].astype(BF16)) for hh in range(heads)]

    accs = _sb_sweep(chains, load_kv, tri_ref[...], carries, accs)
    for hh in range(heads):
        o_ref[:, hh * HEAD_DIM:(hh + 1) * HEAD_DIM] = accs[hh].astype(o_ref.dtype)


def _sb_tri(nk):
    j = jnp.arange(nk)[:, None]
    s = jnp.arange(nk + HEAD_DIM)[None, :]
    return ((s >= nk) | (j > s)).astype(BF16)


def _sb_prompt(q, k, v, sp, out):
    heads = q.shape[1] // HEAD_DIM
    tq = SB_KEYS
    tb = _tile(sp, 16 * tq)
    assert sp % SB_KEYS == 0 and tb % tq == 0
    rows = pl.BlockSpec((tb, HEAD_DIM), lambda h, i: (i, h))
    return pl.pallas_call(
        _ignore_input(functools.partial(_sb_prompt_kernel, tq=tq), 4),
        grid=(heads, sp // tb),
        in_specs=[rows,
                  pl.BlockSpec((sp, HEAD_DIM), lambda h, i: (0, h)),
                  pl.BlockSpec((sp, HEAD_DIM), lambda h, i: (0, h)),
                  pl.BlockSpec((SB_KEYS, SB_KEYS + HEAD_DIM), lambda h, i: (0, 0)),
                  pl.BlockSpec(memory_space=pl.ANY)],
        out_specs=rows,
        out_shape=jax.ShapeDtypeStruct(out.shape, out.dtype),
        input_output_aliases={4: 0},
        compiler_params=_params("parallel", "arbitrary"),
        name="stickbreak_prompt",
    )(q, k, v, _sb_tri(SB_KEYS), out)


def _sb_sample(q, k, v, cache_k, cache_v, layer, sp, ls, out):
    n_layers, bs, past, heads, _ = cache_k.shape
    hg = 8
    assert past % SB_KEYS == 0 and past >= SB_KEYS and sp % ls == 0 and heads % hg == 0
    row0 = sp // ls
    gw = hg * HEAD_DIM
    blk_new = pl.BlockSpec((ls, gw), lambda b, g: (row0 + b, g))
    blk_kv = pl.BlockSpec((ls, gw), lambda b, g: (b, g))
    blk_old = pl.BlockSpec((None, None, past, None, hg, HEAD_DIM), lambda b, g: (layer, b, 0, g, 0, 0))
    grouped = (n_layers, bs, past, heads // hg, hg, HEAD_DIM)
    return pl.pallas_call(
        _ignore_input(_sb_sample_kernel, 7),
        grid=(bs, heads // hg),
        in_specs=[blk_new, blk_kv, blk_kv, blk_old, blk_old,
                  pl.BlockSpec((ls, ls + HEAD_DIM), lambda b, g: (0, 0)),
                  pl.BlockSpec((SB_KEYS, SB_KEYS + HEAD_DIM), lambda b, g: (0, 0)),
                  pl.BlockSpec(memory_space=pl.ANY)],
        out_specs=blk_new,
        out_shape=jax.ShapeDtypeStruct(out.shape, out.dtype),
        input_output_aliases={7: 0},
        compiler_params=_params("parallel", "arbitrary"),
        name="stickbreak_sample",
    )(q, k, v, cache_k.reshape(grouped), cache_v.reshape(grouped), _sb_tri(ls), _sb_tri(SB_KEYS), out)


def _peer_scores_kernel(x_ref, w_ref, key_ref, o_ref):
    half = key_ref.shape[2]
    pq = _dot(x_ref[...], w_ref[...]).astype(BF16)
    for c in range(key_ref.shape[0]):
        o_ref[c] = _dot_nt(key_ref[c], pq[:, c * half:(c + 1) * half])


def _peer_scores(xn, w_q, sub_keys, layer):
    t, d = xn.shape
    _, nhc, nkeys, half = sub_keys.shape
    tm = _tile(t, 1024)
    per = 4
    assert nhc % per == 0
    return pl.pallas_call(
        _peer_scores_kernel,
        grid=(t // tm, nhc // per),
        in_specs=[pl.BlockSpec((tm, d), lambda i, j: (i, 0)),
                  pl.BlockSpec((None, d, per * half), lambda i, j: (layer, 0, j)),
                  pl.BlockSpec((None, per, nkeys, half), lambda i, j: (layer, j, 0, 0))],
        out_specs=pl.BlockSpec((per, nkeys, tm), lambda i, j: (j, 0, i)),
        out_shape=jax.ShapeDtypeStruct((nhc, nkeys, t), F32),
        compiler_params=_params("parallel", "parallel"),
        name="peer_scores",
    )(xn, w_q, sub_keys)


def _extract_top(s, k, tie_safe):
    rows = lax.broadcasted_iota(jnp.int32, s.shape, 0)
    rank = jnp.full(s.shape, float(k), F32)
    vals = []
    for r in range(k):
        m = jnp.max(s, axis=0, keepdims=True)
        vals.append(m)
        hit = s == m
        if tie_safe:
            hit = rows == jnp.min(jnp.where(hit, rows, s.shape[0]), axis=0, keepdims=True)
        rank = jnp.where(hit, float(r), rank)
        s = jnp.where(hit, -jnp.inf, s)
    return vals, rank, s


def _peer_select_kernel(s_ref, m1x_ref, e1x_ref, rank2_ref, e2_ref, *, ne):
    def select(h, tie_safe):
        s1 = s_ref[2 * h]
        s2 = s_ref[2 * h + 1]
        a, rank1, _ = _extract_top(s1, PEER_TOPK, tie_safe)
        b, rank2, _ = _extract_top(s2, PEER_TOPK, tie_safe)
        b = jnp.concatenate(b, axis=0)
        sub = lax.broadcasted_iota(jnp.int32, (8, b.shape[1]), 0)
        cand = [a[0] + b, a[1] + b[:8]]
        for i in range(2, 8):
            cand.append(jnp.where(sub < PEER_TOPK // (i + 1), a[i] + b[:8], -jnp.inf))
        cand.append(jnp.concatenate(a[8:], axis=0) + b[0:1])
        cand = jnp.concatenate(cand, axis=0)
        g, _, left = _extract_top(cand, PEER_TOPK, tie_safe)
        z = jnp.ones_like(g[0])
        for gk in g[1:]:
            z = z + jnp.exp(gk - g[0])
        took = (left != cand).astype(F32)
        taken = [jnp.sum(took[0:16], axis=0, keepdims=True), jnp.sum(took[16:24], axis=0, keepdims=True)]
        taken += [jnp.sum(took[8 * i + 8:8 * i + 16], axis=0, keepdims=True) for i in range(2, 8)]
        taken += [took[72 + i:73 + i] for i in range(8)]
        m1 = jnp.zeros_like(s1)
        for i in range(PEER_TOPK):
            m1 = jnp.where(rank1 == float(i), taken[i], m1)
        e1 = jnp.exp(s1 - a[0]) / z
        rank2_ref[h] = rank2.astype(rank2_ref.dtype)
        e2_ref[h] = jnp.exp(s2 - b[0:1]).astype(e2_ref.dtype)
        row = pl.multiple_of(h * ne, ne)
        for j in range(N_KEYS // ne):
            m1x_ref[j, pl.ds(row, ne), :] = m1[j * ne:(j + 1) * ne]
            e1x_ref[j, pl.ds(row, ne), :] = e1[j * ne:(j + 1) * ne]
        removed = (jnp.sum((rank1 < PEER_TOPK).astype(F32), axis=0, keepdims=True)
                   + jnp.sum((rank2 < PEER_TOPK).astype(F32), axis=0, keepdims=True)
                   + jnp.sum(took, axis=0, keepdims=True))
        return jnp.max(removed) - 3.0 * PEER_TOPK

    def one_head(h, _):
        extra = select(h, tie_safe=False)

        @pl.when(extra > 0.0)
        def _():
            select(h, tie_safe=True)

        return 0

    lax.fori_loop(0, PEER_HEADS, one_head, 0)


def _peer_select(scores, ne):
    nhc, nkeys, t = scores.shape
    tt = _tile(t, 512)
    x_shape = jax.ShapeDtypeStruct((nkeys // ne, PEER_HEADS * ne, t), F32)
    x_spec = pl.BlockSpec((nkeys // ne, PEER_HEADS * ne, tt), lambda i: (0, 0, i))
    y_shape = jax.ShapeDtypeStruct((PEER_HEADS, nkeys, t), BF16)
    y_spec = pl.BlockSpec((PEER_HEADS, nkeys, tt), lambda i: (0, 0, i))
    return pl.pallas_call(
        functools.partial(_peer_select_kernel, ne=ne),
        grid=(t // tt,),
        in_specs=[pl.BlockSpec((nhc, nkeys, tt), lambda i: (0, 0, i))],
        out_specs=[x_spec, x_spec, y_spec, y_spec],
        out_shape=[x_shape, x_shape, y_shape, y_shape],
        compiler_params=_params("parallel"),
        name="peer_select",
    )(scores)


def _peer_act_kernel(*refs, ne, n_parts):
    u_ref, xt_ref = refs[0], refs[1]
    part_refs = refs[2:2 + 4 * n_parts]
    o_ref, gate_ref = refs[2 + 4 * n_parts], refs[3 + 4 * n_parts]
    first_row = (pl.program_id(1) % (ROW_BLOCK // ne)) * ne
    for part in range(n_parts):
        m1x_ref, e1x_ref, rank2_ref, e2_ref = part_refs[4 * part:4 * part + 4]
        tp = rank2_ref.shape[2]
        for e in range(ne):
            gate = jnp.zeros((N_KEYS, tp), BF16)
            for h in range(PEER_HEADS):
                r = pl.ds(h * ROW_BLOCK + first_row + e, 1)
                hit = rank2_ref[h] < m1x_ref[r, :].astype(BF16)
                w = e2_ref[h] * e1x_ref[r, :].astype(BF16)
                gate = gate + jnp.where(hit, w, jnp.zeros_like(w))
            gate_ref[e * N_KEYS:(e + 1) * N_KEYS, part * tp:(part + 1) * tp] = gate
    xt = xt_ref[...]
    for p in range(ne // 2):
        rows = slice(2 * p * N_KEYS, 2 * (p + 1) * N_KEYS)
        ht = _dot(u_ref[rows, :].astype(BF16), xt)
        gelu = 0.5 * ht * (1.0 + lax.erf(ht * (0.5 ** 0.5)))
        act_t = gate_ref[rows, :] * gelu.astype(BF16)
        o_ref[:, rows] = act_t.T


def _peer_act(xt, u, layer, m1x, e1x, rank2, e2, ne):
    d, t = xt.shape
    n_exp = u.shape[1]
    tp = _tile(t, 512)
    n_parts = 2 if t % (2 * tp) == 0 else 1
    tt = n_parts * tp
    te = ne * N_KEYS
    once = pl.Buffered(1)
    part_specs, part_args = [], []
    for part in range(n_parts):
        row_spec = pl.BlockSpec((None, PEER_HEADS * ROW_BLOCK, tp),
                                lambda i, j, part=part: (j // (ROW_BLOCK // ne), 0, n_parts * i + part))
        tile_spec = pl.BlockSpec((PEER_HEADS, N_KEYS, tp), lambda i, j, part=part: (0, 0, n_parts * i + part),
                                 pipeline_mode=once)
        part_specs += [row_spec, row_spec, tile_spec, tile_spec]
        part_args += [m1x, e1x, rank2, e2]
    return pl.pallas_call(
        functools.partial(_peer_act_kernel, ne=ne, n_parts=n_parts),
        grid=(t // tt, n_exp // te),
        in_specs=[pl.BlockSpec((None, te, d), lambda i, j: (layer, j, 0)),
                  pl.BlockSpec((d, tt), lambda i, j: (0, i), pipeline_mode=once)] + part_specs,
        out_specs=pl.BlockSpec((tt, te), lambda i, j: (i, j)),
        out_shape=jax.ShapeDtypeStruct((t, n_exp), BF16),
        scratch_shapes=[pltpu.VMEM((te, tt), BF16)],
        compiler_params=_params("parallel", "arbitrary"),
        name="peer_act",
    )(u, xt, *part_args)


def _peer_ffn(h, layer, norm_g, w_q, sub_keys, u, v, split_at=None):
    ne = 4
    xn, xn_t = _rmsnorm(h, norm_g, with_transpose=True)
    scores = _peer_scores(xn, w_q, sub_keys, layer)
    m1x, e1x, rank2, e2 = _peer_select(scores, ROW_BLOCK)
    act = _peer_act(xn_t, u, layer, m1x, e1x, rank2, e2, ne)
    down = functools.partial(_matmul, act, v, layer, mode="residual", aux=h, tm=1024, tn=1024, tk=2048)
    if split_at is None:
        return down(name="peer_down")
    return (down(row0=0, m=split_at, name="peer_down_head"),
            down(row0=split_at, m=h.shape[0] - split_at, name="peer_down_tail"))


def kernel(x_prompt, x_sample, cache_a_k, cache_a_v, state_b_pool, cache_c_k, cache_c_v, norm_mix, norm_ffn, ab_w_in, ab_q_gain, ab_k_gain, ab_rel_bias, ab_pool_w, ab_pool_scale, ab_w_out, c_w_in, c_w_out, peer_w_q, peer_sub_keys, peer_u, peer_v):
    bp, sp, d = x_prompt.shape
    bs, ls, _ = x_sample.shape
    assert bp == 1
    depth = norm_mix.shape[0]
    past_len = cache_c_k.shape[2]
    a_heads = ab_rel_bias.shape[1]
    a_width = a_heads * HEAD_DIM
    b_width = ab_pool_scale.shape[1]
    c_width = c_w_out.shape[1]
    win_p = min(A_PAST_CHUNKS * CHUNK, sp)
    n_s = bs * ls

    h = jnp.concatenate([x_prompt.reshape(sp, d), x_sample.reshape(n_s, d)], axis=0)

    ab_w_in_b, ab_w_out_b = ab_w_in.astype(BF16), ab_w_out.astype(BF16)
    c_w_in_b, c_w_out_b = c_w_in.astype(BF16), c_w_out.astype(BF16)
    peer_w_q_b, peer_v_b = peer_w_q.astype(BF16), peer_v.astype(BF16)
    sub_keys_b = peer_sub_keys.astype(BF16).reshape(depth, PEER_HEADS * 2, N_KEYS, -1)

    outs = {name: [] for name in ("a_k_p", "a_v_p", "b_p", "c_k_p", "c_v_p",
                                  "a_k_s", "a_v_s", "b_s", "c_k_s", "c_v_s")}
    for layer in range(depth):
        xn = _rmsnorm(h, norm_mix[layer])
        if layer % 2 == 0:
            i = layer // 2
            q = _matmul(xn, ab_w_in_b, i, col0=0, n=a_width, out_dtype=BF16, mode="headnorm",
                        aux=ab_q_gain[i], name="a_q")
            k = _matmul(xn, ab_w_in_b, i, col0=a_width, n=a_width, mode="headnorm", aux=ab_k_gain[i], name="a_k")
            v = _matmul(xn, ab_w_in_b, i, col0=2 * a_width, n=a_width, name="a_v")
            p = _matmul(xn, ab_w_in_b, i, col0=3 * a_width, n=b_width, name="b_in")

            att = jnp.zeros((sp + n_s, a_width), BF16)
            att = _band_prompt(q, k, v, ab_rel_bias[i], sp, att)
            att = _band_sample(q, k, v, cache_a_k, cache_a_v, i, ab_rel_bias[i], sp, ls, past_len, att)

            pool_w = ab_pool_w[i].astype(BF16)
            tr = _tile(sp, 256)
            pool = jnp.zeros((sp + n_s, b_width), BF16)
            pool = _pool(p, None, pool_w, ab_pool_scale[i], pool, row_block0=0, nblk=sp // tr, tr=tr,
                         pos_base=0, pos_step=tr, name="pool_prompt")
            hist_s = jnp.pad(state_b_pool[i], ((0, 0), (POOL_HIST_PAD - POOL_HIST, 0), (0, 0)))
            pool = _pool(p, hist_s, pool_w, ab_pool_scale[i], pool, row_block0=sp // ls, nblk=bs, tr=ls,
                         pos_base=past_len, pos_step=0, name="pool_sample")

            h = _matmul(att, ab_w_out_b, i, x2=pool, mode="residual", aux=h, name="a_out")

            ps = p[sp:].reshape(bs, ls, b_width)
            outs["a_k_p"].append(k[sp - win_p:sp].reshape(bp, win_p, a_heads, HEAD_DIM))
            outs["a_v_p"].append(v[sp - win_p:sp].reshape(bp, win_p, a_heads, HEAD_DIM))
            outs["b_p"].append(p[sp - POOL_HIST:sp].reshape(bp, POOL_HIST, b_width))
            outs["a_k_s"].append(k[sp:].reshape(bs, ls, a_heads, HEAD_DIM))
            outs["a_v_s"].append(v[sp:].reshape(bs, ls, a_heads, HEAD_DIM))
            outs["b_s"].append(jnp.concatenate([state_b_pool[i], ps], axis=1)[:, ls:])
        else:
            j = layer // 2
            c_heads = c_width // HEAD_DIM
            q = _matmul(xn, c_w_in_b, j, col0=0, n=c_width, out_dtype=BF16, name="c_q")
            k_p, k_p3 = _proj_heads(xn, c_w_in_b, j, col0=c_width, n=c_width, row0=0, m=sp, name="c_k_prompt")
            v_p, v_p3 = _proj_heads(xn, c_w_in_b, j, col0=2 * c_width, n=c_width, row0=0, m=sp, name="c_v_prompt")
            k_s, k_s3 = _proj_heads(xn, c_w_in_b, j, col0=c_width, n=c_width, row0=sp, m=n_s, name="c_k_sample")
            v_s, v_s3 = _proj_heads(xn, c_w_in_b, j, col0=2 * c_width, n=c_width, row0=sp, m=n_s, name="c_v_sample")
            att = jnp.zeros((sp + n_s, c_width), BF16)
            att = _sb_prompt(q, k_p, v_p, sp, att)
            att = _sb_sample(q, k_s, v_s, cache_c_k, cache_c_v, j, sp, ls, att)
            h = _matmul(att, c_w_out_b, j, mode="residual", aux=h, name="c_out")
            outs["c_k_p"].append(k_p3.reshape(bp, sp, c_heads, HEAD_DIM))
            outs["c_v_p"].append(v_p3.reshape(bp, sp, c_heads, HEAD_DIM))
            outs["c_k_s"].append(k_s3.reshape(bs, ls, c_heads, HEAD_DIM))
            outs["c_v_s"].append(v_s3.reshape(bs, ls, c_heads, HEAD_DIM))

        h = _peer_ffn(h, layer, norm_ffn[layer], peer_w_q_b, sub_keys_b, peer_u, peer_v_b,
                      split_at=sp if layer == depth - 1 else None)

    h_prompt, h_sample = h
    st = {name: jnp.stack(vals) for name, vals in outs.items()}
    return (h_prompt.reshape(bp, sp, d), h_sample.reshape(bs, ls, d),
            st["a_k_p"], st["a_v_p"], st["b_p"], st["c_k_p"], st["c_v_p"],
            st["a_k_s"], st["a_v_s"], st["b_s"], st["c_k_s"], st["c_v_s"])
```

```python
import functools

import jax
import jax.numpy as jnp
from jax import lax
from jax.experimental import pallas as pl
from jax.experimental.pallas import tpu as pltpu

F32 = jnp.float32
BF16 = jnp.bfloat16

HEAD_DIM = 128
CHUNK = 64
A_PAST_CHUNKS = 8
REL_CLIP = 128
POOL_WINDOWS = (2, 4, 8, 16)
POOL_HIST = max(POOL_WINDOWS) - 1
POOL_HIST_PAD = 16
B_GROUP_WIDTH = 512
PEER_HEADS = 8
N_KEYS = 128
PEER_TOPK = 16
ROW_BLOCK = 8
EPS = 1e-6
NEG_INF = -1e30
ATTN_SCALE = HEAD_DIM ** -0.5
SB_EXIT = -104.0
SB_KEYS = 128
VMEM_LIMIT_BYTES = 56 * 1024 * 1024


def _params(*sem):
    return pltpu.CompilerParams(dimension_semantics=sem, vmem_limit_bytes=VMEM_LIMIT_BYTES)


def _tile(dim, pref):
    return pref if dim % pref == 0 else dim


def _ignore_input(body, index):
    def wrapped(*refs):
        return body(*refs[:index], *refs[index + 1:])
    return wrapped


def _dot(a, b):
    return jnp.dot(a, b, preferred_element_type=F32)


def _dot_nt(a, b):
    return lax.dot_general(a, b, (((1,), (1,)), ((), ())), preferred_element_type=F32)


def _rmsnorm_kernel(x_ref, g_ref, o_ref, *transposed_ref):
    x = x_ref[...]
    ms = jnp.mean(x * x, axis=-1, keepdims=True)
    y = (x * lax.rsqrt(ms + EPS) * g_ref[...]).astype(o_ref.dtype)
    o_ref[...] = y
    for ot_ref in transposed_ref:
        ot_ref[...] = y.T


def _rmsnorm(x, g, with_transpose=False):
    t, d = x.shape
    tr = _tile(t, 256)
    out_specs = [pl.BlockSpec((tr, d), lambda i: (i, 0))]
    out_shape = [jax.ShapeDtypeStruct((t, d), BF16)]
    if with_transpose:
        out_specs.append(pl.BlockSpec((d, tr), lambda i: (0, i)))
        out_shape.append(jax.ShapeDtypeStruct((d, t), BF16))
    outs = pl.pallas_call(
        _rmsnorm_kernel,
        grid=(t // tr,),
        in_specs=[pl.BlockSpec((tr, d), lambda i: (i, 0)),
                  pl.BlockSpec((1, d), lambda i: (0, 0))],
        out_specs=out_specs,
        out_shape=out_shape,
        compiler_params=_params("parallel"),
        name="rmsnorm",
    )(x, g.reshape(1, d))
    return outs if with_transpose else outs[0]


def _mm_kernel(*refs, nk, mode, two_x):
    x_ref, w_ref = refs[0], refs[1]
    refs = refs[2:]
    if two_x:
        x2_ref, refs = refs[0], refs[1:]
    if mode in ("headnorm", "residual"):
        aux_ref, o_ref, scratch = refs[0], refs[1], refs[2:]
    else:
        aux_ref, o_ref, scratch = None, refs[0], refs[1:]

    def finish(acc):
        if mode == "headnorm":
            g = aux_ref[...]
            for c in range(acc.shape[1] // HEAD_DIM):
                sl = slice(c * HEAD_DIM, (c + 1) * HEAD_DIM)
                y = acc[:, sl]
                ms = jnp.mean(y * y, axis=-1, keepdims=True)
                o_ref[:, sl] = (y * lax.rsqrt(ms + EPS) * g).astype(o_ref.dtype)
        elif mode == "residual":
            o_ref[...] = aux_ref[...] + acc
        else:
            o_ref[...] = acc.astype(o_ref.dtype)

    if two_x:
        k1 = x_ref.shape[1]
        finish(_dot(x_ref[...], w_ref[:k1, :]) + _dot(x2_ref[...], w_ref[k1:, :]))
    elif nk == 1:
        finish(_dot(x_ref[...], w_ref[...]))
    else:
        acc_ref = scratch[0]
        k = pl.program_id(2)

        @pl.when(k == 0)
        def _():
            acc_ref[...] = jnp.zeros_like(acc_ref)

        acc_ref[...] += _dot(x_ref[...], w_ref[...])

        @pl.when(k == nk - 1)
        def _():
            finish(acc_ref[...])


def _matmul(x, w, layer, *, x2=None, col0=0, n=None, row0=0, m=None, out_dtype=F32, mode="plain", aux=None,
            tm=1024, tn=512, tk=None, name="matmul"):
    kdim = x.shape[1]
    m = x.shape[0] if m is None else m
    if x2 is not None:
        assert tk is None
        kdim += x2.shape[1]
    n = w.shape[2] if n is None else n
    tm, tn = _tile(m, tm), _tile(n, tn)
    tk = kdim if tk is None else _tile(kdim, tk)
    nk = kdim // tk
    rb, cb = row0 // tm, col0 // tn
    assert row0 % tm == 0 and col0 % tn == 0
    in_specs = [pl.BlockSpec((tm, x.shape[1] if x2 is not None else tk), lambda i, j, k: (i + rb, k)),
                pl.BlockSpec((None, tk, tn), lambda i, j, k: (layer, k, j + cb))]
    args = [x, w]
    if x2 is not None:
        in_specs.append(pl.BlockSpec((tm, x2.shape[1]), lambda i, j, k: (i + rb, 0)))
        args.append(x2)
    if mode == "headnorm":
        in_specs.append(pl.BlockSpec((1, HEAD_DIM), lambda i, j, k: (0, 0)))
        args.append(aux.reshape(1, HEAD_DIM))
    elif mode == "residual":
        in_specs.append(pl.BlockSpec((tm, tn), lambda i, j, k: (i + rb, j)))
        args.append(aux)
    scratch = [pltpu.VMEM((tm, tn), F32)] if nk > 1 else []
    return pl.pallas_call(
        functools.partial(_mm_kernel, nk=nk, mode=mode, two_x=x2 is not None),
        grid=(m // tm, n // tn, nk),
        in_specs=in_specs,
        out_specs=pl.BlockSpec((tm, tn), lambda i, j, k: (i, j)),
        out_shape=jax.ShapeDtypeStruct((m, n), out_dtype),
        scratch_shapes=scratch,
        compiler_params=_params("parallel", "parallel", "arbitrary"),
        name=name,
    )(*args)


def _proj_heads_kernel(x_ref, w_ref, o2_ref, o3_ref):
    acc = _dot(x_ref[...], w_ref[...])
    o2_ref[...] = acc.astype(o2_ref.dtype)
    o3_ref[...] = pltpu.einshape("m(hd)->mhd", acc, h=o3_ref.shape[1])


def _proj_heads(x, w, layer, *, col0, n, row0, m, name):
    kdim = x.shape[1]
    tm, tn = _tile(m, 1024), 8 * HEAD_DIM
    assert row0 % tm == 0 and col0 % tn == 0 and n % tn == 0
    rb, cb = row0 // tm, col0 // tn
    return pl.pallas_call(
        _proj_heads_kernel,
        grid=(m // tm, n // tn),
        in_specs=[pl.BlockSpec((tm, kdim), lambda i, j: (i + rb, 0)),
                  pl.BlockSpec((None, kdim, tn), lambda i, j: (layer, 0, j + cb))],
        out_specs=[pl.BlockSpec((tm, tn), lambda i, j: (i, j)),
                   pl.BlockSpec((tm, tn // HEAD_DIM, HEAD_DIM), lambda i, j: (i, j, 0))],
        out_shape=[jax.ShapeDtypeStruct((m, n), BF16),
                   jax.ShapeDtypeStruct((m, n // HEAD_DIM, HEAD_DIM), F32)],
        compiler_params=_params("parallel", "parallel"),
        name=name,
    )(x, w)


def _band_head(q, ka, kb, va, vb, bias_a, bias_b, past_visible):
    sa = _dot_nt(q, ka.astype(BF16)) * ATTN_SCALE + bias_a
    sb = _dot_nt(q, kb.astype(BF16)) * ATTN_SCALE + bias_b
    if past_visible is not None:
        sa = jnp.where(past_visible, sa, NEG_INF)
    m = jnp.maximum(jnp.max(sa, axis=-1, keepdims=True), jnp.max(sb, axis=-1, keepdims=True))
    pa = jnp.exp(sa - m)
    pb = jnp.exp(sb - m)
    l = jnp.sum(pa, axis=-1, keepdims=True) + jnp.sum(pb, axis=-1, keepdims=True)
    o = _dot(pa.astype(BF16), va.astype(BF16)) + _dot(pb.astype(BF16), vb.astype(BF16))
    return o / l


def _band_prompt_kernel(q_ref, ka_ref, kb_ref, va_ref, vb_ref, bias_ref, o_ref):
    tq = q_ref.shape[0]
    half = tq // 2
    past = pl.program_id(1) > 0
    lo = _band_head(q_ref[:half, :], ka_ref[...], kb_ref[:half, :], va_ref[...], vb_ref[:half, :],
                    bias_ref[:half, :tq], bias_ref[:half, tq:tq + half], past)
    o_ref[:half, :] = lo.astype(o_ref.dtype)
    hi = _band_head(q_ref[half:, :], ka_ref[half:, :], kb_ref[...], va_ref[half:, :], vb_ref[...],
                    bias_ref[half:, half:tq], bias_ref[half:, tq:], past)
    o_ref[half:, :] = hi.astype(o_ref.dtype)


def _band_sample_kernel(q_ref, kn_ref, vn_ref, kc_ref, vc_ref, bias_ref, o_ref):
    win, heads = kc_ref.shape[0], kc_ref.shape[1]
    kc = pltpu.einshape("phd->hpd", kc_ref[...])
    vc = pltpu.einshape("phd->hpd", vc_ref[...])
    for hh in range(heads):
        sl = slice(hh * HEAD_DIM, (hh + 1) * HEAD_DIM)
        o = _band_head(q_ref[:, sl], kc[hh], kn_ref[:, sl], vc[hh], vn_ref[:, sl],
                       bias_ref[hh, :, :win], bias_ref[hh, :, win:], None)
        o_ref[:, sl] = o.astype(o_ref.dtype)


def _band_bias(rel_bias, q_pos, k_pos):
    lq, lk = q_pos.shape[0], k_pos.shape[0]
    d_min = q_pos[0] - k_pos[lk - 1]
    n_diag = lq + lk - 1
    diag = jnp.clip(d_min + jnp.arange(n_diag, dtype=jnp.int32), -REL_CLIP, REL_CLIP) + REL_CLIP
    g = rel_bias.astype(F32)[:, diag]
    row_len = -(-n_diag // HEAD_DIM) * HEAD_DIM
    period = row_len + 1
    u = jnp.pad(g[:, ::-1], ((0, 0), (0, period - n_diag)))
    skew = jnp.tile(u, (1, lq))[:, :lq * row_len].reshape(-1, lq, row_len)
    bias = skew[:, :, lq - 1:lq - 1 + lk]
    qc = q_pos[:, None] // CHUNK
    kc = k_pos[None, :] // CHUNK
    mask = (kc <= qc) & (kc >= qc - A_PAST_CHUNKS)
    return jnp.where(mask[None], bias, NEG_INF)


def _band_prompt(q, k, v, rel_bias, sp, out):
    heads = rel_bias.shape[0]
    tq = A_PAST_CHUNKS * CHUNK
    assert sp % tq == 0
    pos = jnp.arange(tq, dtype=jnp.int32)
    bias = _band_bias(rel_bias, tq + pos, jnp.arange(2 * tq, dtype=jnp.int32))
    prev = lambda h, i: (jnp.maximum(i - 1, 0), h)
    cur = lambda h, i: (i, h)
    blk = lambda im: pl.BlockSpec((tq, HEAD_DIM), im)
    return pl.pallas_call(
        _ignore_input(_band_prompt_kernel, 6),
        grid=(heads, sp // tq),
        in_specs=[blk(cur), blk(prev), blk(cur), blk(prev), blk(cur),
                  pl.BlockSpec((None, tq, 2 * tq), lambda h, i: (h, 0, 0)),
                  pl.BlockSpec(memory_space=pl.ANY)],
        out_specs=blk(cur),
        out_shape=jax.ShapeDtypeStruct(out.shape, out.dtype),
        input_output_aliases={6: 0},
        compiler_params=_params("parallel", "arbitrary"),
        name="band_prompt",
    )(q, k, k, v, v, bias, out)


def _band_sample(q, k, v, cache_k, cache_v, layer, rel_bias, sp, ls, past_len, out):
    _, bs, win, heads, _ = cache_k.shape
    width = heads * HEAD_DIM
    assert sp % ls == 0
    row0 = sp // ls
    q_pos = past_len + jnp.arange(ls, dtype=jnp.int32)
    k_pos = past_len - win + jnp.arange(win + ls, dtype=jnp.int32)
    bias = _band_bias(rel_bias, q_pos, k_pos)
    blk_new = pl.BlockSpec((ls, width), lambda b: (row0 + b, 0))
    blk_old = pl.BlockSpec((None, None, win, heads, HEAD_DIM), lambda b: (layer, b, 0, 0, 0))
    return pl.pallas_call(
        _ignore_input(_band_sample_kernel, 6),
        grid=(bs,),
        in_specs=[blk_new, blk_new, blk_new, blk_old, blk_old,
                  pl.BlockSpec((heads, ls, win + ls), lambda b: (0, 0, 0)),
                  pl.BlockSpec(memory_space=pl.ANY)],
        out_specs=blk_new,
        out_shape=jax.ShapeDtypeStruct(out.shape, out.dtype),
        input_output_aliases={6: 0},
        compiler_params=_params("parallel"),
        name="band_sample",
    )(q, k, v, cache_k, cache_v, bias, out)


def _pool_kernel(cur_ref, hist_ref, w_ref, sc_ref, o_ref, ext_ref, *, pos_base, pos_step, first_has_no_past):
    tr = cur_ref.shape[0]
    hist = hist_ref[...]
    if first_has_no_past:
        hist = jnp.where(pl.program_id(0) > 0, hist, 0.0)
    ext_ref[0:POOL_HIST_PAD, :] = hist
    ext_ref[POOL_HIST_PAD:POOL_HIST_PAD + tr, :] = cur_ref[...]
    pos = pos_base + pl.program_id(0) * pos_step + lax.broadcasted_iota(jnp.int32, (tr, 1), 0)
    for g, w in enumerate(POOL_WINDOWS):
        sl = slice(g * B_GROUP_WIDTH, (g + 1) * B_GROUP_WIDTH)
        cur = cur_ref[:, sl]
        tot = cur
        for j in range(1, w):
            tot = tot + ext_ref[POOL_HIST_PAD - j:POOL_HIST_PAD - j + tr, sl]
        cnt = jnp.minimum(pos + 1, w).astype(F32)
        d = tot / cnt - cur
        y = _dot(d.astype(BF16), w_ref[g]) * sc_ref[:, sl]
        o_ref[:, sl] = y.astype(o_ref.dtype)


def _pool(p, hist, pool_w, pool_scale, out, *, row_block0, nblk, tr, pos_base, pos_step, name):
    width = p.shape[1]
    rows = pl.BlockSpec((tr, width), lambda i: (row_block0 + i, 0))
    body = functools.partial(_pool_kernel, pos_base=pos_base, pos_step=pos_step, first_has_no_past=hist is None)
    if hist is None:
        assert row_block0 == 0 and tr % POOL_HIST_PAD == 0
        per = tr // POOL_HIST_PAD
        hist_spec = pl.BlockSpec((POOL_HIST_PAD, width), lambda i: (jnp.maximum(i * per - 1, 0), 0))
        hist = p
    else:
        hist_spec = pl.BlockSpec((None, POOL_HIST_PAD, width), lambda i: (i, 0, 0))
    return pl.pallas_call(
        _ignore_input(body, 4),
        grid=(nblk,),
        in_specs=[rows, hist_spec,
                  pl.BlockSpec(pool_w.shape, lambda i: (0, 0, 0)),
                  pl.BlockSpec((1, width), lambda i: (0, 0)),
                  pl.BlockSpec(memory_space=pl.ANY)],
        out_specs=rows,
        out_shape=jax.ShapeDtypeStruct(out.shape, out.dtype),
        input_output_aliases={4: 0},
        scratch_shapes=[pltpu.VMEM((POOL_HIST_PAD + tr, width), F32)],
        compiler_params=_params("parallel"),
        name=name,
    )(p, hist, pool_w, pool_scale.reshape(1, width), out)


def _sb_steps(qs, kvs, tri, carries, accs, visibles):
    nk = kvs[0][0].shape[0]
    zs = [_dot_nt(q, k) * ATTN_SCALE for q, (k, _) in zip(qs, kvs)]
    log_keeps = [-(jnp.maximum(z, 0.0) + jnp.log(1.0 + jnp.exp(-jnp.abs(z)))) for z in zs]
    if visibles is not None:
        log_keeps = [jnp.where(vis, lk, 0.0) for lk, vis in zip(log_keeps, visibles)]
    his = [lk.astype(BF16) for lk in log_keeps]
    los = [(lk - hi.astype(F32)).astype(BF16) for lk, hi in zip(log_keeps, his)]
    sums = [_dot(hi, tri) + _dot(lo, tri) for hi, lo in zip(his, los)]
    weights = [jnp.exp(z + lk + s[:, :nk] + carry[:, :nk])
               for z, lk, s, carry in zip(zs, log_keeps, sums, carries)]
    if visibles is not None:
        weights = [jnp.where(vis, a, 0.0) for a, vis in zip(weights, visibles)]
    weights = [a.astype(BF16) for a in weights]
    new_accs = [acc + _dot(a, v) for acc, a, (_, v) in zip(accs, weights, kvs)]
    new_carries = [carry + s[:, nk:] for carry, s in zip(carries, sums)]
    return new_carries, new_accs


def _sb_sweep(chains, load_kv, tri, carries, accs):
    rows = chains[0][0].shape[0]

    def cond(st):
        return st[1] == 0

    def body(st):
        n, _, carries, accs = st
        kbs = [kb0 - n for _, kb0 in chains]
        starts = [pl.multiple_of(jnp.maximum(kb, 0) * SB_KEYS, SB_KEYS) for kb in kbs]
        carries = [jnp.where(kb >= 0, carry, -jnp.inf) for kb, carry in zip(kbs, carries)]
        carries, accs = _sb_steps([q for q, _ in chains], load_kv(starts), tri, carries, accs, None)
        pending = jnp.full((rows, HEAD_DIM), -jnp.inf, F32)
        for kb, carry in zip(kbs, carries):
            pending = jnp.maximum(pending, jnp.where(kb > 0, carry, -jnp.inf))
        done = (jnp.max(pending) < SB_EXIT).astype(jnp.int32)
        return n + 1, done, carries, accs

    return lax.while_loop(cond, body, (jnp.int32(0), jnp.int32(0), list(carries), list(accs)))[3]


def _sb_prompt_kernel(q_ref, k_ref, v_ref, tri_ref, o_ref, *, tq):
    n_chains = q_ref.shape[0] // tq
    block0 = pl.program_id(1) * n_chains
    causal = lax.broadcasted_iota(jnp.int32, (tq, tq), 1) < lax.broadcasted_iota(jnp.int32, (tq, tq), 0)

    def load_kv(starts):
        return [(k_ref[pl.ds(s, SB_KEYS), :].astype(BF16), v_ref[pl.ds(s, SB_KEYS), :].astype(BF16))
                for s in starts]

    qs = [q_ref[c * tq:(c + 1) * tq, :] for c in range(n_chains)]
    own = [pl.multiple_of((block0 + c) * SB_KEYS, SB_KEYS) for c in range(n_chains)]
    zeros = [jnp.zeros((tq, HEAD_DIM), F32)] * n_chains
    tri = tri_ref[...]
    carries, accs = _sb_steps(qs, load_kv(own), tri, zeros, zeros, [causal] * n_chains)
    chains = [(q, block0 + c - 1) for c, q in enumerate(qs)]
    accs = _sb_sweep(chains, load_kv, tri, carries, accs)
    for c in range(n_chains):
        o_ref[c * tq:(c + 1) * tq, :] = accs[c].astype(o_ref.dtype)


def _sb_sample_kernel(q_ref, kn_ref, vn_ref, kc_ref, vc_ref, trin_ref, tri_ref, o_ref):
    ls = q_ref.shape[0]
    past, heads = kc_ref.shape[0], kc_ref.shape[1]
    causal = lax.broadcasted_iota(jnp.int32, (ls, ls), 1) < lax.broadcasted_iota(jnp.int32, (ls, ls), 0)
    zeros = jnp.zeros((ls, HEAD_DIM), F32)
    head_cols = [slice(hh * HEAD_DIM, (hh + 1) * HEAD_DIM) for hh in range(heads)]
    chains = [(q_ref[:, sl], past // SB_KEYS - 1) for sl in head_cols]
    carries, accs = _sb_steps([q for q, _ in chains],
                              [(kn_ref[:, sl].astype(BF16), vn_ref[:, sl].astype(BF16)) for sl in head_cols],
                              trin_ref[...], [zeros] * heads, [zeros] * heads, [causal] * heads)

    def load_kv(starts):
        k = pltpu.einshape("phd->hpd", kc_ref[pl.ds(starts[0], SB_KEYS), :, :])
        v = pltpu.einshape("phd->hpd", vc_ref[pl.ds(starts[0], SB_KEYS), :, :])
        return [(k[hh].astype(BF16), v[hh].astype(BF16)) for hh in range(heads)]

    accs = _sb_sweep(chains, load_kv, tri_ref[...], carries, accs)
    for hh in range(heads):
        o_ref[:, hh * HEAD_DIM:(hh + 1) * HEAD_DIM] = accs[hh].astype(o_ref.dtype)


def _sb_tri(nk):
    j = jnp.arange(nk)[:, None]
    s = jnp.arange(nk + HEAD_DIM)[None, :]
    return ((s >= nk) | (j > s)).astype(BF16)


def _sb_prompt(q, k, v, sp, out):
    heads = q.shape[1] // HEAD_DIM
    tq = SB_KEYS
    tb = _tile(sp, 32 * tq)
    assert sp % SB_KEYS == 0 and tb % tq == 0
    rows = pl.BlockSpec((tb, HEAD_DIM), lambda h, i: (i, h))
    return pl.pallas_call(
        _ignore_input(functools.partial(_sb_prompt_kernel, tq=tq), 4),
        grid=(heads, sp // tb),
        in_specs=[rows,
                  pl.BlockSpec((sp, HEAD_DIM), lambda h, i: (0, h)),
                  pl.BlockSpec((sp, HEAD_DIM), lambda h, i: (0, h)),
                  pl.BlockSpec((SB_KEYS, SB_KEYS + HEAD_DIM), lambda h, i: (0, 0)),
                  pl.BlockSpec(memory_space=pl.ANY)],
        out_specs=rows,
        out_shape=jax.ShapeDtypeStruct(out.shape, out.dtype),
        input_output_aliases={4: 0},
        compiler_params=_params("parallel", "arbitrary"),
        name="stickbreak_prompt",
    )(q, k, v, _sb_tri(SB_KEYS), out)


def _sb_sample(q, k, v, cache_k, cache_v, layer, sp, ls, out):
    n_layers, bs, past, heads, _ = cache_k.shape
    hg = 8
    assert past % SB_KEYS == 0 and past >= SB_KEYS and sp % ls == 0 and heads % hg == 0
    row0 = sp // ls
    gw = hg * HEAD_DIM
    blk_new = pl.BlockSpec((ls, gw), lambda b, g: (row0 + b, g))
    blk_kv = pl.BlockSpec((ls, gw), lambda b, g: (b, g))
    blk_old = pl.BlockSpec((None, None, past, None, hg, HEAD_DIM), lambda b, g: (layer, b, 0, g, 0, 0))
    grouped = (n_layers, bs, past, heads // hg, hg, HEAD_DIM)
    return pl.pallas_call(
        _ignore_input(_sb_sample_kernel, 7),
        grid=(bs, heads // hg),
        in_specs=[blk_new, blk_kv, blk_kv, blk_old, blk_old,
                  pl.BlockSpec((ls, ls + HEAD_DIM), lambda b, g: (0, 0)),
                  pl.BlockSpec((SB_KEYS, SB_KEYS + HEAD_DIM), lambda b, g: (0, 0)),
                  pl.BlockSpec(memory_space=pl.ANY)],
        out_specs=blk_new,
        out_shape=jax.ShapeDtypeStruct(out.shape, out.dtype),
        input_output_aliases={7: 0},
        compiler_params=_params("parallel", "arbitrary"),
        name="stickbreak_sample",
    )(q, k, v, cache_k.reshape(grouped), cache_v.reshape(grouped), _sb_tri(ls), _sb_tri(SB_KEYS), out)


def _peer_scores_kernel(x_ref, w_ref, key_ref, o_ref):
    half = key_ref.shape[2]
    pq = _dot(x_ref[...], w_ref[...]).astype(BF16)
    for c in range(key_ref.shape[0]):
        o_ref[c] = _dot_nt(key_ref[c], pq[:, c * half:(c + 1) * half])


def _peer_scores(xn, w_q, sub_keys, layer):
    t, d = xn.shape
    _, nhc, nkeys, half = sub_keys.shape
    tm = _tile(t, 1024)
    per = 4
    assert nhc % per == 0
    return pl.pallas_call(
        _peer_scores_kernel,
        grid=(t // tm, nhc // per),
        in_specs=[pl.BlockSpec((tm, d), lambda i, j: (i, 0)),
                  pl.BlockSpec((None, d, per * half), lambda i, j: (layer, 0, j)),
                  pl.BlockSpec((None, per, nkeys, half), lambda i, j: (layer, j, 0, 0))],
        out_specs=pl.BlockSpec((per, nkeys, tm), lambda i, j: (j, 0, i)),
        out_shape=jax.ShapeDtypeStruct((nhc, nkeys, t), F32),
        compiler_params=_params("parallel", "parallel"),
        name="peer_scores",
    )(xn, w_q, sub_keys)


def _extract_top(s, k, tie_safe):
    rows = lax.broadcasted_iota(jnp.int32, s.shape, 0)
    rank = jnp.full(s.shape, float(k), F32)
    vals = []
    for r in range(k):
        m = jnp.max(s, axis=0, keepdims=True)
        vals.append(m)
        hit = s == m
        if tie_safe:
            hit = rows == jnp.min(jnp.where(hit, rows, s.shape[0]), axis=0, keepdims=True)
        rank = jnp.where(hit, float(r), rank)
        s = jnp.where(hit, -jnp.inf, s)
    return vals, rank, s


def _peer_select_kernel(s_ref, m1x_ref, e1x_ref, rank2_ref, e2_ref, *, ne):
    def select(h, tie_safe):
        s1 = s_ref[2 * h]
        s2 = s_ref[2 * h + 1]
        a, rank1, _ = _extract_top(s1, PEER_TOPK, tie_safe)
        b, rank2, _ = _extract_top(s2, PEER_TOPK, tie_safe)
        b = jnp.concatenate(b, axis=0)
        sub = lax.broadcasted_iota(jnp.int32, (8, b.shape[1]), 0)
        cand = [a[0] + b, a[1] + b[:8]]
        for i in range(2, 8):
            cand.append(jnp.where(sub < PEER_TOPK // (i + 1), a[i] + b[:8], -jnp.inf))
        cand.append(jnp.concatenate(a[8:], axis=0) + b[0:1])
        cand = jnp.concatenate(cand, axis=0)
        g, _, left = _extract_top(cand, PEER_TOPK, tie_safe)
        z = jnp.ones_like(g[0])
        for gk in g[1:]:
            z = z + jnp.exp(gk - g[0])
        took = (left != cand).astype(F32)
        taken = [jnp.sum(took[0:16], axis=0, keepdims=True), jnp.sum(took[16:24], axis=0, keepdims=True)]
        taken += [jnp.sum(took[8 * i + 8:8 * i + 16], axis=0, keepdims=True) for i in range(2, 8)]
        taken += [took[72 + i:73 + i] for i in range(8)]
        m1 = jnp.zeros_like(s1)
        for i in range(PEER_TOPK):
            m1 = jnp.where(rank1 == float(i), taken[i], m1)
        e1 = jnp.exp(s1 - a[0]) / z
        rank2_ref[h] = rank2.astype(rank2_ref.dtype)
        e2_ref[h] = jnp.exp(s2 - b[0:1]).astype(e2_ref.dtype)
        row = pl.multiple_of(h * ne, ne)
        for j in range(N_KEYS // ne):
            m1x_ref[j, pl.ds(row, ne), :] = m1[j * ne:(j + 1) * ne]
            e1x_ref[j, pl.ds(row, ne), :] = e1[j * ne:(j + 1) * ne]
        removed = (jnp.sum((rank1 < PEER_TOPK).astype(F32), axis=0, keepdims=True)
                   + jnp.sum((rank2 < PEER_TOPK).astype(F32), axis=0, keepdims=True)
                   + jnp.sum(took, axis=0, keepdims=True))
        return jnp.max(removed) - 3.0 * PEER_TOPK

    def one_head(h, _):
        extra = select(h, tie_safe=False)

        @pl.when(extra > 0.0)
        def _():
            select(h, tie_safe=True)

        return 0

    lax.fori_loop(0, PEER_HEADS, one_head, 0)


def _peer_select(scores, ne):
    nhc, nkeys, t = scores.shape
    tt = _tile(t, 512)
    x_shape = jax.ShapeDtypeStruct((nkeys // ne, PEER_HEADS * ne, t), F32)
    x_spec = pl.BlockSpec((nkeys // ne, PEER_HEADS * ne, tt), lambda i: (0, 0, i))
    y_shape = jax.ShapeDtypeStruct((PEER_HEADS, nkeys, t), BF16)
    y_spec = pl.BlockSpec((PEER_HEADS, nkeys, tt), lambda i: (0, 0, i))
    return pl.pallas_call(
        functools.partial(_peer_select_kernel, ne=ne),
        grid=(t // tt,),
        in_specs=[pl.BlockSpec((nhc, nkeys, tt), lambda i: (0, 0, i))],
        out_specs=[x_spec, x_spec, y_spec, y_spec],
        out_shape=[x_shape, x_shape, y_shape, y_shape],
        compiler_params=_params("parallel"),
        name="peer_select",
    )(scores)


def _peer_act_kernel(*refs, ne, n_parts):
    u_ref, xt_ref = refs[0], refs[1]
    part_refs = refs[2:2 + 4 * n_parts]
    o_ref, gate_ref = refs[2 + 4 * n_parts], refs[3 + 4 * n_parts]
    first_row = (pl.program_id(1) % (ROW_BLOCK // ne)) * ne
    for part in range(n_parts):
        m1x_ref, e1x_ref, rank2_ref, e2_ref = part_refs[4 * part:4 * part + 4]
        tp = rank2_ref.shape[2]
        for e in range(ne):
            gate = jnp.zeros((N_KEYS, tp), BF16)
            for h in range(PEER_HEADS):
                r = pl.ds(h * ROW_BLOCK + first_row + e, 1)
                hit = rank2_ref[h] < m1x_ref[r, :].astype(BF16)
                w = e2_ref[h] * e1x_ref[r, :].astype(BF16)
                gate = gate + jnp.where(hit, w, jnp.zeros_like(w))
            gate_ref[e * N_KEYS:(e + 1) * N_KEYS, part * tp:(part + 1) * tp] = gate
    xt = xt_ref[...]
    for p in range(ne // 2):
        rows = slice(2 * p * N_KEYS, 2 * (p + 1) * N_KEYS)
        ht = _dot(u_ref[rows, :].astype(BF16), xt)
        gelu = 0.5 * ht * (1.0 + lax.erf(ht * (0.5 ** 0.5)))
        act_t = gate_ref[rows, :] * gelu.astype(BF16)
        o_ref[:, rows] = act_t.T


def _peer_act(xt, u, layer, m1x, e1x, rank2, e2, ne):
    d, t = xt.shape
    n_exp = u.shape[1]
    tp = _tile(t, 512)
    n_parts = 2 if t % (2 * tp) == 0 else 1
    tt = n_parts * tp
    te = ne * N_KEYS
    once = pl.Buffered(1)
    part_specs, part_args = [], []
    for part in range(n_parts):
        row_spec = pl.BlockSpec((None, PEER_HEADS * ROW_BLOCK, tp),
                                lambda i, j, part=part: (j // (ROW_BLOCK // ne), 0, n_parts * i + part))
        tile_spec = pl.BlockSpec((PEER_HEADS, N_KEYS, tp), lambda i, j, part=part: (0, 0, n_parts * i + part),
                                 pipeline_mode=once)
        part_specs += [row_spec, row_spec, tile_spec, tile_spec]
        part_args += [m1x, e1x, rank2, e2]
    return pl.pallas_call(
        functools.partial(_peer_act_kernel, ne=ne, n_parts=n_parts),
        grid=(t // tt, n_exp // te),
        in_specs=[pl.BlockSpec((None, te, d), lambda i, j: (layer, j, 0)),
                  pl.BlockSpec((d, tt), lambda i, j: (0, i), pipeline_mode=once)] + part_specs,
        out_specs=pl.BlockSpec((tt, te), lambda i, j: (i, j)),
        out_shape=jax.ShapeDtypeStruct((t, n_exp), BF16),
        scratch_shapes=[pltpu.VMEM((te, tt), BF16)],
        compiler_params=_params("parallel", "arbitrary"),
        name="peer_act",
    )(u, xt, *part_args)


def _peer_ffn(h, layer, norm_g, w_q, sub_keys, u, v, split_at=None):
    ne = 4
    xn, xn_t = _rmsnorm(h, norm_g, with_transpose=True)
    scores = _peer_scores(xn, w_q, sub_keys, layer)
    m1x, e1x, rank2, e2 = _peer_select(scores, ROW_BLOCK)
    act = _peer_act(xn_t, u, layer, m1x, e1x, rank2, e2, ne)
    down = functools.partial(_matmul, act, v, layer, mode="residual", aux=h, tm=1024, tn=1024, tk=2048)
    if split_at is None:
        return down(name="peer_down")
    return (down(row0=0, m=split_at, name="peer_down_head"),
            down(row0=split_at, m=h.shape[0] - split_at, name="peer_down_tail"))


def kernel(x_prompt, x_sample, cache_a_k, cache_a_v, state_b_pool, cache_c_k, cache_c_v, norm_mix, norm_ffn, ab_w_in, ab_q_gain, ab_k_gain, ab_rel_bias, ab_pool_w, ab_pool_scale, ab_w_out, c_w_in, c_w_out, peer_w_q, peer_sub_keys, peer_u, peer_v):
    bp, sp, d = x_prompt.shape
    bs, ls, _ = x_sample.shape
    assert bp == 1
    depth = norm_mix.shape[0]
    past_len = cache_c_k.shape[2]
    a_heads = ab_rel_bias.shape[1]
    a_width = a_heads * HEAD_DIM
    b_width = ab_pool_scale.shape[1]
    c_width = c_w_out.shape[1]
    win_p = min(A_PAST_CHUNKS * CHUNK, sp)
    n_s = bs * ls

    h = jnp.concatenate([x_prompt.reshape(sp, d), x_sample.reshape(n_s, d)], axis=0)

    ab_w_in_b, ab_w_out_b = ab_w_in.astype(BF16), ab_w_out.astype(BF16)
    c_w_in_b, c_w_out_b = c_w_in.astype(BF16), c_w_out.astype(BF16)
    peer_w_q_b, peer_v_b = peer_w_q.astype(BF16), peer_v.astype(BF16)
    sub_keys_b = peer_sub_keys.astype(BF16).reshape(depth, PEER_HEADS * 2, N_KEYS, -1)

    outs = {name: [] for name in ("a_k_p", "a_v_p", "b_p", "c_k_p", "c_v_p",
                                  "a_k_s", "a_v_s", "b_s", "c_k_s", "c_v_s")}
    for layer in range(depth):
        xn = _rmsnorm(h, norm_mix[layer])
        if layer % 2 == 0:
            i = layer // 2
            q = _matmul(xn, ab_w_in_b, i, col0=0, n=a_width, out_dtype=BF16, mode="headnorm",
                        aux=ab_q_gain[i], name="a_q")
            k = _matmul(xn, ab_w_in_b, i, col0=a_width, n=a_width, mode="headnorm", aux=ab_k_gain[i], name="a_k")
            v = _matmul(xn, ab_w_in_b, i, col0=2 * a_width, n=a_width, name="a_v")
            p = _matmul(xn, ab_w_in_b, i, col0=3 * a_width, n=b_width, name="b_in")

            att = jnp.zeros((sp + n_s, a_width), BF16)
            att = _band_prompt(q, k, v, ab_rel_bias[i], sp, att)
            att = _band_sample(q, k, v, cache_a_k, cache_a_v, i, ab_rel_bias[i], sp, ls, past_len, att)

            pool_w = ab_pool_w[i].astype(BF16)
            tr = _tile(sp, 256)
            pool = jnp.zeros((sp + n_s, b_width), BF16)
            pool = _pool(p, None, pool_w, ab_pool_scale[i], pool, row_block0=0, nblk=sp // tr, tr=tr,
                         pos_base=0, pos_step=tr, name="pool_prompt")
            hist_s = jnp.pad(state_b_pool[i], ((0, 0), (POOL_HIST_PAD - POOL_HIST, 0), (0, 0)))
            pool = _pool(p, hist_s, pool_w, ab_pool_scale[i], pool, row_block0=sp // ls, nblk=bs, tr=ls,
                         pos_base=past_len, pos_step=0, name="pool_sample")

            h = _matmul(att, ab_w_out_b, i, x2=pool, mode="residual", aux=h, name="a_out")

            ps = p[sp:].reshape(bs, ls, b_width)
            outs["a_k_p"].append(k[sp - win_p:sp].reshape(bp, win_p, a_heads, HEAD_DIM))
            outs["a_v_p"].append(v[sp - win_p:sp].reshape(bp, win_p, a_heads, HEAD_DIM))
            outs["b_p"].append(p[sp - POOL_HIST:sp].reshape(bp, POOL_HIST, b_width))
            outs["a_k_s"].append(k[sp:].reshape(bs, ls, a_heads, HEAD_DIM))
            outs["a_v_s"].append(v[sp:].reshape(bs, ls, a_heads, HEAD_DIM))
            outs["b_s"].append(jnp.concatenate([state_b_pool[i], ps], axis=1)[:, ls:])
        else:
            j = layer // 2
            c_heads = c_width // HEAD_DIM
            q = _matmul(xn, c_w_in_b, j, col0=0, n=c_width, out_dtype=BF16, name="c_q")
            k_p, k_p3 = _proj_heads(xn, c_w_in_b, j, col0=c_width, n=c_width, row0=0, m=sp, name="c_k_prompt")
            v_p, v_p3 = _proj_heads(xn, c_w_in_b, j, col0=2 * c_width, n=c_width, row0=0, m=sp, name="c_v_prompt")
            k_s, k_s3 = _proj_heads(xn, c_w_in_b, j, col0=c_width, n=c_width, row0=sp, m=n_s, name="c_k_sample")
            v_s, v_s3 = _proj_heads(xn, c_w_in_b, j, col0=2 * c_width, n=c_width, row0=sp, m=n_s, name="c_v_sample")
            att = jnp.zeros((sp + n_s, c_width), BF16)
            att = _sb_prompt(q, k_p, v_p, sp, att)
            att = _sb_sample(q, k_s, v_s, cache_c_k, cache_c_v, j, sp, ls, att)
            h = _matmul(att, c_w_out_b, j, mode="residual", aux=h, name="c_out")
            outs["c_k_p"].append(k_p3.reshape(bp, sp, c_heads, HEAD_DIM))
            outs["c_v_p"].append(v_p3.reshape(bp, sp, c_heads, HEAD_DIM))
            outs["c_k_s"].append(k_s3.reshape(bs, ls, c_heads, HEAD_DIM))
            outs["c_v_s"].append(v_s3.reshape(bs, ls, c_heads, HEAD_DIM))

        h = _peer_ffn(h, layer, norm_ffn[layer], peer_w_q_b, sub_keys_b, peer_u, peer_v_b,
                      split_at=sp if layer == depth - 1 else None)

    h_prompt, h_sample = h
    st = {name: jnp.stack(vals) for name, vals in outs.items()}
    return (h_prompt.reshape(bp, sp, d), h_sample.reshape(bs, ls, d),
            st["a_k_p"], st["a_v_p"], st["b_p"], st["c_k_p"], st["c_v_p"],
            st["a_k_s"], st["a_v_s"], st["b_s"], st["c_k_s"], st["c_v_s"])
```
